```python
import math
import jax, jax.numpy as jnp
from jax import lax
import numpy as np

D_MODEL = 2048
BATCH = 8
SEQ = 4096
DEPTH = 4

N_A = DEPTH // 2
N_B = DEPTH - N_A

CHUNK = 128
A_WIDTH = D_MODEL
A_GROUPS = 16
A_GROUP_DIM = A_WIDTH // A_GROUPS

N_HEADS = 16
HEAD_DIM = 128
ATT_WIDTH = N_HEADS * HEAD_DIM
Q_BLOCK = 128

FFN_DIM = 5632
CONV_W = 3

EPS = 1e-6

kernel_name = "hybrid_gmlp_fox_yoco_convffn"


def rmsnorm(x, g):
    xf = x.astype(jnp.float32)
    r = lax.rsqrt(jnp.mean(xf * xf, axis=-1, keepdims=True) + EPS)
    return (xf * r).astype(x.dtype) * g


def chunked_gmlp(xn, w_in, v_norm, w_s, b_s, w_out):
    B, S, _ = xn.shape
    z = jax.nn.gelu(xn @ w_in, approximate=False)
    u, v = jnp.split(z, 2, axis=-1)
    v = rmsnorm(v, v_norm)
    v = v.reshape(B, S // CHUNK, CHUNK, A_GROUPS, A_GROUP_DIM)
    causal = jnp.tril(jnp.ones((CHUNK, CHUNK), dtype=bool))
    w = jnp.where(causal[None], w_s, jnp.zeros_like(w_s))
    mixed = jnp.einsum('gts,bnsgc->bntgc', w, v) + b_s.T[:, :, None]
    gated = u * mixed.reshape(B, S, A_WIDTH)
    return gated @ w_out


def conv_ffn(xn, w_up, conv_w, conv_b, w_down):
    S = xn.shape[1]
    h = xn @ w_up
    hp = jnp.pad(h, ((0, 0), (CONV_W - 1, 0), (0, 0)))
    h = sum(hp[:, k:k + S] * conv_w[k] for k in range(CONV_W)) + conv_b
    gate, val = jnp.split(h, 2, axis=-1)
    return (jax.nn.silu(gate) * val) @ w_down


def fox_shared_kv(h, kv_norm, w_kvf, b_f, k_norm):
    B, S, _ = h.shape
    xn = rmsnorm(h, kv_norm)
    kvf = xn @ w_kvf
    k = kvf[..., :ATT_WIDTH].reshape(B, S, N_HEADS, HEAD_DIM)
    v = kvf[..., ATT_WIDTH:2 * ATT_WIDTH].reshape(B, S, N_HEADS, HEAD_DIM)
    f = kvf[..., 2 * ATT_WIDTH:] + b_f
    k = rmsnorm(k, k_norm).transpose(0, 2, 1, 3)
    v = v.transpose(0, 2, 1, 3)
    log_f = jax.nn.log_sigmoid(f.astype(jnp.float32))
    c = jnp.cumsum(log_f, axis=1).transpose(0, 2, 1)
    return k, v, c


def forgetting_attention(xn, w_qg, q_norm, w_out, k, v, c):
    B, S, _ = xn.shape
    qg = xn @ w_qg
    q, g = jnp.split(qg, 2, axis=-1)
    q = rmsnorm(q.reshape(B, S, N_HEADS, HEAD_DIM), q_norm).transpose(0, 2, 1, 3)
    scale = HEAD_DIM ** -0.5
    outs = []
    for i in range(S // Q_BLOCK):
        lo, hi = i * Q_BLOCK, (i + 1) * Q_BLOCK
        qb = q[:, :, lo:hi]
        kb = k[:, :, :hi]
        vb = v[:, :, :hi]
        s = (jnp.einsum('bhqd,bhkd->bhqk', qb, kb).astype(jnp.float32) * scale
             + c[:, :, lo:hi, None] - c[:, :, None, :hi])
        mask = (lo + jnp.arange(Q_BLOCK))[:, None] >= jnp.arange(hi)[None, :]
        s = jnp.where(mask, s, -jnp.inf)
        p = jax.nn.softmax(s, axis=-1).astype(vb.dtype)
        outs.append(jnp.einsum('bhqk,bhkd->bhqd', p, vb))
    o = jnp.concatenate(outs, axis=2).transpose(0, 2, 1, 3).reshape(B, S, ATT_WIDTH)
    o = o * jax.nn.sigmoid(g)
    return o @ w_out


def _fwd_setup_inputs(seed: int = 0) -> dict:
    key = jax.random.key(seed)
    ks = jax.random.split(key, 24)
    f32 = jnp.float32

    def nrm(k, shape, scale):
        return jax.random.normal(k, shape, f32) * scale

    def gain(k, shape):
        return 1.0 + 0.02 * jax.random.normal(k, shape, f32)

    return {
        "x": jax.random.normal(ks[0], (BATCH, SEQ, D_MODEL), f32),
        "a_norm": gain(ks[1], (N_A, D_MODEL)),
        "a_w_in": nrm(ks[2], (N_A, D_MODEL, 2 * A_WIDTH), D_MODEL ** -0.5),
        "a_v_norm": gain(ks[3], (N_A, A_WIDTH)),
        "a_w_s": nrm(ks[4], (N_A, A_GROUPS, CHUNK, CHUNK), 0.5 * CHUNK ** -0.5),
        "a_b_s": 1.0 + 0.1 * jax.random.normal(ks[5], (N_A, A_GROUPS, CHUNK), f32),
        "a_w_out": nrm(ks[6], (N_A, A_WIDTH, D_MODEL), A_WIDTH ** -0.5),
        "kv_norm": gain(ks[7], (D_MODEL,)),
        "w_kvf": nrm(ks[8], (D_MODEL, 2 * ATT_WIDTH + N_HEADS), D_MODEL ** -0.5),
        "b_f": jax.random.uniform(ks[9], (N_HEADS,), f32, 2.0, 5.0),
        "k_norm": gain(ks[10], (HEAD_DIM,)),
        "b_norm": gain(ks[11], (N_B, D_MODEL)),
        "b_w_qg": nrm(ks[12], (N_B, D_MODEL, 2 * ATT_WIDTH), D_MODEL ** -0.5),
        "q_norm": gain(ks[13], (N_B, HEAD_DIM)),
        "b_w_out": nrm(ks[14], (N_B, ATT_WIDTH, D_MODEL), ATT_WIDTH ** -0.5),
        "f_norm": gain(ks[15], (DEPTH, D_MODEL)),
        "f_w_up": nrm(ks[16], (DEPTH, D_MODEL, 2 * FFN_DIM), D_MODEL ** -0.5),
        "f_conv_w": jnp.zeros((DEPTH, CONV_W, 2 * FFN_DIM), f32).at[:, CONV_W - 1].set(1.0)
                    + nrm(ks[17], (DEPTH, CONV_W, 2 * FFN_DIM), 0.3),
        "f_conv_b": nrm(ks[18], (DEPTH, 2 * FFN_DIM), 0.01),
        "f_w_down": nrm(ks[19], (DEPTH, FFN_DIM, D_MODEL), FFN_DIM ** -0.5),
        "final_norm": gain(ks[20], (D_MODEL,)),
    }


def _fwd_reference(x, a_norm, a_w_in, a_v_norm, a_w_s, a_b_s, a_w_out,
              kv_norm, w_kvf, b_f, k_norm,
              b_norm, b_w_qg, q_norm, b_w_out,
              f_norm, f_w_up, f_conv_w, f_conv_b, f_w_down, final_norm):
    h = x
    k_sh = v_sh = c_sh = None
    for l in range(DEPTH):
        if l < N_A:
            h = h + chunked_gmlp(rmsnorm(h, a_norm[l]), a_w_in[l], a_v_norm[l],
                                 a_w_s[l], a_b_s[l], a_w_out[l])
        else:
            j = l - N_A
            h = h + forgetting_attention(rmsnorm(h, b_norm[j]), b_w_qg[j], q_norm[j],
                                         b_w_out[j], k_sh, v_sh, c_sh)
        h = h + conv_ffn(rmsnorm(h, f_norm[l]), f_w_up[l], f_conv_w[l],
                         f_conv_b[l], f_w_down[l])
        if l == N_A - 1:
            k_sh, v_sh, c_sh = fox_shared_kv(h, kv_norm, w_kvf, b_f, k_norm)
    return rmsnorm(h, final_norm)


import jax as _jax
import jax.numpy as _jnp

TWIN_FORMAT = 'train_step'
FWD_PARAMS = ['x', 'a_norm', 'a_w_in', 'a_v_norm', 'a_w_s', 'a_b_s', 'a_w_out', 'kv_norm', 'w_kvf', 'b_f', 'k_norm', 'b_norm', 'b_w_qg', 'q_norm', 'b_w_out', 'f_norm', 'f_w_up', 'f_conv_w', 'f_conv_b', 'f_w_down', 'final_norm']
TWIN_WEIGHTS = ['a_norm', 'a_w_in', 'a_v_norm', 'a_w_s', 'a_b_s', 'a_w_out', 'kv_norm', 'w_kvf', 'b_f', 'k_norm', 'b_norm', 'b_w_qg', 'q_norm', 'b_w_out', 'f_norm', 'f_w_up', 'f_conv_w', 'f_conv_b', 'f_w_down', 'final_norm']
TWIN_DIFF_INPUT = 'x'
TWIN_INPUTS = ['x', 'a_norm', 'a_w_in', 'a_v_norm', 'a_w_s', 'a_b_s', 'a_w_out', 'kv_norm', 'w_kvf', 'b_f', 'k_norm', 'b_norm', 'b_w_qg', 'q_norm', 'b_w_out', 'f_norm', 'f_w_up', 'f_conv_w', 'f_conv_b', 'f_w_down', 'final_norm', 'loss_target', 'm_a_norm', 'm_a_w_in', 'm_a_v_norm', 'm_a_w_s', 'm_a_b_s', 'm_a_w_out', 'm_kv_norm', 'm_w_kvf', 'm_b_f', 'm_k_norm', 'm_b_norm', 'm_b_w_qg', 'm_q_norm', 'm_b_w_out', 'm_f_norm', 'm_f_w_up', 'm_f_conv_w', 'm_f_conv_b', 'm_f_w_down', 'm_final_norm', 'v_a_norm', 'v_a_w_in', 'v_a_v_norm', 'v_a_w_s', 'v_a_b_s', 'v_a_w_out', 'v_kv_norm', 'v_w_kvf', 'v_b_f', 'v_k_norm', 'v_b_norm', 'v_b_w_qg', 'v_q_norm', 'v_b_w_out', 'v_f_norm', 'v_f_w_up', 'v_f_conv_w', 'v_f_conv_b', 'v_f_w_down', 'v_final_norm']
TWIN_OUTPUTS = ['loss', 'grad_x', 'grad_a_norm', 'grad_a_w_in', 'grad_a_v_norm', 'grad_a_w_s', 'grad_a_b_s', 'grad_a_w_out', 'grad_kv_norm', 'grad_w_kvf', 'grad_b_f', 'grad_k_norm', 'grad_b_norm', 'grad_b_w_qg', 'grad_q_norm', 'grad_b_w_out', 'grad_f_norm', 'grad_f_w_up', 'grad_f_conv_w', 'grad_f_conv_b', 'grad_f_w_down', 'grad_final_norm', 'delta_a_norm', 'delta_a_w_in', 'delta_a_v_norm', 'delta_a_w_s', 'delta_a_b_s', 'delta_a_w_out', 'delta_kv_norm', 'delta_w_kvf', 'delta_b_f', 'delta_k_norm', 'delta_b_norm', 'delta_b_w_qg', 'delta_q_norm', 'delta_b_w_out', 'delta_f_norm', 'delta_f_w_up', 'delta_f_conv_w', 'delta_f_conv_b', 'delta_f_w_down', 'delta_final_norm', 'new_m_a_norm', 'new_m_a_w_in', 'new_m_a_v_norm', 'new_m_a_w_s', 'new_m_a_b_s', 'new_m_a_w_out', 'new_m_kv_norm', 'new_m_w_kvf', 'new_m_b_f', 'new_m_k_norm', 'new_m_b_norm', 'new_m_b_w_qg', 'new_m_q_norm', 'new_m_b_w_out', 'new_m_f_norm', 'new_m_f_w_up', 'new_m_f_conv_w', 'new_m_f_conv_b', 'new_m_f_w_down', 'new_m_final_norm', 'new_v_a_norm', 'new_v_a_w_in', 'new_v_a_v_norm', 'new_v_a_w_s', 'new_v_a_b_s', 'new_v_a_w_out', 'new_v_kv_norm', 'new_v_w_kvf', 'new_v_b_f', 'new_v_k_norm', 'new_v_b_norm', 'new_v_b_w_qg', 'new_v_q_norm', 'new_v_b_w_out', 'new_v_f_norm', 'new_v_f_w_up', 'new_v_f_conv_w', 'new_v_f_conv_b', 'new_v_f_w_down', 'new_v_final_norm']
TWIN_LEAF_KINDS = {'loss': 'loss', 'grad_x': 'grad_x', 'grad_a_norm': 'grad_w', 'grad_a_w_in': 'grad_w', 'grad_a_v_norm': 'grad_w', 'grad_a_w_s': 'grad_w', 'grad_a_b_s': 'grad_w', 'grad_a_w_out': 'grad_w', 'grad_kv_norm': 'grad_w', 'grad_w_kvf': 'grad_w', 'grad_b_f': 'grad_w', 'grad_k_norm': 'grad_w', 'grad_b_norm': 'grad_w', 'grad_b_w_qg': 'grad_w', 'grad_q_norm': 'grad_w', 'grad_b_w_out': 'grad_w', 'grad_f_norm': 'grad_w', 'grad_f_w_up': 'grad_w', 'grad_f_conv_w': 'grad_w', 'grad_f_conv_b': 'grad_w', 'grad_f_w_down': 'grad_w', 'grad_final_norm': 'grad_w', 'delta_a_norm': 'delta_w', 'delta_a_w_in': 'delta_w', 'delta_a_v_norm': 'delta_w', 'delta_a_w_s': 'delta_w', 'delta_a_b_s': 'delta_w', 'delta_a_w_out': 'delta_w', 'delta_kv_norm': 'delta_w', 'delta_w_kvf': 'delta_w', 'delta_b_f': 'delta_w', 'delta_k_norm': 'delta_w', 'delta_b_norm': 'delta_w', 'delta_b_w_qg': 'delta_w', 'delta_q_norm': 'delta_w', 'delta_b_w_out': 'delta_w', 'delta_f_norm': 'delta_w', 'delta_f_w_up': 'delta_w', 'delta_f_conv_w': 'delta_w', 'delta_f_conv_b': 'delta_w', 'delta_f_w_down': 'delta_w', 'delta_final_norm': 'delta_w', 'new_m_a_norm': 'new_m', 'new_m_a_w_in': 'new_m', 'new_m_a_v_norm': 'new_m', 'new_m_a_w_s': 'new_m', 'new_m_a_b_s': 'new_m', 'new_m_a_w_out': 'new_m', 'new_m_kv_norm': 'new_m', 'new_m_w_kvf': 'new_m', 'new_m_b_f': 'new_m', 'new_m_k_norm': 'new_m', 'new_m_b_norm': 'new_m', 'new_m_b_w_qg': 'new_m', 'new_m_q_norm': 'new_m', 'new_m_b_w_out': 'new_m', 'new_m_f_norm': 'new_m', 'new_m_f_w_up': 'new_m', 'new_m_f_conv_w': 'new_m', 'new_m_f_conv_b': 'new_m', 'new_m_f_w_down': 'new_m', 'new_m_final_norm': 'new_m', 'new_v_a_norm': 'new_v', 'new_v_a_w_in': 'new_v', 'new_v_a_v_norm': 'new_v', 'new_v_a_w_s': 'new_v', 'new_v_a_b_s': 'new_v', 'new_v_a_w_out': 'new_v', 'new_v_kv_norm': 'new_v', 'new_v_w_kvf': 'new_v', 'new_v_b_f': 'new_v', 'new_v_k_norm': 'new_v', 'new_v_b_norm': 'new_v', 'new_v_b_w_qg': 'new_v', 'new_v_q_norm': 'new_v', 'new_v_b_w_out': 'new_v', 'new_v_f_norm': 'new_v', 'new_v_f_w_up': 'new_v', 'new_v_f_conv_w': 'new_v', 'new_v_f_conv_b': 'new_v', 'new_v_f_w_down': 'new_v', 'new_v_final_norm': 'new_v'}


def _forward(args):
    return _fwd_reference(*[args[k] for k in FWD_PARAMS])


def _output_shape():
    def fwd():
        inp = _fwd_setup_inputs(0)
        return _fwd_reference(*[inp[k] for k in FWD_PARAMS])
    out = _jax.eval_shape(fwd)
    return out.shape, out.dtype

N_MICROBATCH = 1
ADAM_LR = 0.001
ADAM_B1 = 0.9
ADAM_B2 = 0.999
ADAM_EPS = 1e-08
ADAM_WD = 0.01
ADAM_STEP = 10
PER_EXAMPLE_BATCH_AXIS = {'x': 0, 'loss_target': 0}
SHARED_INPUTS = []
_WEIGHT_DTYPES = {'a_norm': _jnp.float32, 'a_w_in': _jnp.float32, 'a_v_norm': _jnp.float32, 'a_w_s': _jnp.float32, 'a_b_s': _jnp.float32, 'a_w_out': _jnp.float32, 'kv_norm': _jnp.float32, 'w_kvf': _jnp.float32, 'b_f': _jnp.float32, 'k_norm': _jnp.float32, 'b_norm': _jnp.float32, 'b_w_qg': _jnp.float32, 'q_norm': _jnp.float32, 'b_w_out': _jnp.float32, 'f_norm': _jnp.float32, 'f_w_up': _jnp.float32, 'f_conv_w': _jnp.float32, 'f_conv_b': _jnp.float32, 'f_w_down': _jnp.float32, 'final_norm': _jnp.float32}
MOMENT_SCALE = {'a_norm': 7.003372e-02, 'a_w_in': 4.882203e-02, 'a_v_norm': 2.130271e-02, 'a_w_s': 4.157105e-02, 'a_b_s': 5.877180e-02, 'a_w_out': 6.602708e-02, 'kv_norm': 1.796835e-02, 'w_kvf': 1.260015e-02, 'b_f': 9.316317e-02, 'k_norm': 3.829207e-02, 'b_norm': 7.551472e-03, 'b_w_qg': 5.318871e-03, 'q_norm': 2.705800e-02, 'b_w_out': 1.115719e-02, 'f_norm': 6.716135e-02, 'f_w_up': 2.803420e-02, 'f_conv_w': 2.527723e-02, 'f_conv_b': 2.578977e-02, 'f_w_down': 4.585740e-02, 'final_norm': 1.604781e+01}


def _to_microbatches(a, axis):
    t = _jnp.moveaxis(a, axis, 0)
    t = t.reshape((N_MICROBATCH, t.shape[0] // N_MICROBATCH) + t.shape[1:])
    return _jnp.moveaxis(t, 1, axis + 1)


def setup_inputs(seed: int = 0) -> dict:
    inp = _fwd_setup_inputs(seed)
    key = _jax.random.fold_in(_jax.random.key(seed), 7919)
    shape, _ = _output_shape()
    out = dict(inp)
    out["loss_target"] = _jax.random.normal(_jax.random.fold_in(key, 0), shape, _jnp.float32)
    for i, name in enumerate(TWIN_WEIGHTS):
        w = inp[name].astype(_jnp.float32)
        if MOMENT_SCALE is None:
            s = _jnp.sqrt(_jnp.mean(_jnp.square(w)) + 1e-30)
        else:
            s = MOMENT_SCALE[name]
        km, kv = _jax.random.split(_jax.random.fold_in(key, i + 1))
        out[name] = w
        out["m_" + name] = s * _jax.random.normal(km, w.shape, _jnp.float32)
        out["v_" + name] = (s * s) * _jax.random.uniform(kv, w.shape, _jnp.float32, 0.5, 1.5)
    if N_MICROBATCH > 1:
        for name, axis in PER_EXAMPLE_BATCH_AXIS.items():
            out[name] = _to_microbatches(out[name], axis)
    return {'x': out['x'], 'a_norm': out['a_norm'], 'a_w_in': out['a_w_in'], 'a_v_norm': out['a_v_norm'], 'a_w_s': out['a_w_s'], 'a_b_s': out['a_b_s'], 'a_w_out': out['a_w_out'], 'kv_norm': out['kv_norm'], 'w_kvf': out['w_kvf'], 'b_f': out['b_f'], 'k_norm': out['k_norm'], 'b_norm': out['b_norm'], 'b_w_qg': out['b_w_qg'], 'q_norm': out['q_norm'], 'b_w_out': out['b_w_out'], 'f_norm': out['f_norm'], 'f_w_up': out['f_w_up'], 'f_conv_w': out['f_conv_w'], 'f_conv_b': out['f_conv_b'], 'f_w_down': out['f_w_down'], 'final_norm': out['final_norm'], 'loss_target': out['loss_target'], 'm_a_norm': out['m_a_norm'], 'm_a_w_in': out['m_a_w_in'], 'm_a_v_norm': out['m_a_v_norm'], 'm_a_w_s': out['m_a_w_s'], 'm_a_b_s': out['m_a_b_s'], 'm_a_w_out': out['m_a_w_out'], 'm_kv_norm': out['m_kv_norm'], 'm_w_kvf': out['m_w_kvf'], 'm_b_f': out['m_b_f'], 'm_k_norm': out['m_k_norm'], 'm_b_norm': out['m_b_norm'], 'm_b_w_qg': out['m_b_w_qg'], 'm_q_norm': out['m_q_norm'], 'm_b_w_out': out['m_b_w_out'], 'm_f_norm': out['m_f_norm'], 'm_f_w_up': out['m_f_w_up'], 'm_f_conv_w': out['m_f_conv_w'], 'm_f_conv_b': out['m_f_conv_b'], 'm_f_w_down': out['m_f_w_down'], 'm_final_norm': out['m_final_norm'], 'v_a_norm': out['v_a_norm'], 'v_a_w_in': out['v_a_w_in'], 'v_a_v_norm': out['v_a_v_norm'], 'v_a_w_s': out['v_a_w_s'], 'v_a_b_s': out['v_a_b_s'], 'v_a_w_out': out['v_a_w_out'], 'v_kv_norm': out['v_kv_norm'], 'v_w_kvf': out['v_w_kvf'], 'v_b_f': out['v_b_f'], 'v_k_norm': out['v_k_norm'], 'v_b_norm': out['v_b_norm'], 'v_b_w_qg': out['v_b_w_qg'], 'v_q_norm': out['v_q_norm'], 'v_b_w_out': out['v_b_w_out'], 'v_f_norm': out['v_f_norm'], 'v_f_w_up': out['v_f_w_up'], 'v_f_conv_w': out['v_f_conv_w'], 'v_f_conv_b': out['v_f_conv_b'], 'v_f_w_down': out['v_f_w_down'], 'v_final_norm': out['v_final_norm']}


def _loss(weights, diff, rest, loss_target):
    with _jax.named_scope("forward"):
        args = {**rest, TWIN_DIFF_INPUT: diff, **{k: w.astype(_WEIGHT_DTYPES[k]) for k, w in weights.items()}}
        y = _forward(args)
    with _jax.named_scope("loss_head"):
        err = _jnp.square(y.astype(_jnp.float32) - loss_target)
        return 0.5 * _jnp.sum(_jnp.mean(err, axis=-1)) if err.ndim else 0.5 * err


def _adamw(w, g, m, v):
    m = ADAM_B1 * m + (1.0 - ADAM_B1) * g
    v = ADAM_B2 * v + (1.0 - ADAM_B2) * _jnp.square(g)
    m_hat = m / (1.0 - ADAM_B1 ** ADAM_STEP)
    v_hat = v / (1.0 - ADAM_B2 ** ADAM_STEP)
    delta = -ADAM_LR * (m_hat / (_jnp.sqrt(v_hat) + ADAM_EPS) + ADAM_WD * w)
    return delta, m, v


def reference(x, a_norm, a_w_in, a_v_norm, a_w_s, a_b_s, a_w_out, kv_norm, w_kvf, b_f, k_norm, b_norm, b_w_qg, q_norm, b_w_out, f_norm, f_w_up, f_conv_w, f_conv_b, f_w_down, final_norm, loss_target, m_a_norm, m_a_w_in, m_a_v_norm, m_a_w_s, m_a_b_s, m_a_w_out, m_kv_norm, m_w_kvf, m_b_f, m_k_norm, m_b_norm, m_b_w_qg, m_q_norm, m_b_w_out, m_f_norm, m_f_w_up, m_f_conv_w, m_f_conv_b, m_f_w_down, m_final_norm, v_a_norm, v_a_w_in, v_a_v_norm, v_a_w_s, v_a_b_s, v_a_w_out, v_kv_norm, v_w_kvf, v_b_f, v_k_norm, v_b_norm, v_b_w_qg, v_q_norm, v_b_w_out, v_f_norm, v_f_w_up, v_f_conv_w, v_f_conv_b, v_f_w_down, v_final_norm):
    given = dict(x=x, a_norm=a_norm, a_w_in=a_w_in, a_v_norm=a_v_norm, a_w_s=a_w_s, a_b_s=a_b_s, a_w_out=a_w_out, kv_norm=kv_norm, w_kvf=w_kvf, b_f=b_f, k_norm=k_norm, b_norm=b_norm, b_w_qg=b_w_qg, q_norm=q_norm, b_w_out=b_w_out, f_norm=f_norm, f_w_up=f_w_up, f_conv_w=f_conv_w, f_conv_b=f_conv_b, f_w_down=f_w_down, final_norm=final_norm, loss_target=loss_target, m_a_norm=m_a_norm, m_a_w_in=m_a_w_in, m_a_v_norm=m_a_v_norm, m_a_w_s=m_a_w_s, m_a_b_s=m_a_b_s, m_a_w_out=m_a_w_out, m_kv_norm=m_kv_norm, m_w_kvf=m_w_kvf, m_b_f=m_b_f, m_k_norm=m_k_norm, m_b_norm=m_b_norm, m_b_w_qg=m_b_w_qg, m_q_norm=m_q_norm, m_b_w_out=m_b_w_out, m_f_norm=m_f_norm, m_f_w_up=m_f_w_up, m_f_conv_w=m_f_conv_w, m_f_conv_b=m_f_conv_b, m_f_w_down=m_f_w_down, m_final_norm=m_final_norm, v_a_norm=v_a_norm, v_a_w_in=v_a_w_in, v_a_v_norm=v_a_v_norm, v_a_w_s=v_a_w_s, v_a_b_s=v_a_b_s, v_a_w_out=v_a_w_out, v_kv_norm=v_kv_norm, v_w_kvf=v_w_kvf, v_b_f=v_b_f, v_k_norm=v_k_norm, v_b_norm=v_b_norm, v_b_w_qg=v_b_w_qg, v_q_norm=v_q_norm, v_b_w_out=v_b_w_out, v_f_norm=v_f_norm, v_f_w_up=v_f_w_up, v_f_conv_w=v_f_conv_w, v_f_conv_b=v_f_conv_b, v_f_w_down=v_f_w_down, v_final_norm=v_final_norm)
    weights = {n: given[n] for n in TWIN_WEIGHTS}
    shared = {n: given[n] for n in SHARED_INPUTS}
    per_example = {n: given[n] for n in ['x']}
    grad_fn = _jax.value_and_grad(_loss, argnums=(0, 1))

    def one_microbatch(ex, loss_target):
        ex = dict(ex)
        diff = ex.pop(TWIN_DIFF_INPUT)
        return grad_fn(weights, diff, {**shared, **ex}, loss_target)

    if N_MICROBATCH == 1:
        loss, (grad_w, grad_x) = one_microbatch(per_example, given["loss_target"])
    else:
        def body(carry, xs):
            loss_sum, grad_sum = carry
            l_k, (gw_k, gx_k) = one_microbatch(xs[0], xs[1])
            with _jax.named_scope("update"):
                return (loss_sum + l_k, _jax.tree.map(_jnp.add, grad_sum, gw_k)), gx_k

        init = (_jnp.zeros((), _jnp.float32), _jax.tree.map(_jnp.zeros_like, weights))
        (loss, grad_w), grad_x = _jax.lax.scan(body, init, (per_example, given["loss_target"]))
    with _jax.named_scope("update"):
        delta_w, new_m, new_v = {}, {}, {}
        for n in TWIN_WEIGHTS:
            delta_w[n], new_m[n], new_v[n] = _adamw(weights[n], grad_w[n], given["m_" + n], given["v_" + n])
    return (loss, grad_x, *[grad_w[n] for n in TWIN_WEIGHTS], *[delta_w[n] for n in TWIN_WEIGHTS],
            *[new_m[n] for n in TWIN_WEIGHTS], *[new_v[n] for n in TWIN_WEIGHTS])
```

```python
import functools
import math

import jax
import jax.numpy as jnp
from jax import lax
from jax.experimental import pallas as pl
from jax.experimental.pallas import tpu as pltpu

F32, BF16 = jnp.float32, jnp.bfloat16
EPS = 1e-6
CHUNK = 128
HEAD = 128
LANE = 128
HALO = 16
N_CHIPS = 4
VMEM_LIMIT = 48 * 1024 * 1024
MESH = pl.DeviceIdType.MESH
ANY = pl.BlockSpec(memory_space=pl.ANY)

ADAM_LR, ADAM_B1, ADAM_B2, ADAM_EPS, ADAM_WD, ADAM_STEP = 0.001, 0.9, 0.999, 1e-08, 0.01, 10
NEG = -1e30


def _cp(*sem):
    return pltpu.CompilerParams(dimension_semantics=sem, vmem_limit_bytes=VMEM_LIMIT)


def _t(dim, pref, mult=LANE):
    if dim <= pref:
        return dim
    t = (pref // mult) * mult
    while t >= mult:
        if dim % t == 0:
            return t
        t -= mult
    return dim


def _sds(shape, dtype):
    return jax.ShapeDtypeStruct(tuple(shape), dtype)


class Mat:
    def __init__(self, arr, shape, rlim, clim, block, index):
        self.arr, self.shape, self.rlim, self.clim = arr, shape, rlim, clim
        self._block, self._index = block, index

    def spec(self, tr, tc, gmap):
        assert self.rlim % tr == 0 and self.clim % tc == 0, (self.shape, self.rlim, self.clim, tr, tc)
        index = self._index(tr, tc)
        return pl.BlockSpec(self._block(tr, tc), lambda *g: index(*gmap(*g)))


def m2(arr):
    R, C = arr.shape
    return Mat(arr, (R, C), R, C, lambda tr, tc: (tr, tc), lambda tr, tc: (lambda i, j: (i, j)))


def mcs(arr):
    ns, R, Cs = arr.shape
    return Mat(arr, (R, ns * Cs), R, Cs, lambda tr, tc: (None, tr, tc),
               lambda tr, tc: (lambda i, j: (j // (Cs // tc), i, j % (Cs // tc))))


def mhalf(arr, p):
    ns, R, Cs = arr.shape
    return Mat(arr, (R, Cs), R, Cs, lambda tr, tc: (None, tr, tc), lambda tr, tc: (lambda i, j: (p, i, j)))


def mgc(arr, l):
    ns, L, R, Cs = arr.shape
    return Mat(arr, (R, ns * Cs), R, Cs, lambda tr, tc: (None, None, tr, tc),
               lambda tr, tc: (lambda i, j: (j // (Cs // tc), l, i, j % (Cs // tc))))


def mgr(arr, l):
    ns, L, Rs, C = arr.shape
    return Mat(arr, (ns * Rs, C), Rs, C, lambda tr, tc: (None, None, tr, tc),
               lambda tr, tc: (lambda i, j: (i // (Rs // tr), l, i % (Rs // tr), j)))


_DIMS = {"nn": ((1,), (0,)), "nt": ((1,), (1,)), "tn": ((0,), (0,))}


def _mm(a, b, mode, out, *, res=None, alias=False, tm=512, tn=512, tk=4096, name):
    if mode == "tn":
        (K, M), (K2, N) = a.shape, b.shape
        alim_m, alim_k, blim_k, blim_n = a.clim, a.rlim, b.rlim, b.clim
    elif mode == "nt":
        (M, K), (N, K2) = a.shape, b.shape
        alim_m, alim_k, blim_k, blim_n = a.rlim, a.clim, b.clim, b.rlim
    else:
        (M, K), (K2, N) = a.shape, b.shape
        alim_m, alim_k, blim_k, blim_n = a.rlim, a.clim, b.rlim, b.clim
    assert K == K2 and out.shape == (M, N), (name, a.shape, b.shape, out.shape)
    tm = _t(math.gcd(alim_m, out.rlim), tm)
    tn = _t(math.gcd(blim_n, out.clim), tn)
    tk = _t(math.gcd(alim_k, blim_k), tk)
    grid = (M // tm, N // tn, K // tk)
    nk = grid[2]
    if mode == "tn":
        a_spec = a.spec(tk, tm, lambda i, j, k: (k, i))
    else:
        a_spec = a.spec(tm, tk, lambda i, j, k: (i, k))
    if mode == "nt":
        b_spec = b.spec(tn, tk, lambda i, j, k: (j, k))
    else:
        b_spec = b.spec(tk, tn, lambda i, j, k: (k, j))
    o_spec = out.spec(tm, tn, lambda i, j, k: (i, j))
    operands, in_specs = [a.arr, b.arr], [a_spec, b_spec]
    if res is not None:
        operands.append(res.arr)
        in_specs.append(res.spec(tm, tn, lambda i, j, k: (i, j)))
    aliases = {}
    if alias:
        aliases = {len(operands): 0}
        operands.append(out.arr)
        in_specs.append(ANY)
    dims = (_DIMS[mode], ((), ()))
    has_res = res is not None

    def body(*refs):
        a_ref, b_ref = refs[0], refs[1]
        res_ref = refs[2] if has_res else None
        n_in = 2 + has_res + alias
        o_ref = refs[n_in]
        p = lax.dot_general(a_ref[...].astype(BF16), b_ref[...].astype(BF16), dims, preferred_element_type=F32)

        def finish(v):
            if has_res:
                v = v + res_ref[...].astype(F32)
            o_ref[...] = v.astype(o_ref.dtype)

        if nk == 1:
            finish(p)
        else:
            acc = refs[n_in + 1]
            k = pl.program_id(2)

            @pl.when(k == 0)
            def _():
                acc[...] = p

            @pl.when(k > 0)
            def _():
                acc[...] += p

            @pl.when(k == nk - 1)
            def _():
                finish(acc[...])

    return pl.pallas_call(
        body, name=name, grid=grid, in_specs=in_specs, out_specs=o_spec,
        out_shape=_sds(out.arr.shape, out.arr.dtype),
        scratch_shapes=[pltpu.VMEM((tm, tn), F32)] if nk > 1 else [],
        input_output_aliases=aliases,
        compiler_params=_cp("parallel", "parallel", "arbitrary"),
    )(*operands)


def _rms_fwd(x, g, name):
    S, D = x.shape
    tr = _t(S, 512)

    def body(x_ref, g_ref, o_ref):
        xv = x_ref[...]
        r = lax.rsqrt(jnp.mean(xv * xv, axis=-1, keepdims=True) + EPS)
        o_ref[...] = ((xv * r) * g_ref[...]).astype(o_ref.dtype)

    return pl.pallas_call(
        body, name=name, grid=(S // tr,),
        in_specs=[pl.BlockSpec((tr, D), lambda i: (i, 0)), pl.BlockSpec((1, D), lambda i: (0, 0))],
        out_specs=pl.BlockSpec((tr, D), lambda i: (i, 0)), out_shape=_sds((S, D), BF16),
        compiler_params=_cp("parallel"),
    )(x, g)


def _rms_bwd(dy, x, g, dres, name):
    S, D = x.shape
    tr = _t(S, 512)

    def body(dy_ref, x_ref, g_ref, dres_ref, dx_ref, dg_ref):
        xv = x_ref[...]
        dyv = dy_ref[...].astype(F32)
        r = lax.rsqrt(jnp.mean(xv * xv, axis=-1, keepdims=True) + EPS)
        xh = xv * r
        dxh = dyv * g_ref[...]
        m = jnp.mean(dxh * xh, axis=-1, keepdims=True)
        dx_ref[...] = dres_ref[...] + r * (dxh - xh * m)

        @pl.when(pl.program_id(0) == 0)
        def _():
            dg_ref[...] = jnp.zeros_like(dg_ref)

        dg_ref[...] += jnp.sum(dyv * xh, axis=0, keepdims=True)

    row = pl.BlockSpec((tr, D), lambda i: (i, 0))
    vec = pl.BlockSpec((1, D), lambda i: (0, 0))
    return pl.pallas_call(
        body, name=name, grid=(S // tr,), in_specs=[row, row, vec, row], out_specs=[row, vec],
        out_shape=[_sds((S, D), F32), _sds((1, D), F32)], compiler_params=_cp("arbitrary"),
    )(dy, x, g, dres)


def _final_loss(h, g, target, name):
    S, D = h.shape
    tr = _t(S, 512)

    def body(x_ref, g_ref, t_ref, loss_ref, dx_ref, dg_ref):
        xv = x_ref[...]
        r = lax.rsqrt(jnp.mean(xv * xv, axis=-1, keepdims=True) + EPS)
        xh = xv * r
        err = xh * g_ref[...] - t_ref[...]
        part = 0.5 * jnp.sum(jnp.mean(err * err, axis=-1, keepdims=True), axis=0, keepdims=True)
        dyv = err * (1.0 / D)
        dxh = dyv * g_ref[...]
        m = jnp.mean(dxh * xh, axis=-1, keepdims=True)
        dx_ref[...] = r * (dxh - xh * m)

        @pl.when(pl.program_id(0) == 0)
        def _():
            dg_ref[...] = jnp.zeros_like(dg_ref)
            loss_ref[...] = jnp.zeros_like(loss_ref)

        dg_ref[...] += jnp.sum(dyv * xh, axis=0, keepdims=True)
        loss_ref[...] += part

    row = pl.BlockSpec((tr, D), lambda i: (i, 0))
    vec = pl.BlockSpec((1, D), lambda i: (0, 0))
    one = pl.BlockSpec((1, 1), lambda i: (0, 0))
    return pl.pallas_call(
        body, name=name, grid=(S // tr,), in_specs=[row, vec, row], out_specs=[one, row, vec],
        out_shape=[_sds((1, 1), F32), _sds((S, D), F32), _sds((1, D), F32)], compiler_params=_cp("arbitrary"),
    )(h, g, target)


_RSQRT2 = 0.7071067811865476
_RSQRT2PI = 0.3989422804014327


def _gelu(x):
    return 0.5 * x * (1.0 + lax.erf(x * _RSQRT2))


def _gelu_grad(x):
    return 0.5 * (1.0 + lax.erf(x * _RSQRT2)) + x * (jnp.exp(-0.5 * x * x) * _RSQRT2PI)


def _sgu_fwd(zpre, wm, bcol, vnorm, name):
    S, W2 = zpre.shape
    W = W2 // 2
    G = W // CHUNK

    def body(z_ref, wm_ref, b_ref, vn_ref, o_ref):
        zp = z_ref[...].astype(F32)
        u = _gelu(zp[:, :W])
        v = _gelu(zp[:, W:])
        rv = lax.rsqrt(jnp.mean(v * v, axis=-1, keepdims=True) + EPS)
        vn = ((v * rv) * vn_ref[...]).astype(BF16)
        for g in range(G):
            sl = slice(g * CHUNK, (g + 1) * CHUNK)
            mixed = jnp.dot(wm_ref[g], vn[:, sl], preferred_element_type=F32) + b_ref[g]
            o_ref[:, sl] = (u[:, sl] * mixed).astype(o_ref.dtype)

    return pl.pallas_call(
        body, name=name, grid=(S // CHUNK,),
        in_specs=[pl.BlockSpec((CHUNK, W2), lambda i: (i, 0)),
                  pl.BlockSpec((G, CHUNK, CHUNK), lambda i: (0, 0, 0)),
                  pl.BlockSpec((G, CHUNK, 1), lambda i: (0, 0, 0)),
                  pl.BlockSpec((1, W), lambda i: (0, 0))],
        out_specs=pl.BlockSpec((CHUNK, W), lambda i: (i, 0)), out_shape=_sds((S, W), BF16),
        compiler_params=_cp("parallel"),
    )(zpre, wm, bcol, vnorm)


def _sgu_bwd(zpre, dgated, wm, wmt, bcol, vnorm, name):
    S, W2 = zpre.shape
    W = W2 // 2
    G = W // CHUNK

    def body(z_ref, dg_ref, wm_ref, wmt_ref, b_ref, vn_ref, dz_ref, dws_ref, dbs_ref, dvn_ref):
        @pl.when(pl.program_id(0) == 0)
        def _():
            dws_ref[...] = jnp.zeros_like(dws_ref)
            dbs_ref[...] = jnp.zeros_like(dbs_ref)
            dvn_ref[...] = jnp.zeros_like(dvn_ref)

        zp = z_ref[...].astype(F32)
        zu, zv = zp[:, :W], zp[:, W:]
        u = _gelu(zu)
        v = _gelu(zv)
        rv = lax.rsqrt(jnp.mean(v * v, axis=-1, keepdims=True) + EPS)
        vh = v * rv
        vn = (vh * vn_ref[...]).astype(BF16)
        dgv = dg_ref[...].astype(F32)
        du_parts, dvn_parts = [], []
        for g in range(G):
            sl = slice(g * CHUNK, (g + 1) * CHUNK)
            vg = vn[:, sl]
            mixed = jnp.dot(wm_ref[g], vg, preferred_element_type=F32) + b_ref[g]
            dgg = dgv[:, sl]
            du_parts.append(dgg * mixed)
            dmixed = dgg * u[:, sl]
            dbs_ref[g] += jnp.sum(dmixed, axis=1, keepdims=True)
            dmb = dmixed.astype(BF16)
            dws_ref[g] += lax.dot_general(dmb, vg, (_DIMS["nt"], ((), ())), preferred_element_type=F32)
            dvn_parts.append(jnp.dot(wmt_ref[g], dmb, preferred_element_type=F32))
        du = jnp.concatenate(du_parts, axis=1)
        dvn = jnp.concatenate(dvn_parts, axis=1)
        dvn_ref[...] += jnp.sum(dvn * vh, axis=0, keepdims=True)
        dvh = dvn * vn_ref[...]
        dv = rv * (dvh - vh * jnp.mean(dvh * vh, axis=-1, keepdims=True))
        dz_ref[:, :W] = (du * _gelu_grad(zu)).astype(dz_ref.dtype)
        dz_ref[:, W:] = (dv * _gelu_grad(zv)).astype(dz_ref.dtype)

    full3 = lambda shape: pl.BlockSpec(shape, lambda i: (0, 0, 0))
    return pl.pallas_call(
        body, name=name, grid=(S // CHUNK,),
        in_specs=[pl.BlockSpec((CHUNK, W2), lambda i: (i, 0)), pl.BlockSpec((CHUNK, W), lambda i: (i, 0)),
                  full3((G, CHUNK, CHUNK)), full3((G, CHUNK, CHUNK)), full3((G, CHUNK, 1)),
                  pl.BlockSpec((1, W), lambda i: (0, 0))],
        out_specs=[pl.BlockSpec((CHUNK, W2), lambda i: (i, 0)), full3((G, CHUNK, CHUNK)), full3((G, CHUNK, 1)),
                   pl.BlockSpec((1, W), lambda i: (0, 0))],
        out_shape=[_sds((S, W2), BF16), _sds((G, CHUNK, CHUNK), F32), _sds((G, CHUNK, 1), F32), _sds((1, W), F32)],
        compiler_params=_cp("arbitrary"),
    )(zpre, dgated, wm, wmt, bcol, vnorm)


def _conv_taps(h_ref, half, r0, R, tc):
    if r0 == 0:
        xe = jnp.concatenate([jnp.zeros((HALO, tc), F32), h_ref[half, 0:R, :].astype(F32)], axis=0)
    else:
        xe = h_ref[half, r0 - HALO:r0 + R, :].astype(F32)
    return xe[HALO:], pltpu.roll(xe, 1, 0)[HALO:], pltpu.roll(xe, 2, 0)[HALO:]


def _conv_apply(taps, w, b):
    x0, x1, x2 = taps
    return x2 * w[0:1] + x1 * w[1:2] + x0 * w[2:3] + b


def _convgate_fwd(hup, cw, cb, name):
    _, S, F = hup.shape
    tc = _t(F, 256)
    R = _t(S, 512)

    def body(h_ref, w_ref, b_ref, o_ref):
        for r0 in range(0, S, R):
            gate = _conv_apply(_conv_taps(h_ref, 0, r0, R, tc), w_ref[0], b_ref[0])
            val = _conv_apply(_conv_taps(h_ref, 1, r0, R, tc), w_ref[1], b_ref[1])
            o_ref[r0:r0 + R, :] = (gate * jax.nn.sigmoid(gate) * val).astype(o_ref.dtype)

    return pl.pallas_call(
        body, name=name, grid=(F // tc,),
        in_specs=[pl.BlockSpec((2, S, tc), lambda j: (0, 0, j)), pl.BlockSpec((2, 3, tc), lambda j: (0, 0, j)),
                  pl.BlockSpec((2, 1, tc), lambda j: (0, 0, j))],
        out_specs=pl.BlockSpec((S, tc), lambda j: (0, j)), out_shape=_sds((S, F), BF16),
        compiler_params=_cp("parallel"),
    )(hup, cw, cb)


def _convgate_bwd(hup, dact, cw, cb, name):
    _, S, F = hup.shape
    tc = _t(F, 256)
    R = _t(S, 512)

    def body(h_ref, da_ref, w_ref, b_ref, dh_ref, dw_ref, db_ref, dhc):
        dhc[:, S:S + HALO, :] = jnp.zeros((2, HALO, tc), F32)
        dw_acc = [[jnp.zeros((1, tc), F32) for _ in range(3)] for _ in range(2)]
        db_acc = [jnp.zeros((1, tc), F32) for _ in range(2)]
        for r0 in range(0, S, R):
            taps = [_conv_taps(h_ref, p, r0, R, tc) for p in range(2)]
            gate = _conv_apply(taps[0], w_ref[0], b_ref[0])
            val = _conv_apply(taps[1], w_ref[1], b_ref[1])
            da = da_ref[r0:r0 + R, :].astype(F32)
            sg = jax.nn.sigmoid(gate)
            d = [da * val * (sg * (1.0 + gate * (1.0 - sg))), da * (gate * sg)]
            for p in range(2):
                dhc[p, r0:r0 + R, :] = d[p]
                db_acc[p] = db_acc[p] + jnp.sum(d[p], axis=0, keepdims=True)
                for k in range(3):
                    dw_acc[p][k] = dw_acc[p][k] + jnp.sum(d[p] * taps[p][2 - k], axis=0, keepdims=True)
        for p in range(2):
            db_ref[p] = db_acc[p]
            dw_ref[p] = jnp.concatenate(dw_acc[p], axis=0)
            w = w_ref[p]
            for r0 in range(0, S, R):
                de = dhc[p, r0:r0 + R + HALO, :]
                d1 = pltpu.roll(de, R + HALO - 1, 0)[:R]
                d2 = pltpu.roll(de, R + HALO - 2, 0)[:R]
                dh_ref[p, r0:r0 + R, :] = (de[:R] * w[2:3] + d1 * w[1:2] + d2 * w[0:1]).astype(dh_ref.dtype)

    blk = lambda rows: pl.BlockSpec((2, rows, tc), lambda j: (0, 0, j))
    return pl.pallas_call(
        body, name=name, grid=(F // tc,),
        in_specs=[blk(S), pl.BlockSpec((S, tc), lambda j: (0, j)), blk(3), blk(1)],
        out_specs=[blk(S), blk(3), blk(1)],
        out_shape=[_sds((2, S, F), BF16), _sds((2, 3, F), F32), _sds((2, 1, F), F32)],
        scratch_shapes=[pltpu.VMEM((2, S + HALO, tc), F32)],
        compiler_params=_cp("parallel"),
    )(hup, dact, cw, cb)


def _headnorm_fwd(x3, part, gain, name):
    _, S, W = x3.shape
    tr = _t(S, 512)

    def body(x_ref, g_ref, o_ref):
        xv = x_ref[...].astype(F32)
        for h in range(W // HEAD):
            sl = slice(h * HEAD, (h + 1) * HEAD)
            xh = xv[:, sl]
            r = lax.rsqrt(jnp.mean(xh * xh, axis=-1, keepdims=True) + EPS)
            o_ref[:, sl] = ((xh * r) * g_ref[...]).astype(o_ref.dtype)

    return pl.pallas_call(
        body, name=name, grid=(S // tr,),
        in_specs=[pl.BlockSpec((None, tr, W), lambda i: (part, i, 0)), pl.BlockSpec((1, HEAD), lambda i: (0, 0))],
        out_specs=pl.BlockSpec((tr, W), lambda i: (i, 0)), out_shape=_sds((S, W), BF16),
        compiler_params=_cp("parallel"),
    )(x3, gain)


def _headnorm_bwd(dys, x3, gain, passes, name):
    _, S, W = x3.shape
    tr = _t(S, 256)
    nd, npass = len(dys), len(passes)

    def body(*refs):
        dy_refs = refs[:nd]
        x_ref, g_ref = refs[nd], refs[nd + 1]
        p_refs = refs[nd + 2:nd + 2 + npass]
        o_ref, dg_ref = refs[nd + 2 + npass], refs[nd + 3 + npass]

        @pl.when(pl.program_id(0) == 0)
        def _():
            dg_ref[...] = jnp.zeros_like(dg_ref)

        xv = x_ref[...].astype(F32)
        dyv = dy_refs[0][...].astype(F32)
        for r in dy_refs[1:]:
            dyv = dyv + r[...].astype(F32)
        dg = jnp.zeros((1, HEAD), F32)
        for h in range(W // HEAD):
            sl = slice(h * HEAD, (h + 1) * HEAD)
            xh = xv[:, sl]
            r = lax.rsqrt(jnp.mean(xh * xh, axis=-1, keepdims=True) + EPS)
            xh = xh * r
            dyh = dyv[:, sl]
            dg = dg + jnp.sum(dyh * xh, axis=0, keepdims=True)
            dxh = dyh * g_ref[...]
            o_ref[0, :, sl] = (r * (dxh - xh * jnp.mean(dxh * xh, axis=-1, keepdims=True))).astype(o_ref.dtype)
        dg_ref[...] += dg
        pv = p_refs[0][...].astype(F32)
        for r in p_refs[1:]:
            pv = pv + r[...].astype(F32)
        o_ref[1] = pv.astype(o_ref.dtype)

    row = pl.BlockSpec((tr, W), lambda i: (i, 0))
    vec = pl.BlockSpec((1, HEAD), lambda i: (0, 0))
    return pl.pallas_call(
        body, name=name, grid=(S // tr,),
        in_specs=[row] * nd + [pl.BlockSpec((None, tr, W), lambda i: (0, i, 0)), vec] + [row] * npass,
        out_specs=[pl.BlockSpec((2, tr, W), lambda i: (0, i, 0)), vec],
        out_shape=[_sds((2, S, W), BF16), _sds((1, HEAD), F32)], compiler_params=_cp("arbitrary"),
    )(*dys, x3, gain, *passes)


def _gate_fwd(o, qg, name):
    S, W = o.shape
    tr = _t(S, 512)

    def body(o_ref, g_ref, y_ref):
        y_ref[...] = (o_ref[...].astype(F32) * jax.nn.sigmoid(g_ref[...].astype(F32))).astype(y_ref.dtype)

    row = pl.BlockSpec((tr, W), lambda i: (i, 0))
    return pl.pallas_call(
        body, name=name, grid=(S // tr,), in_specs=[row, pl.BlockSpec((None, tr, W), lambda i: (1, i, 0))],
        out_specs=row, out_shape=_sds((S, W), BF16), compiler_params=_cp("parallel"),
    )(o, qg)


def _gate_bwd(dog, o, qg, name):
    S, W = o.shape
    tr = _t(S, 512)

    def body(dy_ref, o_ref, g_ref, do_ref, dg_ref):
        sg = jax.nn.sigmoid(g_ref[...].astype(F32))
        dy = dy_ref[...].astype(F32)
        do_ref[...] = (dy * sg).astype(do_ref.dtype)
        dg_ref[...] = (dy * o_ref[...].astype(F32) * (sg * (1.0 - sg))).astype(dg_ref.dtype)

    row = pl.BlockSpec((tr, W), lambda i: (i, 0))
    return pl.pallas_call(
        body, name=name, grid=(S // tr,), in_specs=[row, row, pl.BlockSpec((None, tr, W), lambda i: (1, i, 0))],
        out_specs=[row, row], out_shape=[_sds((S, W), BF16), _sds((S, W), BF16)], compiler_params=_cp("parallel"),
    )(dog, o, qg)


def _logf_cumsum(fpre, bf, name):
    S, C = fpre.shape
    n = S // CHUNK

    def body(f_ref, b_ref, c_ref, carry):
        @pl.when(pl.program_id(0) == 0)
        def _():
            carry[...] = jnp.zeros_like(carry)

        lf = jax.nn.log_sigmoid(f_ref[...] + b_ref[...])
        tri = (lax.broadcasted_iota(jnp.int32, (CHUNK, CHUNK), 0)
               >= lax.broadcasted_iota(jnp.int32, (CHUNK, CHUNK), 1)).astype(F32)
        c_ref[...] = jnp.dot(tri, lf, preferred_element_type=F32, precision=lax.Precision.HIGHEST) + carry[...]
        carry[...] += jnp.sum(lf, axis=0, keepdims=True)

    return pl.pallas_call(
        body, name=name, grid=(n,),
        in_specs=[pl.BlockSpec((CHUNK, C), lambda i: (i, 0)), pl.BlockSpec((1, C), lambda i: (0, 0))],
        out_specs=pl.BlockSpec((CHUNK, C), lambda i: (i, 0)), out_shape=_sds((S, C), F32),
        scratch_shapes=[pltpu.VMEM((1, C), F32)], compiler_params=_cp("arbitrary"),
    )(fpre, bf)


def _logf_cumsum_bwd(dc, fpre, bf, name):
    S, C = fpre.shape
    n = S // CHUNK

    def body(dc_ref, f_ref, b_ref, df_ref, db_ref, carry):
        @pl.when(pl.program_id(0) == 0)
        def _():
            carry[...] = jnp.zeros_like(carry)
            db_ref[...] = jnp.zeros_like(db_ref)

        dcv = dc_ref[...]
        tri = (lax.broadcasted_iota(jnp.int32, (CHUNK, CHUNK), 0)
               <= lax.broadcasted_iota(jnp.int32, (CHUNK, CHUNK), 1)).astype(F32)
        dlf = jnp.dot(tri, dcv, preferred_element_type=F32, precision=lax.Precision.HIGHEST) + carry[...]
        carry[...] += jnp.sum(dcv, axis=0, keepdims=True)
        df = dlf * jax.nn.sigmoid(-(f_ref[...] + b_ref[...]))
        df_ref[...] = df.astype(df_ref.dtype)
        db_ref[...] += jnp.sum(df, axis=0, keepdims=True)

    rev = pl.BlockSpec((CHUNK, C), lambda i: (n - 1 - i, 0))
    vec = pl.BlockSpec((1, C), lambda i: (0, 0))
    return pl.pallas_call(
        body, name=name, grid=(n,), in_specs=[rev, rev, vec], out_specs=[rev, vec],
        out_shape=[_sds((S, C), BF16), _sds((1, C), F32)],
        scratch_shapes=[pltpu.VMEM((1, C), F32)], compiler_params=_cp("arbitrary"),
    )(dc, fpre, bf)


def _attn_tiles(S):
    return _t(S, 1024), _t(S, 512)


def _scores(q, k, cq, ck, qi, kj, tq, tk):
    s = lax.dot_general(q, k, (_DIMS["nt"], ((), ())), preferred_element_type=F32) * (HEAD ** -0.5)
    s = s + cq - ck
    row = qi * tq + lax.broadcasted_iota(jnp.int32, (tq, tk), 0)
    col = kj * tk + lax.broadcasted_iota(jnp.int32, (tq, tk), 1)
    return jnp.where(row >= col, s, NEG)


def _attn_fwd(qn, kn, kv, ccol, crow, name):
    S, W = qn.shape
    H = W // HEAD
    tq, tk = _attn_tiles(S)
    nq, nk = S // tq, S // tk
    last = lambda i: ((i + 1) * tq - 1) // tk

    def body(q_ref, k_ref, v_ref, cq_ref, ck_ref, o_ref, lse_ref, m_sc, l_sc, acc_sc):
        qi, kj = pl.program_id(1), pl.program_id(2)

        @pl.when(kj == 0)
        def _():
            m_sc[...] = jnp.full_like(m_sc, NEG)
            l_sc[...] = jnp.zeros_like(l_sc)
            acc_sc[...] = jnp.zeros_like(acc_sc)

        @pl.when(kj <= last(qi))
        def _():
            s = _scores(q_ref[...], k_ref[...], cq_ref[...], ck_ref[...], qi, kj, tq, tk)
            m_new = jnp.maximum(m_sc[...], jnp.max(s, axis=-1, keepdims=True))
            alpha = jnp.exp(m_sc[...] - m_new)
            p = jnp.exp(s - m_new)
            l_sc[...] = alpha * l_sc[...] + jnp.sum(p, axis=-1, keepdims=True)
            acc_sc[...] = alpha * acc_sc[...] + jnp.dot(p.astype(BF16), v_ref[...], preferred_element_type=F32)
            m_sc[...] = m_new

        @pl.when(kj == nk - 1)
        def _():
            o_ref[...] = (acc_sc[...] / l_sc[...]).astype(o_ref.dtype)
            lse_ref[...] = m_sc[...] + jnp.log(l_sc[...])

    kmap = lambda h, i, j: (jnp.minimum(j, last(i)), h)
    return pl.pallas_call(
        body, name=name, grid=(H, nq, nk),
        in_specs=[pl.BlockSpec((tq, HEAD), lambda h, i, j: (i, h)),
                  pl.BlockSpec((tk, HEAD), kmap),
                  pl.BlockSpec((None, tk, HEAD), lambda h, i, j: (1, jnp.minimum(j, last(i)), h)),
                  pl.BlockSpec((None, tq, 1), lambda h, i, j: (h, i, 0)),
                  pl.BlockSpec((None, 1, tk), lambda h, i, j: (h, 0, jnp.minimum(j, last(i))))],
        out_specs=[pl.BlockSpec((tq, HEAD), lambda h, i, j: (i, h)),
                   pl.BlockSpec((None, tq, 1), lambda h, i, j: (h, i, 0))],
        out_shape=[_sds((S, W), BF16), _sds((H, S, 1), F32)],
        scratch_shapes=[pltpu.VMEM((tq, 1), F32), pltpu.VMEM((tq, 1), F32), pltpu.VMEM((tq, HEAD), F32)],
        compiler_params=_cp("parallel", "parallel", "arbitrary"),
    )(qn, kn, kv, ccol, crow)


def _attn_bwd_dq(qn, kn, kv, ccol, crow, do, o, lse, name):
    S, W = qn.shape
    H = W // HEAD
    tq, tk = _attn_tiles(S)
    nq, nk = S // tq, S // tk
    last = lambda i: ((i + 1) * tq - 1) // tk

    def body(q_ref, k_ref, v_ref, cq_ref, ck_ref, do_ref, o_ref, lse_ref, dq_ref, dc_ref, dq_sc, dc_sc, dl_sc):
        qi, kj = pl.program_id(1), pl.program_id(2)

        @pl.when(kj == 0)
        def _():
            dq_sc[...] = jnp.zeros_like(dq_sc)
            dc_sc[...] = jnp.zeros_like(dc_sc)
            dl_sc[...] = jnp.sum(do_ref[...].astype(F32) * o_ref[...].astype(F32), axis=-1, keepdims=True)

        @pl.when(kj <= last(qi))
        def _():
            s = _scores(q_ref[...], k_ref[...], cq_ref[...], ck_ref[...], qi, kj, tq, tk)
            p = jnp.exp(s - lse_ref[...])
            dp = lax.dot_general(do_ref[...], v_ref[...], (_DIMS["nt"], ((), ())), preferred_element_type=F32)
            ds = p * (dp - dl_sc[...])
            dc_sc[...] += jnp.sum(ds, axis=-1, keepdims=True)
            dq_sc[...] += jnp.dot(ds.astype(BF16), k_ref[...], preferred_element_type=F32)

        @pl.when(kj == nk - 1)
        def _():
            dq_ref[...] = dq_sc[...] * (HEAD ** -0.5)
            dc_ref[...] = dc_sc[...]

    qblk = pl.BlockSpec((tq, HEAD), lambda h, i, j: (i, h))
    qcol = pl.BlockSpec((None, tq, 1), lambda h, i, j: (h, i, 0))
    return pl.pallas_call(
        body, name=name, grid=(H, nq, nk),
        in_specs=[qblk, pl.BlockSpec((tk, HEAD), lambda h, i, j: (jnp.minimum(j, last(i)), h)),
                  pl.BlockSpec((None, tk, HEAD), lambda h, i, j: (1, jnp.minimum(j, last(i)), h)),
                  qcol, pl.BlockSpec((None, 1, tk), lambda h, i, j: (h, 0, jnp.minimum(j, last(i)))),
                  qblk, qblk, qcol],
        out_specs=[qblk, qcol], out_shape=[_sds((S, W), F32), _sds((H, S, 1), F32)],
        scratch_shapes=[pltpu.VMEM((tq, HEAD), F32), pltpu.VMEM((tq, 1), F32), pltpu.VMEM((tq, 1), F32)],
        compiler_params=_cp("parallel", "parallel", "arbitrary"),
    )(qn, kn, kv, ccol, crow, do, o, lse)


def _attn_bwd_dkv(qn, kn, kv, ccol, crow, do, o, lse, name):
    S, W = qn.shape
    H = W // HEAD
    tq, tk = _attn_tiles(S)
    nq, nk = S // tq, S // tk
    first = lambda j: (j * tk) // tq

    def body(q_ref, k_ref, v_ref, cq_ref, ck_ref, do_ref, o_ref, lse_ref, dk_ref, dv_ref, dc_ref, dk_sc, dv_sc, dc_sc):
        kj, qi = pl.program_id(1), pl.program_id(2)

        @pl.when(qi == 0)
        def _():
            dk_sc[...] = jnp.zeros_like(dk_sc)
            dv_sc[...] = jnp.zeros_like(dv_sc)
            dc_sc[...] = jnp.zeros_like(dc_sc)

        @pl.when(qi >= first(kj))
        def _():
            s = _scores(q_ref[...], k_ref[...], cq_ref[...], ck_ref[...], qi, kj, tq, tk)
            p = jnp.exp(s - lse_ref[...])
            dov = do_ref[...]
            delta = jnp.sum(dov.astype(F32) * o_ref[...].astype(F32), axis=-1, keepdims=True)
            dp = lax.dot_general(dov, v_ref[...], (_DIMS["nt"], ((), ())), preferred_element_type=F32)
            ds = p * (dp - delta)
            dc_sc[...] -= jnp.sum(ds, axis=0, keepdims=True)
            dv_sc[...] += lax.dot_general(p.astype(BF16), dov, (_DIMS["tn"], ((), ())), preferred_element_type=F32)
            dk_sc[...] += lax.dot_general(ds.astype(BF16), q_ref[...], (_DIMS["tn"], ((), ())), preferred_element_type=F32)

        @pl.when(qi == nq - 1)
        def _():
            dk_ref[...] = dk_sc[...] * (HEAD ** -0.5)
            dv_ref[...] = dv_sc[...]
            dc_ref[...] = dc_sc[...]

    qmap = lambda h, j, i: (jnp.maximum(i, first(j)), h)
    qblk = pl.BlockSpec((tq, HEAD), qmap)
    qcol = pl.BlockSpec((None, tq, 1), lambda h, j, i: (h, jnp.maximum(i, first(j)), 0))
    kblk = pl.BlockSpec((tk, HEAD), lambda h, j, i: (j, h))
    krow = pl.BlockSpec((None, 1, tk), lambda h, j, i: (h, 0, j))
    return pl.pallas_call(
        body, name=name, grid=(H, nk, nq),
        in_specs=[qblk, kblk, pl.BlockSpec((None, tk, HEAD), lambda h, j, i: (1, j, h)), qcol, krow, qblk, qblk, qcol],
        out_specs=[kblk, kblk, krow],
        out_shape=[_sds((S, W), F32), _sds((S, W), F32), _sds((H, 1, S), F32)],
        scratch_shapes=[pltpu.VMEM((tk, HEAD), F32), pltpu.VMEM((tk, HEAD), F32), pltpu.VMEM((1, tk), F32)],
        compiler_params=_cp("parallel", "parallel", "arbitrary"),
    )(qn, kn, kv, ccol, crow, do, o, lse)


def _adamw(w, g, m, v, name):
    R, C = w.shape
    tr = _t(R, max(8, (1 << 19) // max(C, 1)), 8)
    c1 = 1.0 - ADAM_B1 ** ADAM_STEP
    c2 = 1.0 - ADAM_B2 ** ADAM_STEP

    def body(w_ref, g_ref, m_ref, v_ref, d_ref, nm_ref, nv_ref):
        gv = g_ref[...]
        nm = ADAM_B1 * m_ref[...] + (1.0 - ADAM_B1) * gv
        nv = ADAM_B2 * v_ref[...] + (1.0 - ADAM_B2) * (gv * gv)
        nm_ref[...] = nm
        nv_ref[...] = nv
        d_ref[...] = -ADAM_LR * ((nm / c1) / (jnp.sqrt(nv / c2) + ADAM_EPS) + ADAM_WD * w_ref[...])

    blk = pl.BlockSpec((tr, C), lambda i: (i, 0))
    return pl.pallas_call(
        body, name=name, grid=(R // tr,), in_specs=[blk] * 4, out_specs=[blk] * 3,
        out_shape=[_sds((R, C), F32)] * 3, compiler_params=_cp("parallel"),
    )(w, g, m, v)


def _place():
    x, y, c = lax.axis_index("x"), lax.axis_index("y"), lax.axis_index("c")
    chips = [(1 - x, y), (x, 1 - y), (1 - x, 1 - y)]
    return x, y, c, chips


def _gather_chips(parts, name):
    n = len(parts)

    def body(*refs):
        ins, outs = refs[:n], refs[n:2 * n]
        send_sems, recv_sems, local_sems = refs[2 * n:]
        x, y, c, chips = _place()
        me = 2 * x + y
        sibling = (x, y, 1 - c)

        def rows(i, chip_index, core):
            h = ins[i].shape[1] // 2
            return outs[i].at[chip_index, :, pl.ds(core * h, h), :]

        def copy(i, k, chip_index, core, to, src=None):
            dst = rows(i, chip_index, core)
            return pltpu.make_async_remote_copy(
                src_ref=dst if src is None else src, dst_ref=dst, send_sem=send_sems.at[i, k],
                recv_sem=recv_sems.at[i, k], device_id=to, device_id_type=MESH)

        local, sent = [], []
        for i in range(n):
            h = ins[i].shape[1] // 2
            own = pltpu.make_async_copy(ins[i], outs[i].at[me], local_sems.at[i])
            own.start()
            local.append(own)
            for j, chip in enumerate(chips):
                cp = copy(i, j, me, c, (*chip, c), src=ins[i].at[:, pl.ds(c * h, h), :])
                cp.start()
                sent.append(cp)
        for i in range(n):
            for j, chip in enumerate(chips):
                idx = 2 * chip[0] + chip[1]
                copy(i, j, idx, c, (x, y, c)).wait_recv()
                fw = copy(i, 3 + j, idx, c, sibling)
                fw.start()
                sent.append(fw)
        for i in range(n):
            for j, chip in enumerate(chips):
                copy(i, 3 + j, 2 * chip[0] + chip[1], 1 - c, (x, y, c)).wait_recv()
        for cp in sent:
            cp.wait_send()
        for cp in local:
            cp.wait()

    return pl.pallas_call(
        body, name=name, in_specs=[ANY] * n, out_specs=[ANY] * n,
        out_shape=[_sds((N_CHIPS,) + p.shape, p.dtype) for p in parts],
        scratch_shapes=[pltpu.SemaphoreType.DMA((n, 6)), pltpu.SemaphoreType.DMA((n, 6)), pltpu.SemaphoreType.DMA((n,))],
        compiler_params=pltpu.CompilerParams(has_side_effects=True),
    )(*parts)


def _swap_halves(gs, name):
    n = len(gs)

    def body(*refs):
        ins, outs = refs[:n], refs[n:2 * n]
        send_sems, recv_sems = refs[2 * n:]
        x, y, c, _ = _place()
        cps = []
        for i in range(n):
            h = ins[i].shape[2] // 2
            cp = pltpu.make_async_remote_copy(
                src_ref=ins[i].at[:, :, pl.ds((1 - c) * h, h), :], dst_ref=outs[i], send_sem=send_sems.at[i],
                recv_sem=recv_sems.at[i], device_id=(x, y, 1 - c), device_id_type=MESH)
            cp.start()
            cps.append(cp)
        for cp in cps:
            cp.wait()

    return pl.pallas_call(
        body, name=name, in_specs=[ANY] * n, out_specs=[ANY] * n,
        out_shape=[_sds(g.shape[:2] + (g.shape[2] // 2, g.shape[3]), g.dtype) for g in gs],
        scratch_shapes=[pltpu.SemaphoreType.DMA((n,)), pltpu.SemaphoreType.DMA((n,))],
        compiler_params=pltpu.CompilerParams(has_side_effects=True),
    )(*gs)


def _pair_sum(g, gs, core, out_dtype, name):
    ns, L, R, C = g.shape
    h = R // 2
    th = _t(h, max(16, (1 << 20) // C), 16)
    nb = h // th

    def body(core_ref, a_ref, b_ref, o_ref):
        o_ref[...] = (a_ref[...].astype(F32) + b_ref[...].astype(F32)).astype(o_ref.dtype)

    blk = (None, None, th, C)
    return pl.pallas_call(
        body, name=name,
        grid_spec=pltpu.PrefetchScalarGridSpec(
            num_scalar_prefetch=1, grid=(ns, L, nb),
            in_specs=[pl.BlockSpec(blk, lambda o, l, r, cr: (o, l, cr[0] * nb + r, 0)),
                      pl.BlockSpec(blk, lambda o, l, r, cr: (o, l, r, 0))],
            out_specs=pl.BlockSpec(blk, lambda o, l, r, cr: (o, l, r, 0))),
        out_shape=_sds((ns, L, h, C), out_dtype), compiler_params=_cp("parallel", "parallel", "parallel"),
    )(core, g, gs)


def _scatter_chips(ps, name):
    n = len(ps)

    def body(*refs):
        ins, outs = refs[:n], refs[n:2 * n]
        send_sems, recv_sems, local_sems = refs[2 * n:]
        x, y, c, chips = _place()
        me = 2 * x + y
        local, sent = [], []
        for i in range(n):
            own = pltpu.make_async_copy(ins[i].at[me], outs[i].at[me], local_sems.at[i])
            own.start()
            local.append(own)
            for j, chip in enumerate(chips):
                cp = pltpu.make_async_remote_copy(
                    src_ref=ins[i].at[2 * chip[0] + chip[1]], dst_ref=outs[i].at[me], send_sem=send_sems.at[i, j],
                    recv_sem=recv_sems.at[i, j], device_id=(*chip, c), device_id_type=MESH)
                cp.start()
                sent.append(cp)
        for i in range(n):
            for j, chip in enumerate(chips):
                slot = outs[i].at[2 * chip[0] + chip[1]]
                pltpu.make_async_remote_copy(
                    src_ref=slot, dst_ref=slot, send_sem=send_sems.at[i, j], recv_sem=recv_sems.at[i, j],
                    device_id=(x, y, c), device_id_type=MESH).wait_recv()
        for cp in sent:
            cp.wait_send()
        for cp in local:
            cp.wait()

    return pl.pallas_call(
        body, name=name, in_specs=[ANY] * n, out_specs=[ANY] * n,
        out_shape=[_sds(p.shape, p.dtype) for p in ps],
        scratch_shapes=[pltpu.SemaphoreType.DMA((n, 3)), pltpu.SemaphoreType.DMA((n, 3)), pltpu.SemaphoreType.DMA((n,))],
        compiler_params=pltpu.CompilerParams(has_side_effects=True),
    )(*ps)


def _sum_chips(q, name):
    ns, L, h, C = q.shape
    th = _t(h, max(16, (1 << 19) // C), 16)
    blk = (None, None, th, C)

    def body(q0, q1, q2, q3, o_ref):
        o_ref[...] = ((q0[...].astype(F32) + q1[...].astype(F32)) + q2[...].astype(F32)) + q3[...].astype(F32)

    return pl.pallas_call(
        body, name=name, grid=(L, h // th),
        in_specs=[pl.BlockSpec(blk, functools.partial(lambda k, l, r: (k, l, r, 0), k)) for k in range(N_CHIPS)],
        out_specs=pl.BlockSpec((None, th, C), lambda l, r: (l, r, 0)),
        out_shape=_sds((L, h, C), F32), compiler_params=_cp("parallel", "parallel"),
    )(q, q, q, q)


def _join_halves(fs, name):
    n = len(fs)

    def body(*refs):
        ins, outs = refs[:n], refs[n:2 * n]
        send_sems, recv_sems, local_sems = refs[2 * n:]
        x, y, c, _ = _place()
        cps = []
        for i in range(n):
            h = ins[i].shape[1]
            mine = outs[i].at[:, pl.ds(c * h, h), :]
            own = pltpu.make_async_copy(ins[i], mine, local_sems.at[i])
            own.start()
            cp = pltpu.make_async_remote_copy(
                src_ref=ins[i], dst_ref=mine, send_sem=send_sems.at[i], recv_sem=recv_sems.at[i],
                device_id=(x, y, 1 - c), device_id_type=MESH)
            cp.start()
            cps.append((own, cp))
        for own, cp in cps:
            cp.wait()
            own.wait()

    return pl.pallas_call(
        body, name=name, in_specs=[ANY] * n, out_specs=[ANY] * n,
        out_shape=[_sds((f.shape[0], 2 * f.shape[1], f.shape[2]), f.dtype) for f in fs],
        scratch_shapes=[pltpu.SemaphoreType.DMA((n,)), pltpu.SemaphoreType.DMA((n,)), pltpu.SemaphoreType.DMA((n,))],
        compiler_params=pltpu.CompilerParams(has_side_effects=True),
    )(*fs)


def _pack(arrs, rows_mult):
    flat = jnp.concatenate([a.reshape(-1).astype(F32) for a in arrs])
    rows = -(-flat.size // LANE)
    rows = -(-rows // rows_mult) * rows_mult
    return jnp.pad(flat, (0, rows * LANE - flat.size)).reshape(rows, LANE)


def _unpack(packed, like):
    flat = packed.reshape(-1)
    out, pos = [], 0
    for a in like:
        n = math.prod(a.shape)
        out.append(flat[pos:pos + n].reshape(a.shape))
        pos += n
    return out


def _adamw_nd(w, g, m, v, name):
    shape = w.shape
    C = shape[-1]
    d, nm, nv = _adamw(w.reshape(-1, C), g.reshape(-1, C), m.reshape(-1, C), v.reshape(-1, C), name)
    return d.reshape(shape), nm.reshape(shape), nv.reshape(shape)


def kernel(x, a_norm, a_w_in, a_v_norm, a_w_s, a_b_s, a_w_out, kv_norm, w_kvf, b_f, k_norm, b_norm, b_w_qg, q_norm, b_w_out, f_norm, f_w_up, f_conv_w, f_conv_b, f_w_down, final_norm, loss_target, m_a_norm, m_a_w_in, m_a_v_norm, m_a_w_s, m_a_b_s, m_a_w_out, m_kv_norm, m_w_kvf, m_b_f, m_k_norm, m_b_norm, m_b_w_qg, m_q_norm, m_b_w_out, m_f_norm, m_f_w_up, m_f_conv_w, m_f_conv_b, m_f_w_down, m_final_norm, v_a_norm, v_a_w_in, v_a_v_norm, v_a_w_s, v_a_b_s, v_a_w_out, v_kv_norm, v_w_kvf, v_b_f, v_k_norm, v_b_norm, v_b_w_qg, v_q_norm, v_b_w_out, v_f_norm, v_f_w_up, v_f_conv_w, v_f_conv_b, v_f_w_down, v_final_norm):
    weights = dict(a_norm=a_norm, a_w_in=a_w_in, a_v_norm=a_v_norm, a_w_s=a_w_s, a_b_s=a_b_s, a_w_out=a_w_out, kv_norm=kv_norm, w_kvf=w_kvf, b_f=b_f, k_norm=k_norm, b_norm=b_norm, b_w_qg=b_w_qg, q_norm=q_norm, b_w_out=b_w_out, f_norm=f_norm, f_w_up=f_w_up, f_conv_w=f_conv_w, f_conv_b=f_conv_b, f_w_down=f_w_down, final_norm=final_norm)
    mom1 = dict(a_norm=m_a_norm, a_w_in=m_a_w_in, a_v_norm=m_a_v_norm, a_w_s=m_a_w_s, a_b_s=m_a_b_s, a_w_out=m_a_w_out, kv_norm=m_kv_norm, w_kvf=m_w_kvf, b_f=m_b_f, k_norm=m_k_norm, b_norm=m_b_norm, b_w_qg=m_b_w_qg, q_norm=m_q_norm, b_w_out=m_b_w_out, f_norm=m_f_norm, f_w_up=m_f_w_up, f_conv_w=m_f_conv_w, f_conv_b=m_f_conv_b, f_w_down=m_f_w_down, final_norm=m_final_norm)
    mom2 = dict(a_norm=v_a_norm, a_w_in=v_a_w_in, a_v_norm=v_a_v_norm, a_w_s=v_a_w_s, a_b_s=v_a_b_s, a_w_out=v_a_w_out, kv_norm=v_kv_norm, w_kvf=v_w_kvf, b_f=v_b_f, k_norm=v_k_norm, b_norm=v_b_norm, b_w_qg=v_b_w_qg, q_norm=v_q_norm, b_w_out=v_b_w_out, f_norm=v_f_norm, f_w_up=v_f_w_up, f_conv_w=v_f_conv_w, f_conv_b=v_f_conv_b, f_w_down=v_f_w_down, final_norm=v_final_norm)
    names = list(weights)
    big = ["a_w_in", "a_w_out", "w_kvf", "b_w_qg", "b_w_out", "f_w_up", "f_w_down"]
    small_sharded = ["a_norm", "a_v_norm", "f_conv_w"]
    small_repl = ["a_w_s", "a_b_s", "kv_norm", "b_f", "k_norm", "b_norm", "q_norm", "f_norm", "f_conv_b", "final_norm"]

    _, S, D = x.shape
    NA, NB, DEPTH = a_norm.shape[0], b_norm.shape[0], f_norm.shape[0]
    W = a_w_out.shape[1] * N_CHIPS
    G = a_w_s.shape[1]
    H = b_f.shape[0]
    ATT = H * HEAD
    F = f_w_down.shape[1] * N_CHIPS
    Ckv = w_kvf.shape[1]
    Cp = -(-Ckv // LANE) * LANE
    assert W == G * CHUNK and Ckv * N_CHIPS == 2 * ATT + H and S % CHUNK == 0
    core = lax.axis_index("c").astype(jnp.int32).reshape(1)

    small_local = [weights[k] for k in small_sharded]
    parts = [a_w_in.astype(BF16), a_w_out.astype(BF16),
             jnp.pad(w_kvf, ((0, 0), (0, Cp - Ckv))).astype(BF16)[None],
             b_w_qg.astype(BF16), b_w_out.astype(BF16), f_w_up.astype(BF16), f_w_down.astype(BF16),
             _pack(small_local, 32)[None]]
    g_ain, g_aout, g_kvf, g_bqg, g_bout, g_up, g_down, g_small = _gather_chips(parts, "gather_weights")

    per_chip = [_unpack(g_small[j, 0], small_local) for j in range(N_CHIPS)]
    a_norm_f, a_vnorm_f, conv_w_f = [jnp.concatenate([per_chip[j][k] for j in range(N_CHIPS)], axis=-1) for k in range(3)]
    cw = conv_w_f.reshape(DEPTH, 3, 2, F).transpose(0, 2, 1, 3)
    cb = f_conv_b.reshape(DEPTH, 2, 1, F)
    tril = jnp.tril(jnp.ones((CHUNK, CHUNK), dtype=bool))
    wm = jnp.where(tril, a_w_s, 0.0).astype(BF16)
    wmt = jnp.swapaxes(wm, -1, -2)
    bcol = a_b_s[..., None]
    kvf_full = g_kvf[:, 0, :, :Ckv].transpose(1, 0, 2).reshape(D, N_CHIPS * Ckv)
    w_kv = jnp.stack([kvf_full[:, :ATT], kvf_full[:, ATT:2 * ATT]])
    w_f = jnp.pad(kvf_full[:, 2 * ATT:], ((0, 0), (0, LANE - H)))
    bf_pad = jnp.pad(b_f, (0, LANE - H))[None]
    row = lambda v: v.reshape(1, -1)

    h = x[0]
    target = loss_target[0]
    saved = [dict() for _ in range(DEPTH)]
    kvs = {}
    for l in range(DEPTH):
        sv = saved[l]
        sv["h_m"] = h
        if l < NA:
            xn = _rms_fwd(h, row(a_norm_f[l]), f"a{l}_norm")
            zpre = _mm(m2(xn), mgc(g_ain, l), "nn", m2(_sds((S, 2 * W), BF16)), tm=1024, tn=512, name=f"a{l}_in")
            gated = _sgu_fwd(zpre, wm[l], bcol[l], row(a_vnorm_f[l]), f"a{l}_sgu")
            h = _mm(m2(gated), mgr(g_aout, l), "nn", m2(_sds((S, D), F32)), res=m2(h), tm=1024, tn=512, name=f"a{l}_out")
            sv.update(xn_m=xn, zpre=zpre, gated=gated)
        else:
            j = l - NA
            xn = _rms_fwd(h, row(b_norm[j]), f"b{j}_norm")
            qg = _mm(m2(xn), mgc(g_bqg, j), "nn", mcs(_sds((2, S, ATT), BF16)), tm=1024, tn=512, name=f"b{j}_qg")
            qn = _headnorm_fwd(qg, 0, row(q_norm[j]), f"b{j}_qnorm")
            o, lse = _attn_fwd(qn, kvs["kn"], kvs["kv"], kvs["ccol"], kvs["crow"], f"b{j}_attn")
            og = _gate_fwd(o, qg, f"b{j}_gate")
            h = _mm(m2(og), mgr(g_bout, j), "nn", m2(_sds((S, D), F32)), res=m2(h), tm=1024, tn=512, name=f"b{j}_out")
            sv.update(xn_m=xn, qg=qg, qn=qn, o=o, lse=lse, og=og)
        sv["h_f"] = h
        xn = _rms_fwd(h, row(f_norm[l]), f"f{l}_norm")
        hup = _mm(m2(xn), mgc(g_up, l), "nn", mcs(_sds((2, S, F), BF16)), tm=1024, tn=1408, name=f"f{l}_up")
        act = _convgate_fwd(hup, cw[l], cb[l], f"f{l}_conv")
        h = _mm(m2(act), mgr(g_down, l), "nn", m2(_sds((S, D), F32)), res=m2(h), tm=1024, tn=512, tk=1408, name=f"f{l}_down")
        sv.update(xn_f=xn, hup=hup, act=act)
        if l == NA - 1:
            xn_kv = _rms_fwd(h, row(kv_norm), "kv_norm")
            kv = _mm(m2(xn_kv), mcs(w_kv), "nn", mcs(_sds((2, S, ATT), BF16)), tm=1024, tn=512, name="kv_proj")
            fpre = _mm(m2(xn_kv), m2(w_f), "nn", m2(_sds((S, LANE), F32)), tm=1024, name="kv_fproj")
            kn = _headnorm_fwd(kv, 0, row(k_norm), "kv_knorm")
            cums = _logf_cumsum(fpre, bf_pad, "kv_cumsum")
            cT = cums[:, :H].T
            kvs.update(h=h, xn=xn_kv, kv=kv, fpre=fpre, kn=kn, ccol=cT[:, :, None], crow=cT[:, None, :])

    loss11, dh, d_final = _final_loss(h, row(final_norm), target, "final_loss")
    loss = lax.psum(loss11[0, 0], ("x", "y", "c"))

    G_ain = lax.empty(g_ain.shape, BF16)
    G_aout = lax.empty(g_aout.shape, BF16)
    G_bqg = lax.empty(g_bqg.shape, BF16)
    G_bout = lax.empty(g_bout.shape, BF16)
    G_up = lax.empty(g_up.shape, BF16)
    G_down = lax.empty(g_down.shape, BF16)
    d_anorm, d_avnorm, d_ws, d_bs = [None] * NA, [None] * NA, [None] * NA, [None] * NA
    d_bnorm, d_qnorm = [None] * NB, [None] * NB
    d_fnorm, d_cw, d_cb = [None] * DEPTH, [None] * DEPTH, [None] * DEPTH
    dkn, dvv, dcq, dck = [], [], [], []
    G_kvf = d_kvnorm = d_bf = d_knorm = None
    for l in reversed(range(DEPTH)):
        sv = saved[l]
        if l == NA - 1:
            dkv, d_knorm = _headnorm_bwd(dkn, kvs["kv"], row(k_norm), dvv, "kv_knorm_bwd")
            dc = sum(a[:, :, 0].T for a in dcq) + sum(a[:, 0, :].T for a in dck)
            dc = jnp.pad(dc, ((0, 0), (0, LANE - H)))
            df, d_bf = _logf_cumsum_bwd(dc, kvs["fpre"], bf_pad, "kv_cumsum_bwd")
            dxn = _mm(mcs(dkv), mcs(w_kv), "nt", m2(_sds((S, D), F32)), tm=1024, tn=512, tk=2048, name="kv_proj_dx")
            dxn = _mm(m2(df), m2(w_f), "nt", m2(_sds((S, D), BF16)), res=m2(dxn), tm=1024, tn=512, name="kv_fproj_dx")
            dw_kv = _mm(m2(kvs["xn"]), mcs(dkv), "tn", mcs(_sds((2, D, ATT), BF16)), tm=512, tn=1024, tk=2048, name="kv_proj_dw")
            dw_f = _mm(m2(kvs["xn"]), m2(df), "tn", m2(_sds((D, LANE), BF16)), tm=512, tk=2048, name="kv_fproj_dw")
            dh, d_kvnorm = _rms_bwd(dxn, kvs["h"], row(kv_norm), dh, "kv_norm_bwd")
            dfull = jnp.concatenate([dw_kv[0], dw_kv[1], dw_f[:, :H]], axis=1)
            G_kvf = jnp.pad(dfull.reshape(D, N_CHIPS, Ckv).transpose(1, 0, 2), ((0, 0), (0, 0), (0, Cp - Ckv)))[:, None]
        dact = _mm(m2(dh), mgr(g_down, l), "nt", m2(_sds((S, F), BF16)), tm=1024, tn=1408, tk=2048, name=f"f{l}_down_dx")
        G_down = _mm(m2(sv["act"]), m2(dh), "tn", mgr(G_down, l), alias=True, tm=1408, tn=512, tk=2048, name=f"f{l}_down_dw")
        dhup, d_cw[l], d_cb[l] = _convgate_bwd(sv["hup"], dact, cw[l], cb[l], f"f{l}_conv_bwd")
        dxn = _mm(mcs(dhup), mgc(g_up, l), "nt", m2(_sds((S, D), BF16)), tm=1024, tn=512, tk=1408, name=f"f{l}_up_dx")
        G_up = _mm(m2(sv["xn_f"]), mcs(dhup), "tn", mgc(G_up, l), alias=True, tm=512, tn=1408, tk=2048, name=f"f{l}_up_dw")
        dh, d_fnorm[l] = _rms_bwd(dxn, sv["h_f"], row(f_norm[l]), dh, f"f{l}_norm_bwd")
        if l >= NA:
            j = l - NA
            dog = _mm(m2(dh), mgr(g_bout, j), "nt", m2(_sds((S, ATT), BF16)), tm=1024, tn=512, tk=2048, name=f"b{j}_out_dx")
            G_bout = _mm(m2(sv["og"]), m2(dh), "tn", mgr(G_bout, j), alias=True, tm=512, tn=512, tk=2048, name=f"b{j}_out_dw")
            do, dgate = _gate_bwd(dog, sv["o"], sv["qg"], f"b{j}_gate_bwd")
            args = (sv["qn"], kvs["kn"], kvs["kv"], kvs["ccol"], kvs["crow"], do, sv["o"], sv["lse"])
            dqn, dcq_j = _attn_bwd_dq(*args, f"b{j}_attn_dq")
            dkn_j, dv_j, dck_j = _attn_bwd_dkv(*args, f"b{j}_attn_dkv")
            dkn.append(dkn_j); dvv.append(dv_j); dcq.append(dcq_j); dck.append(dck_j)
            dqg, d_qnorm[j] = _headnorm_bwd([dqn], sv["qg"], row(q_norm[j]), [dgate], f"b{j}_qnorm_bwd")
            dxn = _mm(mcs(dqg), mgc(g_bqg, j), "nt", m2(_sds((S, D), BF16)), tm=1024, tn=512, tk=1024, name=f"b{j}_qg_dx")
            G_bqg = _mm(m2(sv["xn_m"]), mcs(dqg), "tn", mgc(G_bqg, j), alias=True, tm=512, tn=1024, tk=2048, name=f"b{j}_qg_dw")
            dh, d_bnorm[j] = _rms_bwd(dxn, sv["h_m"], row(b_norm[j]), dh, f"b{j}_norm_bwd")
        else:
            dgated = _mm(m2(dh), mgr(g_aout, l), "nt", m2(_sds((S, W), BF16)), tm=1024, tn=512, tk=2048, name=f"a{l}_out_dx")
            G_aout = _mm(m2(sv["gated"]), m2(dh), "tn", mgr(G_aout, l), alias=True, tm=512, tn=512, tk=2048, name=f"a{l}_out_dw")
            dz, d_ws[l], d_bs[l], d_avnorm[l] = _sgu_bwd(sv["zpre"], dgated, wm[l], wmt[l], bcol[l], row(a_vnorm_f[l]), f"a{l}_sgu_bwd")
            dxn = _mm(m2(dz), mgc(g_ain, l), "nt", m2(_sds((S, D), BF16)), tm=1024, tn=512, tk=1024, name=f"a{l}_in_dx")
            G_ain = _mm(m2(sv["xn_m"]), m2(dz), "tn", mgc(G_ain, l), alias=True, tm=512, tn=1024, tk=2048, name=f"a{l}_in_dw")
            dh, d_anorm[l] = _rms_bwd(dxn, sv["h_m"], row(a_norm_f[l]), dh, f"a{l}_norm_bwd")
    grad_x = dh[None]

    full = dict(
        a_norm=jnp.concatenate(d_anorm, axis=0), a_v_norm=jnp.concatenate(d_avnorm, axis=0),
        f_conv_w=jnp.stack(d_cw).transpose(0, 2, 1, 3).reshape(DEPTH, 3, 2 * F),
        a_w_s=jnp.where(tril, jnp.stack(d_ws), 0.0), a_b_s=jnp.stack(d_bs)[..., 0],
        kv_norm=d_kvnorm[0], b_f=d_bf[0, :H], k_norm=d_knorm[0], b_norm=jnp.concatenate(d_bnorm, axis=0),
        q_norm=jnp.concatenate(d_qnorm, axis=0), f_norm=jnp.concatenate(d_fnorm, axis=0),
        f_conv_b=jnp.stack(d_cb).reshape(DEPTH, 2 * F), final_norm=d_final[0])
    shard_rows = []
    for j in range(N_CHIPS):
        pieces = []
        for k in small_sharded:
            n = weights[k].shape[-1]
            pieces.append(full[k][..., j * n:(j + 1) * n])
        shard_rows.append(_pack(pieces, 32))
    rs = shard_rows[0].shape[0]
    repl = _pack([full[k] for k in small_repl], N_CHIPS * 32)
    rr = repl.shape[0] // N_CHIPS
    G_small = jnp.concatenate([jnp.stack(shard_rows), repl.reshape(N_CHIPS, rr, LANE)], axis=1)[:, None]

    Gs = [G_ain, G_aout, G_kvf, G_bqg, G_bout, G_up, G_down, G_small]
    others = _swap_halves(Gs, "grads_swap")
    partial = [_pair_sum(g, o, core, g.dtype, f"grads_pair{i}") for i, (g, o) in enumerate(zip(Gs, others))]
    by_chip = _scatter_chips(partial, "grads_scatter")
    halves = [_sum_chips(q, f"grads_sum{i}") for i, q in enumerate(by_chip)]
    F_ain, F_aout, F_kvf, F_bqg, F_bout, F_up, F_down, F_small = _join_halves(halves, "grads_join")
    (repl_all,) = _gather_chips([F_small[:, rs:, :]], "gather_small_grads")

    grads = dict(a_w_in=F_ain, a_w_out=F_aout, w_kvf=F_kvf[0, :, :Ckv], b_w_qg=F_bqg, b_w_out=F_bout, f_w_up=F_up, f_w_down=F_down)
    for k, gk in zip(small_sharded, _unpack(F_small[0, :rs], small_local)):
        grads[k] = gk
    for k, gk in zip(small_repl, _unpack(repl_all.reshape(N_CHIPS * rr, LANE), [weights[k] for k in small_repl])):
        grads[k] = gk

    delta, new_m, new_v = {}, {}, {}
    for k in big:
        delta[k], new_m[k], new_v[k] = _adamw_nd(weights[k], grads[k], mom1[k], mom2[k], f"adamw_{k}")
    small = small_sharded + small_repl
    packed = [_pack([t[k] for k in small], 8) for t in (weights, grads, mom1, mom2)]
    outs = _adamw(*packed, "adamw_small")
    like = [weights[k] for k in small]
    for t, o in zip((delta, new_m, new_v), outs):
        for k, a in zip(small, _unpack(o, like)):
            t[k] = a

    return (loss, grad_x, *[grads[k] for k in names], *[delta[k] for k in names],
            *[new_m[k] for k in names], *[new_v[k] for k in names])
```

```python
import functools
import math

import jax
import jax.numpy as jnp
from jax import lax
from jax.experimental import pallas as pl
from jax.experimental.pallas import tpu as pltpu

F32, BF16 = jnp.float32, jnp.bfloat16
EPS = 1e-6
CHUNK = 128
HEAD = 128
LANE = 128
HALO = 16
N_CHIPS = 4
VMEM_LIMIT = 48 * 1024 * 1024
MESH = pl.DeviceIdType.MESH
ANY = pl.BlockSpec(memory_space=pl.ANY)

ADAM_LR, ADAM_B1, ADAM_B2, ADAM_EPS, ADAM_WD, ADAM_STEP = 0.001, 0.9, 0.999, 1e-08, 0.01, 10
NEG = -1e30
QK_SCALE = HEAD ** -0.5


def _cp(*sem):
    return pltpu.CompilerParams(dimension_semantics=sem, vmem_limit_bytes=VMEM_LIMIT)


def _t(dim, pref, mult=LANE):
    if dim <= pref:
        return dim
    t = (pref // mult) * mult
    while t >= mult:
        if dim % t == 0:
            return t
        t -= mult
    return dim


def _sds(shape, dtype):
    return jax.ShapeDtypeStruct(tuple(shape), dtype)


class Mat:
    def __init__(self, arr, shape, rlim, clim, block, index):
        self.arr, self.shape, self.rlim, self.clim = arr, shape, rlim, clim
        self._block, self._index = block, index

    def spec(self, tr, tc, gmap):
        assert self.rlim % tr == 0 and self.clim % tc == 0, (self.shape, self.rlim, self.clim, tr, tc)
        index = self._index(tr, tc)
        return pl.BlockSpec(self._block(tr, tc), lambda *g: index(*gmap(*g)))


def m2(arr):
    R, C = arr.shape
    return Mat(arr, (R, C), R, C, lambda tr, tc: (tr, tc), lambda tr, tc: (lambda i, j: (i, j)))


def mcs(arr):
    ns, R, Cs = arr.shape
    return Mat(arr, (R, ns * Cs), R, Cs, lambda tr, tc: (None, tr, tc),
               lambda tr, tc: (lambda i, j: (j // (Cs // tc), i, j % (Cs // tc))))


def mhalf(arr, p):
    ns, R, Cs = arr.shape
    return Mat(arr, (R, Cs), R, Cs, lambda tr, tc: (None, tr, tc), lambda tr, tc: (lambda i, j: (p, i, j)))


def mgc(arr, l):
    ns, L, R, Cs = arr.shape
    return Mat(arr, (R, ns * Cs), R, Cs, lambda tr, tc: (None, None, tr, tc),
               lambda tr, tc: (lambda i, j: (j // (Cs // tc), l, i, j % (Cs // tc))))


def mgr(arr, l):
    ns, L, Rs, C = arr.shape
    return Mat(arr, (ns * Rs, C), Rs, C, lambda tr, tc: (None, None, tr, tc),
               lambda tr, tc: (lambda i, j: (i // (Rs // tr), l, i % (Rs // tr), j)))


_DIMS = {"nn": ((1,), (0,)), "nt": ((1,), (1,)), "tn": ((0,), (0,))}


def _mm(a, b, mode, out, *, res=None, alias=False, tm=512, tn=512, tk=4096, name):
    if mode == "tn":
        (K, M), (K2, N) = a.shape, b.shape
        alim_m, alim_k, blim_k, blim_n = a.clim, a.rlim, b.rlim, b.clim
    elif mode == "nt":
        (M, K), (N, K2) = a.shape, b.shape
        alim_m, alim_k, blim_k, blim_n = a.rlim, a.clim, b.clim, b.rlim
    else:
        (M, K), (K2, N) = a.shape, b.shape
        alim_m, alim_k, blim_k, blim_n = a.rlim, a.clim, b.rlim, b.clim
    assert K == K2 and out.shape == (M, N), (name, a.shape, b.shape, out.shape)
    tm = _t(math.gcd(alim_m, out.rlim), tm)
    tn = _t(math.gcd(blim_n, out.clim), tn)
    tk = _t(math.gcd(alim_k, blim_k), tk)
    grid = (M // tm, N // tn, K // tk)
    nk = grid[2]
    if mode == "tn":
        a_spec = a.spec(tk, tm, lambda i, j, k: (k, i))
    else:
        a_spec = a.spec(tm, tk, lambda i, j, k: (i, k))
    if mode == "nt":
        b_spec = b.spec(tn, tk, lambda i, j, k: (j, k))
    else:
        b_spec = b.spec(tk, tn, lambda i, j, k: (k, j))
    o_spec = out.spec(tm, tn, lambda i, j, k: (i, j))
    operands, in_specs = [a.arr, b.arr], [a_spec, b_spec]
    if res is not None:
        operands.append(res.arr)
        in_specs.append(res.spec(tm, tn, lambda i, j, k: (i, j)))
    aliases = {}
    if alias:
        aliases = {len(operands): 0}
        operands.append(out.arr)
        in_specs.append(ANY)
    dims = (_DIMS[mode], ((), ()))
    has_res = res is not None

    def body(*refs):
        a_ref, b_ref = refs[0], refs[1]
        res_ref = refs[2] if has_res else None
        n_in = 2 + has_res + alias
        o_ref = refs[n_in]
        p = lax.dot_general(a_ref[...].astype(BF16), b_ref[...].astype(BF16), dims, preferred_element_type=F32)

        def finish(v):
            if has_res:
                v = v + res_ref[...].astype(F32)
            o_ref[...] = v.astype(o_ref.dtype)

        if nk == 1:
            finish(p)
        else:
            acc = refs[n_in + 1]
            k = pl.program_id(2)

            @pl.when(k == 0)
            def _():
                acc[...] = p

            @pl.when(k > 0)
            def _():
                acc[...] += p

            @pl.when(k == nk - 1)
            def _():
                finish(acc[...])

    return pl.pallas_call(
        body, name=name, grid=grid, in_specs=in_specs, out_specs=o_spec,
        out_shape=_sds(out.arr.shape, out.arr.dtype),
        scratch_shapes=[pltpu.VMEM((tm, tn), F32)] if nk > 1 else [],
        input_output_aliases=aliases,
        compiler_params=_cp("parallel", "parallel", "arbitrary"),
    )(*operands)


def _rms_fwd(x, g, name):
    S, D = x.shape
    tr = _t(S, 512)

    def body(x_ref, g_ref, o_ref):
        xv = x_ref[...]
        r = lax.rsqrt(jnp.mean(xv * xv, axis=-1, keepdims=True) + EPS)
        o_ref[...] = ((xv * r) * g_ref[...]).astype(o_ref.dtype)

    return pl.pallas_call(
        body, name=name, grid=(S // tr,),
        in_specs=[pl.BlockSpec((tr, D), lambda i: (i, 0)), pl.BlockSpec((1, D), lambda i: (0, 0))],
        out_specs=pl.BlockSpec((tr, D), lambda i: (i, 0)), out_shape=_sds((S, D), BF16),
        compiler_params=_cp("parallel"),
    )(x, g)


def _rms_bwd(dy, x, g, dres, name):
    S, D = x.shape
    tr = _t(S, 512)

    def body(dy_ref, x_ref, g_ref, dres_ref, dx_ref, dg_ref):
        xv = x_ref[...]
        dyv = dy_ref[...].astype(F32)
        r = lax.rsqrt(jnp.mean(xv * xv, axis=-1, keepdims=True) + EPS)
        xh = xv * r
        dxh = dyv * g_ref[...]
        m = jnp.mean(dxh * xh, axis=-1, keepdims=True)
        dx_ref[...] = dres_ref[...] + r * (dxh - xh * m)

        @pl.when(pl.program_id(0) == 0)
        def _():
            dg_ref[...] = jnp.zeros_like(dg_ref)

        dg_ref[...] += jnp.sum(dyv * xh, axis=0, keepdims=True)

    row = pl.BlockSpec((tr, D), lambda i: (i, 0))
    vec = pl.BlockSpec((1, D), lambda i: (0, 0))
    return pl.pallas_call(
        body, name=name, grid=(S // tr,), in_specs=[row, row, vec, row], out_specs=[row, vec],
        out_shape=[_sds((S, D), F32), _sds((1, D), F32)], compiler_params=_cp("arbitrary"),
    )(dy, x, g, dres)


def _final_loss(h, g, target, name):
    S, D = h.shape
    tr = _t(S, 512)

    def body(x_ref, g_ref, t_ref, loss_ref, dx_ref, dg_ref):
        xv = x_ref[...]
        r = lax.rsqrt(jnp.mean(xv * xv, axis=-1, keepdims=True) + EPS)
        xh = xv * r
        err = xh * g_ref[...] - t_ref[...]
        part = 0.5 * jnp.sum(jnp.mean(err * err, axis=-1, keepdims=True), axis=0, keepdims=True)
        dyv = err * (1.0 / D)
        dxh = dyv * g_ref[...]
        m = jnp.mean(dxh * xh, axis=-1, keepdims=True)
        dx_ref[...] = r * (dxh - xh * m)

        @pl.when(pl.program_id(0) == 0)
        def _():
            dg_ref[...] = jnp.zeros_like(dg_ref)
            loss_ref[...] = jnp.zeros_like(loss_ref)

        dg_ref[...] += jnp.sum(dyv * xh, axis=0, keepdims=True)
        loss_ref[...] += part

    row = pl.BlockSpec((tr, D), lambda i: (i, 0))
    vec = pl.BlockSpec((1, D), lambda i: (0, 0))
    one = pl.BlockSpec((1, 1), lambda i: (0, 0))
    return pl.pallas_call(
        body, name=name, grid=(S // tr,), in_specs=[row, vec, row], out_specs=[one, row, vec],
        out_shape=[_sds((1, 1), F32), _sds((S, D), F32), _sds((1, D), F32)], compiler_params=_cp("arbitrary"),
    )(h, g, target)


_RSQRT2 = 0.7071067811865476
_RSQRT2PI = 0.3989422804014327


def _gelu(x):
    return 0.5 * x * (1.0 + lax.erf(x * _RSQRT2))


def _gelu_grad(x):
    return 0.5 * (1.0 + lax.erf(x * _RSQRT2)) + x * (jnp.exp(-0.5 * x * x) * _RSQRT2PI)


def _sgu_fwd(zpre, wm, bcol, vnorm, name):
    S, W2 = zpre.shape
    W = W2 // 2
    G = W // CHUNK

    def body(z_ref, wm_ref, b_ref, vn_ref, o_ref):
        zp = z_ref[...].astype(F32)
        u = _gelu(zp[:, :W])
        v = _gelu(zp[:, W:])
        rv = lax.rsqrt(jnp.mean(v * v, axis=-1, keepdims=True) + EPS)
        vn = ((v * rv) * vn_ref[...]).astype(BF16)
        for g in range(G):
            sl = slice(g * CHUNK, (g + 1) * CHUNK)
            mixed = jnp.dot(wm_ref[g], vn[:, sl], preferred_element_type=F32) + b_ref[g]
            o_ref[:, sl] = (u[:, sl] * mixed).astype(o_ref.dtype)

    return pl.pallas_call(
        body, name=name, grid=(S // CHUNK,),
        in_specs=[pl.BlockSpec((CHUNK, W2), lambda i: (i, 0)),
                  pl.BlockSpec((G, CHUNK, CHUNK), lambda i: (0, 0, 0)),
                  pl.BlockSpec((G, CHUNK, 1), lambda i: (0, 0, 0)),
                  pl.BlockSpec((1, W), lambda i: (0, 0))],
        out_specs=pl.BlockSpec((CHUNK, W), lambda i: (i, 0)), out_shape=_sds((S, W), BF16),
        compiler_params=_cp("parallel"),
    )(zpre, wm, bcol, vnorm)


def _sgu_bwd(zpre, dgated, wm, wmt, bcol, vnorm, name):
    S, W2 = zpre.shape
    W = W2 // 2
    G = W // CHUNK

    def body(z_ref, dg_ref, wm_ref, wmt_ref, b_ref, vn_ref, dz_ref, dws_ref, dbs_ref, dvn_ref):
        @pl.when(pl.program_id(0) == 0)
        def _():
            dws_ref[...] = jnp.zeros_like(dws_ref)
            dbs_ref[...] = jnp.zeros_like(dbs_ref)
            dvn_ref[...] = jnp.zeros_like(dvn_ref)

        zp = z_ref[...].astype(F32)
        zu, zv = zp[:, :W], zp[:, W:]
        u = _gelu(zu)
        v = _gelu(zv)
        rv = lax.rsqrt(jnp.mean(v * v, axis=-1, keepdims=True) + EPS)
        vh = v * rv
        vn = (vh * vn_ref[...]).astype(BF16)
        dgv = dg_ref[...].astype(F32)
        du_parts, dvn_parts = [], []
        for g in range(G):
            sl = slice(g * CHUNK, (g + 1) * CHUNK)
            vg = vn[:, sl]
            mixed = jnp.dot(wm_ref[g], vg, preferred_element_type=F32) + b_ref[g]
            dgg = dgv[:, sl]
            du_parts.append(dgg * mixed)
            dmixed = dgg * u[:, sl]
            dbs_ref[g] += jnp.sum(dmixed, axis=1, keepdims=True)
            dmb = dmixed.astype(BF16)
            dws_ref[g] += lax.dot_general(dmb, vg, (_DIMS["nt"], ((), ())), preferred_element_type=F32)
            dvn_parts.append(jnp.dot(wmt_ref[g], dmb, preferred_element_type=F32))
        du = jnp.concatenate(du_parts, axis=1)
        dvn = jnp.concatenate(dvn_parts, axis=1)
        dvn_ref[...] += jnp.sum(dvn * vh, axis=0, keepdims=True)
        dvh = dvn * vn_ref[...]
        dv = rv * (dvh - vh * jnp.mean(dvh * vh, axis=-1, keepdims=True))
        dz_ref[:, :W] = (du * _gelu_grad(zu)).astype(dz_ref.dtype)
        dz_ref[:, W:] = (dv * _gelu_grad(zv)).astype(dz_ref.dtype)

    full3 = lambda shape: pl.BlockSpec(shape, lambda i: (0, 0, 0))
    return pl.pallas_call(
        body, name=name, grid=(S // CHUNK,),
        in_specs=[pl.BlockSpec((CHUNK, W2), lambda i: (i, 0)), pl.BlockSpec((CHUNK, W), lambda i: (i, 0)),
                  full3((G, CHUNK, CHUNK)), full3((G, CHUNK, CHUNK)), full3((G, CHUNK, 1)),
                  pl.BlockSpec((1, W), lambda i: (0, 0))],
        out_specs=[pl.BlockSpec((CHUNK, W2), lambda i: (i, 0)), full3((G, CHUNK, CHUNK)), full3((G, CHUNK, 1)),
                   pl.BlockSpec((1, W), lambda i: (0, 0))],
        out_shape=[_sds((S, W2), BF16), _sds((G, CHUNK, CHUNK), F32), _sds((G, CHUNK, 1), F32), _sds((1, W), F32)],
        compiler_params=_cp("arbitrary"),
    )(zpre, dgated, wm, wmt, bcol, vnorm)


def _conv_taps(h_ref, half, r0, R, tc):
    if r0 == 0:
        xe = jnp.concatenate([jnp.zeros((HALO, tc), F32), h_ref[half, 0:R, :].astype(F32)], axis=0)
    else:
        xe = h_ref[half, r0 - HALO:r0 + R, :].astype(F32)
    return xe[HALO:], pltpu.roll(xe, 1, 0)[HALO:], pltpu.roll(xe, 2, 0)[HALO:]


def _conv_apply(taps, w, b):
    x0, x1, x2 = taps
    return x2 * w[0:1] + x1 * w[1:2] + x0 * w[2:3] + b


def _convgate_fwd(hup, cw, cb, name):
    _, S, F = hup.shape
    tc = _t(F, 256)
    R = _t(S, 512)

    def body(h_ref, w_ref, b_ref, o_ref):
        for r0 in range(0, S, R):
            gate = _conv_apply(_conv_taps(h_ref, 0, r0, R, tc), w_ref[0], b_ref[0])
            val = _conv_apply(_conv_taps(h_ref, 1, r0, R, tc), w_ref[1], b_ref[1])
            o_ref[r0:r0 + R, :] = (gate * jax.nn.sigmoid(gate) * val).astype(o_ref.dtype)

    return pl.pallas_call(
        body, name=name, grid=(F // tc,),
        in_specs=[pl.BlockSpec((2, S, tc), lambda j: (0, 0, j)), pl.BlockSpec((2, 3, tc), lambda j: (0, 0, j)),
                  pl.BlockSpec((2, 1, tc), lambda j: (0, 0, j))],
        out_specs=pl.BlockSpec((S, tc), lambda j: (0, j)), out_shape=_sds((S, F), BF16),
        compiler_params=_cp("parallel"),
    )(hup, cw, cb)


def _convgate_bwd(hup, dact, cw, cb, name):
    _, S, F = hup.shape
    tc = _t(F, 256)
    R = _t(S, 512)

    def body(h_ref, da_ref, w_ref, b_ref, dh_ref, dw_ref, db_ref, dhc):
        dhc[:, S:S + HALO, :] = jnp.zeros((2, HALO, tc), F32)
        dw_acc = [[jnp.zeros((1, tc), F32) for _ in range(3)] for _ in range(2)]
        db_acc = [jnp.zeros((1, tc), F32) for _ in range(2)]
        for r0 in range(0, S, R):
            taps = [_conv_taps(h_ref, p, r0, R, tc) for p in range(2)]
            gate = _conv_apply(taps[0], w_ref[0], b_ref[0])
            val = _conv_apply(taps[1], w_ref[1], b_ref[1])
            da = da_ref[r0:r0 + R, :].astype(F32)
            sg = jax.nn.sigmoid(gate)
            d = [da * val * (sg * (1.0 + gate * (1.0 - sg))), da * (gate * sg)]
            for p in range(2):
                dhc[p, r0:r0 + R, :] = d[p]
                db_acc[p] = db_acc[p] + jnp.sum(d[p], axis=0, keepdims=True)
                for k in range(3):
                    dw_acc[p][k] = dw_acc[p][k] + jnp.sum(d[p] * taps[p][2 - k], axis=0, keepdims=True)
        for p in range(2):
            db_ref[p] = db_acc[p]
            dw_ref[p] = jnp.concatenate(dw_acc[p], axis=0)
            w = w_ref[p]
            for r0 in range(0, S, R):
                de = dhc[p, r0:r0 + R + HALO, :]
                d1 = pltpu.roll(de, R + HALO - 1, 0)[:R]
                d2 = pltpu.roll(de, R + HALO - 2, 0)[:R]
                dh_ref[p, r0:r0 + R, :] = (de[:R] * w[2:3] + d1 * w[1:2] + d2 * w[0:1]).astype(dh_ref.dtype)

    blk = lambda rows: pl.BlockSpec((2, rows, tc), lambda j: (0, 0, j))
    return pl.pallas_call(
        body, name=name, grid=(F // tc,),
        in_specs=[blk(S), pl.BlockSpec((S, tc), lambda j: (0, j)), blk(3), blk(1)],
        out_specs=[blk(S), blk(3), blk(1)],
        out_shape=[_sds((2, S, F), BF16), _sds((2, 3, F), F32), _sds((2, 1, F), F32)],
        scratch_shapes=[pltpu.VMEM((2, S + HALO, tc), F32)],
        compiler_params=_cp("parallel"),
    )(hup, dact, cw, cb)


def _headnorm_fwd(x3, part, gain, name):
    _, S, W = x3.shape
    tr = _t(S, 512)

    def body(x_ref, g_ref, o_ref):
        xv = x_ref[...].astype(F32)
        for h in range(W // HEAD):
            sl = slice(h * HEAD, (h + 1) * HEAD)
            xh = xv[:, sl]
            r = lax.rsqrt(jnp.mean(xh * xh, axis=-1, keepdims=True) + EPS)
            o_ref[:, sl] = ((xh * r) * g_ref[...]).astype(o_ref.dtype)

    return pl.pallas_call(
        body, name=name, grid=(S // tr,),
        in_specs=[pl.BlockSpec((None, tr, W), lambda i: (part, i, 0)), pl.BlockSpec((1, HEAD), lambda i: (0, 0))],
        out_specs=pl.BlockSpec((tr, W), lambda i: (i, 0)), out_shape=_sds((S, W), BF16),
        compiler_params=_cp("parallel"),
    )(x3, gain)


def _headnorm_bwd(dys, x3, gain, passes, name):
    _, S, W = x3.shape
    tr = _t(S, 256)
    nd, npass = len(dys), len(passes)

    def body(*refs):
        dy_refs = refs[:nd]
        x_ref, g_ref = refs[nd], refs[nd + 1]
        p_refs = refs[nd + 2:nd + 2 + npass]
        o_ref, dg_ref = refs[nd + 2 + npass], refs[nd + 3 + npass]

        @pl.when(pl.program_id(0) == 0)
        def _():
            dg_ref[...] = jnp.zeros_like(dg_ref)

        xv = x_ref[...].astype(F32)
        dyv = dy_refs[0][...].astype(F32)
        for r in dy_refs[1:]:
            dyv = dyv + r[...].astype(F32)
        dg = jnp.zeros((1, HEAD), F32)
        for h in range(W // HEAD):
            sl = slice(h * HEAD, (h + 1) * HEAD)
            xh = xv[:, sl]
            r = lax.rsqrt(jnp.mean(xh * xh, axis=-1, keepdims=True) + EPS)
            xh = xh * r
            dyh = dyv[:, sl]
            dg = dg + jnp.sum(dyh * xh, axis=0, keepdims=True)
            dxh = dyh * g_ref[...]
            o_ref[0, :, sl] = (r * (dxh - xh * jnp.mean(dxh * xh, axis=-1, keepdims=True))).astype(o_ref.dtype)
        dg_ref[...] += dg
        pv = p_refs[0][...].astype(F32)
        for r in p_refs[1:]:
            pv = pv + r[...].astype(F32)
        o_ref[1] = pv.astype(o_ref.dtype)

    row = pl.BlockSpec((tr, W), lambda i: (i, 0))
    vec = pl.BlockSpec((1, HEAD), lambda i: (0, 0))
    return pl.pallas_call(
        body, name=name, grid=(S // tr,),
        in_specs=[row] * nd + [pl.BlockSpec((None, tr, W), lambda i: (0, i, 0)), vec] + [row] * npass,
        out_specs=[pl.BlockSpec((2, tr, W), lambda i: (0, i, 0)), vec],
        out_shape=[_sds((2, S, W), BF16), _sds((1, HEAD), F32)], compiler_params=_cp("arbitrary"),
    )(*dys, x3, gain, *passes)


def _gate_fwd(o, qg, name):
    S, W = o.shape
    tr = _t(S, 512)

    def body(o_ref, g_ref, y_ref):
        y_ref[...] = (o_ref[...].astype(F32) * jax.nn.sigmoid(g_ref[...].astype(F32))).astype(y_ref.dtype)

    row = pl.BlockSpec((tr, W), lambda i: (i, 0))
    return pl.pallas_call(
        body, name=name, grid=(S // tr,), in_specs=[row, pl.BlockSpec((None, tr, W), lambda i: (1, i, 0))],
        out_specs=row, out_shape=_sds((S, W), BF16), compiler_params=_cp("parallel"),
    )(o, qg)


def _gate_bwd(dog, o, qg, name):
    S, W = o.shape
    H = W // HEAD
    tr = _t(S, 512)

    def body(dy_ref, o_ref, g_ref, do_ref, dg_ref, dl_ref):
        sg = jax.nn.sigmoid(g_ref[...].astype(F32))
        dy = dy_ref[...].astype(F32)
        ov = o_ref[...].astype(F32)
        dob = (dy * sg).astype(do_ref.dtype)
        do_ref[...] = dob
        dg_ref[...] = (dy * ov * (sg * (1.0 - sg))).astype(dg_ref.dtype)
        prod = dob.astype(F32) * ov
        for h in range(H):
            dl_ref[h] = jnp.sum(prod[:, h * HEAD:(h + 1) * HEAD], axis=-1, keepdims=True)

    row = pl.BlockSpec((tr, W), lambda i: (i, 0))
    return pl.pallas_call(
        body, name=name, grid=(S // tr,), in_specs=[row, row, pl.BlockSpec((None, tr, W), lambda i: (1, i, 0))],
        out_specs=[row, row, pl.BlockSpec((H, tr, 1), lambda i: (0, i, 0))],
        out_shape=[_sds((S, W), BF16), _sds((S, W), BF16), _sds((H, S, 1), F32)], compiler_params=_cp("parallel"),
    )(dog, o, qg)


def _logf_cumsum(fpre, bf, name):
    S, C = fpre.shape
    n = S // CHUNK

    def body(f_ref, b_ref, c_ref, carry):
        @pl.when(pl.program_id(0) == 0)
        def _():
            carry[...] = jnp.zeros_like(carry)

        lf = jax.nn.log_sigmoid(f_ref[...] + b_ref[...])
        tri = (lax.broadcasted_iota(jnp.int32, (CHUNK, CHUNK), 0)
               >= lax.broadcasted_iota(jnp.int32, (CHUNK, CHUNK), 1)).astype(F32)
        c_ref[...] = jnp.dot(tri, lf, preferred_element_type=F32, precision=lax.Precision.HIGHEST) + carry[...]
        carry[...] += jnp.sum(lf, axis=0, keepdims=True)

    return pl.pallas_call(
        body, name=name, grid=(n,),
        in_specs=[pl.BlockSpec((CHUNK, C), lambda i: (i, 0)), pl.BlockSpec((1, C), lambda i: (0, 0))],
        out_specs=pl.BlockSpec((CHUNK, C), lambda i: (i, 0)), out_shape=_sds((S, C), F32),
        scratch_shapes=[pltpu.VMEM((1, C), F32)], compiler_params=_cp("arbitrary"),
    )(fpre, bf)


def _logf_cumsum_bwd(dc, fpre, bf, name):
    S, C = fpre.shape
    n = S // CHUNK

    def body(dc_ref, f_ref, b_ref, df_ref, db_ref, carry):
        @pl.when(pl.program_id(0) == 0)
        def _():
            carry[...] = jnp.zeros_like(carry)
            db_ref[...] = jnp.zeros_like(db_ref)

        dcv = dc_ref[...]
        tri = (lax.broadcasted_iota(jnp.int32, (CHUNK, CHUNK), 0)
               <= lax.broadcasted_iota(jnp.int32, (CHUNK, CHUNK), 1)).astype(F32)
        dlf = jnp.dot(tri, dcv, preferred_element_type=F32, precision=lax.Precision.HIGHEST) + carry[...]
        carry[...] += jnp.sum(dcv, axis=0, keepdims=True)
        df = dlf * jax.nn.sigmoid(-(f_ref[...] + b_ref[...]))
        df_ref[...] = df.astype(df_ref.dtype)
        db_ref[...] += jnp.sum(df, axis=0, keepdims=True)

    rev = pl.BlockSpec((CHUNK, C), lambda i: (n - 1 - i, 0))
    vec = pl.BlockSpec((1, C), lambda i: (0, 0))
    return pl.pallas_call(
        body, name=name, grid=(n,), in_specs=[rev, rev, vec], out_specs=[rev, vec],
        out_shape=[_sds((S, C), BF16), _sds((1, C), F32)],
        scratch_shapes=[pltpu.VMEM((1, C), F32)], compiler_params=_cp("arbitrary"),
    )(dc, fpre, bf)


def _attn_tiles(S):
    return _t(S, 1024), _t(S, 512)


def _scores(q, k, ck, off, tq, tk, masked):
    s = lax.dot_general(q, k, (_DIMS["nt"], ((), ())), preferred_element_type=F32) - ck
    if masked:
        d = lax.broadcasted_iota(jnp.int32, (tq, tk), 1) - lax.broadcasted_iota(jnp.int32, (tq, tk), 0)
        s = jnp.where(d <= off, s, NEG)
    return s


def _attn_fwd(qn, kn, kv, crow, name):
    S, W = qn.shape
    H = W // HEAD
    tq, tk = _attn_tiles(S)
    nq, nk = S // tq, S // tk
    last = lambda i: ((i + 1) * tq - 1) // tk

    def body(q_ref, k_ref, v_ref, ck_ref, o_ref, lse_ref, m_sc, l_sc, acc_sc):
        qi, kj = pl.program_id(1), pl.program_id(2)

        @pl.when(kj == 0)
        def _():
            m_sc[...] = jnp.full_like(m_sc, NEG)
            l_sc[...] = jnp.zeros_like(l_sc)
            acc_sc[...] = jnp.zeros_like(acc_sc)

        def step(masked):
            s = _scores(q_ref[...], k_ref[...], ck_ref[...], qi * tq - kj * tk, tq, tk, masked)
            m_new = jnp.maximum(m_sc[...], jnp.max(s, axis=-1, keepdims=True))
            alpha = jnp.exp(m_sc[...] - m_new)
            p = jnp.exp(s - m_new)
            l_sc[...] = alpha * l_sc[...] + jnp.sum(p, axis=-1, keepdims=True)
            acc_sc[...] = alpha * acc_sc[...] + jnp.dot(p.astype(BF16), v_ref[...], preferred_element_type=F32)
            m_sc[...] = m_new

        below = (kj + 1) * tk - 1 <= qi * tq

        @pl.when(below)
        def _():
            step(False)

        @pl.when(jnp.logical_and(kj <= last(qi), jnp.logical_not(below)))
        def _():
            step(True)

        @pl.when(kj == nk - 1)
        def _():
            o_ref[...] = (acc_sc[...] / l_sc[...]).astype(o_ref.dtype)
            lse_ref[...] = m_sc[...] + jnp.log(l_sc[...])

    return pl.pallas_call(
        body, name=name, grid=(H, nq, nk),
        in_specs=[pl.BlockSpec((tq, HEAD), lambda h, i, j: (i, h)),
                  pl.BlockSpec((tk, HEAD), lambda h, i, j: (jnp.minimum(j, last(i)), h)),
                  pl.BlockSpec((None, tk, HEAD), lambda h, i, j: (1, jnp.minimum(j, last(i)), h)),
                  pl.BlockSpec((None, 1, tk), lambda h, i, j: (h, 0, jnp.minimum(j, last(i))))],
        out_specs=[pl.BlockSpec((tq, HEAD), lambda h, i, j: (i, h)),
                   pl.BlockSpec((None, tq, 1), lambda h, i, j: (h, i, 0))],
        out_shape=[_sds((S, W), BF16), _sds((H, S, 1), F32)],
        scratch_shapes=[pltpu.VMEM((tq, 1), F32), pltpu.VMEM((tq, 1), F32), pltpu.VMEM((tq, HEAD), F32)],
        compiler_params=_cp("parallel", "parallel", "arbitrary"),
    )(qn, kn, kv, crow)


def _attn_bwd(qn, kn, kv, crow, do, lse, delta, name):
    S, W = qn.shape
    H = W // HEAD
    tq, tk = _attn_tiles(S)
    nq, nk = S // tq, S // tk
    first = lambda j: (j * tk) // tq

    def body(q_ref, k_ref, v_ref, ck_ref, do_ref, lse_ref, dl_ref, dq_ref, dr_ref, dk_ref, dv_ref, dc_ref, dk_sc, dv_sc, dc_sc):
        kj, qi = pl.program_id(1), pl.program_id(2)

        @pl.when(jnp.logical_and(kj == 0, qi == 0))
        def _():
            dq_ref[...] = jnp.zeros_like(dq_ref)
            dr_ref[...] = jnp.zeros_like(dr_ref)

        @pl.when(qi == 0)
        def _():
            dk_sc[...] = jnp.zeros_like(dk_sc)
            dv_sc[...] = jnp.zeros_like(dv_sc)
            dc_sc[...] = jnp.zeros_like(dc_sc)

        def step(masked):
            q, k, dov = q_ref[...], k_ref[...], do_ref[...]
            s = _scores(q, k, ck_ref[...], qi * tq - kj * tk, tq, tk, masked)
            p = jnp.exp(s - lse_ref[...])
            dp = lax.dot_general(dov, v_ref[...], (_DIMS["nt"], ((), ())), preferred_element_type=F32)
            ds = p * (dp - dl_ref[...])
            dsb = ds.astype(BF16)
            dc_sc[...] += jnp.sum(ds, axis=0, keepdims=True)
            dv_sc[...] += lax.dot_general(p.astype(BF16), dov, (_DIMS["tn"], ((), ())), preferred_element_type=F32)
            dk_sc[...] += lax.dot_general(dsb, q, (_DIMS["tn"], ((), ())), preferred_element_type=F32)
            rows = pl.ds(pl.multiple_of(qi * tq, tq), tq)
            dq_ref[rows, :] += jnp.dot(dsb, k, preferred_element_type=F32)
            dr_ref[rows, :] += jnp.sum(ds, axis=1, keepdims=True)

        below = (kj + 1) * tk - 1 <= qi * tq

        @pl.when(below)
        def _():
            step(False)

        @pl.when(jnp.logical_and(qi >= first(kj), jnp.logical_not(below)))
        def _():
            step(True)

        @pl.when(qi == nq - 1)
        def _():
            dk_ref[...] = dk_sc[...]
            dv_ref[...] = dv_sc[...]
            dc_ref[...] = dc_sc[...]

    qblk = pl.BlockSpec((tq, HEAD), lambda h, j, i: (jnp.maximum(i, first(j)), h))
    qcol = pl.BlockSpec((None, tq, 1), lambda h, j, i: (h, jnp.maximum(i, first(j)), 0))
    kblk = pl.BlockSpec((tk, HEAD), lambda h, j, i: (j, h))
    krow = pl.BlockSpec((None, 1, tk), lambda h, j, i: (h, 0, j))
    return pl.pallas_call(
        body, name=name, grid=(H, nk, nq),
        in_specs=[qblk, kblk, pl.BlockSpec((None, tk, HEAD), lambda h, j, i: (1, j, h)), krow, qblk, qcol, qcol],
        out_specs=[pl.BlockSpec((S, HEAD), lambda h, j, i: (0, h)), pl.BlockSpec((None, S, 1), lambda h, j, i: (h, 0, 0)),
                   kblk, kblk, krow],
        out_shape=[_sds((S, W), F32), _sds((H, S, 1), F32), _sds((S, W), F32), _sds((S, W), F32), _sds((H, 1, S), F32)],
        scratch_shapes=[pltpu.VMEM((tk, HEAD), F32), pltpu.VMEM((tk, HEAD), F32), pltpu.VMEM((1, tk), F32)],
        compiler_params=_cp("parallel", "arbitrary", "arbitrary"),
    )(qn, kn, kv, crow, do, lse, delta)


def _adamw(w, g, m, v, name):
    R, C = w.shape
    tr = _t(R, max(8, (1 << 19) // max(C, 1)), 8)
    c1 = 1.0 - ADAM_B1 ** ADAM_STEP
    c2 = 1.0 - ADAM_B2 ** ADAM_STEP

    def body(w_ref, g_ref, m_ref, v_ref, d_ref, nm_ref, nv_ref):
        gv = g_ref[...]
        nm = ADAM_B1 * m_ref[...] + (1.0 - ADAM_B1) * gv
        nv = ADAM_B2 * v_ref[...] + (1.0 - ADAM_B2) * (gv * gv)
        nm_ref[...] = nm
        nv_ref[...] = nv
        d_ref[...] = -ADAM_LR * ((nm / c1) / (jnp.sqrt(nv / c2) + ADAM_EPS) + ADAM_WD * w_ref[...])

    blk = pl.BlockSpec((tr, C), lambda i: (i, 0))
    return pl.pallas_call(
        body, name=name, grid=(R // tr,), in_specs=[blk] * 4, out_specs=[blk] * 3,
        out_shape=[_sds((R, C), F32)] * 3, compiler_params=_cp("parallel"),
    )(w, g, m, v)


def _place():
    x, y, c = lax.axis_index("x"), lax.axis_index("y"), lax.axis_index("c")
    chips = [(1 - x, y), (x, 1 - y), (1 - x, 1 - y)]
    return x, y, c, chips


def _place_part(part, me, dtype, name):
    L, R, C = part.shape
    tr = _t(R, max(16, (1 << 20) // C), 16)

    def body(me_ref, x_ref, o_ref):
        o_ref[...] = x_ref[...].astype(o_ref.dtype)

    return pl.pallas_call(
        body, name=name,
        grid_spec=pltpu.PrefetchScalarGridSpec(
            num_scalar_prefetch=1, grid=(L, R // tr),
            in_specs=[pl.BlockSpec((None, tr, C), lambda l, r, m: (l, r, 0))],
            out_specs=pl.BlockSpec((None, None, tr, C), lambda l, r, m: (m[0], l, r, 0))),
        out_shape=_sds((N_CHIPS, L, R, C), dtype), compiler_params=_cp("parallel", "parallel"),
    )(me, part)


def _gather_chips(bufs, name):
    n = len(bufs)

    def body(*refs):
        outs = refs[n:2 * n]
        send_sems, recv_sems = refs[2 * n:]
        x, y, c, chips = _place()
        me = 2 * x + y
        sibling = (x, y, 1 - c)

        def copy(i, k, chip_index, core, to):
            h = outs[i].shape[2] // 2
            rows = outs[i].at[chip_index, :, pl.ds(core * h, h), :]
            return pltpu.make_async_remote_copy(
                src_ref=rows, dst_ref=rows, send_sem=send_sems.at[i, k], recv_sem=recv_sems.at[i, k],
                device_id=to, device_id_type=MESH)

        sent = []
        for i in range(n):
            for j, chip in enumerate(chips):
                cp = copy(i, j, me, c, (*chip, c))
                cp.start()
                sent.append(cp)
        for i in range(n):
            for j, chip in enumerate(chips):
                idx = 2 * chip[0] + chip[1]
                copy(i, j, idx, c, (x, y, c)).wait_recv()
                fw = copy(i, 3 + j, idx, c, sibling)
                fw.start()
                sent.append(fw)
        for i in range(n):
            for j, chip in enumerate(chips):
                copy(i, 3 + j, 2 * chip[0] + chip[1], 1 - c, (x, y, c)).wait_recv()
        for cp in sent:
            cp.wait_send()

    return pl.pallas_call(
        body, name=name, in_specs=[ANY] * n, out_specs=[ANY] * n,
        out_shape=[_sds(b.shape, b.dtype) for b in bufs],
        scratch_shapes=[pltpu.SemaphoreType.DMA((n, 6)), pltpu.SemaphoreType.DMA((n, 6))],
        input_output_aliases={i: i for i in range(n)},
        compiler_params=pltpu.CompilerParams(has_side_effects=True),
    )(*bufs)


def _swap_halves(gs, name):
    n = len(gs)

    def body(*refs):
        ins, outs = refs[:n], refs[n:2 * n]
        send_sems, recv_sems = refs[2 * n:]
        x, y, c, _ = _place()
        cps = []
        for i in range(n):
            h = ins[i].shape[2] // 2
            cp = pltpu.make_async_remote_copy(
                src_ref=ins[i].at[:, :, pl.ds((1 - c) * h, h), :], dst_ref=outs[i], send_sem=send_sems.at[i],
                recv_sem=recv_sems.at[i], device_id=(x, y, 1 - c), device_id_type=MESH)
            cp.start()
            cps.append(cp)
        for cp in cps:
            cp.wait()

    return pl.pallas_call(
        body, name=name, in_specs=[ANY] * n, out_specs=[ANY] * n,
        out_shape=[_sds(g.shape[:2] + (g.shape[2] // 2, g.shape[3]), g.dtype) for g in gs],
        scratch_shapes=[pltpu.SemaphoreType.DMA((n,)), pltpu.SemaphoreType.DMA((n,))],
        compiler_params=pltpu.CompilerParams(has_side_effects=True),
    )(*gs)


def _pair_sum(g, gs, core, out_dtype, name):
    ns, L, R, C = g.shape
    h = R // 2
    th = _t(h, max(16, (1 << 20) // C), 16)
    nb = h // th

    def body(core_ref, a_ref, b_ref, o_ref):
        o_ref[...] = (a_ref[...].astype(F32) + b_ref[...].astype(F32)).astype(o_ref.dtype)

    blk = (None, None, th, C)
    return pl.pallas_call(
        body, name=name,
        grid_spec=pltpu.PrefetchScalarGridSpec(
            num_scalar_prefetch=1, grid=(ns, L, nb),
            in_specs=[pl.BlockSpec(blk, lambda o, l, r, cr: (o, l, cr[0] * nb + r, 0)),
                      pl.BlockSpec(blk, lambda o, l, r, cr: (o, l, r, 0))],
            out_specs=pl.BlockSpec(blk, lambda o, l, r, cr: (o, l, r, 0))),
        out_shape=_sds((ns, L, h, C), out_dtype), compiler_params=_cp("parallel", "parallel", "parallel"),
    )(core, g, gs)


def _scatter_chips(ps, name):
    n = len(ps)

    def body(*refs):
        ins, outs = refs[:n], refs[n:2 * n]
        send_sems, recv_sems = refs[2 * n:]
        x, y, c, chips = _place()
        me = 2 * x + y
        sent = []
        for i in range(n):
            for j, chip in enumerate(chips):
                cp = pltpu.make_async_remote_copy(
                    src_ref=ins[i].at[2 * chip[0] + chip[1]], dst_ref=outs[i].at[me], send_sem=send_sems.at[i, j],
                    recv_sem=recv_sems.at[i, j], device_id=(*chip, c), device_id_type=MESH)
                cp.start()
                sent.append(cp)
        for i in range(n):
            for j, chip in enumerate(chips):
                slot = outs[i].at[2 * chip[0] + chip[1]]
                pltpu.make_async_remote_copy(
                    src_ref=slot, dst_ref=slot, send_sem=send_sems.at[i, j], recv_sem=recv_sems.at[i, j],
                    device_id=(x, y, c), device_id_type=MESH).wait_recv()
        for cp in sent:
            cp.wait_send()

    return pl.pallas_call(
        body, name=name, in_specs=[ANY] * n, out_specs=[ANY] * n,
        out_shape=[_sds(p.shape, p.dtype) for p in ps],
        scratch_shapes=[pltpu.SemaphoreType.DMA((n, 3)), pltpu.SemaphoreType.DMA((n, 3))],
        compiler_params=pltpu.CompilerParams(has_side_effects=True),
    )(*ps)


def _sum_chips(q, p, sel, name):
    ns, L, h, C = q.shape
    th = _t(h, max(16, (1 << 19) // C), 16)
    nb = h // th
    blk = (None, None, th, C)

    def body(sel_ref, q0, q1, q2, q3, p_ref, o_ref):
        me = sel_ref[0]
        own = p_ref[...].astype(F32)
        parts = [jnp.where(me == k, own, r[...].astype(F32)) for k, r in enumerate((q0, q1, q2, q3))]
        o_ref[...] = ((parts[0] + parts[1]) + parts[2]) + parts[3]

    def other(k):
        return lambda l, r, s: (jnp.where(s[0] == k, (k + 1) % N_CHIPS, k), l, r, 0)

    return pl.pallas_call(
        body, name=name,
        grid_spec=pltpu.PrefetchScalarGridSpec(
            num_scalar_prefetch=1, grid=(L, nb),
            in_specs=[pl.BlockSpec(blk, other(k)) for k in range(N_CHIPS)]
            + [pl.BlockSpec(blk, lambda l, r, s: (s[0], l, r, 0))],
            out_specs=pl.BlockSpec((None, th, C), lambda l, r, s: (l, s[1] * nb + r, 0))),
        out_shape=_sds((L, 2 * h, C), F32), compiler_params=_cp("parallel", "parallel"),
    )(sel, q, q, q, q, p)


def _join_halves(bufs, name):
    n = len(bufs)

    def body(*refs):
        outs = refs[n:2 * n]
        send_sems, recv_sems = refs[2 * n:]
        x, y, c, _ = _place()
        cps = []
        for i in range(n):
            h = outs[i].shape[1] // 2
            mine = outs[i].at[:, pl.ds(c * h, h), :]
            cp = pltpu.make_async_remote_copy(
                src_ref=mine, dst_ref=mine, send_sem=send_sems.at[i], recv_sem=recv_sems.at[i],
                device_id=(x, y, 1 - c), device_id_type=MESH)
            cp.start()
            cps.append(cp)
        for cp in cps:
            cp.wait()

    return pl.pallas_call(
        body, name=name, in_specs=[ANY] * n, out_specs=[ANY] * n,
        out_shape=[_sds(b.shape, b.dtype) for b in bufs],
        scratch_shapes=[pltpu.SemaphoreType.DMA((n,)), pltpu.SemaphoreType.DMA((n,))],
        input_output_aliases={i: i for i in range(n)},
        compiler_params=pltpu.CompilerParams(has_side_effects=True),
    )(*bufs)


def _pack(arrs, rows_mult):
    flat = jnp.concatenate([a.reshape(-1).astype(F32) for a in arrs])
    rows = -(-flat.size // LANE)
    rows = -(-rows // rows_mult) * rows_mult
    return jnp.pad(flat, (0, rows * LANE - flat.size)).reshape(rows, LANE)


def _unpack(packed, like):
    flat = packed.reshape(-1)
    out, pos = [], 0
    for a in like:
        n = math.prod(a.shape)
        out.append(flat[pos:pos + n].reshape(a.shape))
        pos += n
    return out


def _adamw_nd(w, g, m, v, name):
    shape = w.shape
    C = shape[-1]
    d, nm, nv = _adamw(w.reshape(-1, C), g.reshape(-1, C), m.reshape(-1, C), v.reshape(-1, C), name)
    return d.reshape(shape), nm.reshape(shape), nv.reshape(shape)


def kernel(x, a_norm, a_w_in, a_v_norm, a_w_s, a_b_s, a_w_out, kv_norm, w_kvf, b_f, k_norm, b_norm, b_w_qg, q_norm, b_w_out, f_norm, f_w_up, f_conv_w, f_conv_b, f_w_down, final_norm, loss_target, m_a_norm, m_a_w_in, m_a_v_norm, m_a_w_s, m_a_b_s, m_a_w_out, m_kv_norm, m_w_kvf, m_b_f, m_k_norm, m_b_norm, m_b_w_qg, m_q_norm, m_b_w_out, m_f_norm, m_f_w_up, m_f_conv_w, m_f_conv_b, m_f_w_down, m_final_norm, v_a_norm, v_a_w_in, v_a_v_norm, v_a_w_s, v_a_b_s, v_a_w_out, v_kv_norm, v_w_kvf, v_b_f, v_k_norm, v_b_norm, v_b_w_qg, v_q_norm, v_b_w_out, v_f_norm, v_f_w_up, v_f_conv_w, v_f_conv_b, v_f_w_down, v_final_norm):
    weights = dict(a_norm=a_norm, a_w_in=a_w_in, a_v_norm=a_v_norm, a_w_s=a_w_s, a_b_s=a_b_s, a_w_out=a_w_out, kv_norm=kv_norm, w_kvf=w_kvf, b_f=b_f, k_norm=k_norm, b_norm=b_norm, b_w_qg=b_w_qg, q_norm=q_norm, b_w_out=b_w_out, f_norm=f_norm, f_w_up=f_w_up, f_conv_w=f_conv_w, f_conv_b=f_conv_b, f_w_down=f_w_down, final_norm=final_norm)
    mom1 = dict(a_norm=m_a_norm, a_w_in=m_a_w_in, a_v_norm=m_a_v_norm, a_w_s=m_a_w_s, a_b_s=m_a_b_s, a_w_out=m_a_w_out, kv_norm=m_kv_norm, w_kvf=m_w_kvf, b_f=m_b_f, k_norm=m_k_norm, b_norm=m_b_norm, b_w_qg=m_b_w_qg, q_norm=m_q_norm, b_w_out=m_b_w_out, f_norm=m_f_norm, f_w_up=m_f_w_up, f_conv_w=m_f_conv_w, f_conv_b=m_f_conv_b, f_w_down=m_f_w_down, final_norm=m_final_norm)
    mom2 = dict(a_norm=v_a_norm, a_w_in=v_a_w_in, a_v_norm=v_a_v_norm, a_w_s=v_a_w_s, a_b_s=v_a_b_s, a_w_out=v_a_w_out, kv_norm=v_kv_norm, w_kvf=v_w_kvf, b_f=v_b_f, k_norm=v_k_norm, b_norm=v_b_norm, b_w_qg=v_b_w_qg, q_norm=v_q_norm, b_w_out=v_b_w_out, f_norm=v_f_norm, f_w_up=v_f_w_up, f_conv_w=v_f_conv_w, f_conv_b=v_f_conv_b, f_w_down=v_f_w_down, final_norm=v_final_norm)
    names = list(weights)
    big = ["a_w_in", "a_w_out", "w_kvf", "b_w_qg", "b_w_out", "f_w_up", "f_w_down"]
    small_sharded = ["a_norm", "a_v_norm", "f_conv_w"]
    small_repl = ["a_w_s", "a_b_s", "kv_norm", "b_f", "k_norm", "b_norm", "q_norm", "f_norm", "f_conv_b", "final_norm"]

    _, S, D = x.shape
    NA, NB, DEPTH = a_norm.shape[0], b_norm.shape[0], f_norm.shape[0]
    W = a_w_out.shape[1] * N_CHIPS
    G = a_w_s.shape[1]
    H = b_f.shape[0]
    ATT = H * HEAD
    F = f_w_down.shape[1] * N_CHIPS
    Ckv = w_kvf.shape[1]
    Cp = -(-Ckv // LANE) * LANE
    assert W == G * CHUNK and Ckv * N_CHIPS == 2 * ATT + H and S % CHUNK == 0
    core = lax.axis_index("c").astype(jnp.int32).reshape(1)
    me = (2 * lax.axis_index("x") + lax.axis_index("y")).astype(jnp.int32).reshape(1)
    sel = jnp.concatenate([me, core])

    small_local = [weights[k] for k in small_sharded]
    parts = dict(a_w_in=a_w_in, a_w_out=a_w_out, w_kvf=jnp.pad(w_kvf, ((0, 0), (0, Cp - Ckv)))[None],
                 b_w_qg=b_w_qg, b_w_out=b_w_out, f_w_up=f_w_up, f_w_down=f_w_down)
    bufs = [_place_part(p, me, BF16, f"place_{k}") for k, p in parts.items()]
    bufs.append(_place_part(_pack(small_local, 32)[None], me, F32, "place_small"))
    g_ain, g_aout, g_kvf, g_bqg, g_bout, g_up, g_down, g_small = _gather_chips(bufs, "gather_weights")

    per_chip = [_unpack(g_small[j, 0], small_local) for j in range(N_CHIPS)]
    a_norm_f, a_vnorm_f, conv_w_f = [jnp.concatenate([per_chip[j][k] for j in range(N_CHIPS)], axis=-1) for k in range(3)]
    cw = conv_w_f.reshape(DEPTH, 3, 2, F).transpose(0, 2, 1, 3)
    cb = f_conv_b.reshape(DEPTH, 2, 1, F)
    tril = jnp.tril(jnp.ones((CHUNK, CHUNK), dtype=bool))
    wm = jnp.where(tril, a_w_s, 0.0).astype(BF16)
    wmt = jnp.swapaxes(wm, -1, -2)
    bcol = a_b_s[..., None]
    kvf_full = g_kvf[:, 0, :, :Ckv].transpose(1, 0, 2).reshape(D, N_CHIPS * Ckv)
    w_kv = jnp.stack([kvf_full[:, :ATT], kvf_full[:, ATT:2 * ATT]])
    w_f = jnp.pad(kvf_full[:, 2 * ATT:], ((0, 0), (0, LANE - H)))
    bf_pad = jnp.pad(b_f, (0, LANE - H))[None]
    row = lambda v: v.reshape(1, -1)

    h = x[0]
    target = loss_target[0]
    saved = [dict() for _ in range(DEPTH)]
    kvs = {}
    for l in range(DEPTH):
        sv = saved[l]
        sv["h_m"] = h
        if l < NA:
            xn = _rms_fwd(h, row(a_norm_f[l]), f"a{l}_norm")
            zpre = _mm(m2(xn), mgc(g_ain, l), "nn", m2(_sds((S, 2 * W), BF16)), tm=1024, tn=512, name=f"a{l}_in")
            gated = _sgu_fwd(zpre, wm[l], bcol[l], row(a_vnorm_f[l]), f"a{l}_sgu")
            h = _mm(m2(gated), mgr(g_aout, l), "nn", m2(_sds((S, D), F32)), res=m2(h), tm=1024, tn=512, name=f"a{l}_out")
            sv.update(xn_m=xn, zpre=zpre, gated=gated)
        else:
            j = l - NA
            xn = _rms_fwd(h, row(b_norm[j]), f"b{j}_norm")
            qg = _mm(m2(xn), mgc(g_bqg, j), "nn", mcs(_sds((2, S, ATT), BF16)), tm=1024, tn=512, name=f"b{j}_qg")
            qn = _headnorm_fwd(qg, 0, row(q_norm[j]) * QK_SCALE, f"b{j}_qnorm")
            o, lse = _attn_fwd(qn, kvs["kn"], kvs["kv"], kvs["crow"], f"b{j}_attn")
            og = _gate_fwd(o, qg, f"b{j}_gate")
            h = _mm(m2(og), mgr(g_bout, j), "nn", m2(_sds((S, D), F32)), res=m2(h), tm=1024, tn=512, name=f"b{j}_out")
            sv.update(xn_m=xn, qg=qg, qn=qn, o=o, lse=lse, og=og)
        sv["h_f"] = h
        xn = _rms_fwd(h, row(f_norm[l]), f"f{l}_norm")
        hup = _mm(m2(xn), mgc(g_up, l), "nn", mcs(_sds((2, S, F), BF16)), tm=1024, tn=1408, name=f"f{l}_up")
        act = _convgate_fwd(hup, cw[l], cb[l], f"f{l}_conv")
        h = _mm(m2(act), mgr(g_down, l), "nn", m2(_sds((S, D), F32)), res=m2(h), tm=1024, tn=512, tk=1408, name=f"f{l}_down")
        sv.update(xn_f=xn, hup=hup, act=act)
        if l == NA - 1:
            xn_kv = _rms_fwd(h, row(kv_norm), "kv_norm")
            kv = _mm(m2(xn_kv), mcs(w_kv), "nn", mcs(_sds((2, S, ATT), BF16)), tm=1024, tn=512, name="kv_proj")
            fpre = _mm(m2(xn_kv), m2(w_f), "nn", m2(_sds((S, LANE), F32)), tm=1024, name="kv_fproj")
            kn = _headnorm_fwd(kv, 0, row(k_norm), "kv_knorm")
            cums = _logf_cumsum(fpre, bf_pad, "kv_cumsum")
            cT = cums[:, :H].T
            kvs.update(h=h, xn=xn_kv, kv=kv, fpre=fpre, kn=kn, crow=cT[:, None, :])

    loss11, dh, d_final = _final_loss(h, row(final_norm), target, "final_loss")
    loss = lax.psum(loss11[0, 0], ("x", "y", "c"))

    G_ain = lax.empty(g_ain.shape, BF16)
    G_aout = lax.empty(g_aout.shape, BF16)
    G_bqg = lax.empty(g_bqg.shape, BF16)
    G_bout = lax.empty(g_bout.shape, BF16)
    G_up = lax.empty(g_up.shape, BF16)
    G_down = lax.empty(g_down.shape, BF16)
    d_anorm, d_avnorm, d_ws, d_bs = [None] * NA, [None] * NA, [None] * NA, [None] * NA
    d_bnorm, d_qnorm = [None] * NB, [None] * NB
    d_fnorm, d_cw, d_cb = [None] * DEPTH, [None] * DEPTH, [None] * DEPTH
    dkn, dvv, dck = [], [], []
    G_kvf = d_kvnorm = d_bf = d_knorm = None
    for l in reversed(range(DEPTH)):
        sv = saved[l]
        if l == NA - 1:
            dkv, d_knorm = _headnorm_bwd(dkn, kvs["kv"], row(k_norm), dvv, "kv_knorm_bwd")
            dc = sum(dck)
            dc = jnp.pad(dc, ((0, 0), (0, LANE - H)))
            df, d_bf = _logf_cumsum_bwd(dc, kvs["fpre"], bf_pad, "kv_cumsum_bwd")
            dxn = _mm(mcs(dkv), mcs(w_kv), "nt", m2(_sds((S, D), F32)), tm=1024, tn=512, tk=2048, name="kv_proj_dx")
            dxn = _mm(m2(df), m2(w_f), "nt", m2(_sds((S, D), BF16)), res=m2(dxn), tm=1024, tn=512, name="kv_fproj_dx")
            dw_kv = _mm(m2(kvs["xn"]), mcs(dkv), "tn", mcs(_sds((2, D, ATT), BF16)), tm=512, tn=1024, tk=2048, name="kv_proj_dw")
            dw_f = _mm(m2(kvs["xn"]), m2(df), "tn", m2(_sds((D, LANE), BF16)), tm=512, tk=2048, name="kv_fproj_dw")
            dh, d_kvnorm = _rms_bwd(dxn, kvs["h"], row(kv_norm), dh, "kv_norm_bwd")
            dfull = jnp.concatenate([dw_kv[0], dw_kv[1], dw_f[:, :H]], axis=1)
            G_kvf = jnp.pad(dfull.reshape(D, N_CHIPS, Ckv).transpose(1, 0, 2), ((0, 0), (0, 0), (0, Cp - Ckv)))[:, None]
        dact = _mm(m2(dh), mgr(g_down, l), "nt", m2(_sds((S, F), BF16)), tm=1024, tn=1408, tk=2048, name=f"f{l}_down_dx")
        G_down = _mm(m2(sv["act"]), m2(dh), "tn", mgr(G_down, l), alias=True, tm=1408, tn=512, tk=2048, name=f"f{l}_down_dw")
        dhup, d_cw[l], d_cb[l] = _convgate_bwd(sv["hup"], dact, cw[l], cb[l], f"f{l}_conv_bwd")
        dxn = _mm(mcs(dhup), mgc(g_up, l), "nt", m2(_sds((S, D), BF16)), tm=1024, tn=512, tk=1408, name=f"f{l}_up_dx")
        G_up = _mm(m2(sv["xn_f"]), mcs(dhup), "tn", mgc(G_up, l), alias=True, tm=512, tn=1408, tk=2048, name=f"f{l}_up_dw")
        dh, d_fnorm[l] = _rms_bwd(dxn, sv["h_f"], row(f_norm[l]), dh, f"f{l}_norm_bwd")
        if l >= NA:
            j = l - NA
            dog = _mm(m2(dh), mgr(g_bout, j), "nt", m2(_sds((S, ATT), BF16)), tm=1024, tn=512, tk=2048, name=f"b{j}_out_dx")
            G_bout = _mm(m2(sv["og"]), m2(dh), "tn", mgr(G_bout, j), alias=True, tm=512, tn=512, tk=2048, name=f"b{j}_out_dw")
            do, dgate, delta = _gate_bwd(dog, sv["o"], sv["qg"], f"b{j}_gate_bwd")
            dqn, drow, dkn_j, dv_j, dcol = _attn_bwd(sv["qn"], kvs["kn"], kvs["kv"], kvs["crow"], do, sv["lse"], delta, f"b{j}_attn_bwd")
            dkn.append(dkn_j); dvv.append(dv_j); dck.append(drow[:, :, 0].T - dcol[:, 0, :].T)
            dqg, d_qs = _headnorm_bwd([dqn], sv["qg"], row(q_norm[j]) * QK_SCALE, [dgate], f"b{j}_qnorm_bwd")
            d_qnorm[j] = d_qs * QK_SCALE
            dxn = _mm(mcs(dqg), mgc(g_bqg, j), "nt", m2(_sds((S, D), BF16)), tm=1024, tn=512, tk=1024, name=f"b{j}_qg_dx")
            G_bqg = _mm(m2(sv["xn_m"]), mcs(dqg), "tn", mgc(G_bqg, j), alias=True, tm=512, tn=1024, tk=2048, name=f"b{j}_qg_dw")
            dh, d_bnorm[j] = _rms_bwd(dxn, sv["h_m"], row(b_norm[j]), dh, f"b{j}_norm_bwd")
        else:
            dgated = _mm(m2(dh), mgr(g_aout, l), "nt", m2(_sds((S, W), BF16)), tm=1024, tn=512, tk=2048, name=f"a{l}_out_dx")
            G_aout = _mm(m2(sv["gated"]), m2(dh), "tn", mgr(G_aout, l), alias=True, tm=512, tn=512, tk=2048, name=f"a{l}_out_dw")
            dz, d_ws[l], d_bs[l], d_avnorm[l] = _sgu_bwd(sv["zpre"], dgated, wm[l], wmt[l], bcol[l], row(a_vnorm_f[l]), f"a{l}_sgu_bwd")
            dxn = _mm(m2(dz), mgc(g_ain, l), "nt", m2(_sds((S, D), BF16)), tm=1024, tn=512, tk=1024, name=f"a{l}_in_dx")
            G_ain = _mm(m2(sv["xn_m"]), m2(dz), "tn", mgc(G_ain, l), alias=True, tm=512, tn=1024, tk=2048, name=f"a{l}_in_dw")
            dh, d_anorm[l] = _rms_bwd(dxn, sv["h_m"], row(a_norm_f[l]), dh, f"a{l}_norm_bwd")
    grad_x = dh[None]

    full = dict(
        a_norm=jnp.concatenate(d_anorm, axis=0), a_v_norm=jnp.concatenate(d_avnorm, axis=0),
        f_conv_w=jnp.stack(d_cw).transpose(0, 2, 1, 3).reshape(DEPTH, 3, 2 * F),
        a_w_s=jnp.where(tril, jnp.stack(d_ws), 0.0), a_b_s=jnp.stack(d_bs)[..., 0],
        kv_norm=d_kvnorm[0], b_f=d_bf[0, :H], k_norm=d_knorm[0], b_norm=jnp.concatenate(d_bnorm, axis=0),
        q_norm=jnp.concatenate(d_qnorm, axis=0), f_norm=jnp.concatenate(d_fnorm, axis=0),
        f_conv_b=jnp.stack(d_cb).reshape(DEPTH, 2 * F), final_norm=d_final[0])
    shard_rows = []
    for j in range(N_CHIPS):
        pieces = []
        for k in small_sharded:
            n = weights[k].shape[-1]
            pieces.append(full[k][..., j * n:(j + 1) * n])
        shard_rows.append(_pack(pieces, 32))
    rs = shard_rows[0].shape[0]
    repl = _pack([full[k] for k in small_repl], N_CHIPS * 32)
    rr = repl.shape[0] // N_CHIPS
    G_small = jnp.concatenate([jnp.stack(shard_rows), repl.reshape(N_CHIPS, rr, LANE)], axis=1)[:, None]

    Gs = [G_ain, G_aout, G_kvf, G_bqg, G_bout, G_up, G_down, G_small]
    others = _swap_halves(Gs, "grads_swap")
    partial = [_pair_sum(g, o, core, g.dtype, f"grads_pair{i}") for i, (g, o) in enumerate(zip(Gs, others))]
    by_chip = _scatter_chips(partial, "grads_scatter")
    halves = [_sum_chips(q, p, sel, f"grads_sum{i}") for i, (q, p) in enumerate(zip(by_chip, partial))]
    F_ain, F_aout, F_kvf, F_bqg, F_bout, F_up, F_down, F_small = _join_halves(halves, "grads_join")
    repl_buf = lax.dynamic_update_slice(jnp.zeros((N_CHIPS, 1, rr, LANE), F32), F_small[None, :, rs:, :], (me[0], 0, 0, 0))
    (repl_all,) = _gather_chips([repl_buf], "gather_small_grads")

    grads = dict(a_w_in=F_ain, a_w_out=F_aout, w_kvf=F_kvf[0, :, :Ckv], b_w_qg=F_bqg, b_w_out=F_bout, f_w_up=F_up, f_w_down=F_down)
    for k, gk in zip(small_sharded, _unpack(F_small[0, :rs], small_local)):
        grads[k] = gk
    for k, gk in zip(small_repl, _unpack(repl_all.reshape(N_CHIPS * rr, LANE), [weights[k] for k in small_repl])):
        grads[k] = gk

    delta, new_m, new_v = {}, {}, {}
    for k in big:
        delta[k], new_m[k], new_v[k] = _adamw_nd(weights[k], grads[k], mom1[k], mom2[k], f"adamw_{k}")
    small = small_sharded + small_repl
    packed = [_pack([t[k] for k in small], 8) for t in (weights, grads, mom1, mom2)]
    outs = _adamw(*packed, "adamw_small")
    like = [weights[k] for k in small]
    for t, o in zip((delta, new_m, new_v), outs):
        for k, a in zip(small, _unpack(o, like)):
            t[k] = a

    return (loss, grad_x, *[grads[k] for k in names], *[delta[k] for k in names],
            *[new_m[k] for k in names], *[new_v[k] for k in names])
```

```python
import functools
import math

import jax
import jax.numpy as jnp
from jax import lax
from jax.experimental import pallas as pl
from jax.experimental.pallas import tpu as pltpu

F32, BF16 = jnp.float32, jnp.bfloat16
EPS = 1e-6
CHUNK = 128
HEAD = 128
LANE = 128
HALO = 16
N_CHIPS = 4
VMEM_LIMIT = 48 * 1024 * 1024
MESH = pl.DeviceIdType.MESH
ANY = pl.BlockSpec(memory_space=pl.ANY)

ADAM_LR, ADAM_B1, ADAM_B2, ADAM_EPS, ADAM_WD, ADAM_STEP = 0.001, 0.9, 0.999, 1e-08, 0.01, 10
NEG = -1e30
QK_SCALE = HEAD ** -0.5


def _cp(*sem):
    return pltpu.CompilerParams(dimension_semantics=sem, vmem_limit_bytes=VMEM_LIMIT)


def _t(dim, pref, mult=LANE):
    if dim <= pref:
        return dim
    t = (pref // mult) * mult
    while t >= mult:
        if dim % t == 0:
            return t
        t -= mult
    return dim


def _sds(shape, dtype):
    return jax.ShapeDtypeStruct(tuple(shape), dtype)


class Mat:
    def __init__(self, arr, shape, rlim, clim, block, index):
        self.arr, self.shape, self.rlim, self.clim = arr, shape, rlim, clim
        self._block, self._index = block, index

    def spec(self, tr, tc, gmap):
        assert self.rlim % tr == 0 and self.clim % tc == 0, (self.shape, self.rlim, self.clim, tr, tc)
        index = self._index(tr, tc)
        return pl.BlockSpec(self._block(tr, tc), lambda *g: index(*gmap(*g)))


def m2(arr):
    R, C = arr.shape
    return Mat(arr, (R, C), R, C, lambda tr, tc: (tr, tc), lambda tr, tc: (lambda i, j: (i, j)))


def mcs(arr):
    ns, R, Cs = arr.shape
    return Mat(arr, (R, ns * Cs), R, Cs, lambda tr, tc: (None, tr, tc),
               lambda tr, tc: (lambda i, j: (j // (Cs // tc), i, j % (Cs // tc))))


def mhalf(arr, p):
    ns, R, Cs = arr.shape
    return Mat(arr, (R, Cs), R, Cs, lambda tr, tc: (None, tr, tc), lambda tr, tc: (lambda i, j: (p, i, j)))


def mgc(arr, l):
    ns, L, R, Cs = arr.shape
    return Mat(arr, (R, ns * Cs), R, Cs, lambda tr, tc: (None, None, tr, tc),
               lambda tr, tc: (lambda i, j: (j // (Cs // tc), l, i, j % (Cs // tc))))


def mgr(arr, l):
    ns, L, Rs, C = arr.shape
    return Mat(arr, (ns * Rs, C), Rs, C, lambda tr, tc: (None, None, tr, tc),
               lambda tr, tc: (lambda i, j: (i // (Rs // tr), l, i % (Rs // tr), j)))


_DIMS = {"nn": ((1,), (0,)), "nt": ((1,), (1,)), "tn": ((0,), (0,))}


def _mm(a, b, mode, out, *, res=None, alias=False, tm=512, tn=512, tk=4096, name):
    if mode == "tn":
        (K, M), (K2, N) = a.shape, b.shape
        alim_m, alim_k, blim_k, blim_n = a.clim, a.rlim, b.rlim, b.clim
    elif mode == "nt":
        (M, K), (N, K2) = a.shape, b.shape
        alim_m, alim_k, blim_k, blim_n = a.rlim, a.clim, b.clim, b.rlim
    else:
        (M, K), (K2, N) = a.shape, b.shape
        alim_m, alim_k, blim_k, blim_n = a.rlim, a.clim, b.rlim, b.clim
    assert K == K2 and out.shape == (M, N), (name, a.shape, b.shape, out.shape)
    tm = _t(math.gcd(alim_m, out.rlim), tm)
    tn = _t(math.gcd(blim_n, out.clim), tn)
    tk = _t(math.gcd(alim_k, blim_k), tk)
    grid = (M // tm, N // tn, K // tk)
    nk = grid[2]
    if mode == "tn":
        a_spec = a.spec(tk, tm, lambda i, j, k: (k, i))
    else:
        a_spec = a.spec(tm, tk, lambda i, j, k: (i, k))
    if mode == "nt":
        b_spec = b.spec(tn, tk, lambda i, j, k: (j, k))
    else:
        b_spec = b.spec(tk, tn, lambda i, j, k: (k, j))
    o_spec = out.spec(tm, tn, lambda i, j, k: (i, j))
    operands, in_specs = [a.arr, b.arr], [a_spec, b_spec]
    if res is not None:
        operands.append(res.arr)
        in_specs.append(res.spec(tm, tn, lambda i, j, k: (i, j)))
    aliases = {}
    if alias:
        aliases = {len(operands): 0}
        operands.append(out.arr)
        in_specs.append(ANY)
    dims = (_DIMS[mode], ((), ()))
    has_res = res is not None

    def body(*refs):
        a_ref, b_ref = refs[0], refs[1]
        res_ref = refs[2] if has_res else None
        n_in = 2 + has_res + alias
        o_ref = refs[n_in]
        p = lax.dot_general(a_ref[...].astype(BF16), b_ref[...].astype(BF16), dims, preferred_element_type=F32)

        def finish(v):
            if has_res:
                v = v + res_ref[...].astype(F32)
            o_ref[...] = v.astype(o_ref.dtype)

        if nk == 1:
            finish(p)
        else:
            acc = refs[n_in + 1]
            k = pl.program_id(2)

            @pl.when(k == 0)
            def _():
                acc[...] = p

            @pl.when(k > 0)
            def _():
                acc[...] += p

            @pl.when(k == nk - 1)
            def _():
                finish(acc[...])

    return pl.pallas_call(
        body, name=name, grid=grid, in_specs=in_specs, out_specs=o_spec,
        out_shape=_sds(out.arr.shape, out.arr.dtype),
        scratch_shapes=[pltpu.VMEM((tm, tn), F32)] if nk > 1 else [],
        input_output_aliases=aliases,
        compiler_params=_cp("parallel", "parallel", "arbitrary"),
    )(*operands)


def _rms_fwd(x, g, name):
    S, D = x.shape
    tr = _t(S, 512)

    def body(x_ref, g_ref, o_ref):
        xv = x_ref[...]
        r = lax.rsqrt(jnp.mean(xv * xv, axis=-1, keepdims=True) + EPS)
        o_ref[...] = ((xv * r) * g_ref[...]).astype(o_ref.dtype)

    return pl.pallas_call(
        body, name=name, grid=(S // tr,),
        in_specs=[pl.BlockSpec((tr, D), lambda i: (i, 0)), pl.BlockSpec((1, D), lambda i: (0, 0))],
        out_specs=pl.BlockSpec((tr, D), lambda i: (i, 0)), out_shape=_sds((S, D), BF16),
        compiler_params=_cp("parallel"),
    )(x, g)


def _rms_bwd(dy, x, g, dres, name):
    S, D = x.shape
    tr = _t(S, 512)

    def body(dy_ref, x_ref, g_ref, dres_ref, dx_ref, dxb_ref, dg_ref):
        xv = x_ref[...]
        dyv = dy_ref[...].astype(F32)
        r = lax.rsqrt(jnp.mean(xv * xv, axis=-1, keepdims=True) + EPS)
        xh = xv * r
        dxh = dyv * g_ref[...]
        m = jnp.mean(dxh * xh, axis=-1, keepdims=True)
        dx = dres_ref[...] + r * (dxh - xh * m)
        dx_ref[...] = dx
        dxb_ref[...] = dx.astype(dxb_ref.dtype)

        @pl.when(pl.program_id(0) == 0)
        def _():
            dg_ref[...] = jnp.zeros_like(dg_ref)

        dg_ref[...] += jnp.sum(dyv * xh, axis=0, keepdims=True)

    row = pl.BlockSpec((tr, D), lambda i: (i, 0))
    vec = pl.BlockSpec((1, D), lambda i: (0, 0))
    return pl.pallas_call(
        body, name=name, grid=(S // tr,), in_specs=[row, row, vec, row], out_specs=[row, row, vec],
        out_shape=[_sds((S, D), F32), _sds((S, D), BF16), _sds((1, D), F32)], compiler_params=_cp("arbitrary"),
    )(dy, x, g, dres)


def _final_loss(h, g, target, name):
    S, D = h.shape
    tr = _t(S, 512)

    def body(x_ref, g_ref, t_ref, loss_ref, dx_ref, dxb_ref, dg_ref):
        xv = x_ref[...]
        r = lax.rsqrt(jnp.mean(xv * xv, axis=-1, keepdims=True) + EPS)
        xh = xv * r
        err = xh * g_ref[...] - t_ref[...]
        part = 0.5 * jnp.sum(jnp.mean(err * err, axis=-1, keepdims=True), axis=0, keepdims=True)
        dyv = err * (1.0 / D)
        dxh = dyv * g_ref[...]
        m = jnp.mean(dxh * xh, axis=-1, keepdims=True)
        dx = r * (dxh - xh * m)
        dx_ref[...] = dx
        dxb_ref[...] = dx.astype(dxb_ref.dtype)

        @pl.when(pl.program_id(0) == 0)
        def _():
            dg_ref[...] = jnp.zeros_like(dg_ref)
            loss_ref[...] = jnp.zeros_like(loss_ref)

        dg_ref[...] += jnp.sum(dyv * xh, axis=0, keepdims=True)
        loss_ref[...] += part

    row = pl.BlockSpec((tr, D), lambda i: (i, 0))
    vec = pl.BlockSpec((1, D), lambda i: (0, 0))
    one = pl.BlockSpec((1, 1), lambda i: (0, 0))
    return pl.pallas_call(
        body, name=name, grid=(S // tr,), in_specs=[row, vec, row], out_specs=[one, row, row, vec],
        out_shape=[_sds((1, 1), F32), _sds((S, D), F32), _sds((S, D), BF16), _sds((1, D), F32)], compiler_params=_cp("arbitrary"),
    )(h, g, target)


_RSQRT2 = 0.7071067811865476
_RSQRT2PI = 0.3989422804014327


def _gelu(x):
    return 0.5 * x * (1.0 + lax.erf(x * _RSQRT2))


def _gelu_grad(x):
    return 0.5 * (1.0 + lax.erf(x * _RSQRT2)) + x * (jnp.exp(-0.5 * x * x) * _RSQRT2PI)


def _sgu_fwd(zpre, wm, bcol, vnorm, name):
    S, W2 = zpre.shape
    W = W2 // 2
    G = W // CHUNK

    def body(z_ref, wm_ref, b_ref, vn_ref, o_ref):
        zp = z_ref[...].astype(F32)
        u = _gelu(zp[:, :W])
        v = _gelu(zp[:, W:])
        rv = lax.rsqrt(jnp.mean(v * v, axis=-1, keepdims=True) + EPS)
        vn = ((v * rv) * vn_ref[...]).astype(BF16)
        for g in range(G):
            sl = slice(g * CHUNK, (g + 1) * CHUNK)
            mixed = jnp.dot(wm_ref[g], vn[:, sl], preferred_element_type=F32) + b_ref[g]
            o_ref[:, sl] = (u[:, sl] * mixed).astype(o_ref.dtype)

    return pl.pallas_call(
        body, name=name, grid=(S // CHUNK,),
        in_specs=[pl.BlockSpec((CHUNK, W2), lambda i: (i, 0)),
                  pl.BlockSpec((G, CHUNK, CHUNK), lambda i: (0, 0, 0)),
                  pl.BlockSpec((G, CHUNK, 1), lambda i: (0, 0, 0)),
                  pl.BlockSpec((1, W), lambda i: (0, 0))],
        out_specs=pl.BlockSpec((CHUNK, W), lambda i: (i, 0)), out_shape=_sds((S, W), BF16),
        compiler_params=_cp("parallel"),
    )(zpre, wm, bcol, vnorm)


def _sgu_bwd(zpre, dgated, wm, wmt, bcol, vnorm, name):
    S, W2 = zpre.shape
    W = W2 // 2
    G = W // CHUNK

    def body(z_ref, dg_ref, wm_ref, wmt_ref, b_ref, vn_ref, dz_ref, dws_ref, dbs_ref, dvn_ref):
        @pl.when(pl.program_id(0) == 0)
        def _():
            dws_ref[...] = jnp.zeros_like(dws_ref)
            dbs_ref[...] = jnp.zeros_like(dbs_ref)
            dvn_ref[...] = jnp.zeros_like(dvn_ref)

        zp = z_ref[...].astype(F32)
        zu, zv = zp[:, :W], zp[:, W:]
        u = _gelu(zu)
        v = _gelu(zv)
        rv = lax.rsqrt(jnp.mean(v * v, axis=-1, keepdims=True) + EPS)
        vh = v * rv
        vn = (vh * vn_ref[...]).astype(BF16)
        dgv = dg_ref[...].astype(F32)
        du_parts, dvn_parts = [], []
        for g in range(G):
            sl = slice(g * CHUNK, (g + 1) * CHUNK)
            vg = vn[:, sl]
            mixed = jnp.dot(wm_ref[g], vg, preferred_element_type=F32) + b_ref[g]
            dgg = dgv[:, sl]
            du_parts.append(dgg * mixed)
            dmixed = dgg * u[:, sl]
            dbs_ref[g] += jnp.sum(dmixed, axis=1, keepdims=True)
            dmb = dmixed.astype(BF16)
            dws_ref[g] += lax.dot_general(dmb, vg, (_DIMS["nt"], ((), ())), preferred_element_type=F32)
            dvn_parts.append(jnp.dot(wmt_ref[g], dmb, preferred_element_type=F32))
        du = jnp.concatenate(du_parts, axis=1)
        dvn = jnp.concatenate(dvn_parts, axis=1)
        dvn_ref[...] += jnp.sum(dvn * vh, axis=0, keepdims=True)
        dvh = dvn * vn_ref[...]
        dv = rv * (dvh - vh * jnp.mean(dvh * vh, axis=-1, keepdims=True))
        dz_ref[:, :W] = (du * _gelu_grad(zu)).astype(dz_ref.dtype)
        dz_ref[:, W:] = (dv * _gelu_grad(zv)).astype(dz_ref.dtype)

    full3 = lambda shape: pl.BlockSpec(shape, lambda i: (0, 0, 0))
    return pl.pallas_call(
        body, name=name, grid=(S // CHUNK,),
        in_specs=[pl.BlockSpec((CHUNK, W2), lambda i: (i, 0)), pl.BlockSpec((CHUNK, W), lambda i: (i, 0)),
                  full3((G, CHUNK, CHUNK)), full3((G, CHUNK, CHUNK)), full3((G, CHUNK, 1)),
                  pl.BlockSpec((1, W), lambda i: (0, 0))],
        out_specs=[pl.BlockSpec((CHUNK, W2), lambda i: (i, 0)), full3((G, CHUNK, CHUNK)), full3((G, CHUNK, 1)),
                   pl.BlockSpec((1, W), lambda i: (0, 0))],
        out_shape=[_sds((S, W2), BF16), _sds((G, CHUNK, CHUNK), F32), _sds((G, CHUNK, 1), F32), _sds((1, W), F32)],
        compiler_params=_cp("arbitrary"),
    )(zpre, dgated, wm, wmt, bcol, vnorm)


def _conv_taps(h_ref, half, r0, R, tc):
    if r0 == 0:
        xe = jnp.concatenate([jnp.zeros((HALO, tc), F32), h_ref[half, 0:R, :].astype(F32)], axis=0)
    else:
        xe = h_ref[half, r0 - HALO:r0 + R, :].astype(F32)
    return xe[HALO:], pltpu.roll(xe, 1, 0)[HALO:], pltpu.roll(xe, 2, 0)[HALO:]


def _conv_apply(taps, w, b):
    x0, x1, x2 = taps
    return x2 * w[0:1] + x1 * w[1:2] + x0 * w[2:3] + b


def _convgate_fwd(hup, cw, cb, name):
    _, S, F = hup.shape
    tc = _t(F, 256)
    R = _t(S, 512)

    def body(h_ref, w_ref, b_ref, o_ref):
        for r0 in range(0, S, R):
            gate = _conv_apply(_conv_taps(h_ref, 0, r0, R, tc), w_ref[0], b_ref[0])
            val = _conv_apply(_conv_taps(h_ref, 1, r0, R, tc), w_ref[1], b_ref[1])
            o_ref[r0:r0 + R, :] = (gate * jax.nn.sigmoid(gate) * val).astype(o_ref.dtype)

    return pl.pallas_call(
        body, name=name, grid=(F // tc,),
        in_specs=[pl.BlockSpec((2, S, tc), lambda j: (0, 0, j)), pl.BlockSpec((2, 3, tc), lambda j: (0, 0, j)),
                  pl.BlockSpec((2, 1, tc), lambda j: (0, 0, j))],
        out_specs=pl.BlockSpec((S, tc), lambda j: (0, j)), out_shape=_sds((S, F), BF16),
        compiler_params=_cp("parallel"),
    )(hup, cw, cb)


def _convgate_bwd(hup, dact, cw, cb, name):
    _, S, F = hup.shape
    tc = _t(F, 256)
    R = _t(S, 512)

    def body(h_ref, da_ref, w_ref, b_ref, dh_ref, dw_ref, db_ref, dhc):
        dhc[:, S:S + HALO, :] = jnp.zeros((2, HALO, tc), F32)
        dw_acc = [[jnp.zeros((1, tc), F32) for _ in range(3)] for _ in range(2)]
        db_acc = [jnp.zeros((1, tc), F32) for _ in range(2)]
        for r0 in range(0, S, R):
            taps = [_conv_taps(h_ref, p, r0, R, tc) for p in range(2)]
            gate = _conv_apply(taps[0], w_ref[0], b_ref[0])
            val = _conv_apply(taps[1], w_ref[1], b_ref[1])
            da = da_ref[r0:r0 + R, :].astype(F32)
            sg = jax.nn.sigmoid(gate)
            d = [da * val * (sg * (1.0 + gate * (1.0 - sg))), da * (gate * sg)]
            for p in range(2):
                dhc[p, r0:r0 + R, :] = d[p]
                db_acc[p] = db_acc[p] + jnp.sum(d[p], axis=0, keepdims=True)
                for k in range(3):
                    dw_acc[p][k] = dw_acc[p][k] + jnp.sum(d[p] * taps[p][2 - k], axis=0, keepdims=True)
        for p in range(2):
            db_ref[p] = db_acc[p]
            dw_ref[p] = jnp.concatenate(dw_acc[p], axis=0)
            w = w_ref[p]
            for r0 in range(0, S, R):
                de = dhc[p, r0:r0 + R + HALO, :]
                d1 = pltpu.roll(de, R + HALO - 1, 0)[:R]
                d2 = pltpu.roll(de, R + HALO - 2, 0)[:R]
                dh_ref[p, r0:r0 + R, :] = (de[:R] * w[2:3] + d1 * w[1:2] + d2 * w[0:1]).astype(dh_ref.dtype)

    blk = lambda rows: pl.BlockSpec((2, rows, tc), lambda j: (0, 0, j))
    return pl.pallas_call(
        body, name=name, grid=(F // tc,),
        in_specs=[blk(S), pl.BlockSpec((S, tc), lambda j: (0, j)), blk(3), blk(1)],
        out_specs=[blk(S), blk(3), blk(1)],
        out_shape=[_sds((2, S, F), BF16), _sds((2, 3, F), F32), _sds((2, 1, F), F32)],
        scratch_shapes=[pltpu.VMEM((2, S + HALO, tc), F32)],
        compiler_params=_cp("parallel"),
    )(hup, dact, cw, cb)


def _headnorm_fwd(x3, part, gain, name):
    _, S, W = x3.shape
    tr = _t(S, 512)

    def body(x_ref, g_ref, o_ref):
        xv = x_ref[...].astype(F32)
        for h in range(W // HEAD):
            sl = slice(h * HEAD, (h + 1) * HEAD)
            xh = xv[:, sl]
            r = lax.rsqrt(jnp.mean(xh * xh, axis=-1, keepdims=True) + EPS)
            o_ref[:, sl] = ((xh * r) * g_ref[...]).astype(o_ref.dtype)

    return pl.pallas_call(
        body, name=name, grid=(S // tr,),
        in_specs=[pl.BlockSpec((None, tr, W), lambda i: (part, i, 0)), pl.BlockSpec((1, HEAD), lambda i: (0, 0))],
        out_specs=pl.BlockSpec((tr, W), lambda i: (i, 0)), out_shape=_sds((S, W), BF16),
        compiler_params=_cp("parallel"),
    )(x3, gain)


def _headnorm_bwd(dys, x3, gain, passes, name):
    _, S, W = x3.shape
    tr = _t(S, 256)
    nd, npass = len(dys), len(passes)

    def body(*refs):
        dy_refs = refs[:nd]
        x_ref, g_ref = refs[nd], refs[nd + 1]
        p_refs = refs[nd + 2:nd + 2 + npass]
        o_ref, dg_ref = refs[nd + 2 + npass], refs[nd + 3 + npass]

        @pl.when(pl.program_id(0) == 0)
        def _():
            dg_ref[...] = jnp.zeros_like(dg_ref)

        xv = x_ref[...].astype(F32)
        dyv = dy_refs[0][...].astype(F32)
        for r in dy_refs[1:]:
            dyv = dyv + r[...].astype(F32)
        dg = jnp.zeros((1, HEAD), F32)
        for h in range(W // HEAD):
            sl = slice(h * HEAD, (h + 1) * HEAD)
            xh = xv[:, sl]
            r = lax.rsqrt(jnp.mean(xh * xh, axis=-1, keepdims=True) + EPS)
            xh = xh * r
            dyh = dyv[:, sl]
            dg = dg + jnp.sum(dyh * xh, axis=0, keepdims=True)
            dxh = dyh * g_ref[...]
            o_ref[0, :, sl] = (r * (dxh - xh * jnp.mean(dxh * xh, axis=-1, keepdims=True))).astype(o_ref.dtype)
        dg_ref[...] += dg
        pv = p_refs[0][...].astype(F32)
        for r in p_refs[1:]:
            pv = pv + r[...].astype(F32)
        o_ref[1] = pv.astype(o_ref.dtype)

    row = pl.BlockSpec((tr, W), lambda i: (i, 0))
    vec = pl.BlockSpec((1, HEAD), lambda i: (0, 0))
    return pl.pallas_call(
        body, name=name, grid=(S // tr,),
        in_specs=[row] * nd + [pl.BlockSpec((None, tr, W), lambda i: (0, i, 0)), vec] + [row] * npass,
        out_specs=[pl.BlockSpec((2, tr, W), lambda i: (0, i, 0)), vec],
        out_shape=[_sds((2, S, W), BF16), _sds((1, HEAD), F32)], compiler_params=_cp("arbitrary"),
    )(*dys, x3, gain, *passes)


def _gate_fwd(o, qg, name):
    S, W = o.shape
    tr = _t(S, 512)

    def body(o_ref, g_ref, y_ref):
        y_ref[...] = (o_ref[...].astype(F32) * jax.nn.sigmoid(g_ref[...].astype(F32))).astype(y_ref.dtype)

    row = pl.BlockSpec((tr, W), lambda i: (i, 0))
    return pl.pallas_call(
        body, name=name, grid=(S // tr,), in_specs=[row, pl.BlockSpec((None, tr, W), lambda i: (1, i, 0))],
        out_specs=row, out_shape=_sds((S, W), BF16), compiler_params=_cp("parallel"),
    )(o, qg)


def _gate_bwd(dog, o, qg, name):
    S, W = o.shape
    H = W // HEAD
    tr = _t(S, 512)

    def body(dy_ref, o_ref, g_ref, do_ref, dg_ref, dl_ref):
        sg = jax.nn.sigmoid(g_ref[...].astype(F32))
        dy = dy_ref[...].astype(F32)
        ov = o_ref[...].astype(F32)
        dob = (dy * sg).astype(do_ref.dtype)
        do_ref[...] = dob
        dg_ref[...] = (dy * ov * (sg * (1.0 - sg))).astype(dg_ref.dtype)
        prod = dob.astype(F32) * ov
        for h in range(H):
            dl_ref[h] = jnp.sum(prod[:, h * HEAD:(h + 1) * HEAD], axis=-1, keepdims=True)

    row = pl.BlockSpec((tr, W), lambda i: (i, 0))
    return pl.pallas_call(
        body, name=name, grid=(S // tr,), in_specs=[row, row, pl.BlockSpec((None, tr, W), lambda i: (1, i, 0))],
        out_specs=[row, row, pl.BlockSpec((H, tr, 1), lambda i: (0, i, 0))],
        out_shape=[_sds((S, W), BF16), _sds((S, W), BF16), _sds((H, S, 1), F32)], compiler_params=_cp("parallel"),
    )(dog, o, qg)


def _logf_cumsum(fpre, bf, name):
    S, C = fpre.shape
    n = S // CHUNK

    def body(f_ref, b_ref, c_ref, carry):
        @pl.when(pl.program_id(0) == 0)
        def _():
            carry[...] = jnp.zeros_like(carry)

        lf = jax.nn.log_sigmoid(f_ref[...] + b_ref[...])
        tri = (lax.broadcasted_iota(jnp.int32, (CHUNK, CHUNK), 0)
               >= lax.broadcasted_iota(jnp.int32, (CHUNK, CHUNK), 1)).astype(F32)
        c_ref[...] = jnp.dot(tri, lf, preferred_element_type=F32, precision=lax.Precision.HIGHEST) + carry[...]
        carry[...] += jnp.sum(lf, axis=0, keepdims=True)

    return pl.pallas_call(
        body, name=name, grid=(n,),
        in_specs=[pl.BlockSpec((CHUNK, C), lambda i: (i, 0)), pl.BlockSpec((1, C), lambda i: (0, 0))],
        out_specs=pl.BlockSpec((CHUNK, C), lambda i: (i, 0)), out_shape=_sds((S, C), F32),
        scratch_shapes=[pltpu.VMEM((1, C), F32)], compiler_params=_cp("arbitrary"),
    )(fpre, bf)


def _logf_cumsum_bwd(dc, fpre, bf, name):
    S, C = fpre.shape
    n = S // CHUNK

    def body(dc_ref, f_ref, b_ref, df_ref, db_ref, carry):
        @pl.when(pl.program_id(0) == 0)
        def _():
            carry[...] = jnp.zeros_like(carry)
            db_ref[...] = jnp.zeros_like(db_ref)

        dcv = dc_ref[...]
        tri = (lax.broadcasted_iota(jnp.int32, (CHUNK, CHUNK), 0)
               <= lax.broadcasted_iota(jnp.int32, (CHUNK, CHUNK), 1)).astype(F32)
        dlf = jnp.dot(tri, dcv, preferred_element_type=F32, precision=lax.Precision.HIGHEST) + carry[...]
        carry[...] += jnp.sum(dcv, axis=0, keepdims=True)
        df = dlf * jax.nn.sigmoid(-(f_ref[...] + b_ref[...]))
        df_ref[...] = df.astype(df_ref.dtype)
        db_ref[...] += jnp.sum(df, axis=0, keepdims=True)

    rev = pl.BlockSpec((CHUNK, C), lambda i: (n - 1 - i, 0))
    vec = pl.BlockSpec((1, C), lambda i: (0, 0))
    return pl.pallas_call(
        body, name=name, grid=(n,), in_specs=[rev, rev, vec], out_specs=[rev, vec],
        out_shape=[_sds((S, C), BF16), _sds((1, C), F32)],
        scratch_shapes=[pltpu.VMEM((1, C), F32)], compiler_params=_cp("arbitrary"),
    )(dc, fpre, bf)


def _attn_tiles(S):
    return _t(S, 1024), _t(S, 512)


def _scores(q, k, ck, off, tq, tk, masked):
    s = lax.dot_general(q, k, (_DIMS["nt"], ((), ())), preferred_element_type=F32) - ck
    if masked:
        d = lax.broadcasted_iota(jnp.int32, (tq, tk), 1) - lax.broadcasted_iota(jnp.int32, (tq, tk), 0)
        s = jnp.where(d <= off, s, NEG)
    return s


def _attn_fwd(qn, kn, kv, crow, name):
    S, W = qn.shape
    H = W // HEAD
    tq, tk = _attn_tiles(S)
    nq, nk = S // tq, S // tk
    last = lambda i: ((i + 1) * tq - 1) // tk

    def body(q_ref, k_ref, v_ref, ck_ref, o_ref, lse_ref, m_sc, l_sc, acc_sc):
        qi, kj = pl.program_id(1), pl.program_id(2)

        @pl.when(kj == 0)
        def _():
            m_sc[...] = jnp.full_like(m_sc, NEG)
            l_sc[...] = jnp.zeros_like(l_sc)
            acc_sc[...] = jnp.zeros_like(acc_sc)

        def step(masked):
            s = _scores(q_ref[...], k_ref[...], ck_ref[...], qi * tq - kj * tk, tq, tk, masked)
            m_new = jnp.maximum(m_sc[...], jnp.max(s, axis=-1, keepdims=True))
            alpha = jnp.exp(m_sc[...] - m_new)
            p = jnp.exp(s - m_new)
            l_sc[...] = alpha * l_sc[...] + jnp.sum(p, axis=-1, keepdims=True)
            acc_sc[...] = alpha * acc_sc[...] + jnp.dot(p.astype(BF16), v_ref[...], preferred_element_type=F32)
            m_sc[...] = m_new

        @pl.when(kj <= last(qi))
        def _():
            step(True)

        @pl.when(kj == nk - 1)
        def _():
            o_ref[...] = (acc_sc[...] / l_sc[...]).astype(o_ref.dtype)
            lse_ref[...] = m_sc[...] + jnp.log(l_sc[...])

    return pl.pallas_call(
        body, name=name, grid=(H, nq, nk),
        in_specs=[pl.BlockSpec((tq, HEAD), lambda h, i, j: (i, h)),
                  pl.BlockSpec((tk, HEAD), lambda h, i, j: (jnp.minimum(j, last(i)), h)),
                  pl.BlockSpec((None, tk, HEAD), lambda h, i, j: (1, jnp.minimum(j, last(i)), h)),
                  pl.BlockSpec((None, 1, tk), lambda h, i, j: (h, 0, jnp.minimum(j, last(i))))],
        out_specs=[pl.BlockSpec((tq, HEAD), lambda h, i, j: (i, h)),
                   pl.BlockSpec((None, tq, 1), lambda h, i, j: (h, i, 0))],
        out_shape=[_sds((S, W), BF16), _sds((H, S, 1), F32)],
        scratch_shapes=[pltpu.VMEM((tq, 1), F32), pltpu.VMEM((tq, 1), F32), pltpu.VMEM((tq, HEAD), F32)],
        compiler_params=_cp("parallel", "parallel", "arbitrary"),
    )(qn, kn, kv, crow)


def _attn_bwd(qn, kn, kv, crow, do, lse, delta, name):
    S, W = qn.shape
    H = W // HEAD
    tq, tk = _attn_tiles(S)
    nq, nk = S // tq, S // tk
    first = lambda j: (j * tk) // tq

    def body(q_ref, k_ref, v_ref, ck_ref, do_ref, lse_ref, dl_ref, dq_ref, dr_ref, dk_ref, dv_ref, dc_ref, dk_sc, dv_sc, dc_sc):
        kj, qi = pl.program_id(1), pl.program_id(2)

        @pl.when(jnp.logical_and(kj == 0, qi == 0))
        def _():
            dq_ref[...] = jnp.zeros_like(dq_ref)
            dr_ref[...] = jnp.zeros_like(dr_ref)

        @pl.when(qi == 0)
        def _():
            dk_sc[...] = jnp.zeros_like(dk_sc)
            dv_sc[...] = jnp.zeros_like(dv_sc)
            dc_sc[...] = jnp.zeros_like(dc_sc)

        def step(masked):
            q, k, dov = q_ref[...], k_ref[...], do_ref[...]
            s = _scores(q, k, ck_ref[...], qi * tq - kj * tk, tq, tk, masked)
            p = jnp.exp(s - lse_ref[...])
            dp = lax.dot_general(dov, v_ref[...], (_DIMS["nt"], ((), ())), preferred_element_type=F32)
            ds = p * (dp - dl_ref[...])
            dsb = ds.astype(BF16)
            dc_sc[...] += jnp.sum(ds, axis=0, keepdims=True)
            dv_sc[...] += lax.dot_general(p.astype(BF16), dov, (_DIMS["tn"], ((), ())), preferred_element_type=F32)
            dk_sc[...] += lax.dot_general(dsb, q, (_DIMS["tn"], ((), ())), preferred_element_type=F32)
            rows = pl.ds(pl.multiple_of(qi * tq, tq), tq)
            dq_ref[rows, :] += jnp.dot(dsb, k, preferred_element_type=F32)
            dr_ref[rows, :] += jnp.sum(ds, axis=1, keepdims=True)

        below = (kj + 1) * tk - 1 <= qi * tq

        @pl.when(below)
        def _():
            step(False)

        @pl.when(jnp.logical_and(qi >= first(kj), jnp.logical_not(below)))
        def _():
            step(True)

        @pl.when(qi == nq - 1)
        def _():
            dk_ref[...] = dk_sc[...]
            dv_ref[...] = dv_sc[...]
            dc_ref[...] = dc_sc[...]

    qblk = pl.BlockSpec((tq, HEAD), lambda h, j, i: (jnp.maximum(i, first(j)), h))
    qcol = pl.BlockSpec((None, tq, 1), lambda h, j, i: (h, jnp.maximum(i, first(j)), 0))
    kblk = pl.BlockSpec((tk, HEAD), lambda h, j, i: (j, h))
    krow = pl.BlockSpec((None, 1, tk), lambda h, j, i: (h, 0, j))
    return pl.pallas_call(
        body, name=name, grid=(H, nk, nq),
        in_specs=[qblk, kblk, pl.BlockSpec((None, tk, HEAD), lambda h, j, i: (1, j, h)), krow, qblk, qcol, qcol],
        out_specs=[pl.BlockSpec((S, HEAD), lambda h, j, i: (0, h)), pl.BlockSpec((None, S, 1), lambda h, j, i: (h, 0, 0)),
                   kblk, kblk, krow],
        out_shape=[_sds((S, W), F32), _sds((H, S, 1), F32), _sds((S, W), F32), _sds((S, W), F32), _sds((H, 1, S), F32)],
        scratch_shapes=[pltpu.VMEM((tk, HEAD), F32), pltpu.VMEM((tk, HEAD), F32), pltpu.VMEM((1, tk), F32)],
        compiler_params=_cp("parallel", "arbitrary", "arbitrary"),
    )(qn, kn, kv, crow, do, lse, delta)


def _adamw(w, g, m, v, name):
    R, C = w.shape
    tr = _t(R, max(8, (1 << 19) // max(C, 1)), 8)
    c1 = 1.0 - ADAM_B1 ** ADAM_STEP
    c2 = 1.0 - ADAM_B2 ** ADAM_STEP

    def body(w_ref, g_ref, m_ref, v_ref, d_ref, nm_ref, nv_ref):
        gv = g_ref[...]
        nm = ADAM_B1 * m_ref[...] + (1.0 - ADAM_B1) * gv
        nv = ADAM_B2 * v_ref[...] + (1.0 - ADAM_B2) * (gv * gv)
        nm_ref[...] = nm
        nv_ref[...] = nv
        d_ref[...] = -ADAM_LR * ((nm / c1) / (jnp.sqrt(nv / c2) + ADAM_EPS) + ADAM_WD * w_ref[...])

    blk = pl.BlockSpec((tr, C), lambda i: (i, 0))
    return pl.pallas_call(
        body, name=name, grid=(R // tr,), in_specs=[blk] * 4, out_specs=[blk] * 3,
        out_shape=[_sds((R, C), F32)] * 3, compiler_params=_cp("parallel"),
    )(w, g, m, v)


def _place():
    x, y, c = lax.axis_index("x"), lax.axis_index("y"), lax.axis_index("c")
    chips = [(1 - x, y), (x, 1 - y), (1 - x, 1 - y)]
    return x, y, c, chips


def _place_part(part, me, dtype, name):
    L, R, C = part.shape
    tr = _t(R, max(16, (1 << 20) // C), 16)

    def body(me_ref, x_ref, o_ref):
        o_ref[...] = x_ref[...].astype(o_ref.dtype)

    return pl.pallas_call(
        body, name=name,
        grid_spec=pltpu.PrefetchScalarGridSpec(
            num_scalar_prefetch=1, grid=(L, R // tr),
            in_specs=[pl.BlockSpec((None, tr, C), lambda l, r, m: (l, r, 0))],
            out_specs=pl.BlockSpec((None, None, tr, C), lambda l, r, m: (m[0], l, r, 0))),
        out_shape=_sds((N_CHIPS, L, R, C), dtype), compiler_params=_cp("parallel", "parallel"),
    )(me, part)


def _gather_chips(bufs, name):
    n = len(bufs)

    def body(*refs):
        outs = refs[n:2 * n]
        send_sems, recv_sems = refs[2 * n:]
        x, y, c, chips = _place()
        me = 2 * x + y
        sibling = (x, y, 1 - c)

        def copy(i, k, chip_index, core, to):
            h = outs[i].shape[2] // 2
            rows = outs[i].at[chip_index, :, pl.ds(core * h, h), :]
            return pltpu.make_async_remote_copy(
                src_ref=rows, dst_ref=rows, send_sem=send_sems.at[i, k], recv_sem=recv_sems.at[i, k],
                device_id=to, device_id_type=MESH)

        sent = []
        for i in range(n):
            for j, chip in enumerate(chips):
                cp = copy(i, j, me, c, (*chip, c))
                cp.start()
                sent.append(cp)
        for i in range(n):
            for j, chip in enumerate(chips):
                idx = 2 * chip[0] + chip[1]
                copy(i, j, idx, c, (x, y, c)).wait_recv()
                fw = copy(i, 3 + j, idx, c, sibling)
                fw.start()
                sent.append(fw)
        for i in range(n):
            for j, chip in enumerate(chips):
                copy(i, 3 + j, 2 * chip[0] + chip[1], 1 - c, (x, y, c)).wait_recv()
        for cp in sent:
            cp.wait_send()

    return pl.pallas_call(
        body, name=name, in_specs=[ANY] * n, out_specs=[ANY] * n,
        out_shape=[_sds(b.shape, b.dtype) for b in bufs],
        scratch_shapes=[pltpu.SemaphoreType.DMA((n, 6)), pltpu.SemaphoreType.DMA((n, 6))],
        input_output_aliases={i: i for i in range(n)},
        compiler_params=pltpu.CompilerParams(has_side_effects=True),
    )(*bufs)


def _swap_halves(gs, name):
    n = len(gs)

    def body(*refs):
        ins, outs = refs[:n], refs[n:2 * n]
        send_sems, recv_sems = refs[2 * n:]
        x, y, c, _ = _place()
        cps = []
        for i in range(n):
            h = ins[i].shape[2] // 2
            cp = pltpu.make_async_remote_copy(
                src_ref=ins[i].at[:, :, pl.ds((1 - c) * h, h), :], dst_ref=outs[i], send_sem=send_sems.at[i],
                recv_sem=recv_sems.at[i], device_id=(x, y, 1 - c), device_id_type=MESH)
            cp.start()
            cps.append(cp)
        for cp in cps:
            cp.wait()

    return pl.pallas_call(
        body, name=name, in_specs=[ANY] * n, out_specs=[ANY] * n,
        out_shape=[_sds(g.shape[:2] + (g.shape[2] // 2, g.shape[3]), g.dtype) for g in gs],
        scratch_shapes=[pltpu.SemaphoreType.DMA((n,)), pltpu.SemaphoreType.DMA((n,))],
        compiler_params=pltpu.CompilerParams(has_side_effects=True),
    )(*gs)


def _pair_sum(g, gs, core, out_dtype, name):
    ns, L, R, C = g.shape
    h = R // 2
    th = _t(h, max(16, (1 << 20) // C), 16)
    nb = h // th

    def body(core_ref, a_ref, b_ref, o_ref):
        o_ref[...] = (a_ref[...].astype(F32) + b_ref[...].astype(F32)).astype(o_ref.dtype)

    blk = (None, None, th, C)
    return pl.pallas_call(
        body, name=name,
        grid_spec=pltpu.PrefetchScalarGridSpec(
            num_scalar_prefetch=1, grid=(ns, L, nb),
            in_specs=[pl.BlockSpec(blk, lambda o, l, r, cr: (o, l, cr[0] * nb + r, 0)),
                      pl.BlockSpec(blk, lambda o, l, r, cr: (o, l, r, 0))],
            out_specs=pl.BlockSpec(blk, lambda o, l, r, cr: (o, l, r, 0))),
        out_shape=_sds((ns, L, h, C), out_dtype), compiler_params=_cp("parallel", "parallel", "parallel"),
    )(core, g, gs)


def _scatter_chips(ps, name):
    n = len(ps)

    def body(*refs):
        ins, outs = refs[:n], refs[n:2 * n]
        send_sems, recv_sems = refs[2 * n:]
        x, y, c, chips = _place()
        sent = []
        for i in range(n):
            for j, chip in enumerate(chips):
                cp = pltpu.make_async_remote_copy(
                    src_ref=ins[i].at[2 * chip[0] + chip[1]], dst_ref=outs[i].at[j], send_sem=send_sems.at[i, j],
                    recv_sem=recv_sems.at[i, j], device_id=(*chip, c), device_id_type=MESH)
                cp.start()
                sent.append(cp)
        for i in range(n):
            for j in range(len(chips)):
                slot = outs[i].at[j]
                pltpu.make_async_remote_copy(
                    src_ref=slot, dst_ref=slot, send_sem=send_sems.at[i, j], recv_sem=recv_sems.at[i, j],
                    device_id=(x, y, c), device_id_type=MESH).wait_recv()
        for cp in sent:
            cp.wait_send()

    return pl.pallas_call(
        body, name=name, in_specs=[ANY] * n, out_specs=[ANY] * n,
        out_shape=[_sds((N_CHIPS - 1,) + p.shape[1:], p.dtype) for p in ps],
        scratch_shapes=[pltpu.SemaphoreType.DMA((n, 3)), pltpu.SemaphoreType.DMA((n, 3))],
        compiler_params=pltpu.CompilerParams(has_side_effects=True),
    )(*ps)


def _sum_chips(q, p, me, core, name):
    _, L, h, C = q.shape
    th = _t(h, max(16, (1 << 19) // C), 16)
    nb = h // th
    blk = (None, None, th, C)

    def body(me_ref, core_ref, p_ref, q0, q1, q2, o_ref):
        o_ref[...] = ((p_ref[...].astype(F32) + q0[...].astype(F32)) + q1[...].astype(F32)) + q2[...].astype(F32)

    return pl.pallas_call(
        body, name=name,
        grid_spec=pltpu.PrefetchScalarGridSpec(
            num_scalar_prefetch=2, grid=(L, nb),
            in_specs=[pl.BlockSpec(blk, lambda l, r, m, c: (m[0], l, r, 0))]
            + [pl.BlockSpec(blk, functools.partial(lambda k, l, r, m, c: (k, l, r, 0), k)) for k in range(N_CHIPS - 1)],
            out_specs=pl.BlockSpec((None, th, C), lambda l, r, m, c: (l, c[0] * nb + r, 0))),
        out_shape=_sds((L, 2 * h, C), F32), compiler_params=_cp("parallel", "parallel"),
    )(me, core, p, q, q, q)


def _sum_small(own, q, name):
    def body(p_ref, q_ref, o_ref):
        o_ref[...] = ((p_ref[...] + q_ref[0]) + q_ref[1]) + q_ref[2]

    return pl.pallas_call(body, name=name, out_shape=_sds(own.shape, F32))(own, q)


def _join_halves(bufs, name):
    n = len(bufs)

    def body(*refs):
        outs = refs[n:2 * n]
        send_sems, recv_sems = refs[2 * n:]
        x, y, c, _ = _place()
        cps = []
        for i in range(n):
            h = outs[i].shape[1] // 2
            mine = outs[i].at[:, pl.ds(c * h, h), :]
            cp = pltpu.make_async_remote_copy(
                src_ref=mine, dst_ref=mine, send_sem=send_sems.at[i], recv_sem=recv_sems.at[i],
                device_id=(x, y, 1 - c), device_id_type=MESH)
            cp.start()
            cps.append(cp)
        for cp in cps:
            cp.wait()

    return pl.pallas_call(
        body, name=name, in_specs=[ANY] * n, out_specs=[ANY] * n,
        out_shape=[_sds(b.shape, b.dtype) for b in bufs],
        scratch_shapes=[pltpu.SemaphoreType.DMA((n,)), pltpu.SemaphoreType.DMA((n,))],
        input_output_aliases={i: i for i in range(n)},
        compiler_params=pltpu.CompilerParams(has_side_effects=True),
    )(*bufs)


def _pack(arrs, rows_mult):
    flat = jnp.concatenate([a.reshape(-1).astype(F32) for a in arrs])
    rows = -(-flat.size // LANE)
    rows = -(-rows // rows_mult) * rows_mult
    return jnp.pad(flat, (0, rows * LANE - flat.size)).reshape(rows, LANE)


def _unpack(packed, like):
    flat = packed.reshape(-1)
    out, pos = [], 0
    for a in like:
        n = math.prod(a.shape)
        out.append(flat[pos:pos + n].reshape(a.shape))
        pos += n
    return out


def _adamw_nd(w, g, m, v, name):
    shape = w.shape
    C = shape[-1]
    d, nm, nv = _adamw(w.reshape(-1, C), g.reshape(-1, C), m.reshape(-1, C), v.reshape(-1, C), name)
    return d.reshape(shape), nm.reshape(shape), nv.reshape(shape)


def kernel(x, a_norm, a_w_in, a_v_norm, a_w_s, a_b_s, a_w_out, kv_norm, w_kvf, b_f, k_norm, b_norm, b_w_qg, q_norm, b_w_out, f_norm, f_w_up, f_conv_w, f_conv_b, f_w_down, final_norm, loss_target, m_a_norm, m_a_w_in, m_a_v_norm, m_a_w_s, m_a_b_s, m_a_w_out, m_kv_norm, m_w_kvf, m_b_f, m_k_norm, m_b_norm, m_b_w_qg, m_q_norm, m_b_w_out, m_f_norm, m_f_w_up, m_f_conv_w, m_f_conv_b, m_f_w_down, m_final_norm, v_a_norm, v_a_w_in, v_a_v_norm, v_a_w_s, v_a_b_s, v_a_w_out, v_kv_norm, v_w_kvf, v_b_f, v_k_norm, v_b_norm, v_b_w_qg, v_q_norm, v_b_w_out, v_f_norm, v_f_w_up, v_f_conv_w, v_f_conv_b, v_f_w_down, v_final_norm):
    weights = dict(a_norm=a_norm, a_w_in=a_w_in, a_v_norm=a_v_norm, a_w_s=a_w_s, a_b_s=a_b_s, a_w_out=a_w_out, kv_norm=kv_norm, w_kvf=w_kvf, b_f=b_f, k_norm=k_norm, b_norm=b_norm, b_w_qg=b_w_qg, q_norm=q_norm, b_w_out=b_w_out, f_norm=f_norm, f_w_up=f_w_up, f_conv_w=f_conv_w, f_conv_b=f_conv_b, f_w_down=f_w_down, final_norm=final_norm)
    mom1 = dict(a_norm=m_a_norm, a_w_in=m_a_w_in, a_v_norm=m_a_v_norm, a_w_s=m_a_w_s, a_b_s=m_a_b_s, a_w_out=m_a_w_out, kv_norm=m_kv_norm, w_kvf=m_w_kvf, b_f=m_b_f, k_norm=m_k_norm, b_norm=m_b_norm, b_w_qg=m_b_w_qg, q_norm=m_q_norm, b_w_out=m_b_w_out, f_norm=m_f_norm, f_w_up=m_f_w_up, f_conv_w=m_f_conv_w, f_conv_b=m_f_conv_b, f_w_down=m_f_w_down, final_norm=m_final_norm)
    mom2 = dict(a_norm=v_a_norm, a_w_in=v_a_w_in, a_v_norm=v_a_v_norm, a_w_s=v_a_w_s, a_b_s=v_a_b_s, a_w_out=v_a_w_out, kv_norm=v_kv_norm, w_kvf=v_w_kvf, b_f=v_b_f, k_norm=v_k_norm, b_norm=v_b_norm, b_w_qg=v_b_w_qg, q_norm=v_q_norm, b_w_out=v_b_w_out, f_norm=v_f_norm, f_w_up=v_f_w_up, f_conv_w=v_f_conv_w, f_conv_b=v_f_conv_b, f_w_down=v_f_w_down, final_norm=v_final_norm)
    names = list(weights)
    big = ["a_w_in", "a_w_out", "w_kvf", "b_w_qg", "b_w_out", "f_w_up", "f_w_down"]
    small_sharded = ["a_norm", "a_v_norm", "f_conv_w"]
    small_repl = ["a_w_s", "a_b_s", "kv_norm", "b_f", "k_norm", "b_norm", "q_norm", "f_norm", "f_conv_b", "final_norm"]

    _, S, D = x.shape
    NA, NB, DEPTH = a_norm.shape[0], b_norm.shape[0], f_norm.shape[0]
    W = a_w_out.shape[1] * N_CHIPS
    G = a_w_s.shape[1]
    H = b_f.shape[0]
    ATT = H * HEAD
    F = f_w_down.shape[1] * N_CHIPS
    Ckv = w_kvf.shape[1]
    Cp = -(-Ckv // LANE) * LANE
    assert W == G * CHUNK and Ckv * N_CHIPS == 2 * ATT + H and S % CHUNK == 0
    core = lax.axis_index("c").astype(jnp.int32).reshape(1)
    me = (2 * lax.axis_index("x") + lax.axis_index("y")).astype(jnp.int32).reshape(1)

    small_local = [weights[k] for k in small_sharded]
    parts = dict(a_w_in=a_w_in, a_w_out=a_w_out, w_kvf=jnp.pad(w_kvf, ((0, 0), (0, Cp - Ckv)))[None],
                 b_w_qg=b_w_qg, b_w_out=b_w_out, f_w_up=f_w_up, f_w_down=f_w_down)
    bufs = [_place_part(p, me, BF16, f"place_{k}") for k, p in parts.items()]
    bufs.append(_place_part(_pack(small_local, 32)[None], me, F32, "place_small"))
    g_ain, g_aout, g_kvf, g_bqg, g_bout, g_up, g_down, g_small = _gather_chips(bufs, "gather_weights")

    per_chip = [_unpack(g_small[j, 0], small_local) for j in range(N_CHIPS)]
    a_norm_f, a_vnorm_f, conv_w_f = [jnp.concatenate([per_chip[j][k] for j in range(N_CHIPS)], axis=-1) for k in range(3)]
    cw = conv_w_f.reshape(DEPTH, 3, 2, F).transpose(0, 2, 1, 3)
    cb = f_conv_b.reshape(DEPTH, 2, 1, F)
    tril = jnp.tril(jnp.ones((CHUNK, CHUNK), dtype=bool))
    wm = jnp.where(tril, a_w_s, 0.0).astype(BF16)
    wmt = jnp.swapaxes(wm, -1, -2)
    bcol = a_b_s[..., None]
    kvf_full = g_kvf[:, 0, :, :Ckv].transpose(1, 0, 2).reshape(D, N_CHIPS * Ckv)
    w_kv = jnp.stack([kvf_full[:, :ATT], kvf_full[:, ATT:2 * ATT]])
    w_f = jnp.pad(kvf_full[:, 2 * ATT:], ((0, 0), (0, LANE - H)))
    bf_pad = jnp.pad(b_f, (0, LANE - H))[None]
    row = lambda v: v.reshape(1, -1)

    h = x[0]
    target = loss_target[0]
    saved = [dict() for _ in range(DEPTH)]
    kvs = {}
    for l in range(DEPTH):
        sv = saved[l]
        sv["h_m"] = h
        if l < NA:
            xn = _rms_fwd(h, row(a_norm_f[l]), f"a{l}_norm")
            zpre = _mm(m2(xn), mgc(g_ain, l), "nn", m2(_sds((S, 2 * W), BF16)), tm=1024, tn=512, name=f"a{l}_in")
            gated = _sgu_fwd(zpre, wm[l], bcol[l], row(a_vnorm_f[l]), f"a{l}_sgu")
            h = _mm(m2(gated), mgr(g_aout, l), "nn", m2(_sds((S, D), F32)), res=m2(h), tm=1024, tn=1024, name=f"a{l}_out")
            sv.update(xn_m=xn, zpre=zpre, gated=gated)
        else:
            j = l - NA
            xn = _rms_fwd(h, row(b_norm[j]), f"b{j}_norm")
            qg = _mm(m2(xn), mgc(g_bqg, j), "nn", mcs(_sds((2, S, ATT), BF16)), tm=1024, tn=512, name=f"b{j}_qg")
            qn = _headnorm_fwd(qg, 0, row(q_norm[j]) * QK_SCALE, f"b{j}_qnorm")
            o, lse = _attn_fwd(qn, kvs["kn"], kvs["kv"], kvs["crow"], f"b{j}_attn")
            og = _gate_fwd(o, qg, f"b{j}_gate")
            h = _mm(m2(og), mgr(g_bout, j), "nn", m2(_sds((S, D), F32)), res=m2(h), tm=1024, tn=1024, name=f"b{j}_out")
            sv.update(xn_m=xn, qg=qg, qn=qn, o=o, lse=lse, og=og)
        sv["h_f"] = h
        xn = _rms_fwd(h, row(f_norm[l]), f"f{l}_norm")
        hup = _mm(m2(xn), mgc(g_up, l), "nn", mcs(_sds((2, S, F), BF16)), tm=1024, tn=1408, name=f"f{l}_up")
        act = _convgate_fwd(hup, cw[l], cb[l], f"f{l}_conv")
        h = _mm(m2(act), mgr(g_down, l), "nn", m2(_sds((S, D), F32)), res=m2(h), tm=1024, tn=1024, tk=1408, name=f"f{l}_down")
        sv.update(xn_f=xn, hup=hup, act=act)
        if l == NA - 1:
            xn_kv = _rms_fwd(h, row(kv_norm), "kv_norm")
            kv = _mm(m2(xn_kv), mcs(w_kv), "nn", mcs(_sds((2, S, ATT), BF16)), tm=1024, tn=512, name="kv_proj")
            fpre = _mm(m2(xn_kv), m2(w_f), "nn", m2(_sds((S, LANE), F32)), tm=1024, name="kv_fproj")
            kn = _headnorm_fwd(kv, 0, row(k_norm), "kv_knorm")
            cums = _logf_cumsum(fpre, bf_pad, "kv_cumsum")
            cT = cums[:, :H].T
            kvs.update(h=h, xn=xn_kv, kv=kv, fpre=fpre, kn=kn, crow=cT[:, None, :])

    loss11, dh, dhb, d_final = _final_loss(h, row(final_norm), target, "final_loss")
    loss = lax.psum(loss11[0, 0], ("x", "y", "c"))

    G_ain = lax.empty(g_ain.shape, BF16)
    G_aout = lax.empty(g_aout.shape, BF16)
    G_bqg = lax.empty(g_bqg.shape, BF16)
    G_bout = lax.empty(g_bout.shape, BF16)
    G_up = lax.empty(g_up.shape, BF16)
    G_down = lax.empty(g_down.shape, BF16)
    d_anorm, d_avnorm, d_ws, d_bs = [None] * NA, [None] * NA, [None] * NA, [None] * NA
    d_bnorm, d_qnorm = [None] * NB, [None] * NB
    d_fnorm, d_cw, d_cb = [None] * DEPTH, [None] * DEPTH, [None] * DEPTH
    dkn, dvv, dck = [], [], []
    G_kvf = d_kvnorm = d_bf = d_knorm = None
    for l in reversed(range(DEPTH)):
        sv = saved[l]
        if l == NA - 1:
            dkv, d_knorm = _headnorm_bwd(dkn, kvs["kv"], row(k_norm), dvv, "kv_knorm_bwd")
            dc = sum(dck)
            dc = jnp.pad(dc, ((0, 0), (0, LANE - H)))
            df, d_bf = _logf_cumsum_bwd(dc, kvs["fpre"], bf_pad, "kv_cumsum_bwd")
            dxn = _mm(mcs(dkv), mcs(w_kv), "nt", m2(_sds((S, D), F32)), tm=1024, tn=1024, tk=2048, name="kv_proj_dx")
            dxn = _mm(m2(df), m2(w_f), "nt", m2(_sds((S, D), BF16)), res=m2(dxn), tm=1024, tn=512, name="kv_fproj_dx")
            dw_kv = _mm(m2(kvs["xn"]), mcs(dkv), "tn", mcs(_sds((2, D, ATT), BF16)), tm=1024, tn=1024, tk=2048, name="kv_proj_dw")
            dw_f = _mm(m2(kvs["xn"]), m2(df), "tn", m2(_sds((D, LANE), BF16)), tm=512, tk=2048, name="kv_fproj_dw")
            dh, dhb, d_kvnorm = _rms_bwd(dxn, kvs["h"], row(kv_norm), dh, "kv_norm_bwd")
            dfull = jnp.concatenate([dw_kv[0], dw_kv[1], dw_f[:, :H]], axis=1)
            G_kvf = jnp.pad(dfull.reshape(D, N_CHIPS, Ckv).transpose(1, 0, 2), ((0, 0), (0, 0), (0, Cp - Ckv)))[:, None]
        dact = _mm(m2(dhb), mgr(g_down, l), "nt", m2(_sds((S, F), BF16)), tm=1024, tn=1408, tk=2048, name=f"f{l}_down_dx")
        G_down = _mm(m2(sv["act"]), m2(dhb), "tn", mgr(G_down, l), alias=True, tm=1408, tn=1024, tk=2048, name=f"f{l}_down_dw")
        dhup, d_cw[l], d_cb[l] = _convgate_bwd(sv["hup"], dact, cw[l], cb[l], f"f{l}_conv_bwd")
        dxn = _mm(mcs(dhup), mgc(g_up, l), "nt", m2(_sds((S, D), BF16)), tm=1024, tn=2048, tk=1408, name=f"f{l}_up_dx")
        G_up = _mm(m2(sv["xn_f"]), mcs(dhup), "tn", mgc(G_up, l), alias=True, tm=1024, tn=1408, tk=2048, name=f"f{l}_up_dw")
        dh, dhb, d_fnorm[l] = _rms_bwd(dxn, sv["h_f"], row(f_norm[l]), dh, f"f{l}_norm_bwd")
        if l >= NA:
            j = l - NA
            dog = _mm(m2(dhb), mgr(g_bout, j), "nt", m2(_sds((S, ATT), BF16)), tm=1024, tn=512, tk=2048, name=f"b{j}_out_dx")
            G_bout = _mm(m2(sv["og"]), m2(dhb), "tn", mgr(G_bout, j), alias=True, tm=512, tn=1024, tk=4096, name=f"b{j}_out_dw")
            do, dgate, delta = _gate_bwd(dog, sv["o"], sv["qg"], f"b{j}_gate_bwd")
            dqn, drow, dkn_j, dv_j, dcol = _attn_bwd(sv["qn"], kvs["kn"], kvs["kv"], kvs["crow"], do, sv["lse"], delta, f"b{j}_attn_bwd")
            dkn.append(dkn_j); dvv.append(dv_j); dck.append(drow[:, :, 0].T - dcol[:, 0, :].T)
            dqg, d_qs = _headnorm_bwd([dqn], sv["qg"], row(q_norm[j]) * QK_SCALE, [dgate], f"b{j}_qnorm_bwd")
            d_qnorm[j] = d_qs * QK_SCALE
            dxn = _mm(mcs(dqg), mgc(g_bqg, j), "nt", m2(_sds((S, D), BF16)), tm=1024, tn=2048, tk=1024, name=f"b{j}_qg_dx")
            G_bqg = _mm(m2(sv["xn_m"]), mcs(dqg), "tn", mgc(G_bqg, j), alias=True, tm=1024, tn=1024, tk=2048, name=f"b{j}_qg_dw")
            dh, dhb, d_bnorm[j] = _rms_bwd(dxn, sv["h_m"], row(b_norm[j]), dh, f"b{j}_norm_bwd")
        else:
            dgated = _mm(m2(dhb), mgr(g_aout, l), "nt", m2(_sds((S, W), BF16)), tm=1024, tn=512, tk=2048, name=f"a{l}_out_dx")
            G_aout = _mm(m2(sv["gated"]), m2(dhb), "tn", mgr(G_aout, l), alias=True, tm=512, tn=1024, tk=4096, name=f"a{l}_out_dw")
            dz, d_ws[l], d_bs[l], d_avnorm[l] = _sgu_bwd(sv["zpre"], dgated, wm[l], wmt[l], bcol[l], row(a_vnorm_f[l]), f"a{l}_sgu_bwd")
            dxn = _mm(m2(dz), mgc(g_ain, l), "nt", m2(_sds((S, D), BF16)), tm=1024, tn=2048, tk=1024, name=f"a{l}_in_dx")
            G_ain = _mm(m2(sv["xn_m"]), m2(dz), "tn", mgc(G_ain, l), alias=True, tm=1024, tn=1024, tk=2048, name=f"a{l}_in_dw")
            dh, dhb, d_anorm[l] = _rms_bwd(dxn, sv["h_m"], row(a_norm_f[l]), dh, f"a{l}_norm_bwd")
    grad_x = dh[None]

    full = dict(
        a_norm=jnp.concatenate(d_anorm, axis=0), a_v_norm=jnp.concatenate(d_avnorm, axis=0),
        f_conv_w=jnp.stack(d_cw).transpose(0, 2, 1, 3).reshape(DEPTH, 3, 2 * F),
        a_w_s=jnp.where(tril, jnp.stack(d_ws), 0.0), a_b_s=jnp.stack(d_bs)[..., 0],
        kv_norm=d_kvnorm[0], b_f=d_bf[0, :H], k_norm=d_knorm[0], b_norm=jnp.concatenate(d_bnorm, axis=0),
        q_norm=jnp.concatenate(d_qnorm, axis=0), f_norm=jnp.concatenate(d_fnorm, axis=0),
        f_conv_b=jnp.stack(d_cb).reshape(DEPTH, 2 * F), final_norm=d_final[0])
    shard_rows = []
    for j in range(N_CHIPS):
        pieces = []
        for k in small_sharded:
            n = weights[k].shape[-1]
            pieces.append(full[k][..., j * n:(j + 1) * n])
        shard_rows.append(_pack(pieces, 32))
    rs = shard_rows[0].shape[0]
    repl = _pack([full[k] for k in small_repl], N_CHIPS * 32)
    rr = repl.shape[0] // N_CHIPS
    G_small = jnp.concatenate([jnp.stack(shard_rows), repl.reshape(N_CHIPS, rr, LANE)], axis=1)[:, None]

    Gs = [G_ain, G_aout, G_kvf, G_bqg, G_bout, G_up, G_down, G_small]
    others = _swap_halves(Gs, "grads_swap")
    partial = [_pair_sum(g, o, core, g.dtype, f"grads_pair{i}") for i, (g, o) in enumerate(zip(Gs, others))]
    by_chip = _scatter_chips(partial, "grads_scatter")
    halves = [_sum_chips(q, p, me, core, f"grads_sum{i}") for i, (q, p) in enumerate(zip(by_chip[:-1], partial[:-1]))]
    small_half = _sum_small(lax.dynamic_index_in_dim(partial[-1], me[0], 0, keepdims=False), by_chip[-1], "grads_sum_small")
    hs = small_half.shape[1]
    halves.append(lax.dynamic_update_slice(jnp.zeros((1, 2 * hs, LANE), F32), small_half, (0, core[0] * hs, 0)))
    F_ain, F_aout, F_kvf, F_bqg, F_bout, F_up, F_down, F_small = _join_halves(halves, "grads_join")
    repl_buf = lax.dynamic_update_slice(jnp.zeros((N_CHIPS, 1, rr, LANE), F32), F_small[None, :, rs:, :], (me[0], 0, 0, 0))
    (repl_all,) = _gather_chips([repl_buf], "gather_small_grads")

    grads = dict(a_w_in=F_ain, a_w_out=F_aout, w_kvf=F_kvf[0, :, :Ckv], b_w_qg=F_bqg, b_w_out=F_bout, f_w_up=F_up, f_w_down=F_down)
    for k, gk in zip(small_sharded, _unpack(F_small[0, :rs], small_local)):
        grads[k] = gk
    for k, gk in zip(small_repl, _unpack(repl_all.reshape(N_CHIPS * rr, LANE), [weights[k] for k in small_repl])):
        grads[k] = gk

    delta, new_m, new_v = {}, {}, {}
    for k in big:
        delta[k], new_m[k], new_v[k] = _adamw_nd(weights[k], grads[k], mom1[k], mom2[k], f"adamw_{k}")
    small = small_sharded + small_repl
    packed = [_pack([t[k] for k in small], 8) for t in (weights, grads, mom1, mom2)]
    outs = _adamw(*packed, "adamw_small")
    like = [weights[k] for k in small]
    for t, o in zip((delta, new_m, new_v), outs):
        for k, a in zip(small, _unpack(o, like)):
            t[k] = a

    return (loss, grad_x, *[grads[k] for k in names], *[delta[k] for k in names],
            *[new_m[k] for k in names], *[new_v[k] for k in names])
```

```python
import functools
import math

import jax
import jax.numpy as jnp
from jax import lax
from jax.experimental import pallas as pl
from jax.experimental.pallas import tpu as pltpu
from jax.experimental.pallas import tpu_sc as plsc

F32, BF16 = jnp.float32, jnp.bfloat16
EPS = 1e-6
CHUNK = 128
HEAD = 128
LANE = 128
HALO = 16
N_CHIPS = 4
VMEM_LIMIT = 48 * 1024 * 1024
MESH = pl.DeviceIdType.MESH
ANY = pl.BlockSpec(memory_space=pl.ANY)

ADAM_LR, ADAM_B1, ADAM_B2, ADAM_EPS, ADAM_WD, ADAM_STEP = 0.001, 0.9, 0.999, 1e-08, 0.01, 10
NEG = -1e30
QK_SCALE = HEAD ** -0.5


def _cp(*sem):
    return pltpu.CompilerParams(dimension_semantics=sem, vmem_limit_bytes=VMEM_LIMIT)


def _t(dim, pref, mult=LANE):
    if dim <= pref:
        return dim
    t = (pref // mult) * mult
    while t >= mult:
        if dim % t == 0:
            return t
        t -= mult
    return dim


def _sds(shape, dtype):
    return jax.ShapeDtypeStruct(tuple(shape), dtype)


class Mat:
    def __init__(self, arr, shape, rlim, clim, block, index):
        self.arr, self.shape, self.rlim, self.clim = arr, shape, rlim, clim
        self._block, self._index = block, index

    def spec(self, tr, tc, gmap):
        assert self.rlim % tr == 0 and self.clim % tc == 0, (self.shape, self.rlim, self.clim, tr, tc)
        index = self._index(tr, tc)
        return pl.BlockSpec(self._block(tr, tc), lambda *g: index(*gmap(*g)))


def m2(arr):
    R, C = arr.shape
    return Mat(arr, (R, C), R, C, lambda tr, tc: (tr, tc), lambda tr, tc: (lambda i, j: (i, j)))


def mcs(arr):
    ns, R, Cs = arr.shape
    return Mat(arr, (R, ns * Cs), R, Cs, lambda tr, tc: (None, tr, tc),
               lambda tr, tc: (lambda i, j: (j // (Cs // tc), i, j % (Cs // tc))))


def mhalf(arr, p):
    ns, R, Cs = arr.shape
    return Mat(arr, (R, Cs), R, Cs, lambda tr, tc: (None, tr, tc), lambda tr, tc: (lambda i, j: (p, i, j)))


def mgc(arr, l):
    ns, L, R, Cs = arr.shape
    return Mat(arr, (R, ns * Cs), R, Cs, lambda tr, tc: (None, None, tr, tc),
               lambda tr, tc: (lambda i, j: (j // (Cs // tc), l, i, j % (Cs // tc))))


def mgr(arr, l):
    ns, L, Rs, C = arr.shape
    return Mat(arr, (ns * Rs, C), Rs, C, lambda tr, tc: (None, None, tr, tc),
               lambda tr, tc: (lambda i, j: (i // (Rs // tr), l, i % (Rs // tr), j)))


_DIMS = {"nn": ((1,), (0,)), "nt": ((1,), (1,)), "tn": ((0,), (0,))}


def _mm(a, b, mode, out, *, res=None, alias=False, tm=512, tn=512, tk=4096, name):
    if mode == "tn":
        (K, M), (K2, N) = a.shape, b.shape
        alim_m, alim_k, blim_k, blim_n = a.clim, a.rlim, b.rlim, b.clim
    elif mode == "nt":
        (M, K), (N, K2) = a.shape, b.shape
        alim_m, alim_k, blim_k, blim_n = a.rlim, a.clim, b.clim, b.rlim
    else:
        (M, K), (K2, N) = a.shape, b.shape
        alim_m, alim_k, blim_k, blim_n = a.rlim, a.clim, b.rlim, b.clim
    assert K == K2 and out.shape == (M, N), (name, a.shape, b.shape, out.shape)
    tm = _t(math.gcd(alim_m, out.rlim), tm)
    tn = _t(math.gcd(blim_n, out.clim), tn)
    tk = _t(math.gcd(alim_k, blim_k), tk)
    grid = (M // tm, N // tn, K // tk)
    nk = grid[2]
    if mode == "tn":
        a_spec = a.spec(tk, tm, lambda i, j, k: (k, i))
    else:
        a_spec = a.spec(tm, tk, lambda i, j, k: (i, k))
    if mode == "nt":
        b_spec = b.spec(tn, tk, lambda i, j, k: (j, k))
    else:
        b_spec = b.spec(tk, tn, lambda i, j, k: (k, j))
    o_spec = out.spec(tm, tn, lambda i, j, k: (i, j))
    operands, in_specs = [a.arr, b.arr], [a_spec, b_spec]
    if res is not None:
        operands.append(res.arr)
        in_specs.append(res.spec(tm, tn, lambda i, j, k: (i, j)))
    aliases = {}
    if alias:
        aliases = {len(operands): 0}
        operands.append(out.arr)
        in_specs.append(ANY)
    dims = (_DIMS[mode], ((), ()))
    has_res = res is not None

    def body(*refs):
        a_ref, b_ref = refs[0], refs[1]
        res_ref = refs[2] if has_res else None
        n_in = 2 + has_res + alias
        o_ref = refs[n_in]
        p = lax.dot_general(a_ref[...].astype(BF16), b_ref[...].astype(BF16), dims, preferred_element_type=F32)

        def finish(v):
            if has_res:
                v = v + res_ref[...].astype(F32)
            o_ref[...] = v.astype(o_ref.dtype)

        if nk == 1:
            finish(p)
        else:
            acc = refs[n_in + 1]
            k = pl.program_id(2)

            @pl.when(k == 0)
            def _():
                acc[...] = p

            @pl.when(k > 0)
            def _():
                acc[...] += p

            @pl.when(k == nk - 1)
            def _():
                finish(acc[...])

    return pl.pallas_call(
        body, name=name, grid=grid, in_specs=in_specs, out_specs=o_spec,
        out_shape=_sds(out.arr.shape, out.arr.dtype),
        scratch_shapes=[pltpu.VMEM((tm, tn), F32)] if nk > 1 else [],
        input_output_aliases=aliases,
        compiler_params=_cp("parallel", "parallel", "arbitrary"),
    )(*operands)


def _rms_fwd(x, g, name):
    S, D = x.shape
    tr = _t(S, 512)

    def body(x_ref, g_ref, o_ref):
        xv = x_ref[...]
        r = lax.rsqrt(jnp.mean(xv * xv, axis=-1, keepdims=True) + EPS)
        o_ref[...] = ((xv * r) * g_ref[...]).astype(o_ref.dtype)

    return pl.pallas_call(
        body, name=name, grid=(S // tr,),
        in_specs=[pl.BlockSpec((tr, D), lambda i: (i, 0)), pl.BlockSpec((1, D), lambda i: (0, 0))],
        out_specs=pl.BlockSpec((tr, D), lambda i: (i, 0)), out_shape=_sds((S, D), BF16),
        compiler_params=_cp("parallel"),
    )(x, g)


def _rms_bwd(dy, x, g, dres, name):
    S, D = x.shape
    tr = _t(S, 512)

    def body(dy_ref, x_ref, g_ref, dres_ref, dx_ref, dxb_ref, dg_ref):
        xv = x_ref[...]
        dyv = dy_ref[...].astype(F32)
        r = lax.rsqrt(jnp.mean(xv * xv, axis=-1, keepdims=True) + EPS)
        xh = xv * r
        dxh = dyv * g_ref[...]
        m = jnp.mean(dxh * xh, axis=-1, keepdims=True)
        dx = dres_ref[...] + r * (dxh - xh * m)
        dx_ref[...] = dx
        dxb_ref[...] = dx.astype(dxb_ref.dtype)

        @pl.when(pl.program_id(0) == 0)
        def _():
            dg_ref[...] = jnp.zeros_like(dg_ref)

        dg_ref[...] += jnp.sum(dyv * xh, axis=0, keepdims=True)

    row = pl.BlockSpec((tr, D), lambda i: (i, 0))
    vec = pl.BlockSpec((1, D), lambda i: (0, 0))
    return pl.pallas_call(
        body, name=name, grid=(S // tr,), in_specs=[row, row, vec, row], out_specs=[row, row, vec],
        out_shape=[_sds((S, D), F32), _sds((S, D), BF16), _sds((1, D), F32)], compiler_params=_cp("arbitrary"),
    )(dy, x, g, dres)


def _final_loss(h, g, target, name):
    S, D = h.shape
    tr = _t(S, 512)

    def body(x_ref, g_ref, t_ref, loss_ref, dx_ref, dxb_ref, dg_ref):
        xv = x_ref[...]
        r = lax.rsqrt(jnp.mean(xv * xv, axis=-1, keepdims=True) + EPS)
        xh = xv * r
        err = xh * g_ref[...] - t_ref[...]
        part = 0.5 * jnp.sum(jnp.mean(err * err, axis=-1, keepdims=True), axis=0, keepdims=True)
        dyv = err * (1.0 / D)
        dxh = dyv * g_ref[...]
        m = jnp.mean(dxh * xh, axis=-1, keepdims=True)
        dx = r * (dxh - xh * m)
        dx_ref[...] = dx
        dxb_ref[...] = dx.astype(dxb_ref.dtype)

        @pl.when(pl.program_id(0) == 0)
        def _():
            dg_ref[...] = jnp.zeros_like(dg_ref)
            loss_ref[...] = jnp.zeros_like(loss_ref)

        dg_ref[...] += jnp.sum(dyv * xh, axis=0, keepdims=True)
        loss_ref[...] += part

    row = pl.BlockSpec((tr, D), lambda i: (i, 0))
    vec = pl.BlockSpec((1, D), lambda i: (0, 0))
    one = pl.BlockSpec((1, 1), lambda i: (0, 0))
    return pl.pallas_call(
        body, name=name, grid=(S // tr,), in_specs=[row, vec, row], out_specs=[one, row, row, vec],
        out_shape=[_sds((1, 1), F32), _sds((S, D), F32), _sds((S, D), BF16), _sds((1, D), F32)], compiler_params=_cp("arbitrary"),
    )(h, g, target)


_RSQRT2 = 0.7071067811865476
_RSQRT2PI = 0.3989422804014327


def _gelu(x):
    return 0.5 * x * (1.0 + lax.erf(x * _RSQRT2))


def _gelu_grad(x):
    return 0.5 * (1.0 + lax.erf(x * _RSQRT2)) + x * (jnp.exp(-0.5 * x * x) * _RSQRT2PI)


def _sgu_fwd(zpre, wm, bcol, vnorm, name):
    S, W2 = zpre.shape
    W = W2 // 2
    G = W // CHUNK

    def body(z_ref, wm_ref, b_ref, vn_ref, o_ref):
        zp = z_ref[...].astype(F32)
        u = _gelu(zp[:, :W])
        v = _gelu(zp[:, W:])
        rv = lax.rsqrt(jnp.mean(v * v, axis=-1, keepdims=True) + EPS)
        vn = ((v * rv) * vn_ref[...]).astype(BF16)
        for g in range(G):
            sl = slice(g * CHUNK, (g + 1) * CHUNK)
            mixed = jnp.dot(wm_ref[g], vn[:, sl], preferred_element_type=F32) + b_ref[g]
            o_ref[:, sl] = (u[:, sl] * mixed).astype(o_ref.dtype)

    return pl.pallas_call(
        body, name=name, grid=(S // CHUNK,),
        in_specs=[pl.BlockSpec((CHUNK, W2), lambda i: (i, 0)),
                  pl.BlockSpec((G, CHUNK, CHUNK), lambda i: (0, 0, 0)),
                  pl.BlockSpec((G, CHUNK, 1), lambda i: (0, 0, 0)),
                  pl.BlockSpec((1, W), lambda i: (0, 0))],
        out_specs=pl.BlockSpec((CHUNK, W), lambda i: (i, 0)), out_shape=_sds((S, W), BF16),
        compiler_params=_cp("parallel"),
    )(zpre, wm, bcol, vnorm)


def _sgu_bwd(zpre, dgated, wm, wmt, bcol, vnorm, name):
    S, W2 = zpre.shape
    W = W2 // 2
    G = W // CHUNK

    def body(z_ref, dg_ref, wm_ref, wmt_ref, b_ref, vn_ref, dz_ref, dws_ref, dbs_ref, dvn_ref):
        @pl.when(pl.program_id(0) == 0)
        def _():
            dws_ref[...] = jnp.zeros_like(dws_ref)
            dbs_ref[...] = jnp.zeros_like(dbs_ref)
            dvn_ref[...] = jnp.zeros_like(dvn_ref)

        zp = z_ref[...].astype(F32)
        zu, zv = zp[:, :W], zp[:, W:]
        u = _gelu(zu)
        v = _gelu(zv)
        rv = lax.rsqrt(jnp.mean(v * v, axis=-1, keepdims=True) + EPS)
        vh = v * rv
        vn = (vh * vn_ref[...]).astype(BF16)
        dgv = dg_ref[...].astype(F32)
        du_parts, dvn_parts = [], []
        for g in range(G):
            sl = slice(g * CHUNK, (g + 1) * CHUNK)
            vg = vn[:, sl]
            mixed = jnp.dot(wm_ref[g], vg, preferred_element_type=F32) + b_ref[g]
            dgg = dgv[:, sl]
            du_parts.append(dgg * mixed)
            dmixed = dgg * u[:, sl]
            dbs_ref[g] += jnp.sum(dmixed, axis=1, keepdims=True)
            dmb = dmixed.astype(BF16)
            dws_ref[g] += lax.dot_general(dmb, vg, (_DIMS["nt"], ((), ())), preferred_element_type=F32)
            dvn_parts.append(jnp.dot(wmt_ref[g], dmb, preferred_element_type=F32))
        du = jnp.concatenate(du_parts, axis=1)
        dvn = jnp.concatenate(dvn_parts, axis=1)
        dvn_ref[...] += jnp.sum(dvn * vh, axis=0, keepdims=True)
        dvh = dvn * vn_ref[...]
        dv = rv * (dvh - vh * jnp.mean(dvh * vh, axis=-1, keepdims=True))
        dz_ref[:, :W] = (du * _gelu_grad(zu)).astype(dz_ref.dtype)
        dz_ref[:, W:] = (dv * _gelu_grad(zv)).astype(dz_ref.dtype)

    full3 = lambda shape: pl.BlockSpec(shape, lambda i: (0, 0, 0))
    return pl.pallas_call(
        body, name=name, grid=(S // CHUNK,),
        in_specs=[pl.BlockSpec((CHUNK, W2), lambda i: (i, 0)), pl.BlockSpec((CHUNK, W), lambda i: (i, 0)),
                  full3((G, CHUNK, CHUNK)), full3((G, CHUNK, CHUNK)), full3((G, CHUNK, 1)),
                  pl.BlockSpec((1, W), lambda i: (0, 0))],
        out_specs=[pl.BlockSpec((CHUNK, W2), lambda i: (i, 0)), full3((G, CHUNK, CHUNK)), full3((G, CHUNK, 1)),
                   pl.BlockSpec((1, W), lambda i: (0, 0))],
        out_shape=[_sds((S, W2), BF16), _sds((G, CHUNK, CHUNK), F32), _sds((G, CHUNK, 1), F32), _sds((1, W), F32)],
        compiler_params=_cp("arbitrary"),
    )(zpre, dgated, wm, wmt, bcol, vnorm)


def _conv_taps(h_ref, half, r0, R, tc):
    if r0 == 0:
        xe = jnp.concatenate([jnp.zeros((HALO, tc), F32), h_ref[half, 0:R, :].astype(F32)], axis=0)
    else:
        xe = h_ref[half, r0 - HALO:r0 + R, :].astype(F32)
    return xe[HALO:], pltpu.roll(xe, 1, 0)[HALO:], pltpu.roll(xe, 2, 0)[HALO:]


def _conv_apply(taps, w, b):
    x0, x1, x2 = taps
    return x2 * w[0:1] + x1 * w[1:2] + x0 * w[2:3] + b


def _convgate_fwd(hup, cw, cb, name):
    _, S, F = hup.shape
    tc = _t(F, 256)
    R = _t(S, 512)

    def body(h_ref, w_ref, b_ref, o_ref):
        for r0 in range(0, S, R):
            gate = _conv_apply(_conv_taps(h_ref, 0, r0, R, tc), w_ref[0], b_ref[0])
            val = _conv_apply(_conv_taps(h_ref, 1, r0, R, tc), w_ref[1], b_ref[1])
            o_ref[r0:r0 + R, :] = (gate * jax.nn.sigmoid(gate) * val).astype(o_ref.dtype)

    return pl.pallas_call(
        body, name=name, grid=(F // tc,),
        in_specs=[pl.BlockSpec((2, S, tc), lambda j: (0, 0, j)), pl.BlockSpec((2, 3, tc), lambda j: (0, 0, j)),
                  pl.BlockSpec((2, 1, tc), lambda j: (0, 0, j))],
        out_specs=pl.BlockSpec((S, tc), lambda j: (0, j)), out_shape=_sds((S, F), BF16),
        compiler_params=_cp("parallel"),
    )(hup, cw, cb)


def _convgate_bwd(hup, dact, cw, cb, name):
    _, S, F = hup.shape
    tc = _t(F, 256)
    R = _t(S, 512)

    def body(h_ref, da_ref, w_ref, b_ref, dh_ref, dw_ref, db_ref, dhc):
        dhc[:, S:S + HALO, :] = jnp.zeros((2, HALO, tc), F32)
        dw_acc = [[jnp.zeros((1, tc), F32) for _ in range(3)] for _ in range(2)]
        db_acc = [jnp.zeros((1, tc), F32) for _ in range(2)]
        for r0 in range(0, S, R):
            taps = [_conv_taps(h_ref, p, r0, R, tc) for p in range(2)]
            gate = _conv_apply(taps[0], w_ref[0], b_ref[0])
            val = _conv_apply(taps[1], w_ref[1], b_ref[1])
            da = da_ref[r0:r0 + R, :].astype(F32)
            sg = jax.nn.sigmoid(gate)
            d = [da * val * (sg * (1.0 + gate * (1.0 - sg))), da * (gate * sg)]
            for p in range(2):
                dhc[p, r0:r0 + R, :] = d[p]
                db_acc[p] = db_acc[p] + jnp.sum(d[p], axis=0, keepdims=True)
                for k in range(3):
                    dw_acc[p][k] = dw_acc[p][k] + jnp.sum(d[p] * taps[p][2 - k], axis=0, keepdims=True)
        for p in range(2):
            db_ref[p] = db_acc[p]
            dw_ref[p] = jnp.concatenate(dw_acc[p], axis=0)
            w = w_ref[p]
            for r0 in range(0, S, R):
                de = dhc[p, r0:r0 + R + HALO, :]
                d1 = pltpu.roll(de, R + HALO - 1, 0)[:R]
                d2 = pltpu.roll(de, R + HALO - 2, 0)[:R]
                dh_ref[p, r0:r0 + R, :] = (de[:R] * w[2:3] + d1 * w[1:2] + d2 * w[0:1]).astype(dh_ref.dtype)

    blk = lambda rows: pl.BlockSpec((2, rows, tc), lambda j: (0, 0, j))
    return pl.pallas_call(
        body, name=name, grid=(F // tc,),
        in_specs=[blk(S), pl.BlockSpec((S, tc), lambda j: (0, j)), blk(3), blk(1)],
        out_specs=[blk(S), blk(3), blk(1)],
        out_shape=[_sds((2, S, F), BF16), _sds((2, 3, F), F32), _sds((2, 1, F), F32)],
        scratch_shapes=[pltpu.VMEM((2, S + HALO, tc), F32)],
        compiler_params=_cp("parallel"),
    )(hup, dact, cw, cb)


def _headnorm_fwd(x3, part, gain, name):
    _, S, W = x3.shape
    tr = _t(S, 512)

    def body(x_ref, g_ref, o_ref):
        xv = x_ref[...].astype(F32)
        for h in range(W // HEAD):
            sl = slice(h * HEAD, (h + 1) * HEAD)
            xh = xv[:, sl]
            r = lax.rsqrt(jnp.mean(xh * xh, axis=-1, keepdims=True) + EPS)
            o_ref[:, sl] = ((xh * r) * g_ref[...]).astype(o_ref.dtype)

    return pl.pallas_call(
        body, name=name, grid=(S // tr,),
        in_specs=[pl.BlockSpec((None, tr, W), lambda i: (part, i, 0)), pl.BlockSpec((1, HEAD), lambda i: (0, 0))],
        out_specs=pl.BlockSpec((tr, W), lambda i: (i, 0)), out_shape=_sds((S, W), BF16),
        compiler_params=_cp("parallel"),
    )(x3, gain)


def _headnorm_bwd(dys, x3, gain, passes, name):
    _, S, W = x3.shape
    tr = _t(S, 256)
    nd, npass = len(dys), len(passes)

    def body(*refs):
        dy_refs = refs[:nd]
        x_ref, g_ref = refs[nd], refs[nd + 1]
        p_refs = refs[nd + 2:nd + 2 + npass]
        o_ref, dg_ref = refs[nd + 2 + npass], refs[nd + 3 + npass]

        @pl.when(pl.program_id(0) == 0)
        def _():
            dg_ref[...] = jnp.zeros_like(dg_ref)

        xv = x_ref[...].astype(F32)
        dyv = dy_refs[0][...].astype(F32)
        for r in dy_refs[1:]:
            dyv = dyv + r[...].astype(F32)
        dg = jnp.zeros((1, HEAD), F32)
        for h in range(W // HEAD):
            sl = slice(h * HEAD, (h + 1) * HEAD)
            xh = xv[:, sl]
            r = lax.rsqrt(jnp.mean(xh * xh, axis=-1, keepdims=True) + EPS)
            xh = xh * r
            dyh = dyv[:, sl]
            dg = dg + jnp.sum(dyh * xh, axis=0, keepdims=True)
            dxh = dyh * g_ref[...]
            o_ref[0, :, sl] = (r * (dxh - xh * jnp.mean(dxh * xh, axis=-1, keepdims=True))).astype(o_ref.dtype)
        dg_ref[...] += dg
        pv = p_refs[0][...].astype(F32)
        for r in p_refs[1:]:
            pv = pv + r[...].astype(F32)
        o_ref[1] = pv.astype(o_ref.dtype)

    row = pl.BlockSpec((tr, W), lambda i: (i, 0))
    vec = pl.BlockSpec((1, HEAD), lambda i: (0, 0))
    return pl.pallas_call(
        body, name=name, grid=(S // tr,),
        in_specs=[row] * nd + [pl.BlockSpec((None, tr, W), lambda i: (0, i, 0)), vec] + [row] * npass,
        out_specs=[pl.BlockSpec((2, tr, W), lambda i: (0, i, 0)), vec],
        out_shape=[_sds((2, S, W), BF16), _sds((1, HEAD), F32)], compiler_params=_cp("arbitrary"),
    )(*dys, x3, gain, *passes)


def _gate_fwd(o, qg, name):
    S, W = o.shape
    tr = _t(S, 512)

    def body(o_ref, g_ref, y_ref):
        y_ref[...] = (o_ref[...].astype(F32) * jax.nn.sigmoid(g_ref[...].astype(F32))).astype(y_ref.dtype)

    row = pl.BlockSpec((tr, W), lambda i: (i, 0))
    return pl.pallas_call(
        body, name=name, grid=(S // tr,), in_specs=[row, pl.BlockSpec((None, tr, W), lambda i: (1, i, 0))],
        out_specs=row, out_shape=_sds((S, W), BF16), compiler_params=_cp("parallel"),
    )(o, qg)


def _gate_bwd(dog, o, qg, name):
    S, W = o.shape
    H = W // HEAD
    tr = _t(S, 512)

    def body(dy_ref, o_ref, g_ref, do_ref, dg_ref, dl_ref):
        sg = jax.nn.sigmoid(g_ref[...].astype(F32))
        dy = dy_ref[...].astype(F32)
        ov = o_ref[...].astype(F32)
        dob = (dy * sg).astype(do_ref.dtype)
        do_ref[...] = dob
        dg_ref[...] = (dy * ov * (sg * (1.0 - sg))).astype(dg_ref.dtype)
        prod = dob.astype(F32) * ov
        for h in range(H):
            dl_ref[h] = jnp.sum(prod[:, h * HEAD:(h + 1) * HEAD], axis=-1, keepdims=True)

    row = pl.BlockSpec((tr, W), lambda i: (i, 0))
    return pl.pallas_call(
        body, name=name, grid=(S // tr,), in_specs=[row, row, pl.BlockSpec((None, tr, W), lambda i: (1, i, 0))],
        out_specs=[row, row, pl.BlockSpec((H, tr, 1), lambda i: (0, i, 0))],
        out_shape=[_sds((S, W), BF16), _sds((S, W), BF16), _sds((H, S, 1), F32)], compiler_params=_cp("parallel"),
    )(dog, o, qg)


def _logf_cumsum(fpre, bf, name):
    S, C = fpre.shape
    n = S // CHUNK

    def body(f_ref, b_ref, c_ref, carry):
        @pl.when(pl.program_id(0) == 0)
        def _():
            carry[...] = jnp.zeros_like(carry)

        lf = jax.nn.log_sigmoid(f_ref[...] + b_ref[...])
        tri = (lax.broadcasted_iota(jnp.int32, (CHUNK, CHUNK), 0)
               >= lax.broadcasted_iota(jnp.int32, (CHUNK, CHUNK), 1)).astype(F32)
        c_ref[...] = jnp.dot(tri, lf, preferred_element_type=F32, precision=lax.Precision.HIGHEST) + carry[...]
        carry[...] += jnp.sum(lf, axis=0, keepdims=True)

    return pl.pallas_call(
        body, name=name, grid=(n,),
        in_specs=[pl.BlockSpec((CHUNK, C), lambda i: (i, 0)), pl.BlockSpec((1, C), lambda i: (0, 0))],
        out_specs=pl.BlockSpec((CHUNK, C), lambda i: (i, 0)), out_shape=_sds((S, C), F32),
        scratch_shapes=[pltpu.VMEM((1, C), F32)], compiler_params=_cp("arbitrary"),
    )(fpre, bf)


def _logf_cumsum_bwd(dc, fpre, bf, name):
    S, C = fpre.shape
    n = S // CHUNK

    def body(dc_ref, f_ref, b_ref, df_ref, db_ref, carry):
        @pl.when(pl.program_id(0) == 0)
        def _():
            carry[...] = jnp.zeros_like(carry)
            db_ref[...] = jnp.zeros_like(db_ref)

        dcv = dc_ref[...]
        tri = (lax.broadcasted_iota(jnp.int32, (CHUNK, CHUNK), 0)
               <= lax.broadcasted_iota(jnp.int32, (CHUNK, CHUNK), 1)).astype(F32)
        dlf = jnp.dot(tri, dcv, preferred_element_type=F32, precision=lax.Precision.HIGHEST) + carry[...]
        carry[...] += jnp.sum(dcv, axis=0, keepdims=True)
        df = dlf * jax.nn.sigmoid(-(f_ref[...] + b_ref[...]))
        df_ref[...] = df.astype(df_ref.dtype)
        db_ref[...] += jnp.sum(df, axis=0, keepdims=True)

    rev = pl.BlockSpec((CHUNK, C), lambda i: (n - 1 - i, 0))
    vec = pl.BlockSpec((1, C), lambda i: (0, 0))
    return pl.pallas_call(
        body, name=name, grid=(n,), in_specs=[rev, rev, vec], out_specs=[rev, vec],
        out_shape=[_sds((S, C), BF16), _sds((1, C), F32)],
        scratch_shapes=[pltpu.VMEM((1, C), F32)], compiler_params=_cp("arbitrary"),
    )(dc, fpre, bf)


def _attn_tiles(S):
    return _t(S, 1024), _t(S, 512)


def _scores(q, k, ck, off, tq, tk, masked):
    s = lax.dot_general(q, k, (_DIMS["nt"], ((), ())), preferred_element_type=F32) - ck
    if masked:
        d = lax.broadcasted_iota(jnp.int32, (tq, tk), 1) - lax.broadcasted_iota(jnp.int32, (tq, tk), 0)
        s = jnp.where(d <= off, s, NEG)
    return s


def _attn_fwd(qn, kn, kv, crow, name):
    S, W = qn.shape
    H = W // HEAD
    tq, tk = _attn_tiles(S)
    nq, nk = S // tq, S // tk
    last = lambda i: ((i + 1) * tq - 1) // tk

    def body(q_ref, k_ref, v_ref, ck_ref, o_ref, lse_ref, m_sc, l_sc, acc_sc):
        qi, kj = pl.program_id(1), pl.program_id(2)

        @pl.when(kj == 0)
        def _():
            m_sc[...] = jnp.full_like(m_sc, NEG)
            l_sc[...] = jnp.zeros_like(l_sc)
            acc_sc[...] = jnp.zeros_like(acc_sc)

        def step(masked):
            s = _scores(q_ref[...], k_ref[...], ck_ref[...], qi * tq - kj * tk, tq, tk, masked)
            m_new = jnp.maximum(m_sc[...], jnp.max(s, axis=-1, keepdims=True))
            alpha = jnp.exp(m_sc[...] - m_new)
            p = jnp.exp(s - m_new)
            l_sc[...] = alpha * l_sc[...] + jnp.sum(p, axis=-1, keepdims=True)
            acc_sc[...] = alpha * acc_sc[...] + jnp.dot(p.astype(BF16), v_ref[...], preferred_element_type=F32)
            m_sc[...] = m_new

        @pl.when(kj <= last(qi))
        def _():
            step(True)

        @pl.when(kj == nk - 1)
        def _():
            o_ref[...] = (acc_sc[...] / l_sc[...]).astype(o_ref.dtype)
            lse_ref[...] = m_sc[...] + jnp.log(l_sc[...])

    return pl.pallas_call(
        body, name=name, grid=(H, nq, nk),
        in_specs=[pl.BlockSpec((tq, HEAD), lambda h, i, j: (i, h)),
                  pl.BlockSpec((tk, HEAD), lambda h, i, j: (jnp.minimum(j, last(i)), h)),
                  pl.BlockSpec((None, tk, HEAD), lambda h, i, j: (1, jnp.minimum(j, last(i)), h)),
                  pl.BlockSpec((None, 1, tk), lambda h, i, j: (h, 0, jnp.minimum(j, last(i))))],
        out_specs=[pl.BlockSpec((tq, HEAD), lambda h, i, j: (i, h)),
                   pl.BlockSpec((None, tq, 1), lambda h, i, j: (h, i, 0))],
        out_shape=[_sds((S, W), BF16), _sds((H, S, 1), F32)],
        scratch_shapes=[pltpu.VMEM((tq, 1), F32), pltpu.VMEM((tq, 1), F32), pltpu.VMEM((tq, HEAD), F32)],
        compiler_params=_cp("parallel", "parallel", "arbitrary"),
    )(qn, kn, kv, crow)


def _attn_bwd(qn, kn, kv, crow, do, lse, delta, name):
    S, W = qn.shape
    H = W // HEAD
    tq, tk = _attn_tiles(S)
    nq, nk = S // tq, S // tk
    first = lambda j: (j * tk) // tq

    def body(q_ref, k_ref, v_ref, ck_ref, do_ref, lse_ref, dl_ref, dq_ref, dr_ref, dk_ref, dv_ref, dc_ref, dk_sc, dv_sc, dc_sc):
        kj, qi = pl.program_id(1), pl.program_id(2)

        @pl.when(jnp.logical_and(kj == 0, qi == 0))
        def _():
            dq_ref[...] = jnp.zeros_like(dq_ref)
            dr_ref[...] = jnp.zeros_like(dr_ref)

        @pl.when(qi == 0)
        def _():
            dk_sc[...] = jnp.zeros_like(dk_sc)
            dv_sc[...] = jnp.zeros_like(dv_sc)
            dc_sc[...] = jnp.zeros_like(dc_sc)

        def step(masked):
            q, k, dov = q_ref[...], k_ref[...], do_ref[...]
            s = _scores(q, k, ck_ref[...], qi * tq - kj * tk, tq, tk, masked)
            p = jnp.exp(s - lse_ref[...])
            dp = lax.dot_general(dov, v_ref[...], (_DIMS["nt"], ((), ())), preferred_element_type=F32)
            ds = p * (dp - dl_ref[...])
            dsb = ds.astype(BF16)
            dc_sc[...] += jnp.sum(ds, axis=0, keepdims=True)
            dv_sc[...] += lax.dot_general(p.astype(BF16), dov, (_DIMS["tn"], ((), ())), preferred_element_type=F32)
            dk_sc[...] += lax.dot_general(dsb, q, (_DIMS["tn"], ((), ())), preferred_element_type=F32)
            rows = pl.ds(pl.multiple_of(qi * tq, tq), tq)
            dq_ref[rows, :] += jnp.dot(dsb, k, preferred_element_type=F32)
            dr_ref[rows, :] += jnp.sum(ds, axis=1, keepdims=True)

        below = (kj + 1) * tk - 1 <= qi * tq

        @pl.when(below)
        def _():
            step(False)

        @pl.when(jnp.logical_and(qi >= first(kj), jnp.logical_not(below)))
        def _():
            step(True)

        @pl.when(qi == nq - 1)
        def _():
            dk_ref[...] = dk_sc[...]
            dv_ref[...] = dv_sc[...]
            dc_ref[...] = dc_sc[...]

    qblk = pl.BlockSpec((tq, HEAD), lambda h, j, i: (jnp.maximum(i, first(j)), h))
    qcol = pl.BlockSpec((None, tq, 1), lambda h, j, i: (h, jnp.maximum(i, first(j)), 0))
    kblk = pl.BlockSpec((tk, HEAD), lambda h, j, i: (j, h))
    krow = pl.BlockSpec((None, 1, tk), lambda h, j, i: (h, 0, j))
    return pl.pallas_call(
        body, name=name, grid=(H, nk, nq),
        in_specs=[qblk, kblk, pl.BlockSpec((None, tk, HEAD), lambda h, j, i: (1, j, h)), krow, qblk, qcol, qcol],
        out_specs=[pl.BlockSpec((S, HEAD), lambda h, j, i: (0, h)), pl.BlockSpec((None, S, 1), lambda h, j, i: (h, 0, 0)),
                   kblk, kblk, krow],
        out_shape=[_sds((S, W), F32), _sds((H, S, 1), F32), _sds((S, W), F32), _sds((S, W), F32), _sds((H, 1, S), F32)],
        scratch_shapes=[pltpu.VMEM((tk, HEAD), F32), pltpu.VMEM((tk, HEAD), F32), pltpu.VMEM((1, tk), F32)],
        compiler_params=_cp("parallel", "arbitrary", "arbitrary"),
    )(qn, kn, kv, crow, do, lse, delta)


def _adamw(w, g, m, v, name):
    R, C = w.shape
    tr = _t(R, max(8, (1 << 19) // max(C, 1)), 8)
    c1 = 1.0 - ADAM_B1 ** ADAM_STEP
    c2 = 1.0 - ADAM_B2 ** ADAM_STEP

    def body(w_ref, g_ref, m_ref, v_ref, d_ref, nm_ref, nv_ref):
        gv = g_ref[...]
        nm = ADAM_B1 * m_ref[...] + (1.0 - ADAM_B1) * gv
        nv = ADAM_B2 * v_ref[...] + (1.0 - ADAM_B2) * (gv * gv)
        nm_ref[...] = nm
        nv_ref[...] = nv
        d_ref[...] = -ADAM_LR * ((nm / c1) / (jnp.sqrt(nv / c2) + ADAM_EPS) + ADAM_WD * w_ref[...])

    blk = pl.BlockSpec((tr, C), lambda i: (i, 0))
    return pl.pallas_call(
        body, name=name, grid=(R // tr,), in_specs=[blk] * 4, out_specs=[blk] * 3,
        out_shape=[_sds((R, C), F32)] * 3, compiler_params=_cp("parallel"),
    )(w, g, m, v)


def _place():
    x, y, c = lax.axis_index("x"), lax.axis_index("y"), lax.axis_index("c")
    chips = [(1 - x, y), (x, 1 - y), (1 - x, 1 - y)]
    return x, y, c, chips


def _place_part(part, me, dtype, name, layer=None):
    L, R, C = part.shape
    first, L = (0, L) if layer is None else (layer, 1)
    tr = _t(R, max(16, (1 << 20) // C), 16)

    def body(me_ref, x_ref, o_ref):
        o_ref[...] = x_ref[...].astype(o_ref.dtype)

    return pl.pallas_call(
        body, name=name,
        grid_spec=pltpu.PrefetchScalarGridSpec(
            num_scalar_prefetch=1, grid=(L, R // tr),
            in_specs=[pl.BlockSpec((None, tr, C), lambda l, r, m: (first + l, r, 0))],
            out_specs=pl.BlockSpec((None, None, tr, C), lambda l, r, m: (m[0], l, r, 0))),
        out_shape=_sds((N_CHIPS, L, R, C), dtype), compiler_params=_cp("parallel", "parallel"),
    )(me, part)


def _gather_chips_async(bufs, collective_id, name):
    n = len(bufs)
    refs = [jax.new_ref(b, memory_space=pltpu.MemorySpace.HBM) for b in bufs]

    @pl.kernel(mesh=plsc.ScalarSubcoreMesh(axis_name="seq", num_cores=1), name=name,
               scratch_types=(pltpu.SemaphoreType.DMA((n, 6)), pltpu.SemaphoreType.DMA((n, 6))),
               compiler_params=pltpu.CompilerParams(collective_id=collective_id))
    def launch(send_sems, recv_sems):
        x, y, c, chips = _place()
        me = 2 * x + y
        sibling = (x, y, 1 - c)
        barrier = pltpu.get_barrier_semaphore()
        for peer in [(*chip, c) for chip in chips] + [sibling]:
            pl.semaphore_signal(barrier, inc=1, device_id=peer, device_id_type=MESH)
        pl.semaphore_wait(barrier, len(chips) + 1)

        def copy(i, k, chip_index, core, to):
            h = refs[i].shape[2] // 2
            rows = refs[i].at[chip_index, :, pl.ds(core * h, h), :]
            return pltpu.make_async_remote_copy(
                src_ref=rows, dst_ref=rows, send_sem=send_sems.at[i, k], recv_sem=recv_sems.at[i, k],
                device_id=to, device_id_type=MESH)

        sent = []
        for i in range(n):
            for j, chip in enumerate(chips):
                cp = copy(i, j, me, c, (*chip, c))
                cp.start()
                sent.append(cp)
        for i in range(n):
            for j, chip in enumerate(chips):
                idx = 2 * chip[0] + chip[1]
                copy(i, j, idx, c, (x, y, c)).wait_recv()
                fw = copy(i, 3 + j, idx, c, sibling)
                fw.start()
                sent.append(fw)
        for i in range(n):
            for j, chip in enumerate(chips):
                copy(i, 3 + j, 2 * chip[0] + chip[1], 1 - c, (x, y, c)).wait_recv()
        for cp in sent:
            cp.wait_send()

    launch()
    return [r[...] for r in refs]


def _gather_chips(bufs, name):
    n = len(bufs)

    def body(*refs):
        outs = refs[n:2 * n]
        send_sems, recv_sems = refs[2 * n:]
        x, y, c, chips = _place()
        me = 2 * x + y
        sibling = (x, y, 1 - c)

        def copy(i, k, chip_index, core, to):
            h = outs[i].shape[2] // 2
            rows = outs[i].at[chip_index, :, pl.ds(core * h, h), :]
            return pltpu.make_async_remote_copy(
                src_ref=rows, dst_ref=rows, send_sem=send_sems.at[i, k], recv_sem=recv_sems.at[i, k],
                device_id=to, device_id_type=MESH)

        sent = []
        for i in range(n):
            for j, chip in enumerate(chips):
                cp = copy(i, j, me, c, (*chip, c))
                cp.start()
                sent.append(cp)
        for i in range(n):
            for j, chip in enumerate(chips):
                idx = 2 * chip[0] + chip[1]
                copy(i, j, idx, c, (x, y, c)).wait_recv()
                fw = copy(i, 3 + j, idx, c, sibling)
                fw.start()
                sent.append(fw)
        for i in range(n):
            for j, chip in enumerate(chips):
                copy(i, 3 + j, 2 * chip[0] + chip[1], 1 - c, (x, y, c)).wait_recv()
        for cp in sent:
            cp.wait_send()

    return pl.pallas_call(
        body, name=name, in_specs=[ANY] * n, out_specs=[ANY] * n,
        out_shape=[_sds(b.shape, b.dtype) for b in bufs],
        scratch_shapes=[pltpu.SemaphoreType.DMA((n, 6)), pltpu.SemaphoreType.DMA((n, 6))],
        input_output_aliases={i: i for i in range(n)},
        compiler_params=pltpu.CompilerParams(has_side_effects=True),
    )(*bufs)


def _swap_halves(gs, name):
    n = len(gs)

    def body(*refs):
        ins, outs = refs[:n], refs[n:2 * n]
        send_sems, recv_sems = refs[2 * n:]
        x, y, c, _ = _place()
        cps = []
        for i in range(n):
            h = ins[i].shape[2] // 2
            cp = pltpu.make_async_remote_copy(
                src_ref=ins[i].at[:, :, pl.ds((1 - c) * h, h), :], dst_ref=outs[i], send_sem=send_sems.at[i],
                recv_sem=recv_sems.at[i], device_id=(x, y, 1 - c), device_id_type=MESH)
            cp.start()
            cps.append(cp)
        for cp in cps:
            cp.wait()

    return pl.pallas_call(
        body, name=name, in_specs=[ANY] * n, out_specs=[ANY] * n,
        out_shape=[_sds(g.shape[:2] + (g.shape[2] // 2, g.shape[3]), g.dtype) for g in gs],
        scratch_shapes=[pltpu.SemaphoreType.DMA((n,)), pltpu.SemaphoreType.DMA((n,))],
        compiler_params=pltpu.CompilerParams(has_side_effects=True),
    )(*gs)


def _pair_sum(g, gs, core, out_dtype, name):
    ns, L, R, C = g.shape
    h = R // 2
    th = _t(h, max(16, (1 << 20) // C), 16)
    nb = h // th

    def body(core_ref, a_ref, b_ref, o_ref):
        o_ref[...] = (a_ref[...].astype(F32) + b_ref[...].astype(F32)).astype(o_ref.dtype)

    blk = (None, None, th, C)
    return pl.pallas_call(
        body, name=name,
        grid_spec=pltpu.PrefetchScalarGridSpec(
            num_scalar_prefetch=1, grid=(ns, L, nb),
            in_specs=[pl.BlockSpec(blk, lambda o, l, r, cr: (o, l, cr[0] * nb + r, 0)),
                      pl.BlockSpec(blk, lambda o, l, r, cr: (o, l, r, 0))],
            out_specs=pl.BlockSpec(blk, lambda o, l, r, cr: (o, l, r, 0))),
        out_shape=_sds((ns, L, h, C), out_dtype), compiler_params=_cp("parallel", "parallel", "parallel"),
    )(core, g, gs)


def _scatter_chips(ps, name):
    n = len(ps)

    def body(*refs):
        ins, outs = refs[:n], refs[n:2 * n]
        send_sems, recv_sems = refs[2 * n:]
        x, y, c, chips = _place()
        sent = []
        for i in range(n):
            for j, chip in enumerate(chips):
                cp = pltpu.make_async_remote_copy(
                    src_ref=ins[i].at[2 * chip[0] + chip[1]], dst_ref=outs[i].at[j], send_sem=send_sems.at[i, j],
                    recv_sem=recv_sems.at[i, j], device_id=(*chip, c), device_id_type=MESH)
                cp.start()
                sent.append(cp)
        for i in range(n):
            for j in range(len(chips)):
                slot = outs[i].at[j]
                pltpu.make_async_remote_copy(
                    src_ref=slot, dst_ref=slot, send_sem=send_sems.at[i, j], recv_sem=recv_sems.at[i, j],
                    device_id=(x, y, c), device_id_type=MESH).wait_recv()
        for cp in sent:
            cp.wait_send()

    return pl.pallas_call(
        body, name=name, in_specs=[ANY] * n, out_specs=[ANY] * n,
        out_shape=[_sds((N_CHIPS - 1,) + p.shape[1:], p.dtype) for p in ps],
        scratch_shapes=[pltpu.SemaphoreType.DMA((n, 3)), pltpu.SemaphoreType.DMA((n, 3))],
        compiler_params=pltpu.CompilerParams(has_side_effects=True),
    )(*ps)


def _sum_chips(q, p, me, core, name):
    _, L, h, C = q.shape
    th = _t(h, max(16, (1 << 19) // C), 16)
    nb = h // th
    blk = (None, None, th, C)

    def body(me_ref, core_ref, p_ref, q0, q1, q2, o_ref):
        o_ref[...] = ((p_ref[...].astype(F32) + q0[...].astype(F32)) + q1[...].astype(F32)) + q2[...].astype(F32)

    return pl.pallas_call(
        body, name=name,
        grid_spec=pltpu.PrefetchScalarGridSpec(
            num_scalar_prefetch=2, grid=(L, nb),
            in_specs=[pl.BlockSpec(blk, lambda l, r, m, c: (m[0], l, r, 0))]
            + [pl.BlockSpec(blk, functools.partial(lambda k, l, r, m, c: (k, l, r, 0), k)) for k in range(N_CHIPS - 1)],
            out_specs=pl.BlockSpec((None, th, C), lambda l, r, m, c: (l, c[0] * nb + r, 0))),
        out_shape=_sds((L, 2 * h, C), F32), compiler_params=_cp("parallel", "parallel"),
    )(me, core, p, q, q, q)


def _sum_small(own, q, name):
    def body(p_ref, q_ref, o_ref):
        o_ref[...] = ((p_ref[...] + q_ref[0]) + q_ref[1]) + q_ref[2]

    return pl.pallas_call(body, name=name, out_shape=_sds(own.shape, F32))(own, q)


def _join_halves(bufs, name):
    n = len(bufs)

    def body(*refs):
        outs = refs[n:2 * n]
        send_sems, recv_sems = refs[2 * n:]
        x, y, c, _ = _place()
        cps = []
        for i in range(n):
            h = outs[i].shape[1] // 2
            mine = outs[i].at[:, pl.ds(c * h, h), :]
            cp = pltpu.make_async_remote_copy(
                src_ref=mine, dst_ref=mine, send_sem=send_sems.at[i], recv_sem=recv_sems.at[i],
                device_id=(x, y, 1 - c), device_id_type=MESH)
            cp.start()
            cps.append(cp)
        for cp in cps:
            cp.wait()

    return pl.pallas_call(
        body, name=name, in_specs=[ANY] * n, out_specs=[ANY] * n,
        out_shape=[_sds(b.shape, b.dtype) for b in bufs],
        scratch_shapes=[pltpu.SemaphoreType.DMA((n,)), pltpu.SemaphoreType.DMA((n,))],
        input_output_aliases={i: i for i in range(n)},
        compiler_params=pltpu.CompilerParams(has_side_effects=True),
    )(*bufs)


def _pack(arrs, rows_mult):
    flat = jnp.concatenate([a.reshape(-1).astype(F32) for a in arrs])
    rows = -(-flat.size // LANE)
    rows = -(-rows // rows_mult) * rows_mult
    return jnp.pad(flat, (0, rows * LANE - flat.size)).reshape(rows, LANE)


def _unpack(packed, like):
    flat = packed.reshape(-1)
    out, pos = [], 0
    for a in like:
        n = math.prod(a.shape)
        out.append(flat[pos:pos + n].reshape(a.shape))
        pos += n
    return out


def _adamw_nd(w, g, m, v, name):
    shape = w.shape
    C = shape[-1]
    d, nm, nv = _adamw(w.reshape(-1, C), g.reshape(-1, C), m.reshape(-1, C), v.reshape(-1, C), name)
    return d.reshape(shape), nm.reshape(shape), nv.reshape(shape)


def kernel(x, a_norm, a_w_in, a_v_norm, a_w_s, a_b_s, a_w_out, kv_norm, w_kvf, b_f, k_norm, b_norm, b_w_qg, q_norm, b_w_out, f_norm, f_w_up, f_conv_w, f_conv_b, f_w_down, final_norm, loss_target, m_a_norm, m_a_w_in, m_a_v_norm, m_a_w_s, m_a_b_s, m_a_w_out, m_kv_norm, m_w_kvf, m_b_f, m_k_norm, m_b_norm, m_b_w_qg, m_q_norm, m_b_w_out, m_f_norm, m_f_w_up, m_f_conv_w, m_f_conv_b, m_f_w_down, m_final_norm, v_a_norm, v_a_w_in, v_a_v_norm, v_a_w_s, v_a_b_s, v_a_w_out, v_kv_norm, v_w_kvf, v_b_f, v_k_norm, v_b_norm, v_b_w_qg, v_q_norm, v_b_w_out, v_f_norm, v_f_w_up, v_f_conv_w, v_f_conv_b, v_f_w_down, v_final_norm):
    weights = dict(a_norm=a_norm, a_w_in=a_w_in, a_v_norm=a_v_norm, a_w_s=a_w_s, a_b_s=a_b_s, a_w_out=a_w_out, kv_norm=kv_norm, w_kvf=w_kvf, b_f=b_f, k_norm=k_norm, b_norm=b_norm, b_w_qg=b_w_qg, q_norm=q_norm, b_w_out=b_w_out, f_norm=f_norm, f_w_up=f_w_up, f_conv_w=f_conv_w, f_conv_b=f_conv_b, f_w_down=f_w_down, final_norm=final_norm)
    mom1 = dict(a_norm=m_a_norm, a_w_in=m_a_w_in, a_v_norm=m_a_v_norm, a_w_s=m_a_w_s, a_b_s=m_a_b_s, a_w_out=m_a_w_out, kv_norm=m_kv_norm, w_kvf=m_w_kvf, b_f=m_b_f, k_norm=m_k_norm, b_norm=m_b_norm, b_w_qg=m_b_w_qg, q_norm=m_q_norm, b_w_out=m_b_w_out, f_norm=m_f_norm, f_w_up=m_f_w_up, f_conv_w=m_f_conv_w, f_conv_b=m_f_conv_b, f_w_down=m_f_w_down, final_norm=m_final_norm)
    mom2 = dict(a_norm=v_a_norm, a_w_in=v_a_w_in, a_v_norm=v_a_v_norm, a_w_s=v_a_w_s, a_b_s=v_a_b_s, a_w_out=v_a_w_out, kv_norm=v_kv_norm, w_kvf=v_w_kvf, b_f=v_b_f, k_norm=v_k_norm, b_norm=v_b_norm, b_w_qg=v_b_w_qg, q_norm=v_q_norm, b_w_out=v_b_w_out, f_norm=v_f_norm, f_w_up=v_f_w_up, f_conv_w=v_f_conv_w, f_conv_b=v_f_conv_b, f_w_down=v_f_w_down, final_norm=v_final_norm)
    names = list(weights)
    big = ["a_w_in", "a_w_out", "w_kvf", "b_w_qg", "b_w_out", "f_w_up", "f_w_down"]
    small_sharded = ["a_norm", "a_v_norm", "f_conv_w"]
    small_repl = ["a_w_s", "a_b_s", "kv_norm", "b_f", "k_norm", "b_norm", "q_norm", "f_norm", "f_conv_b", "final_norm"]

    _, S, D = x.shape
    NA, NB, DEPTH = a_norm.shape[0], b_norm.shape[0], f_norm.shape[0]
    W = a_w_out.shape[1] * N_CHIPS
    G = a_w_s.shape[1]
    H = b_f.shape[0]
    ATT = H * HEAD
    F = f_w_down.shape[1] * N_CHIPS
    Ckv = w_kvf.shape[1]
    Cp = -(-Ckv // LANE) * LANE
    assert W == G * CHUNK and Ckv * N_CHIPS == 2 * ATT + H and S % CHUNK == 0
    core = lax.axis_index("c").astype(jnp.int32).reshape(1)
    me = (2 * lax.axis_index("x") + lax.axis_index("y")).astype(jnp.int32).reshape(1)

    small_local = [weights[k] for k in small_sharded]
    place = lambda p, nm, dt=BF16, layer=None: _place_part(p, me, dt, "place_" + nm, layer)
    b_small = place(_pack(small_local, 32)[None], "small", F32)
    b_up = [place(f_w_up, f"f_w_up{l}", layer=l) for l in range(DEPTH)]
    b_down = [place(f_w_down, f"f_w_down{l}", layer=l) for l in range(DEPTH)]
    g_up, g_down = [None] * DEPTH, [None] * DEPTH
    g_small, g_ain, g_aout = _gather_chips_async([b_small, place(a_w_in, "a_w_in"), place(a_w_out, "a_w_out")], 0, "gather_w0")
    for l in range(NA):
        g_up[l], g_down[l] = _gather_chips_async([b_up[l], b_down[l]], 1 + l, f"gather_w{1 + l}")
    g_kvf, g_bqg, g_bout = _gather_chips_async(
        [place(jnp.pad(w_kvf, ((0, 0), (0, Cp - Ckv)))[None], "w_kvf"), place(b_w_qg, "b_w_qg"), place(b_w_out, "b_w_out")],
        1 + NA, f"gather_w{1 + NA}")
    for l in range(NA, DEPTH):
        g_up[l], g_down[l] = _gather_chips_async([b_up[l], b_down[l]], 2 + l, f"gather_w{2 + l}")

    per_chip = [_unpack(g_small[j, 0], small_local) for j in range(N_CHIPS)]
    a_norm_f, a_vnorm_f, conv_w_f = [jnp.concatenate([per_chip[j][k] for j in range(N_CHIPS)], axis=-1) for k in range(3)]
    cw = conv_w_f.reshape(DEPTH, 3, 2, F).transpose(0, 2, 1, 3)
    cb = f_conv_b.reshape(DEPTH, 2, 1, F)
    tril = jnp.tril(jnp.ones((CHUNK, CHUNK), dtype=bool))
    wm = jnp.where(tril, a_w_s, 0.0).astype(BF16)
    wmt = jnp.swapaxes(wm, -1, -2)
    bcol = a_b_s[..., None]
    kvf_full = g_kvf[:, 0, :, :Ckv].transpose(1, 0, 2).reshape(D, N_CHIPS * Ckv)
    w_kv = jnp.stack([kvf_full[:, :ATT], kvf_full[:, ATT:2 * ATT]])
    w_f = jnp.pad(kvf_full[:, 2 * ATT:], ((0, 0), (0, LANE - H)))
    bf_pad = jnp.pad(b_f, (0, LANE - H))[None]
    row = lambda v: v.reshape(1, -1)

    h = x[0]
    target = loss_target[0]
    saved = [dict() for _ in range(DEPTH)]
    kvs = {}
    for l in range(DEPTH):
        sv = saved[l]
        sv["h_m"] = h
        if l < NA:
            xn = _rms_fwd(h, row(a_norm_f[l]), f"a{l}_norm")
            zpre = _mm(m2(xn), mgc(g_ain, l), "nn", m2(_sds((S, 2 * W), BF16)), tm=1024, tn=512, name=f"a{l}_in")
            gated = _sgu_fwd(zpre, wm[l], bcol[l], row(a_vnorm_f[l]), f"a{l}_sgu")
            h = _mm(m2(gated), mgr(g_aout, l), "nn", m2(_sds((S, D), F32)), res=m2(h), tm=1024, tn=1024, name=f"a{l}_out")
            sv.update(xn_m=xn, zpre=zpre, gated=gated)
        else:
            j = l - NA
            xn = _rms_fwd(h, row(b_norm[j]), f"b{j}_norm")
            qg = _mm(m2(xn), mgc(g_bqg, j), "nn", mcs(_sds((2, S, ATT), BF16)), tm=1024, tn=512, name=f"b{j}_qg")
            qn = _headnorm_fwd(qg, 0, row(q_norm[j]) * QK_SCALE, f"b{j}_qnorm")
            o, lse = _attn_fwd(qn, kvs["kn"], kvs["kv"], kvs["crow"], f"b{j}_attn")
            og = _gate_fwd(o, qg, f"b{j}_gate")
            h = _mm(m2(og), mgr(g_bout, j), "nn", m2(_sds((S, D), F32)), res=m2(h), tm=1024, tn=1024, name=f"b{j}_out")
            sv.update(xn_m=xn, qg=qg, qn=qn, o=o, lse=lse, og=og)
        sv["h_f"] = h
        xn = _rms_fwd(h, row(f_norm[l]), f"f{l}_norm")
        hup = _mm(m2(xn), mgc(g_up[l], 0), "nn", mcs(_sds((2, S, F), BF16)), tm=1024, tn=1408, name=f"f{l}_up")
        act = _convgate_fwd(hup, cw[l], cb[l], f"f{l}_conv")
        h = _mm(m2(act), mgr(g_down[l], 0), "nn", m2(_sds((S, D), F32)), res=m2(h), tm=1024, tn=1024, tk=1408, name=f"f{l}_down")
        sv.update(xn_f=xn, hup=hup, act=act)
        if l == NA - 1:
            xn_kv = _rms_fwd(h, row(kv_norm), "kv_norm")
            kv = _mm(m2(xn_kv), mcs(w_kv), "nn", mcs(_sds((2, S, ATT), BF16)), tm=1024, tn=512, name="kv_proj")
            fpre = _mm(m2(xn_kv), m2(w_f), "nn", m2(_sds((S, LANE), F32)), tm=1024, name="kv_fproj")
            kn = _headnorm_fwd(kv, 0, row(k_norm), "kv_knorm")
            cums = _logf_cumsum(fpre, bf_pad, "kv_cumsum")
            cT = cums[:, :H].T
            kvs.update(h=h, xn=xn_kv, kv=kv, fpre=fpre, kn=kn, crow=cT[:, None, :])

    loss11, dh, dhb, d_final = _final_loss(h, row(final_norm), target, "final_loss")
    loss = lax.psum(loss11[0, 0], ("x", "y", "c"))

    G_ain = lax.empty(g_ain.shape, BF16)
    G_aout = lax.empty(g_aout.shape, BF16)
    G_bqg = lax.empty(g_bqg.shape, BF16)
    G_bout = lax.empty(g_bout.shape, BF16)
    G_up = lax.empty((N_CHIPS,) + f_w_up.shape, BF16)
    G_down = lax.empty((N_CHIPS,) + f_w_down.shape, BF16)
    d_anorm, d_avnorm, d_ws, d_bs = [None] * NA, [None] * NA, [None] * NA, [None] * NA
    d_bnorm, d_qnorm = [None] * NB, [None] * NB
    d_fnorm, d_cw, d_cb = [None] * DEPTH, [None] * DEPTH, [None] * DEPTH
    dkn, dvv, dck = [], [], []
    G_kvf = d_kvnorm = d_bf = d_knorm = None
    for l in reversed(range(DEPTH)):
        sv = saved[l]
        if l == NA - 1:
            dkv, d_knorm = _headnorm_bwd(dkn, kvs["kv"], row(k_norm), dvv, "kv_knorm_bwd")
            dc = sum(dck)
            dc = jnp.pad(dc, ((0, 0), (0, LANE - H)))
            df, d_bf = _logf_cumsum_bwd(dc, kvs["fpre"], bf_pad, "kv_cumsum_bwd")
            dxn = _mm(mcs(dkv), mcs(w_kv), "nt", m2(_sds((S, D), F32)), tm=1024, tn=1024, tk=2048, name="kv_proj_dx")
            dxn = _mm(m2(df), m2(w_f), "nt", m2(_sds((S, D), BF16)), res=m2(dxn), tm=1024, tn=512, name="kv_fproj_dx")
            dw_kv = _mm(m2(kvs["xn"]), mcs(dkv), "tn", mcs(_sds((2, D, ATT), BF16)), tm=1024, tn=1024, tk=2048, name="kv_proj_dw")
            dw_f = _mm(m2(kvs["xn"]), m2(df), "tn", m2(_sds((D, LANE), BF16)), tm=512, tk=2048, name="kv_fproj_dw")
            dh, dhb, d_kvnorm = _rms_bwd(dxn, kvs["h"], row(kv_norm), dh, "kv_norm_bwd")
            dfull = jnp.concatenate([dw_kv[0], dw_kv[1], dw_f[:, :H]], axis=1)
            G_kvf = jnp.pad(dfull.reshape(D, N_CHIPS, Ckv).transpose(1, 0, 2), ((0, 0), (0, 0), (0, Cp - Ckv)))[:, None]
        dact = _mm(m2(dhb), mgr(g_down[l], 0), "nt", m2(_sds((S, F), BF16)), tm=1024, tn=1408, tk=2048, name=f"f{l}_down_dx")
        G_down = _mm(m2(sv["act"]), m2(dhb), "tn", mgr(G_down, l), alias=True, tm=1408, tn=1024, tk=2048, name=f"f{l}_down_dw")
        dhup, d_cw[l], d_cb[l] = _convgate_bwd(sv["hup"], dact, cw[l], cb[l], f"f{l}_conv_bwd")
        dxn = _mm(mcs(dhup), mgc(g_up[l], 0), "nt", m2(_sds((S, D), BF16)), tm=1024, tn=2048, tk=1408, name=f"f{l}_up_dx")
        G_up = _mm(m2(sv["xn_f"]), mcs(dhup), "tn", mgc(G_up, l), alias=True, tm=1024, tn=1408, tk=2048, name=f"f{l}_up_dw")
        dh, dhb, d_fnorm[l] = _rms_bwd(dxn, sv["h_f"], row(f_norm[l]), dh, f"f{l}_norm_bwd")
        if l >= NA:
            j = l - NA
            dog = _mm(m2(dhb), mgr(g_bout, j), "nt", m2(_sds((S, ATT), BF16)), tm=1024, tn=512, tk=2048, name=f"b{j}_out_dx")
            G_bout = _mm(m2(sv["og"]), m2(dhb), "tn", mgr(G_bout, j), alias=True, tm=512, tn=1024, tk=4096, name=f"b{j}_out_dw")
            do, dgate, delta = _gate_bwd(dog, sv["o"], sv["qg"], f"b{j}_gate_bwd")
            dqn, drow, dkn_j, dv_j, dcol = _attn_bwd(sv["qn"], kvs["kn"], kvs["kv"], kvs["crow"], do, sv["lse"], delta, f"b{j}_attn_bwd")
            dkn.append(dkn_j); dvv.append(dv_j); dck.append(drow[:, :, 0].T - dcol[:, 0, :].T)
            dqg, d_qs = _headnorm_bwd([dqn], sv["qg"], row(q_norm[j]) * QK_SCALE, [dgate], f"b{j}_qnorm_bwd")
            d_qnorm[j] = d_qs * QK_SCALE
            dxn = _mm(mcs(dqg), mgc(g_bqg, j), "nt", m2(_sds((S, D), BF16)), tm=1024, tn=2048, tk=1024, name=f"b{j}_qg_dx")
            G_bqg = _mm(m2(sv["xn_m"]), mcs(dqg), "tn", mgc(G_bqg, j), alias=True, tm=1024, tn=1024, tk=2048, name=f"b{j}_qg_dw")
            dh, dhb, d_bnorm[j] = _rms_bwd(dxn, sv["h_m"], row(b_norm[j]), dh, f"b{j}_norm_bwd")
        else:
            dgated = _mm(m2(dhb), mgr(g_aout, l), "nt", m2(_sds((S, W), BF16)), tm=1024, tn=512, tk=2048, name=f"a{l}_out_dx")
            G_aout = _mm(m2(sv["gated"]), m2(dhb), "tn", mgr(G_aout, l), alias=True, tm=512, tn=1024, tk=4096, name=f"a{l}_out_dw")
            dz, d_ws[l], d_bs[l], d_avnorm[l] = _sgu_bwd(sv["zpre"], dgated, wm[l], wmt[l], bcol[l], row(a_vnorm_f[l]), f"a{l}_sgu_bwd")
            dxn = _mm(m2(dz), mgc(g_ain, l), "nt", m2(_sds((S, D), BF16)), tm=1024, tn=2048, tk=1024, name=f"a{l}_in_dx")
            G_ain = _mm(m2(sv["xn_m"]), m2(dz), "tn", mgc(G_ain, l), alias=True, tm=1024, tn=1024, tk=2048, name=f"a{l}_in_dw")
            dh, dhb, d_anorm[l] = _rms_bwd(dxn, sv["h_m"], row(a_norm_f[l]), dh, f"a{l}_norm_bwd")
    grad_x = dh[None]

    full = dict(
        a_norm=jnp.concatenate(d_anorm, axis=0), a_v_norm=jnp.concatenate(d_avnorm, axis=0),
        f_conv_w=jnp.stack(d_cw).transpose(0, 2, 1, 3).reshape(DEPTH, 3, 2 * F),
        a_w_s=jnp.where(tril, jnp.stack(d_ws), 0.0), a_b_s=jnp.stack(d_bs)[..., 0],
        kv_norm=d_kvnorm[0], b_f=d_bf[0, :H], k_norm=d_knorm[0], b_norm=jnp.concatenate(d_bnorm, axis=0),
        q_norm=jnp.concatenate(d_qnorm, axis=0), f_norm=jnp.concatenate(d_fnorm, axis=0),
        f_conv_b=jnp.stack(d_cb).reshape(DEPTH, 2 * F), final_norm=d_final[0])
    shard_rows = []
    for j in range(N_CHIPS):
        pieces = []
        for k in small_sharded:
            n = weights[k].shape[-1]
            pieces.append(full[k][..., j * n:(j + 1) * n])
        shard_rows.append(_pack(pieces, 32))
    rs = shard_rows[0].shape[0]
    repl = _pack([full[k] for k in small_repl], N_CHIPS * 32)
    rr = repl.shape[0] // N_CHIPS
    G_small = jnp.concatenate([jnp.stack(shard_rows), repl.reshape(N_CHIPS, rr, LANE)], axis=1)[:, None]

    Gs = [G_ain, G_aout, G_kvf, G_bqg, G_bout, G_up, G_down, G_small]
    others = _swap_halves(Gs, "grads_swap")
    partial = [_pair_sum(g, o, core, g.dtype, f"grads_pair{i}") for i, (g, o) in enumerate(zip(Gs, others))]
    by_chip = _scatter_chips(partial, "grads_scatter")
    halves = [_sum_chips(q, p, me, core, f"grads_sum{i}") for i, (q, p) in enumerate(zip(by_chip[:-1], partial[:-1]))]
    small_half = _sum_small(lax.dynamic_index_in_dim(partial[-1], me[0], 0, keepdims=False), by_chip[-1], "grads_sum_small")
    hs = small_half.shape[1]
    halves.append(lax.dynamic_update_slice(jnp.zeros((1, 2 * hs, LANE), F32), small_half, (0, core[0] * hs, 0)))
    F_ain, F_aout, F_kvf, F_bqg, F_bout, F_up, F_down, F_small = _join_halves(halves, "grads_join")
    repl_buf = lax.dynamic_update_slice(jnp.zeros((N_CHIPS, 1, rr, LANE), F32), F_small[None, :, rs:, :], (me[0], 0, 0, 0))
    (repl_all,) = _gather_chips([repl_buf], "gather_small_grads")

    grads = dict(a_w_in=F_ain, a_w_out=F_aout, w_kvf=F_kvf[0, :, :Ckv], b_w_qg=F_bqg, b_w_out=F_bout, f_w_up=F_up, f_w_down=F_down)
    for k, gk in zip(small_sharded, _unpack(F_small[0, :rs], small_local)):
        grads[k] = gk
    for k, gk in zip(small_repl, _unpack(repl_all.reshape(N_CHIPS * rr, LANE), [weights[k] for k in small_repl])):
        grads[k] = gk

    delta, new_m, new_v = {}, {}, {}
    for k in big:
        delta[k], new_m[k], new_v[k] = _adamw_nd(weights[k], grads[k], mom1[k], mom2[k], f"adamw_{k}")
    small = small_sharded + small_repl
    packed = [_pack([t[k] for k in small], 8) for t in (weights, grads, mom1, mom2)]
    outs = _adamw(*packed, "adamw_small")
    like = [weights[k] for k in small]
    for t, o in zip((delta, new_m, new_v), outs):
        for k, a in zip(small, _unpack(o, like)):
            t[k] = a

    return (loss, grad_x, *[grads[k] for k in names], *[delta[k] for k in names],
            *[new_m[k] for k in names], *[new_v[k] for k in names])
```

```python
import functools
import math

import jax
import jax.numpy as jnp
from jax import lax
from jax.experimental import pallas as pl
from jax.experimental.pallas import tpu as pltpu
from jax.experimental.pallas import tpu_sc as plsc

F32, BF16 = jnp.float32, jnp.bfloat16
EPS = 1e-6
CHUNK = 128
HEAD = 128
LANE = 128
HALO = 16
N_CHIPS = 4
VMEM_LIMIT = 48 * 1024 * 1024
MESH = pl.DeviceIdType.MESH
ANY = pl.BlockSpec(memory_space=pl.ANY)

ADAM_LR, ADAM_B1, ADAM_B2, ADAM_EPS, ADAM_WD, ADAM_STEP = 0.001, 0.9, 0.999, 1e-08, 0.01, 10
NEG = -1e30
QK_SCALE = HEAD ** -0.5


def _cp(*sem):
    return pltpu.CompilerParams(dimension_semantics=sem, vmem_limit_bytes=VMEM_LIMIT)


def _t(dim, pref, mult=LANE):
    if dim <= pref:
        return dim
    t = (pref // mult) * mult
    while t >= mult:
        if dim % t == 0:
            return t
        t -= mult
    return dim


def _sds(shape, dtype):
    return jax.ShapeDtypeStruct(tuple(shape), dtype)


class Mat:
    def __init__(self, arr, shape, rlim, clim, block, index):
        self.arr, self.shape, self.rlim, self.clim = arr, shape, rlim, clim
        self._block, self._index = block, index

    def spec(self, tr, tc, gmap):
        assert self.rlim % tr == 0 and self.clim % tc == 0, (self.shape, self.rlim, self.clim, tr, tc)
        index = self._index(tr, tc)
        return pl.BlockSpec(self._block(tr, tc), lambda *g: index(*gmap(*g)))


def m2(arr):
    R, C = arr.shape
    return Mat(arr, (R, C), R, C, lambda tr, tc: (tr, tc), lambda tr, tc: (lambda i, j: (i, j)))


def mcs(arr):
    ns, R, Cs = arr.shape
    return Mat(arr, (R, ns * Cs), R, Cs, lambda tr, tc: (None, tr, tc),
               lambda tr, tc: (lambda i, j: (j // (Cs // tc), i, j % (Cs // tc))))


def mhalf(arr, p):
    ns, R, Cs = arr.shape
    return Mat(arr, (R, Cs), R, Cs, lambda tr, tc: (None, tr, tc), lambda tr, tc: (lambda i, j: (p, i, j)))


def mgc(arr, l):
    ns, L, R, Cs = arr.shape
    return Mat(arr, (R, ns * Cs), R, Cs, lambda tr, tc: (None, None, tr, tc),
               lambda tr, tc: (lambda i, j: (j // (Cs // tc), l, i, j % (Cs // tc))))


def mgr(arr, l):
    ns, L, Rs, C = arr.shape
    return Mat(arr, (ns * Rs, C), Rs, C, lambda tr, tc: (None, None, tr, tc),
               lambda tr, tc: (lambda i, j: (i // (Rs // tr), l, i % (Rs // tr), j)))


_DIMS = {"nn": ((1,), (0,)), "nt": ((1,), (1,)), "tn": ((0,), (0,))}


def _mm(a, b, mode, out, *, res=None, alias=False, after=None, tm=512, tn=512, tk=4096, name):
    if mode == "tn":
        (K, M), (K2, N) = a.shape, b.shape
        alim_m, alim_k, blim_k, blim_n = a.clim, a.rlim, b.rlim, b.clim
    elif mode == "nt":
        (M, K), (N, K2) = a.shape, b.shape
        alim_m, alim_k, blim_k, blim_n = a.rlim, a.clim, b.clim, b.rlim
    else:
        (M, K), (K2, N) = a.shape, b.shape
        alim_m, alim_k, blim_k, blim_n = a.rlim, a.clim, b.rlim, b.clim
    assert K == K2 and out.shape == (M, N), (name, a.shape, b.shape, out.shape)
    tm = _t(math.gcd(alim_m, out.rlim), tm)
    tn = _t(math.gcd(blim_n, out.clim), tn)
    tk = _t(math.gcd(alim_k, blim_k), tk)
    grid = (M // tm, N // tn, K // tk)
    nk = grid[2]
    if mode == "tn":
        a_spec = a.spec(tk, tm, lambda i, j, k: (k, i))
    else:
        a_spec = a.spec(tm, tk, lambda i, j, k: (i, k))
    if mode == "nt":
        b_spec = b.spec(tn, tk, lambda i, j, k: (j, k))
    else:
        b_spec = b.spec(tk, tn, lambda i, j, k: (k, j))
    o_spec = out.spec(tm, tn, lambda i, j, k: (i, j))
    operands, in_specs = [a.arr, b.arr], [a_spec, b_spec]
    if res is not None:
        operands.append(res.arr)
        in_specs.append(res.spec(tm, tn, lambda i, j, k: (i, j)))
    aliases = {}
    if alias:
        aliases = {len(operands): 0}
        operands.append(out.arr)
        in_specs.append(ANY)
    if after is not None:
        operands.append(after)
        in_specs.append(ANY)
    dims = (_DIMS[mode], ((), ()))
    has_res = res is not None

    def body(*refs):
        a_ref, b_ref = refs[0], refs[1]
        res_ref = refs[2] if has_res else None
        n_in = 2 + has_res + alias + (after is not None)
        o_ref = refs[n_in]
        p = lax.dot_general(a_ref[...].astype(BF16), b_ref[...].astype(BF16), dims, preferred_element_type=F32)

        def finish(v):
            if has_res:
                v = v + res_ref[...].astype(F32)
            o_ref[...] = v.astype(o_ref.dtype)

        if nk == 1:
            finish(p)
        else:
            acc = refs[n_in + 1]
            k = pl.program_id(2)

            @pl.when(k == 0)
            def _():
                acc[...] = p

            @pl.when(k > 0)
            def _():
                acc[...] += p

            @pl.when(k == nk - 1)
            def _():
                finish(acc[...])

    return pl.pallas_call(
        body, name=name, grid=grid, in_specs=in_specs, out_specs=o_spec,
        out_shape=_sds(out.arr.shape, out.arr.dtype),
        scratch_shapes=[pltpu.VMEM((tm, tn), F32)] if nk > 1 else [],
        input_output_aliases=aliases,
        compiler_params=_cp("parallel", "parallel", "arbitrary"),
    )(*operands)


def _rms_fwd(x, g, name):
    S, D = x.shape
    tr = _t(S, 512)

    def body(x_ref, g_ref, o_ref):
        xv = x_ref[...]
        r = lax.rsqrt(jnp.mean(xv * xv, axis=-1, keepdims=True) + EPS)
        o_ref[...] = ((xv * r) * g_ref[...]).astype(o_ref.dtype)

    return pl.pallas_call(
        body, name=name, grid=(S // tr,),
        in_specs=[pl.BlockSpec((tr, D), lambda i: (i, 0)), pl.BlockSpec((1, D), lambda i: (0, 0))],
        out_specs=pl.BlockSpec((tr, D), lambda i: (i, 0)), out_shape=_sds((S, D), BF16),
        compiler_params=_cp("parallel"),
    )(x, g)


def _rms_bwd(dy, x, g, dres, name):
    S, D = x.shape
    tr = _t(S, 512)

    def body(dy_ref, x_ref, g_ref, dres_ref, dx_ref, dxb_ref, dg_ref):
        xv = x_ref[...]
        dyv = dy_ref[...].astype(F32)
        r = lax.rsqrt(jnp.mean(xv * xv, axis=-1, keepdims=True) + EPS)
        xh = xv * r
        dxh = dyv * g_ref[...]
        m = jnp.mean(dxh * xh, axis=-1, keepdims=True)
        dx = dres_ref[...] + r * (dxh - xh * m)
        dx_ref[...] = dx
        dxb_ref[...] = dx.astype(dxb_ref.dtype)

        @pl.when(pl.program_id(0) == 0)
        def _():
            dg_ref[...] = jnp.zeros_like(dg_ref)

        dg_ref[...] += jnp.sum(dyv * xh, axis=0, keepdims=True)

    row = pl.BlockSpec((tr, D), lambda i: (i, 0))
    vec = pl.BlockSpec((1, D), lambda i: (0, 0))
    return pl.pallas_call(
        body, name=name, grid=(S // tr,), in_specs=[row, row, vec, row], out_specs=[row, row, vec],
        out_shape=[_sds((S, D), F32), _sds((S, D), BF16), _sds((1, D), F32)], compiler_params=_cp("arbitrary"),
    )(dy, x, g, dres)


def _final_loss(h, g, target, name):
    S, D = h.shape
    tr = _t(S, 512)

    def body(x_ref, g_ref, t_ref, loss_ref, dx_ref, dxb_ref, dg_ref):
        xv = x_ref[...]
        r = lax.rsqrt(jnp.mean(xv * xv, axis=-1, keepdims=True) + EPS)
        xh = xv * r
        err = xh * g_ref[...] - t_ref[...]
        part = 0.5 * jnp.sum(jnp.mean(err * err, axis=-1, keepdims=True), axis=0, keepdims=True)
        dyv = err * (1.0 / D)
        dxh = dyv * g_ref[...]
        m = jnp.mean(dxh * xh, axis=-1, keepdims=True)
        dx = r * (dxh - xh * m)
        dx_ref[...] = dx
        dxb_ref[...] = dx.astype(dxb_ref.dtype)

        @pl.when(pl.program_id(0) == 0)
        def _():
            dg_ref[...] = jnp.zeros_like(dg_ref)
            loss_ref[...] = jnp.zeros_like(loss_ref)

        dg_ref[...] += jnp.sum(dyv * xh, axis=0, keepdims=True)
        loss_ref[...] += part

    row = pl.BlockSpec((tr, D), lambda i: (i, 0))
    vec = pl.BlockSpec((1, D), lambda i: (0, 0))
    one = pl.BlockSpec((1, 1), lambda i: (0, 0))
    return pl.pallas_call(
        body, name=name, grid=(S // tr,), in_specs=[row, vec, row], out_specs=[one, row, row, vec],
        out_shape=[_sds((1, 1), F32), _sds((S, D), F32), _sds((S, D), BF16), _sds((1, D), F32)], compiler_params=_cp("arbitrary"),
    )(h, g, target)


_RSQRT2 = 0.7071067811865476
_RSQRT2PI = 0.3989422804014327


def _gelu(x):
    return 0.5 * x * (1.0 + lax.erf(x * _RSQRT2))


def _gelu_grad(x):
    return 0.5 * (1.0 + lax.erf(x * _RSQRT2)) + x * (jnp.exp(-0.5 * x * x) * _RSQRT2PI)


def _sgu_fwd(zpre, wm, bcol, vnorm, name):
    S, W2 = zpre.shape
    W = W2 // 2
    G = W // CHUNK

    def body(z_ref, wm_ref, b_ref, vn_ref, o_ref):
        zp = z_ref[...].astype(F32)
        u = _gelu(zp[:, :W])
        v = _gelu(zp[:, W:])
        rv = lax.rsqrt(jnp.mean(v * v, axis=-1, keepdims=True) + EPS)
        vn = ((v * rv) * vn_ref[...]).astype(BF16)
        for g in range(G):
            sl = slice(g * CHUNK, (g + 1) * CHUNK)
            mixed = jnp.dot(wm_ref[g], vn[:, sl], preferred_element_type=F32) + b_ref[g]
            o_ref[:, sl] = (u[:, sl] * mixed).astype(o_ref.dtype)

    return pl.pallas_call(
        body, name=name, grid=(S // CHUNK,),
        in_specs=[pl.BlockSpec((CHUNK, W2), lambda i: (i, 0)),
                  pl.BlockSpec((G, CHUNK, CHUNK), lambda i: (0, 0, 0)),
                  pl.BlockSpec((G, CHUNK, 1), lambda i: (0, 0, 0)),
                  pl.BlockSpec((1, W), lambda i: (0, 0))],
        out_specs=pl.BlockSpec((CHUNK, W), lambda i: (i, 0)), out_shape=_sds((S, W), BF16),
        compiler_params=_cp("parallel"),
    )(zpre, wm, bcol, vnorm)


def _sgu_bwd(zpre, dgated, wm, wmt, bcol, vnorm, name):
    S, W2 = zpre.shape
    W = W2 // 2
    G = W // CHUNK

    def body(z_ref, dg_ref, wm_ref, wmt_ref, b_ref, vn_ref, dz_ref, dws_ref, dbs_ref, dvn_ref):
        @pl.when(pl.program_id(0) == 0)
        def _():
            dws_ref[...] = jnp.zeros_like(dws_ref)
            dbs_ref[...] = jnp.zeros_like(dbs_ref)
            dvn_ref[...] = jnp.zeros_like(dvn_ref)

        zp = z_ref[...].astype(F32)
        zu, zv = zp[:, :W], zp[:, W:]
        u = _gelu(zu)
        v = _gelu(zv)
        rv = lax.rsqrt(jnp.mean(v * v, axis=-1, keepdims=True) + EPS)
        vh = v * rv
        vn = (vh * vn_ref[...]).astype(BF16)
        dgv = dg_ref[...].astype(F32)
        du_parts, dvn_parts = [], []
        for g in range(G):
            sl = slice(g * CHUNK, (g + 1) * CHUNK)
            vg = vn[:, sl]
            mixed = jnp.dot(wm_ref[g], vg, preferred_element_type=F32) + b_ref[g]
            dgg = dgv[:, sl]
            du_parts.append(dgg * mixed)
            dmixed = dgg * u[:, sl]
            dbs_ref[g] += jnp.sum(dmixed, axis=1, keepdims=True)
            dmb = dmixed.astype(BF16)
            dws_ref[g] += lax.dot_general(dmb, vg, (_DIMS["nt"], ((), ())), preferred_element_type=F32)
            dvn_parts.append(jnp.dot(wmt_ref[g], dmb, preferred_element_type=F32))
        du = jnp.concatenate(du_parts, axis=1)
        dvn = jnp.concatenate(dvn_parts, axis=1)
        dvn_ref[...] += jnp.sum(dvn * vh, axis=0, keepdims=True)
        dvh = dvn * vn_ref[...]
        dv = rv * (dvh - vh * jnp.mean(dvh * vh, axis=-1, keepdims=True))
        dz_ref[:, :W] = (du * _gelu_grad(zu)).astype(dz_ref.dtype)
        dz_ref[:, W:] = (dv * _gelu_grad(zv)).astype(dz_ref.dtype)

    full3 = lambda shape: pl.BlockSpec(shape, lambda i: (0, 0, 0))
    return pl.pallas_call(
        body, name=name, grid=(S // CHUNK,),
        in_specs=[pl.BlockSpec((CHUNK, W2), lambda i: (i, 0)), pl.BlockSpec((CHUNK, W), lambda i: (i, 0)),
                  full3((G, CHUNK, CHUNK)), full3((G, CHUNK, CHUNK)), full3((G, CHUNK, 1)),
                  pl.BlockSpec((1, W), lambda i: (0, 0))],
        out_specs=[pl.BlockSpec((CHUNK, W2), lambda i: (i, 0)), full3((G, CHUNK, CHUNK)), full3((G, CHUNK, 1)),
                   pl.BlockSpec((1, W), lambda i: (0, 0))],
        out_shape=[_sds((S, W2), BF16), _sds((G, CHUNK, CHUNK), F32), _sds((G, CHUNK, 1), F32), _sds((1, W), F32)],
        compiler_params=_cp("arbitrary"),
    )(zpre, dgated, wm, wmt, bcol, vnorm)


def _conv_taps(h_ref, half, r0, R, tc):
    if r0 == 0:
        xe = jnp.concatenate([jnp.zeros((HALO, tc), F32), h_ref[half, 0:R, :].astype(F32)], axis=0)
    else:
        xe = h_ref[half, r0 - HALO:r0 + R, :].astype(F32)
    return xe[HALO:], pltpu.roll(xe, 1, 0)[HALO:], pltpu.roll(xe, 2, 0)[HALO:]


def _conv_apply(taps, w, b):
    x0, x1, x2 = taps
    return x2 * w[0:1] + x1 * w[1:2] + x0 * w[2:3] + b


def _convgate_fwd(hup, cw, cb, name):
    _, S, F = hup.shape
    tc = _t(F, 256)
    R = _t(S, 512)

    def body(h_ref, w_ref, b_ref, o_ref):
        for r0 in range(0, S, R):
            gate = _conv_apply(_conv_taps(h_ref, 0, r0, R, tc), w_ref[0], b_ref[0])
            val = _conv_apply(_conv_taps(h_ref, 1, r0, R, tc), w_ref[1], b_ref[1])
            o_ref[r0:r0 + R, :] = (gate * jax.nn.sigmoid(gate) * val).astype(o_ref.dtype)

    return pl.pallas_call(
        body, name=name, grid=(F // tc,),
        in_specs=[pl.BlockSpec((2, S, tc), lambda j: (0, 0, j)), pl.BlockSpec((2, 3, tc), lambda j: (0, 0, j)),
                  pl.BlockSpec((2, 1, tc), lambda j: (0, 0, j))],
        out_specs=pl.BlockSpec((S, tc), lambda j: (0, j)), out_shape=_sds((S, F), BF16),
        compiler_params=_cp("parallel"),
    )(hup, cw, cb)


def _convgate_bwd(hup, dact, cw, cb, name):
    _, S, F = hup.shape
    tc = _t(F, 256)
    R = _t(S, 512)

    def body(h_ref, da_ref, w_ref, b_ref, dh_ref, dw_ref, db_ref, dhc):
        dhc[:, S:S + HALO, :] = jnp.zeros((2, HALO, tc), F32)
        dw_acc = [[jnp.zeros((1, tc), F32) for _ in range(3)] for _ in range(2)]
        db_acc = [jnp.zeros((1, tc), F32) for _ in range(2)]
        for r0 in range(0, S, R):
            taps = [_conv_taps(h_ref, p, r0, R, tc) for p in range(2)]
            gate = _conv_apply(taps[0], w_ref[0], b_ref[0])
            val = _conv_apply(taps[1], w_ref[1], b_ref[1])
            da = da_ref[r0:r0 + R, :].astype(F32)
            sg = jax.nn.sigmoid(gate)
            d = [da * val * (sg * (1.0 + gate * (1.0 - sg))), da * (gate * sg)]
            for p in range(2):
                dhc[p, r0:r0 + R, :] = d[p]
                db_acc[p] = db_acc[p] + jnp.sum(d[p], axis=0, keepdims=True)
                for k in range(3):
                    dw_acc[p][k] = dw_acc[p][k] + jnp.sum(d[p] * taps[p][2 - k], axis=0, keepdims=True)
        for p in range(2):
            db_ref[p] = db_acc[p]
            dw_ref[p] = jnp.concatenate(dw_acc[p], axis=0)
            w = w_ref[p]
            for r0 in range(0, S, R):
                de = dhc[p, r0:r0 + R + HALO, :]
                d1 = pltpu.roll(de, R + HALO - 1, 0)[:R]
                d2 = pltpu.roll(de, R + HALO - 2, 0)[:R]
                dh_ref[p, r0:r0 + R, :] = (de[:R] * w[2:3] + d1 * w[1:2] + d2 * w[0:1]).astype(dh_ref.dtype)

    blk = lambda rows: pl.BlockSpec((2, rows, tc), lambda j: (0, 0, j))
    return pl.pallas_call(
        body, name=name, grid=(F // tc,),
        in_specs=[blk(S), pl.BlockSpec((S, tc), lambda j: (0, j)), blk(3), blk(1)],
        out_specs=[blk(S), blk(3), blk(1)],
        out_shape=[_sds((2, S, F), BF16), _sds((2, 3, F), F32), _sds((2, 1, F), F32)],
        scratch_shapes=[pltpu.VMEM((2, S + HALO, tc), F32)],
        compiler_params=_cp("parallel"),
    )(hup, dact, cw, cb)


def _headnorm_fwd(x3, part, gain, name):
    _, S, W = x3.shape
    tr = _t(S, 512)

    def body(x_ref, g_ref, o_ref):
        xv = x_ref[...].astype(F32)
        for h in range(W // HEAD):
            sl = slice(h * HEAD, (h + 1) * HEAD)
            xh = xv[:, sl]
            r = lax.rsqrt(jnp.mean(xh * xh, axis=-1, keepdims=True) + EPS)
            o_ref[:, sl] = ((xh * r) * g_ref[...]).astype(o_ref.dtype)

    return pl.pallas_call(
        body, name=name, grid=(S // tr,),
        in_specs=[pl.BlockSpec((None, tr, W), lambda i: (part, i, 0)), pl.BlockSpec((1, HEAD), lambda i: (0, 0))],
        out_specs=pl.BlockSpec((tr, W), lambda i: (i, 0)), out_shape=_sds((S, W), BF16),
        compiler_params=_cp("parallel"),
    )(x3, gain)


def _headnorm_bwd(dys, x3, gain, passes, name):
    _, S, W = x3.shape
    tr = _t(S, 256)
    nd, npass = len(dys), len(passes)

    def body(*refs):
        dy_refs = refs[:nd]
        x_ref, g_ref = refs[nd], refs[nd + 1]
        p_refs = refs[nd + 2:nd + 2 + npass]
        o_ref, dg_ref = refs[nd + 2 + npass], refs[nd + 3 + npass]

        @pl.when(pl.program_id(0) == 0)
        def _():
            dg_ref[...] = jnp.zeros_like(dg_ref)

        xv = x_ref[...].astype(F32)
        dyv = dy_refs[0][...].astype(F32)
        for r in dy_refs[1:]:
            dyv = dyv + r[...].astype(F32)
        dg = jnp.zeros((1, HEAD), F32)
        for h in range(W // HEAD):
            sl = slice(h * HEAD, (h + 1) * HEAD)
            xh = xv[:, sl]
            r = lax.rsqrt(jnp.mean(xh * xh, axis=-1, keepdims=True) + EPS)
            xh = xh * r
            dyh = dyv[:, sl]
            dg = dg + jnp.sum(dyh * xh, axis=0, keepdims=True)
            dxh = dyh * g_ref[...]
            o_ref[0, :, sl] = (r * (dxh - xh * jnp.mean(dxh * xh, axis=-1, keepdims=True))).astype(o_ref.dtype)
        dg_ref[...] += dg
        pv = p_refs[0][...].astype(F32)
        for r in p_refs[1:]:
            pv = pv + r[...].astype(F32)
        o_ref[1] = pv.astype(o_ref.dtype)

    row = pl.BlockSpec((tr, W), lambda i: (i, 0))
    vec = pl.BlockSpec((1, HEAD), lambda i: (0, 0))
    return pl.pallas_call(
        body, name=name, grid=(S // tr,),
        in_specs=[row] * nd + [pl.BlockSpec((None, tr, W), lambda i: (0, i, 0)), vec] + [row] * npass,
        out_specs=[pl.BlockSpec((2, tr, W), lambda i: (0, i, 0)), vec],
        out_shape=[_sds((2, S, W), BF16), _sds((1, HEAD), F32)], compiler_params=_cp("arbitrary"),
    )(*dys, x3, gain, *passes)


def _gate_fwd(o, qg, name):
    S, W = o.shape
    tr = _t(S, 512)

    def body(o_ref, g_ref, y_ref):
        y_ref[...] = (o_ref[...].astype(F32) * jax.nn.sigmoid(g_ref[...].astype(F32))).astype(y_ref.dtype)

    row = pl.BlockSpec((tr, W), lambda i: (i, 0))
    return pl.pallas_call(
        body, name=name, grid=(S // tr,), in_specs=[row, pl.BlockSpec((None, tr, W), lambda i: (1, i, 0))],
        out_specs=row, out_shape=_sds((S, W), BF16), compiler_params=_cp("parallel"),
    )(o, qg)


def _gate_bwd(dog, o, qg, name):
    S, W = o.shape
    H = W // HEAD
    tr = _t(S, 512)

    def body(dy_ref, o_ref, g_ref, do_ref, dg_ref, dl_ref):
        sg = jax.nn.sigmoid(g_ref[...].astype(F32))
        dy = dy_ref[...].astype(F32)
        ov = o_ref[...].astype(F32)
        dob = (dy * sg).astype(do_ref.dtype)
        do_ref[...] = dob
        dg_ref[...] = (dy * ov * (sg * (1.0 - sg))).astype(dg_ref.dtype)
        prod = dob.astype(F32) * ov
        for h in range(H):
            dl_ref[h] = jnp.sum(prod[:, h * HEAD:(h + 1) * HEAD], axis=-1, keepdims=True)

    row = pl.BlockSpec((tr, W), lambda i: (i, 0))
    return pl.pallas_call(
        body, name=name, grid=(S // tr,), in_specs=[row, row, pl.BlockSpec((None, tr, W), lambda i: (1, i, 0))],
        out_specs=[row, row, pl.BlockSpec((H, tr, 1), lambda i: (0, i, 0))],
        out_shape=[_sds((S, W), BF16), _sds((S, W), BF16), _sds((H, S, 1), F32)], compiler_params=_cp("parallel"),
    )(dog, o, qg)


def _logf_cumsum(fpre, bf, name):
    S, C = fpre.shape
    n = S // CHUNK

    def body(f_ref, b_ref, c_ref, carry):
        @pl.when(pl.program_id(0) == 0)
        def _():
            carry[...] = jnp.zeros_like(carry)

        lf = jax.nn.log_sigmoid(f_ref[...] + b_ref[...])
        tri = (lax.broadcasted_iota(jnp.int32, (CHUNK, CHUNK), 0)
               >= lax.broadcasted_iota(jnp.int32, (CHUNK, CHUNK), 1)).astype(F32)
        c_ref[...] = jnp.dot(tri, lf, preferred_element_type=F32, precision=lax.Precision.HIGHEST) + carry[...]
        carry[...] += jnp.sum(lf, axis=0, keepdims=True)

    return pl.pallas_call(
        body, name=name, grid=(n,),
        in_specs=[pl.BlockSpec((CHUNK, C), lambda i: (i, 0)), pl.BlockSpec((1, C), lambda i: (0, 0))],
        out_specs=pl.BlockSpec((CHUNK, C), lambda i: (i, 0)), out_shape=_sds((S, C), F32),
        scratch_shapes=[pltpu.VMEM((1, C), F32)], compiler_params=_cp("arbitrary"),
    )(fpre, bf)


def _logf_cumsum_bwd(dc, fpre, bf, name):
    S, C = fpre.shape
    n = S // CHUNK

    def body(dc_ref, f_ref, b_ref, df_ref, db_ref, carry):
        @pl.when(pl.program_id(0) == 0)
        def _():
            carry[...] = jnp.zeros_like(carry)
            db_ref[...] = jnp.zeros_like(db_ref)

        dcv = dc_ref[...]
        tri = (lax.broadcasted_iota(jnp.int32, (CHUNK, CHUNK), 0)
               <= lax.broadcasted_iota(jnp.int32, (CHUNK, CHUNK), 1)).astype(F32)
        dlf = jnp.dot(tri, dcv, preferred_element_type=F32, precision=lax.Precision.HIGHEST) + carry[...]
        carry[...] += jnp.sum(dcv, axis=0, keepdims=True)
        df = dlf * jax.nn.sigmoid(-(f_ref[...] + b_ref[...]))
        df_ref[...] = df.astype(df_ref.dtype)
        db_ref[...] += jnp.sum(df, axis=0, keepdims=True)

    rev = pl.BlockSpec((CHUNK, C), lambda i: (n - 1 - i, 0))
    vec = pl.BlockSpec((1, C), lambda i: (0, 0))
    return pl.pallas_call(
        body, name=name, grid=(n,), in_specs=[rev, rev, vec], out_specs=[rev, vec],
        out_shape=[_sds((S, C), BF16), _sds((1, C), F32)],
        scratch_shapes=[pltpu.VMEM((1, C), F32)], compiler_params=_cp("arbitrary"),
    )(dc, fpre, bf)


def _attn_tiles(S):
    return _t(S, 1024), _t(S, 512)


def _scores(q, k, ck, off, tq, tk, masked):
    s = lax.dot_general(q, k, (_DIMS["nt"], ((), ())), preferred_element_type=F32) - ck
    if masked:
        d = lax.broadcasted_iota(jnp.int32, (tq, tk), 1) - lax.broadcasted_iota(jnp.int32, (tq, tk), 0)
        s = jnp.where(d <= off, s, NEG)
    return s


def _attn_fwd(qn, kn, kv, crow, name):
    S, W = qn.shape
    H = W // HEAD
    tq, tk = _attn_tiles(S)
    nq, nk = S // tq, S // tk
    last = lambda i: ((i + 1) * tq - 1) // tk

    def body(q_ref, k_ref, v_ref, ck_ref, o_ref, lse_ref, m_sc, l_sc, acc_sc):
        qi, kj = pl.program_id(1), pl.program_id(2)

        @pl.when(kj == 0)
        def _():
            m_sc[...] = jnp.full_like(m_sc, NEG)
            l_sc[...] = jnp.zeros_like(l_sc)
            acc_sc[...] = jnp.zeros_like(acc_sc)

        def step(masked):
            s = _scores(q_ref[...], k_ref[...], ck_ref[...], qi * tq - kj * tk, tq, tk, masked)
            m_new = jnp.maximum(m_sc[...], jnp.max(s, axis=-1, keepdims=True))
            alpha = jnp.exp(m_sc[...] - m_new)
            p = jnp.exp(s - m_new)
            l_sc[...] = alpha * l_sc[...] + jnp.sum(p, axis=-1, keepdims=True)
            acc_sc[...] = alpha * acc_sc[...] + jnp.dot(p.astype(BF16), v_ref[...], preferred_element_type=F32)
            m_sc[...] = m_new

        @pl.when(kj <= last(qi))
        def _():
            step(True)

        @pl.when(kj == nk - 1)
        def _():
            o_ref[...] = (acc_sc[...] / l_sc[...]).astype(o_ref.dtype)
            lse_ref[...] = m_sc[...] + jnp.log(l_sc[...])

    return pl.pallas_call(
        body, name=name, grid=(H, nq, nk),
        in_specs=[pl.BlockSpec((tq, HEAD), lambda h, i, j: (i, h)),
                  pl.BlockSpec((tk, HEAD), lambda h, i, j: (jnp.minimum(j, last(i)), h)),
                  pl.BlockSpec((None, tk, HEAD), lambda h, i, j: (1, jnp.minimum(j, last(i)), h)),
                  pl.BlockSpec((None, 1, tk), lambda h, i, j: (h, 0, jnp.minimum(j, last(i))))],
        out_specs=[pl.BlockSpec((tq, HEAD), lambda h, i, j: (i, h)),
                   pl.BlockSpec((None, tq, 1), lambda h, i, j: (h, i, 0))],
        out_shape=[_sds((S, W), BF16), _sds((H, S, 1), F32)],
        scratch_shapes=[pltpu.VMEM((tq, 1), F32), pltpu.VMEM((tq, 1), F32), pltpu.VMEM((tq, HEAD), F32)],
        compiler_params=_cp("parallel", "parallel", "arbitrary"),
    )(qn, kn, kv, crow)


def _attn_bwd(qn, kn, kv, crow, do, lse, delta, name):
    S, W = qn.shape
    H = W // HEAD
    tq, tk = _attn_tiles(S)
    nq, nk = S // tq, S // tk
    first = lambda j: (j * tk) // tq

    def body(q_ref, k_ref, v_ref, ck_ref, do_ref, lse_ref, dl_ref, dq_ref, dr_ref, dk_ref, dv_ref, dc_ref, dk_sc, dv_sc, dc_sc):
        kj, qi = pl.program_id(1), pl.program_id(2)

        @pl.when(jnp.logical_and(kj == 0, qi == 0))
        def _():
            dq_ref[...] = jnp.zeros_like(dq_ref)
            dr_ref[...] = jnp.zeros_like(dr_ref)

        @pl.when(qi == 0)
        def _():
            dk_sc[...] = jnp.zeros_like(dk_sc)
            dv_sc[...] = jnp.zeros_like(dv_sc)
            dc_sc[...] = jnp.zeros_like(dc_sc)

        def step(masked):
            q, k, dov = q_ref[...], k_ref[...], do_ref[...]
            s = _scores(q, k, ck_ref[...], qi * tq - kj * tk, tq, tk, masked)
            p = jnp.exp(s - lse_ref[...])
            dp = lax.dot_general(dov, v_ref[...], (_DIMS["nt"], ((), ())), preferred_element_type=F32)
            ds = p * (dp - dl_ref[...])
            dsb = ds.astype(BF16)
            dc_sc[...] += jnp.sum(ds, axis=0, keepdims=True)
            dv_sc[...] += lax.dot_general(p.astype(BF16), dov, (_DIMS["tn"], ((), ())), preferred_element_type=F32)
            dk_sc[...] += lax.dot_general(dsb, q, (_DIMS["tn"], ((), ())), preferred_element_type=F32)
            rows = pl.ds(pl.multiple_of(qi * tq, tq), tq)
            dq_ref[rows, :] += jnp.dot(dsb, k, preferred_element_type=F32)
            dr_ref[rows, :] += jnp.sum(ds, axis=1, keepdims=True)

        below = (kj + 1) * tk - 1 <= qi * tq

        @pl.when(below)
        def _():
            step(False)

        @pl.when(jnp.logical_and(qi >= first(kj), jnp.logical_not(below)))
        def _():
            step(True)

        @pl.when(qi == nq - 1)
        def _():
            dk_ref[...] = dk_sc[...]
            dv_ref[...] = dv_sc[...]
            dc_ref[...] = dc_sc[...]

    qblk = pl.BlockSpec((tq, HEAD), lambda h, j, i: (jnp.maximum(i, first(j)), h))
    qcol = pl.BlockSpec((None, tq, 1), lambda h, j, i: (h, jnp.maximum(i, first(j)), 0))
    kblk = pl.BlockSpec((tk, HEAD), lambda h, j, i: (j, h))
    krow = pl.BlockSpec((None, 1, tk), lambda h, j, i: (h, 0, j))
    return pl.pallas_call(
        body, name=name, grid=(H, nk, nq),
        in_specs=[qblk, kblk, pl.BlockSpec((None, tk, HEAD), lambda h, j, i: (1, j, h)), krow, qblk, qcol, qcol],
        out_specs=[pl.BlockSpec((S, HEAD), lambda h, j, i: (0, h)), pl.BlockSpec((None, S, 1), lambda h, j, i: (h, 0, 0)),
                   kblk, kblk, krow],
        out_shape=[_sds((S, W), F32), _sds((H, S, 1), F32), _sds((S, W), F32), _sds((S, W), F32), _sds((H, 1, S), F32)],
        scratch_shapes=[pltpu.VMEM((tk, HEAD), F32), pltpu.VMEM((tk, HEAD), F32), pltpu.VMEM((1, tk), F32)],
        compiler_params=_cp("parallel", "arbitrary", "arbitrary"),
    )(qn, kn, kv, crow, do, lse, delta)


def _adamw_math(w, g, m, v):
    c1 = 1.0 - ADAM_B1 ** ADAM_STEP
    c2 = 1.0 - ADAM_B2 ** ADAM_STEP
    nm = ADAM_B1 * m + (1.0 - ADAM_B1) * g
    nv = ADAM_B2 * v + (1.0 - ADAM_B2) * (g * g)
    return -ADAM_LR * ((nm / c1) / (jnp.sqrt(nv / c2) + ADAM_EPS) + ADAM_WD * w), nm, nv


def _adamw_layer(w, g, m, v, layer, prev, name):
    L, R, C = w.shape
    tr = _t(R, max(8, (1 << 19) // C), 8)

    def body(w_ref, g_ref, m_ref, v_ref, *rest):
        go_ref, d_ref, nm_ref, nv_ref = rest[4:]
        gv = g_ref[...]
        go_ref[...] = gv
        d_ref[...], nm_ref[...], nv_ref[...] = _adamw_math(w_ref[...], gv, m_ref[...], v_ref[...])

    lay = pl.BlockSpec((None, tr, C), lambda i: (layer, i, 0))
    return pl.pallas_call(
        body, name=name, grid=(R // tr,),
        in_specs=[lay, pl.BlockSpec((None, tr, C), lambda i: (0, i, 0)), lay, lay] + [ANY] * 4, out_specs=[lay] * 4,
        out_shape=[_sds((L, R, C), F32)] * 4, input_output_aliases={4: 0, 5: 1, 6: 2, 7: 3},
        compiler_params=_cp("parallel"),
    )(w, g, m, v, *prev)


def _adamw(w, g, m, v, name):
    R, C = w.shape
    tr = _t(R, max(8, (1 << 19) // max(C, 1)), 8)

    def body(w_ref, g_ref, m_ref, v_ref, d_ref, nm_ref, nv_ref):
        d_ref[...], nm_ref[...], nv_ref[...] = _adamw_math(w_ref[...], g_ref[...], m_ref[...], v_ref[...])

    blk = pl.BlockSpec((tr, C), lambda i: (i, 0))
    return pl.pallas_call(
        body, name=name, grid=(R // tr,), in_specs=[blk] * 4, out_specs=[blk] * 3,
        out_shape=[_sds((R, C), F32)] * 3, compiler_params=_cp("parallel"),
    )(w, g, m, v)


def _place():
    x, y, c = lax.axis_index("x"), lax.axis_index("y"), lax.axis_index("c")
    chips = [(1 - x, y), (x, 1 - y), (1 - x, 1 - y)]
    return x, y, c, chips


def _place_part(part, me, dtype, name, layer=None, after=None):
    L, R, C = part.shape
    first, L = (0, L) if layer is None else (layer, 1)
    tr = _t(R, max(16, (1 << 20) // C), 16)

    def body(me_ref, x_ref, *rest):
        o_ref = rest[-1]
        o_ref[...] = x_ref[...].astype(o_ref.dtype)

    extra = [] if after is None else [after]
    return pl.pallas_call(
        body, name=name,
        grid_spec=pltpu.PrefetchScalarGridSpec(
            num_scalar_prefetch=1, grid=(L, R // tr),
            in_specs=[pl.BlockSpec((None, tr, C), lambda l, r, m: (first + l, r, 0))] + [ANY] * len(extra),
            out_specs=pl.BlockSpec((None, None, tr, C), lambda l, r, m: (m[0], l, r, 0))),
        out_shape=_sds((N_CHIPS, L, R, C), dtype), compiler_params=_cp("parallel", "parallel"),
    )(me, part, *extra)


def _gather_chips_async(bufs, collective_id, name):
    n = len(bufs)
    refs = [jax.new_ref(b, memory_space=pltpu.MemorySpace.HBM) for b in bufs]

    @pl.kernel(mesh=plsc.ScalarSubcoreMesh(axis_name="seq", num_cores=1), name=name,
               scratch_types=(pltpu.SemaphoreType.DMA((n, 6)), pltpu.SemaphoreType.DMA((n, 6))),
               compiler_params=pltpu.CompilerParams(collective_id=collective_id))
    def launch(send_sems, recv_sems):
        x, y, c, chips = _place()
        me = 2 * x + y
        sibling = (x, y, 1 - c)
        barrier = pltpu.get_barrier_semaphore()
        for peer in [(*chip, c) for chip in chips] + [sibling]:
            pl.semaphore_signal(barrier, inc=1, device_id=peer, device_id_type=MESH)
        pl.semaphore_wait(barrier, len(chips) + 1)

        def copy(i, k, chip_index, core, to):
            h = refs[i].shape[2] // 2
            rows = refs[i].at[chip_index, :, pl.ds(core * h, h), :]
            return pltpu.make_async_remote_copy(
                src_ref=rows, dst_ref=rows, send_sem=send_sems.at[i, k], recv_sem=recv_sems.at[i, k],
                device_id=to, device_id_type=MESH)

        sent = []
        for i in range(n):
            for j, chip in enumerate(chips):
                cp = copy(i, j, me, c, (*chip, c))
                cp.start()
                sent.append(cp)
        for i in range(n):
            for j, chip in enumerate(chips):
                idx = 2 * chip[0] + chip[1]
                copy(i, j, idx, c, (x, y, c)).wait_recv()
                fw = copy(i, 3 + j, idx, c, sibling)
                fw.start()
                sent.append(fw)
        for i in range(n):
            for j, chip in enumerate(chips):
                copy(i, 3 + j, 2 * chip[0] + chip[1], 1 - c, (x, y, c)).wait_recv()
        for cp in sent:
            cp.wait_send()

    launch()
    return [r[...] for r in refs]


def _gather_chips(bufs, name):
    n = len(bufs)

    def body(*refs):
        outs = refs[n:2 * n]
        send_sems, recv_sems = refs[2 * n:]
        x, y, c, chips = _place()
        me = 2 * x + y
        sibling = (x, y, 1 - c)

        def copy(i, k, chip_index, core, to):
            h = outs[i].shape[2] // 2
            rows = outs[i].at[chip_index, :, pl.ds(core * h, h), :]
            return pltpu.make_async_remote_copy(
                src_ref=rows, dst_ref=rows, send_sem=send_sems.at[i, k], recv_sem=recv_sems.at[i, k],
                device_id=to, device_id_type=MESH)

        sent = []
        for i in range(n):
            for j, chip in enumerate(chips):
                cp = copy(i, j, me, c, (*chip, c))
                cp.start()
                sent.append(cp)
        for i in range(n):
            for j, chip in enumerate(chips):
                idx = 2 * chip[0] + chip[1]
                copy(i, j, idx, c, (x, y, c)).wait_recv()
                fw = copy(i, 3 + j, idx, c, sibling)
                fw.start()
                sent.append(fw)
        for i in range(n):
            for j, chip in enumerate(chips):
                copy(i, 3 + j, 2 * chip[0] + chip[1], 1 - c, (x, y, c)).wait_recv()
        for cp in sent:
            cp.wait_send()

    return pl.pallas_call(
        body, name=name, in_specs=[ANY] * n, out_specs=[ANY] * n,
        out_shape=[_sds(b.shape, b.dtype) for b in bufs],
        scratch_shapes=[pltpu.SemaphoreType.DMA((n, 6)), pltpu.SemaphoreType.DMA((n, 6))],
        input_output_aliases={i: i for i in range(n)},
        compiler_params=pltpu.CompilerParams(has_side_effects=True),
    )(*bufs)


def _swap_halves(gs, name):
    n = len(gs)

    def body(*refs):
        ins, outs = refs[:n], refs[n:2 * n]
        send_sems, recv_sems = refs[2 * n:]
        x, y, c, _ = _place()
        cps = []
        for i in range(n):
            h = ins[i].shape[2] // 2
            cp = pltpu.make_async_remote_copy(
                src_ref=ins[i].at[:, :, pl.ds((1 - c) * h, h), :], dst_ref=outs[i], send_sem=send_sems.at[i],
                recv_sem=recv_sems.at[i], device_id=(x, y, 1 - c), device_id_type=MESH)
            cp.start()
            cps.append(cp)
        for cp in cps:
            cp.wait()

    return pl.pallas_call(
        body, name=name, in_specs=[ANY] * n, out_specs=[ANY] * n,
        out_shape=[_sds(g.shape[:2] + (g.shape[2] // 2, g.shape[3]), g.dtype) for g in gs],
        scratch_shapes=[pltpu.SemaphoreType.DMA((n,)), pltpu.SemaphoreType.DMA((n,))],
        compiler_params=pltpu.CompilerParams(has_side_effects=True),
    )(*gs)


def _pair_sum(g, gs, core, out_dtype, name):
    ns, L, R, C = g.shape
    h = R // 2
    th = _t(h, max(16, (1 << 20) // C), 16)
    nb = h // th

    def body(core_ref, a_ref, b_ref, o_ref):
        o_ref[...] = (a_ref[...].astype(F32) + b_ref[...].astype(F32)).astype(o_ref.dtype)

    blk = (None, None, th, C)
    return pl.pallas_call(
        body, name=name,
        grid_spec=pltpu.PrefetchScalarGridSpec(
            num_scalar_prefetch=1, grid=(ns, L, nb),
            in_specs=[pl.BlockSpec(blk, lambda o, l, r, cr: (o, l, cr[0] * nb + r, 0)),
                      pl.BlockSpec(blk, lambda o, l, r, cr: (o, l, r, 0))],
            out_specs=pl.BlockSpec(blk, lambda o, l, r, cr: (o, l, r, 0))),
        out_shape=_sds((ns, L, h, C), out_dtype), compiler_params=_cp("parallel", "parallel", "parallel"),
    )(core, g, gs)


def _scatter_chips(ps, name):
    n = len(ps)

    def body(*refs):
        ins, outs = refs[:n], refs[n:2 * n]
        send_sems, recv_sems = refs[2 * n:]
        x, y, c, chips = _place()
        sent = []
        for i in range(n):
            for j, chip in enumerate(chips):
                cp = pltpu.make_async_remote_copy(
                    src_ref=ins[i].at[2 * chip[0] + chip[1]], dst_ref=outs[i].at[j], send_sem=send_sems.at[i, j],
                    recv_sem=recv_sems.at[i, j], device_id=(*chip, c), device_id_type=MESH)
                cp.start()
                sent.append(cp)
        for i in range(n):
            for j in range(len(chips)):
                slot = outs[i].at[j]
                pltpu.make_async_remote_copy(
                    src_ref=slot, dst_ref=slot, send_sem=send_sems.at[i, j], recv_sem=recv_sems.at[i, j],
                    device_id=(x, y, c), device_id_type=MESH).wait_recv()
        for cp in sent:
            cp.wait_send()

    return pl.pallas_call(
        body, name=name, in_specs=[ANY] * n, out_specs=[ANY] * n,
        out_shape=[_sds((N_CHIPS - 1,) + p.shape[1:], p.dtype) for p in ps],
        scratch_shapes=[pltpu.SemaphoreType.DMA((n, 3)), pltpu.SemaphoreType.DMA((n, 3))],
        compiler_params=pltpu.CompilerParams(has_side_effects=True),
    )(*ps)


def _scatter_chips_async(ps, collective_id, name):
    n = len(ps)
    srcs = [jax.new_ref(p, memory_space=pltpu.MemorySpace.HBM) for p in ps]
    dsts = [jax.empty_ref(_sds((N_CHIPS - 1,) + p.shape[1:], p.dtype), memory_space=pltpu.MemorySpace.HBM) for p in ps]

    @pl.kernel(mesh=plsc.ScalarSubcoreMesh(axis_name="seq", num_cores=1), name=name,
               scratch_types=(pltpu.SemaphoreType.DMA((n, 3)), pltpu.SemaphoreType.DMA((n, 3))),
               compiler_params=pltpu.CompilerParams(collective_id=collective_id))
    def launch(send_sems, recv_sems):
        x, y, c, chips = _place()
        barrier = pltpu.get_barrier_semaphore()
        for chip in chips:
            pl.semaphore_signal(barrier, inc=1, device_id=(*chip, c), device_id_type=MESH)
        pl.semaphore_wait(barrier, len(chips))
        sent = []
        for i in range(n):
            for j, chip in enumerate(chips):
                cp = pltpu.make_async_remote_copy(
                    src_ref=srcs[i].at[2 * chip[0] + chip[1]], dst_ref=dsts[i].at[j], send_sem=send_sems.at[i, j],
                    recv_sem=recv_sems.at[i, j], device_id=(*chip, c), device_id_type=MESH)
                cp.start()
                sent.append(cp)
        for i in range(n):
            for j in range(len(chips)):
                slot = dsts[i].at[j]
                pltpu.make_async_remote_copy(
                    src_ref=slot, dst_ref=slot, send_sem=send_sems.at[i, j], recv_sem=recv_sems.at[i, j],
                    device_id=(x, y, c), device_id_type=MESH).wait_recv()
        for cp in sent:
            cp.wait_send()

    launch()
    return [d[...] for d in dsts]


def _sum_chips(q, p, me, core, name, after=None):
    _, L, h, C = q.shape
    th = _t(h, max(16, (1 << 19) // C), 16)
    nb = h // th
    blk = (None, None, th, C)

    def body(me_ref, core_ref, p_ref, q0, q1, q2, *rest):
        o_ref = rest[-1]
        o_ref[...] = ((p_ref[...].astype(F32) + q0[...].astype(F32)) + q1[...].astype(F32)) + q2[...].astype(F32)

    extra = [] if after is None else [after]
    return pl.pallas_call(
        body, name=name,
        grid_spec=pltpu.PrefetchScalarGridSpec(
            num_scalar_prefetch=2, grid=(L, nb),
            in_specs=[pl.BlockSpec(blk, lambda l, r, m, c: (m[0], l, r, 0))]
            + [pl.BlockSpec(blk, functools.partial(lambda k, l, r, m, c: (k, l, r, 0), k)) for k in range(N_CHIPS - 1)]
            + [ANY] * len(extra),
            out_specs=pl.BlockSpec((None, th, C), lambda l, r, m, c: (l, c[0] * nb + r, 0))),
        out_shape=_sds((L, 2 * h, C), F32), compiler_params=_cp("parallel", "parallel"),
    )(me, core, p, q, q, q, *extra)


def _sum_small(own, q, name):
    def body(p_ref, q_ref, o_ref):
        o_ref[...] = ((p_ref[...] + q_ref[0]) + q_ref[1]) + q_ref[2]

    return pl.pallas_call(body, name=name, out_shape=_sds(own.shape, F32))(own, q)


def _join_halves(bufs, name):
    n = len(bufs)

    def body(*refs):
        outs = refs[n:2 * n]
        send_sems, recv_sems = refs[2 * n:]
        x, y, c, _ = _place()
        cps = []
        for i in range(n):
            h = outs[i].shape[1] // 2
            mine = outs[i].at[:, pl.ds(c * h, h), :]
            cp = pltpu.make_async_remote_copy(
                src_ref=mine, dst_ref=mine, send_sem=send_sems.at[i], recv_sem=recv_sems.at[i],
                device_id=(x, y, 1 - c), device_id_type=MESH)
            cp.start()
            cps.append(cp)
        for cp in cps:
            cp.wait()

    return pl.pallas_call(
        body, name=name, in_specs=[ANY] * n, out_specs=[ANY] * n,
        out_shape=[_sds(b.shape, b.dtype) for b in bufs],
        scratch_shapes=[pltpu.SemaphoreType.DMA((n,)), pltpu.SemaphoreType.DMA((n,))],
        input_output_aliases={i: i for i in range(n)},
        compiler_params=pltpu.CompilerParams(has_side_effects=True),
    )(*bufs)


def _pack(arrs, rows_mult):
    flat = jnp.concatenate([a.reshape(-1).astype(F32) for a in arrs])
    rows = -(-flat.size // LANE)
    rows = -(-rows // rows_mult) * rows_mult
    return jnp.pad(flat, (0, rows * LANE - flat.size)).reshape(rows, LANE)


def _unpack(packed, like):
    flat = packed.reshape(-1)
    out, pos = [], 0
    for a in like:
        n = math.prod(a.shape)
        out.append(flat[pos:pos + n].reshape(a.shape))
        pos += n
    return out


def _adamw_nd(w, g, m, v, name):
    shape = w.shape
    C = shape[-1]
    d, nm, nv = _adamw(w.reshape(-1, C), g.reshape(-1, C), m.reshape(-1, C), v.reshape(-1, C), name)
    return d.reshape(shape), nm.reshape(shape), nv.reshape(shape)


def kernel(x, a_norm, a_w_in, a_v_norm, a_w_s, a_b_s, a_w_out, kv_norm, w_kvf, b_f, k_norm, b_norm, b_w_qg, q_norm, b_w_out, f_norm, f_w_up, f_conv_w, f_conv_b, f_w_down, final_norm, loss_target, m_a_norm, m_a_w_in, m_a_v_norm, m_a_w_s, m_a_b_s, m_a_w_out, m_kv_norm, m_w_kvf, m_b_f, m_k_norm, m_b_norm, m_b_w_qg, m_q_norm, m_b_w_out, m_f_norm, m_f_w_up, m_f_conv_w, m_f_conv_b, m_f_w_down, m_final_norm, v_a_norm, v_a_w_in, v_a_v_norm, v_a_w_s, v_a_b_s, v_a_w_out, v_kv_norm, v_w_kvf, v_b_f, v_k_norm, v_b_norm, v_b_w_qg, v_q_norm, v_b_w_out, v_f_norm, v_f_w_up, v_f_conv_w, v_f_conv_b, v_f_w_down, v_final_norm):
    weights = dict(a_norm=a_norm, a_w_in=a_w_in, a_v_norm=a_v_norm, a_w_s=a_w_s, a_b_s=a_b_s, a_w_out=a_w_out, kv_norm=kv_norm, w_kvf=w_kvf, b_f=b_f, k_norm=k_norm, b_norm=b_norm, b_w_qg=b_w_qg, q_norm=q_norm, b_w_out=b_w_out, f_norm=f_norm, f_w_up=f_w_up, f_conv_w=f_conv_w, f_conv_b=f_conv_b, f_w_down=f_w_down, final_norm=final_norm)
    mom1 = dict(a_norm=m_a_norm, a_w_in=m_a_w_in, a_v_norm=m_a_v_norm, a_w_s=m_a_w_s, a_b_s=m_a_b_s, a_w_out=m_a_w_out, kv_norm=m_kv_norm, w_kvf=m_w_kvf, b_f=m_b_f, k_norm=m_k_norm, b_norm=m_b_norm, b_w_qg=m_b_w_qg, q_norm=m_q_norm, b_w_out=m_b_w_out, f_norm=m_f_norm, f_w_up=m_f_w_up, f_conv_w=m_f_conv_w, f_conv_b=m_f_conv_b, f_w_down=m_f_w_down, final_norm=m_final_norm)
    mom2 = dict(a_norm=v_a_norm, a_w_in=v_a_w_in, a_v_norm=v_a_v_norm, a_w_s=v_a_w_s, a_b_s=v_a_b_s, a_w_out=v_a_w_out, kv_norm=v_kv_norm, w_kvf=v_w_kvf, b_f=v_b_f, k_norm=v_k_norm, b_norm=v_b_norm, b_w_qg=v_b_w_qg, q_norm=v_q_norm, b_w_out=v_b_w_out, f_norm=v_f_norm, f_w_up=v_f_w_up, f_conv_w=v_f_conv_w, f_conv_b=v_f_conv_b, f_w_down=v_f_w_down, final_norm=v_final_norm)
    names = list(weights)
    big = ["a_w_in", "a_w_out", "w_kvf", "b_w_qg", "b_w_out", "f_w_up", "f_w_down"]
    small_sharded = ["a_norm", "a_v_norm", "f_conv_w"]
    small_repl = ["a_w_s", "a_b_s", "kv_norm", "b_f", "k_norm", "b_norm", "q_norm", "f_norm", "f_conv_b", "final_norm"]

    _, S, D = x.shape
    NA, NB, DEPTH = a_norm.shape[0], b_norm.shape[0], f_norm.shape[0]
    W = a_w_out.shape[1] * N_CHIPS
    G = a_w_s.shape[1]
    H = b_f.shape[0]
    ATT = H * HEAD
    F = f_w_down.shape[1] * N_CHIPS
    Ckv = w_kvf.shape[1]
    Cp = -(-Ckv // LANE) * LANE
    assert W == G * CHUNK and Ckv * N_CHIPS == 2 * ATT + H and S % CHUNK == 0
    core = lax.axis_index("c").astype(jnp.int32).reshape(1)
    me = (2 * lax.axis_index("x") + lax.axis_index("y")).astype(jnp.int32).reshape(1)

    small_local = [weights[k] for k in small_sharded]
    place = lambda p, nm, dt=BF16, layer=None, after=None: _place_part(p, me, dt, "place_" + nm, layer, after)
    g_up, g_down, g_attn = [None] * DEPTH, [None] * DEPTH, {}
    groups = [("ffn", l) for l in range(NA)] + [("attn", None)] + [("ffn", l) for l in range(NA, DEPTH)]

    def launch_next(after=None):
        if not groups:
            return
        kind, l = groups.pop(0)
        cid = DEPTH + 1 - len(groups)
        if kind == "ffn":
            bufs = [place(f_w_up, f"f_w_up{l}", layer=l, after=after), place(f_w_down, f"f_w_down{l}", layer=l, after=after)]
            g_up[l], g_down[l] = _gather_chips_async(bufs, cid, f"gather_w{cid}")
        else:
            bufs = [place(jnp.pad(w_kvf, ((0, 0), (0, Cp - Ckv)))[None], "w_kvf", after=after),
                    place(b_w_qg, "b_w_qg", after=after), place(b_w_out, "b_w_out", after=after)]
            g_attn["kvf"], g_attn["qg"], g_attn["out"] = _gather_chips_async(bufs, cid, f"gather_w{cid}")

    g_small, g_ain, g_aout = _gather_chips_async(
        [place(_pack(small_local, 32)[None], "small", F32), place(a_w_in, "a_w_in"), place(a_w_out, "a_w_out")], 0, "gather_w0")
    launch_next()

    per_chip = [_unpack(g_small[j, 0], small_local) for j in range(N_CHIPS)]
    a_norm_f, a_vnorm_f, conv_w_f = [jnp.concatenate([per_chip[j][k] for j in range(N_CHIPS)], axis=-1) for k in range(3)]
    cw = conv_w_f.reshape(DEPTH, 3, 2, F).transpose(0, 2, 1, 3)
    cb = f_conv_b.reshape(DEPTH, 2, 1, F)
    tril = jnp.tril(jnp.ones((CHUNK, CHUNK), dtype=bool))
    wm = jnp.where(tril, a_w_s, 0.0).astype(BF16)
    wmt = jnp.swapaxes(wm, -1, -2)
    bcol = a_b_s[..., None]
    bf_pad = jnp.pad(b_f, (0, LANE - H))[None]
    row = lambda v: v.reshape(1, -1)

    h = x[0]
    target = loss_target[0]
    saved = [dict() for _ in range(DEPTH)]
    kvs = {}
    for l in range(DEPTH):
        sv = saved[l]
        sv["h_m"] = h
        if l < NA:
            xn = _rms_fwd(h, row(a_norm_f[l]), f"a{l}_norm")
            zpre = _mm(m2(xn), mgc(g_ain, l), "nn", m2(_sds((S, 2 * W), BF16)), tm=1024, tn=512, name=f"a{l}_in")
            launch_next(zpre)
            gated = _sgu_fwd(zpre, wm[l], bcol[l], row(a_vnorm_f[l]), f"a{l}_sgu")
            h = _mm(m2(gated), mgr(g_aout, l), "nn", m2(_sds((S, D), F32)), res=m2(h), tm=1024, tn=1024, name=f"a{l}_out")
            sv.update(xn_m=xn, zpre=zpre, gated=gated)
        else:
            j = l - NA
            xn = _rms_fwd(h, row(b_norm[j]), f"b{j}_norm")
            qg = _mm(m2(xn), mgc(g_attn["qg"], j), "nn", mcs(_sds((2, S, ATT), BF16)), tm=1024, tn=512, name=f"b{j}_qg")
            qn = _headnorm_fwd(qg, 0, row(q_norm[j]) * QK_SCALE, f"b{j}_qnorm")
            o, lse = _attn_fwd(qn, kvs["kn"], kvs["kv"], kvs["crow"], f"b{j}_attn")
            og = _gate_fwd(o, qg, f"b{j}_gate")
            h = _mm(m2(og), mgr(g_attn["out"], j), "nn", m2(_sds((S, D), F32)), res=m2(h), tm=1024, tn=1024, name=f"b{j}_out")
            sv.update(xn_m=xn, qg=qg, qn=qn, o=o, lse=lse, og=og)
        sv["h_f"] = h
        xn = _rms_fwd(h, row(f_norm[l]), f"f{l}_norm")
        hup = _mm(m2(xn), mgc(g_up[l], 0), "nn", mcs(_sds((2, S, F), BF16)), tm=1024, tn=1408, name=f"f{l}_up")
        launch_next(hup)
        act = _convgate_fwd(hup, cw[l], cb[l], f"f{l}_conv")
        h = _mm(m2(act), mgr(g_down[l], 0), "nn", m2(_sds((S, D), F32)), res=m2(h), tm=1024, tn=1024, tk=1408, name=f"f{l}_down")
        sv.update(xn_f=xn, hup=hup, act=act)
        if l == NA - 1:
            kvf_full = g_attn["kvf"][:, 0, :, :Ckv].transpose(1, 0, 2).reshape(D, N_CHIPS * Ckv)
            w_kv = jnp.stack([kvf_full[:, :ATT], kvf_full[:, ATT:2 * ATT]])
            w_f = jnp.pad(kvf_full[:, 2 * ATT:], ((0, 0), (0, LANE - H)))
            xn_kv = _rms_fwd(h, row(kv_norm), "kv_norm")
            kv = _mm(m2(xn_kv), mcs(w_kv), "nn", mcs(_sds((2, S, ATT), BF16)), tm=1024, tn=512, name="kv_proj")
            fpre = _mm(m2(xn_kv), m2(w_f), "nn", m2(_sds((S, LANE), F32)), tm=1024, name="kv_fproj")
            kn = _headnorm_fwd(kv, 0, row(k_norm), "kv_knorm")
            cums = _logf_cumsum(fpre, bf_pad, "kv_cumsum")
            cT = cums[:, :H].T
            kvs.update(h=h, xn=xn_kv, kv=kv, fpre=fpre, kn=kn, crow=cT[:, None, :])

    loss11, dh, dhb, d_final = _final_loss(h, row(final_norm), target, "final_loss")
    loss = lax.psum(loss11[0, 0], ("x", "y", "c"))

    stacked = {k: tuple(lax.empty(weights[k].shape, F32) for _ in range(4))
               for k in ("a_w_in", "a_w_out", "b_w_qg", "b_w_out", "f_w_up", "f_w_down")}
    flat = {}
    pending = []
    next_id = [DEPTH + 2]

    def grad_buf(k):
        return _sds((N_CHIPS, 1) + weights[k].shape[1:], BF16)

    def start_chunk(entries, tag):
        Gs = [e[2] for e in entries]
        others = _swap_halves(Gs, f"grads_swap_{tag}")
        partial = [_pair_sum(g, o, core, g.dtype, f"grads_pair_{tag}{i}") for i, (g, o) in enumerate(zip(Gs, others))]
        by_chip = _scatter_chips_async(partial, next_id[0], f"grads_scatter_{tag}")
        next_id[0] += 1
        pending.append((entries, partial, by_chip, tag))
        return partial[0]

    def finish_chunk(after):
        entries, partial, by_chip, tag = pending.pop(0)
        halves = []
        for i, ((k, l, _), p, q) in enumerate(zip(entries, partial, by_chip)):
            if k == "small":
                half = _sum_small(lax.dynamic_index_in_dim(p, me[0], 0, keepdims=False), q, "grads_sum_small")
                hs = half.shape[1]
                halves.append(lax.dynamic_update_slice(jnp.zeros((1, 2 * hs, LANE), F32), half, (0, core[0] * hs, 0)))
            else:
                halves.append(_sum_chips(q, p, me, core, f"grads_sum_{tag}{i}", after=after))
        for (k, l, _), f in zip(entries, _join_halves(halves, f"grads_join_{tag}")):
            if k in stacked:
                stacked[k] = tuple(_adamw_layer(weights[k], f, mom1[k], mom2[k], l, stacked[k], f"adamw_{k}{l}"))
            else:
                flat[k] = f

    d_anorm, d_avnorm, d_ws, d_bs = [None] * NA, [None] * NA, [None] * NA, [None] * NA
    d_bnorm, d_qnorm = [None] * NB, [None] * NB
    d_fnorm, d_cw, d_cb = [None] * DEPTH, [None] * DEPTH, [None] * DEPTH
    dkn, dvv, dck = [], [], []
    G_kvf = d_kvnorm = d_bf = d_knorm = None
    tok = None
    for l in reversed(range(DEPTH)):
        sv = saved[l]
        chunk = []
        if l == NA - 1:
            dkv, d_knorm = _headnorm_bwd(dkn, kvs["kv"], row(k_norm), dvv, "kv_knorm_bwd")
            dc = sum(dck)
            dc = jnp.pad(dc, ((0, 0), (0, LANE - H)))
            df, d_bf = _logf_cumsum_bwd(dc, kvs["fpre"], bf_pad, "kv_cumsum_bwd")
            dxn = _mm(mcs(dkv), mcs(w_kv), "nt", m2(_sds((S, D), F32)), after=tok, tm=1024, tn=1024, tk=2048, name="kv_proj_dx")
            dxn = _mm(m2(df), m2(w_f), "nt", m2(_sds((S, D), BF16)), res=m2(dxn), tm=1024, tn=512, name="kv_fproj_dx")
            dw_kv = _mm(m2(kvs["xn"]), mcs(dkv), "tn", mcs(_sds((2, D, ATT), BF16)), tm=1024, tn=1024, tk=2048, name="kv_proj_dw")
            dw_f = _mm(m2(kvs["xn"]), m2(df), "tn", m2(_sds((D, LANE), BF16)), tm=512, tk=2048, name="kv_fproj_dw")
            dh, dhb, d_kvnorm = _rms_bwd(dxn, kvs["h"], row(kv_norm), dh, "kv_norm_bwd")
            dfull = jnp.concatenate([dw_kv[0], dw_kv[1], dw_f[:, :H]], axis=1)
            G_kvf = jnp.pad(dfull.reshape(D, N_CHIPS, Ckv).transpose(1, 0, 2), ((0, 0), (0, 0), (0, Cp - Ckv)))[:, None]
            chunk.append(("w_kvf", 0, G_kvf))
        dact = _mm(m2(dhb), mgr(g_down[l], 0), "nt", m2(_sds((S, F), BF16)), after=tok, tm=1024, tn=1408, tk=2048, name=f"f{l}_down_dx")
        G_down = _mm(m2(sv["act"]), m2(dhb), "tn", mgr(grad_buf("f_w_down"), 0), tm=1408, tn=1024, tk=2048, name=f"f{l}_down_dw")
        dhup, d_cw[l], d_cb[l] = _convgate_bwd(sv["hup"], dact, cw[l], cb[l], f"f{l}_conv_bwd")
        dxn = _mm(mcs(dhup), mgc(g_up[l], 0), "nt", m2(_sds((S, D), BF16)), tm=1024, tn=2048, tk=1408, name=f"f{l}_up_dx")
        G_up = _mm(m2(sv["xn_f"]), mcs(dhup), "tn", mgc(grad_buf("f_w_up"), 0), tm=1024, tn=1408, tk=2048, name=f"f{l}_up_dw")
        dh, dhb, d_fnorm[l] = _rms_bwd(dxn, sv["h_f"], row(f_norm[l]), dh, f"f{l}_norm_bwd")
        chunk += [("f_w_up", l, G_up), ("f_w_down", l, G_down)]
        if l == 0:
            tok = start_chunk(chunk, "f0")
            finish_chunk(tok)
            chunk = []
        if l >= NA:
            j = l - NA
            dog = _mm(m2(dhb), mgr(g_attn["out"], j), "nt", m2(_sds((S, ATT), BF16)), tm=1024, tn=512, tk=2048, name=f"b{j}_out_dx")
            G_bout = _mm(m2(sv["og"]), m2(dhb), "tn", mgr(grad_buf("b_w_out"), 0), tm=512, tn=1024, tk=4096, name=f"b{j}_out_dw")
            do, dgate, delta = _gate_bwd(dog, sv["o"], sv["qg"], f"b{j}_gate_bwd")
            dqn, drow, dkn_j, dv_j, dcol = _attn_bwd(sv["qn"], kvs["kn"], kvs["kv"], kvs["crow"], do, sv["lse"], delta, f"b{j}_attn_bwd")
            dkn.append(dkn_j); dvv.append(dv_j); dck.append(drow[:, :, 0].T - dcol[:, 0, :].T)
            dqg, d_qs = _headnorm_bwd([dqn], sv["qg"], row(q_norm[j]) * QK_SCALE, [dgate], f"b{j}_qnorm_bwd")
            d_qnorm[j] = d_qs * QK_SCALE
            dxn = _mm(mcs(dqg), mgc(g_attn["qg"], j), "nt", m2(_sds((S, D), BF16)), tm=1024, tn=2048, tk=1024, name=f"b{j}_qg_dx")
            G_bqg = _mm(m2(sv["xn_m"]), mcs(dqg), "tn", mgc(grad_buf("b_w_qg"), 0), tm=1024, tn=1024, tk=2048, name=f"b{j}_qg_dw")
            dh, dhb, d_bnorm[j] = _rms_bwd(dxn, sv["h_m"], row(b_norm[j]), dh, f"b{j}_norm_bwd")
            chunk += [("b_w_qg", j, G_bqg), ("b_w_out", j, G_bout)]
        else:
            dgated = _mm(m2(dhb), mgr(g_aout, l), "nt", m2(_sds((S, W), BF16)), after=tok, tm=1024, tn=512, tk=2048, name=f"a{l}_out_dx")
            G_aout = _mm(m2(sv["gated"]), m2(dhb), "tn", mgr(grad_buf("a_w_out"), 0), tm=512, tn=1024, tk=4096, name=f"a{l}_out_dw")
            dz, d_ws[l], d_bs[l], d_avnorm[l] = _sgu_bwd(sv["zpre"], dgated, wm[l], wmt[l], bcol[l], row(a_vnorm_f[l]), f"a{l}_sgu_bwd")
            dxn = _mm(m2(dz), mgc(g_ain, l), "nt", m2(_sds((S, D), BF16)), tm=1024, tn=2048, tk=1024, name=f"a{l}_in_dx")
            G_ain = _mm(m2(sv["xn_m"]), m2(dz), "tn", mgc(grad_buf("a_w_in"), 0), tm=1024, tn=1024, tk=2048, name=f"a{l}_in_dw")
            dh, dhb, d_anorm[l] = _rms_bwd(dxn, sv["h_m"], row(a_norm_f[l]), dh, f"a{l}_norm_bwd")
            chunk += [("a_w_in", l, G_ain), ("a_w_out", l, G_aout)]
        if l > 0:
            tok = start_chunk(chunk, f"l{l}")
            if len(pending) > 1:
                finish_chunk(tok)
    grad_x = dh[None]

    full = dict(
        a_norm=jnp.concatenate(d_anorm, axis=0), a_v_norm=jnp.concatenate(d_avnorm, axis=0),
        f_conv_w=jnp.stack(d_cw).transpose(0, 2, 1, 3).reshape(DEPTH, 3, 2 * F),
        a_w_s=jnp.where(tril, jnp.stack(d_ws), 0.0), a_b_s=jnp.stack(d_bs)[..., 0],
        kv_norm=d_kvnorm[0], b_f=d_bf[0, :H], k_norm=d_knorm[0], b_norm=jnp.concatenate(d_bnorm, axis=0),
        q_norm=jnp.concatenate(d_qnorm, axis=0), f_norm=jnp.concatenate(d_fnorm, axis=0),
        f_conv_b=jnp.stack(d_cb).reshape(DEPTH, 2 * F), final_norm=d_final[0])
    shard_rows = []
    for j in range(N_CHIPS):
        pieces = []
        for k in small_sharded:
            n = weights[k].shape[-1]
            pieces.append(full[k][..., j * n:(j + 1) * n])
        shard_rows.append(_pack(pieces, 32))
    rs = shard_rows[0].shape[0]
    repl = _pack([full[k] for k in small_repl], N_CHIPS * 32)
    rr = repl.shape[0] // N_CHIPS
    G_small = jnp.concatenate([jnp.stack(shard_rows), repl.reshape(N_CHIPS, rr, LANE)], axis=1)[:, None]

    tok = start_chunk(chunk + [("small", 0, G_small)], "l0")
    finish_chunk(tok)
    finish_chunk(None)
    F_small = flat["small"]
    repl_buf = lax.dynamic_update_slice(jnp.zeros((N_CHIPS, 1, rr, LANE), F32), F_small[None, :, rs:, :], (me[0], 0, 0, 0))
    (repl_all,) = _gather_chips([repl_buf], "gather_small_grads")

    grads, delta, new_m, new_v = ({k: t[i] for k, t in stacked.items()} for i in range(4))
    grads["w_kvf"] = flat["w_kvf"][0, :, :Ckv]
    for k, gk in zip(small_sharded, _unpack(F_small[0, :rs], small_local)):
        grads[k] = gk
    for k, gk in zip(small_repl, _unpack(repl_all.reshape(N_CHIPS * rr, LANE), [weights[k] for k in small_repl])):
        grads[k] = gk

    delta["w_kvf"], new_m["w_kvf"], new_v["w_kvf"] = _adamw_nd(w_kvf, grads["w_kvf"], m_w_kvf, v_w_kvf, "adamw_w_kvf")
    small = small_sharded + small_repl
    packed = [_pack([t[k] for k in small], 8) for t in (weights, grads, mom1, mom2)]
    outs = _adamw(*packed, "adamw_small")
    like = [weights[k] for k in small]
    for t, o in zip((delta, new_m, new_v), outs):
        for k, a in zip(small, _unpack(o, like)):
            t[k] = a

    return (loss, grad_x, *[grads[k] for k in names], *[delta[k] for k in names],
            *[new_m[k] for k in names], *[new_v[k] for k in names])
```

```python
import functools
import math

import jax
import jax.numpy as jnp
from jax import lax
from jax.experimental import pallas as pl
from jax.experimental.pallas import tpu as pltpu
from jax.experimental.pallas import tpu_sc as plsc

F32, BF16 = jnp.float32, jnp.bfloat16
EPS = 1e-6
CHUNK = 128
HEAD = 128
LANE = 128
HALO = 16
N_CHIPS = 4
VMEM_LIMIT = 48 * 1024 * 1024
MESH = pl.DeviceIdType.MESH
ANY = pl.BlockSpec(memory_space=pl.ANY)

ADAM_LR, ADAM_B1, ADAM_B2, ADAM_EPS, ADAM_WD, ADAM_STEP = 0.001, 0.9, 0.999, 1e-08, 0.01, 10
NEG = -1e30
QK_SCALE = HEAD ** -0.5


def _cp(*sem):
    return pltpu.CompilerParams(dimension_semantics=sem, vmem_limit_bytes=VMEM_LIMIT)


def _t(dim, pref, mult=LANE):
    if dim <= pref:
        return dim
    t = (pref // mult) * mult
    while t >= mult:
        if dim % t == 0:
            return t
        t -= mult
    return dim


def _sds(shape, dtype):
    return jax.ShapeDtypeStruct(tuple(shape), dtype)


class Mat:
    def __init__(self, arr, shape, rlim, clim, block, index):
        self.arr, self.shape, self.rlim, self.clim = arr, shape, rlim, clim
        self._block, self._index = block, index

    def spec(self, tr, tc, gmap):
        assert self.rlim % tr == 0 and self.clim % tc == 0, (self.shape, self.rlim, self.clim, tr, tc)
        index = self._index(tr, tc)
        return pl.BlockSpec(self._block(tr, tc), lambda *g: index(*gmap(*g)))


def m2(arr):
    R, C = arr.shape
    return Mat(arr, (R, C), R, C, lambda tr, tc: (tr, tc), lambda tr, tc: (lambda i, j: (i, j)))


def mcs(arr):
    ns, R, Cs = arr.shape
    return Mat(arr, (R, ns * Cs), R, Cs, lambda tr, tc: (None, tr, tc),
               lambda tr, tc: (lambda i, j: (j // (Cs // tc), i, j % (Cs // tc))))


def mhalf(arr, p):
    ns, R, Cs = arr.shape
    return Mat(arr, (R, Cs), R, Cs, lambda tr, tc: (None, tr, tc), lambda tr, tc: (lambda i, j: (p, i, j)))


def mgc(arr, l):
    ns, L, R, Cs = arr.shape
    return Mat(arr, (R, ns * Cs), R, Cs, lambda tr, tc: (None, None, tr, tc),
               lambda tr, tc: (lambda i, j: (j // (Cs // tc), l, i, j % (Cs // tc))))


def mgr(arr, l):
    ns, L, Rs, C = arr.shape
    return Mat(arr, (ns * Rs, C), Rs, C, lambda tr, tc: (None, None, tr, tc),
               lambda tr, tc: (lambda i, j: (i // (Rs // tr), l, i % (Rs // tr), j)))


_DIMS = {"nn": ((1,), (0,)), "nt": ((1,), (1,)), "tn": ((0,), (0,))}


def _tokens(after):
    if after is None:
        return []
    return list(after) if isinstance(after, (list, tuple)) else [after]


def _mm(a, b, mode, out, *, res=None, alias=False, after=None, tm=512, tn=512, tk=4096, name):
    if mode == "tn":
        (K, M), (K2, N) = a.shape, b.shape
        alim_m, alim_k, blim_k, blim_n = a.clim, a.rlim, b.rlim, b.clim
    elif mode == "nt":
        (M, K), (N, K2) = a.shape, b.shape
        alim_m, alim_k, blim_k, blim_n = a.rlim, a.clim, b.clim, b.rlim
    else:
        (M, K), (K2, N) = a.shape, b.shape
        alim_m, alim_k, blim_k, blim_n = a.rlim, a.clim, b.rlim, b.clim
    assert K == K2 and out.shape == (M, N), (name, a.shape, b.shape, out.shape)
    tm = _t(math.gcd(alim_m, out.rlim), tm)
    tn = _t(math.gcd(blim_n, out.clim), tn)
    tk = _t(math.gcd(alim_k, blim_k), tk)
    grid = (M // tm, N // tn, K // tk)
    nk = grid[2]
    if mode == "tn":
        a_spec = a.spec(tk, tm, lambda i, j, k: (k, i))
    else:
        a_spec = a.spec(tm, tk, lambda i, j, k: (i, k))
    if mode == "nt":
        b_spec = b.spec(tn, tk, lambda i, j, k: (j, k))
    else:
        b_spec = b.spec(tk, tn, lambda i, j, k: (k, j))
    o_spec = out.spec(tm, tn, lambda i, j, k: (i, j))
    operands, in_specs = [a.arr, b.arr], [a_spec, b_spec]
    if res is not None:
        operands.append(res.arr)
        in_specs.append(res.spec(tm, tn, lambda i, j, k: (i, j)))
    aliases = {}
    if alias:
        aliases = {len(operands): 0}
        operands.append(out.arr)
        in_specs.append(ANY)
    tokens = _tokens(after)
    operands += tokens
    in_specs += [ANY] * len(tokens)
    dims = (_DIMS[mode], ((), ()))
    has_res = res is not None

    def body(*refs):
        a_ref, b_ref = refs[0], refs[1]
        res_ref = refs[2] if has_res else None
        n_in = 2 + has_res + alias + len(tokens)
        o_ref = refs[n_in]
        p = lax.dot_general(a_ref[...].astype(BF16), b_ref[...].astype(BF16), dims, preferred_element_type=F32)

        def finish(v):
            if has_res:
                v = v + res_ref[...].astype(F32)
            o_ref[...] = v.astype(o_ref.dtype)

        if nk == 1:
            finish(p)
        else:
            acc = refs[n_in + 1]
            k = pl.program_id(2)

            @pl.when(k == 0)
            def _():
                acc[...] = p

            @pl.when(k > 0)
            def _():
                acc[...] += p

            @pl.when(k == nk - 1)
            def _():
                finish(acc[...])

    return pl.pallas_call(
        body, name=name, grid=grid, in_specs=in_specs, out_specs=o_spec,
        out_shape=_sds(out.arr.shape, out.arr.dtype),
        scratch_shapes=[pltpu.VMEM((tm, tn), F32)] if nk > 1 else [],
        input_output_aliases=aliases,
        compiler_params=_cp("parallel", "parallel", "arbitrary"),
    )(*operands)


def _rms_fwd(x, g, name):
    S, D = x.shape
    tr = _t(S, 512)

    def body(x_ref, g_ref, o_ref):
        xv = x_ref[...]
        r = lax.rsqrt(jnp.mean(xv * xv, axis=-1, keepdims=True) + EPS)
        o_ref[...] = ((xv * r) * g_ref[...]).astype(o_ref.dtype)

    return pl.pallas_call(
        body, name=name, grid=(S // tr,),
        in_specs=[pl.BlockSpec((tr, D), lambda i: (i, 0)), pl.BlockSpec((1, D), lambda i: (0, 0))],
        out_specs=pl.BlockSpec((tr, D), lambda i: (i, 0)), out_shape=_sds((S, D), BF16),
        compiler_params=_cp("parallel"),
    )(x, g)


def _rms_bwd(dy, x, g, dres, name):
    S, D = x.shape
    tr = _t(S, 512)

    def body(dy_ref, x_ref, g_ref, dres_ref, dx_ref, dxb_ref, dg_ref):
        xv = x_ref[...]
        dyv = dy_ref[...].astype(F32)
        r = lax.rsqrt(jnp.mean(xv * xv, axis=-1, keepdims=True) + EPS)
        xh = xv * r
        dxh = dyv * g_ref[...]
        m = jnp.mean(dxh * xh, axis=-1, keepdims=True)
        dx = dres_ref[...] + r * (dxh - xh * m)
        dx_ref[...] = dx
        dxb_ref[...] = dx.astype(dxb_ref.dtype)

        @pl.when(pl.program_id(0) == 0)
        def _():
            dg_ref[...] = jnp.zeros_like(dg_ref)

        dg_ref[...] += jnp.sum(dyv * xh, axis=0, keepdims=True)

    row = pl.BlockSpec((tr, D), lambda i: (i, 0))
    vec = pl.BlockSpec((1, D), lambda i: (0, 0))
    return pl.pallas_call(
        body, name=name, grid=(S // tr,), in_specs=[row, row, vec, row], out_specs=[row, row, vec],
        out_shape=[_sds((S, D), F32), _sds((S, D), BF16), _sds((1, D), F32)], compiler_params=_cp("arbitrary"),
    )(dy, x, g, dres)


def _final_loss(h, g, target, name):
    S, D = h.shape
    tr = _t(S, 512)

    def body(x_ref, g_ref, t_ref, loss_ref, dx_ref, dxb_ref, dg_ref):
        xv = x_ref[...]
        r = lax.rsqrt(jnp.mean(xv * xv, axis=-1, keepdims=True) + EPS)
        xh = xv * r
        err = xh * g_ref[...] - t_ref[...]
        part = 0.5 * jnp.sum(jnp.mean(err * err, axis=-1, keepdims=True), axis=0, keepdims=True)
        dyv = err * (1.0 / D)
        dxh = dyv * g_ref[...]
        m = jnp.mean(dxh * xh, axis=-1, keepdims=True)
        dx = r * (dxh - xh * m)
        dx_ref[...] = dx
        dxb_ref[...] = dx.astype(dxb_ref.dtype)

        @pl.when(pl.program_id(0) == 0)
        def _():
            dg_ref[...] = jnp.zeros_like(dg_ref)
            loss_ref[...] = jnp.zeros_like(loss_ref)

        dg_ref[...] += jnp.sum(dyv * xh, axis=0, keepdims=True)
        loss_ref[...] += part

    row = pl.BlockSpec((tr, D), lambda i: (i, 0))
    vec = pl.BlockSpec((1, D), lambda i: (0, 0))
    one = pl.BlockSpec((1, 1), lambda i: (0, 0))
    return pl.pallas_call(
        body, name=name, grid=(S // tr,), in_specs=[row, vec, row], out_specs=[one, row, row, vec],
        out_shape=[_sds((1, 1), F32), _sds((S, D), F32), _sds((S, D), BF16), _sds((1, D), F32)], compiler_params=_cp("arbitrary"),
    )(h, g, target)


_RSQRT2 = 0.7071067811865476
_RSQRT2PI = 0.3989422804014327


def _gelu(x):
    return 0.5 * x * (1.0 + lax.erf(x * _RSQRT2))


def _gelu_grad(x):
    return 0.5 * (1.0 + lax.erf(x * _RSQRT2)) + x * (jnp.exp(-0.5 * x * x) * _RSQRT2PI)


def _sgu_fwd(zpre, wm, bcol, vnorm, name):
    S, W2 = zpre.shape
    W = W2 // 2
    G = W // CHUNK

    def body(z_ref, wm_ref, b_ref, vn_ref, o_ref):
        zp = z_ref[...].astype(F32)
        u = _gelu(zp[:, :W])
        v = _gelu(zp[:, W:])
        rv = lax.rsqrt(jnp.mean(v * v, axis=-1, keepdims=True) + EPS)
        vn = ((v * rv) * vn_ref[...]).astype(BF16)
        for g in range(G):
            sl = slice(g * CHUNK, (g + 1) * CHUNK)
            mixed = jnp.dot(wm_ref[g], vn[:, sl], preferred_element_type=F32) + b_ref[g]
            o_ref[:, sl] = (u[:, sl] * mixed).astype(o_ref.dtype)

    return pl.pallas_call(
        body, name=name, grid=(S // CHUNK,),
        in_specs=[pl.BlockSpec((CHUNK, W2), lambda i: (i, 0)),
                  pl.BlockSpec((G, CHUNK, CHUNK), lambda i: (0, 0, 0)),
                  pl.BlockSpec((G, CHUNK, 1), lambda i: (0, 0, 0)),
                  pl.BlockSpec((1, W), lambda i: (0, 0))],
        out_specs=pl.BlockSpec((CHUNK, W), lambda i: (i, 0)), out_shape=_sds((S, W), BF16),
        compiler_params=_cp("parallel"),
    )(zpre, wm, bcol, vnorm)


def _sgu_bwd(zpre, dgated, wm, wmt, bcol, vnorm, name):
    S, W2 = zpre.shape
    W = W2 // 2
    G = W // CHUNK

    def body(z_ref, dg_ref, wm_ref, wmt_ref, b_ref, vn_ref, dz_ref, dws_ref, dbs_ref, dvn_ref):
        @pl.when(pl.program_id(0) == 0)
        def _():
            dws_ref[...] = jnp.zeros_like(dws_ref)
            dbs_ref[...] = jnp.zeros_like(dbs_ref)
            dvn_ref[...] = jnp.zeros_like(dvn_ref)

        zp = z_ref[...].astype(F32)
        zu, zv = zp[:, :W], zp[:, W:]
        u = _gelu(zu)
        v = _gelu(zv)
        rv = lax.rsqrt(jnp.mean(v * v, axis=-1, keepdims=True) + EPS)
        vh = v * rv
        vn = (vh * vn_ref[...]).astype(BF16)
        dgv = dg_ref[...].astype(F32)
        du_parts, dvn_parts = [], []
        for g in range(G):
            sl = slice(g * CHUNK, (g + 1) * CHUNK)
            vg = vn[:, sl]
            mixed = jnp.dot(wm_ref[g], vg, preferred_element_type=F32) + b_ref[g]
            dgg = dgv[:, sl]
            du_parts.append(dgg * mixed)
            dmixed = dgg * u[:, sl]
            dbs_ref[g] += jnp.sum(dmixed, axis=1, keepdims=True)
            dmb = dmixed.astype(BF16)
            dws_ref[g] += lax.dot_general(dmb, vg, (_DIMS["nt"], ((), ())), preferred_element_type=F32)
            dvn_parts.append(jnp.dot(wmt_ref[g], dmb, preferred_element_type=F32))
        du = jnp.concatenate(du_parts, axis=1)
        dvn = jnp.concatenate(dvn_parts, axis=1)
        dvn_ref[...] += jnp.sum(dvn * vh, axis=0, keepdims=True)
        dvh = dvn * vn_ref[...]
        dv = rv * (dvh - vh * jnp.mean(dvh * vh, axis=-1, keepdims=True))
        dz_ref[:, :W] = (du * _gelu_grad(zu)).astype(dz_ref.dtype)
        dz_ref[:, W:] = (dv * _gelu_grad(zv)).astype(dz_ref.dtype)

    full3 = lambda shape: pl.BlockSpec(shape, lambda i: (0, 0, 0))
    return pl.pallas_call(
        body, name=name, grid=(S // CHUNK,),
        in_specs=[pl.BlockSpec((CHUNK, W2), lambda i: (i, 0)), pl.BlockSpec((CHUNK, W), lambda i: (i, 0)),
                  full3((G, CHUNK, CHUNK)), full3((G, CHUNK, CHUNK)), full3((G, CHUNK, 1)),
                  pl.BlockSpec((1, W), lambda i: (0, 0))],
        out_specs=[pl.BlockSpec((CHUNK, W2), lambda i: (i, 0)), full3((G, CHUNK, CHUNK)), full3((G, CHUNK, 1)),
                   pl.BlockSpec((1, W), lambda i: (0, 0))],
        out_shape=[_sds((S, W2), BF16), _sds((G, CHUNK, CHUNK), F32), _sds((G, CHUNK, 1), F32), _sds((1, W), F32)],
        compiler_params=_cp("arbitrary"),
    )(zpre, dgated, wm, wmt, bcol, vnorm)


def _conv_taps(h_ref, half, r0, R, tc):
    if r0 == 0:
        xe = jnp.concatenate([jnp.zeros((HALO, tc), F32), h_ref[half, 0:R, :].astype(F32)], axis=0)
    else:
        xe = h_ref[half, r0 - HALO:r0 + R, :].astype(F32)
    return xe[HALO:], pltpu.roll(xe, 1, 0)[HALO:], pltpu.roll(xe, 2, 0)[HALO:]


def _conv_apply(taps, w, b):
    x0, x1, x2 = taps
    return x2 * w[0:1] + x1 * w[1:2] + x0 * w[2:3] + b


def _convgate_fwd(hup, cw, cb, name):
    _, S, F = hup.shape
    tc = _t(F, 256)
    R = _t(S, 512)

    def body(h_ref, w_ref, b_ref, o_ref):
        for r0 in range(0, S, R):
            gate = _conv_apply(_conv_taps(h_ref, 0, r0, R, tc), w_ref[0], b_ref[0])
            val = _conv_apply(_conv_taps(h_ref, 1, r0, R, tc), w_ref[1], b_ref[1])
            o_ref[r0:r0 + R, :] = (gate * jax.nn.sigmoid(gate) * val).astype(o_ref.dtype)

    return pl.pallas_call(
        body, name=name, grid=(F // tc,),
        in_specs=[pl.BlockSpec((2, S, tc), lambda j: (0, 0, j)), pl.BlockSpec((2, 3, tc), lambda j: (0, 0, j)),
                  pl.BlockSpec((2, 1, tc), lambda j: (0, 0, j))],
        out_specs=pl.BlockSpec((S, tc), lambda j: (0, j)), out_shape=_sds((S, F), BF16),
        compiler_params=_cp("parallel"),
    )(hup, cw, cb)


def _convgate_bwd(hup, dact, cw, cb, name):
    _, S, F = hup.shape
    tc = _t(F, 256)
    R = _t(S, 512)

    def body(h_ref, da_ref, w_ref, b_ref, dh_ref, dw_ref, db_ref, dhc):
        dhc[:, S:S + HALO, :] = jnp.zeros((2, HALO, tc), F32)
        dw_acc = [[jnp.zeros((1, tc), F32) for _ in range(3)] for _ in range(2)]
        db_acc = [jnp.zeros((1, tc), F32) for _ in range(2)]
        for r0 in range(0, S, R):
            taps = [_conv_taps(h_ref, p, r0, R, tc) for p in range(2)]
            gate = _conv_apply(taps[0], w_ref[0], b_ref[0])
            val = _conv_apply(taps[1], w_ref[1], b_ref[1])
            da = da_ref[r0:r0 + R, :].astype(F32)
            sg = jax.nn.sigmoid(gate)
            d = [da * val * (sg * (1.0 + gate * (1.0 - sg))), da * (gate * sg)]
            for p in range(2):
                dhc[p, r0:r0 + R, :] = d[p]
                db_acc[p] = db_acc[p] + jnp.sum(d[p], axis=0, keepdims=True)
                for k in range(3):
                    dw_acc[p][k] = dw_acc[p][k] + jnp.sum(d[p] * taps[p][2 - k], axis=0, keepdims=True)
        for p in range(2):
            db_ref[p] = db_acc[p]
            dw_ref[p] = jnp.concatenate(dw_acc[p], axis=0)
            w = w_ref[p]
            for r0 in range(0, S, R):
                de = dhc[p, r0:r0 + R + HALO, :]
                d1 = pltpu.roll(de, R + HALO - 1, 0)[:R]
                d2 = pltpu.roll(de, R + HALO - 2, 0)[:R]
                dh_ref[p, r0:r0 + R, :] = (de[:R] * w[2:3] + d1 * w[1:2] + d2 * w[0:1]).astype(dh_ref.dtype)

    blk = lambda rows: pl.BlockSpec((2, rows, tc), lambda j: (0, 0, j))
    return pl.pallas_call(
        body, name=name, grid=(F // tc,),
        in_specs=[blk(S), pl.BlockSpec((S, tc), lambda j: (0, j)), blk(3), blk(1)],
        out_specs=[blk(S), blk(3), blk(1)],
        out_shape=[_sds((2, S, F), BF16), _sds((2, 3, F), F32), _sds((2, 1, F), F32)],
        scratch_shapes=[pltpu.VMEM((2, S + HALO, tc), F32)],
        compiler_params=_cp("parallel"),
    )(hup, dact, cw, cb)


def _headnorm_fwd(x3, part, gain, name):
    _, S, W = x3.shape
    tr = _t(S, 512)

    def body(x_ref, g_ref, o_ref):
        xv = x_ref[...].astype(F32)
        for h in range(W // HEAD):
            sl = slice(h * HEAD, (h + 1) * HEAD)
            xh = xv[:, sl]
            r = lax.rsqrt(jnp.mean(xh * xh, axis=-1, keepdims=True) + EPS)
            o_ref[:, sl] = ((xh * r) * g_ref[...]).astype(o_ref.dtype)

    return pl.pallas_call(
        body, name=name, grid=(S // tr,),
        in_specs=[pl.BlockSpec((None, tr, W), lambda i: (part, i, 0)), pl.BlockSpec((1, HEAD), lambda i: (0, 0))],
        out_specs=pl.BlockSpec((tr, W), lambda i: (i, 0)), out_shape=_sds((S, W), BF16),
        compiler_params=_cp("parallel"),
    )(x3, gain)


def _headnorm_bwd(dys, x3, gain, passes, name):
    _, S, W = x3.shape
    tr = _t(S, 256)
    nd, npass = len(dys), len(passes)

    def body(*refs):
        dy_refs = refs[:nd]
        x_ref, g_ref = refs[nd], refs[nd + 1]
        p_refs = refs[nd + 2:nd + 2 + npass]
        o_ref, dg_ref = refs[nd + 2 + npass], refs[nd + 3 + npass]

        @pl.when(pl.program_id(0) == 0)
        def _():
            dg_ref[...] = jnp.zeros_like(dg_ref)

        xv = x_ref[...].astype(F32)
        dyv = dy_refs[0][...].astype(F32)
        for r in dy_refs[1:]:
            dyv = dyv + r[...].astype(F32)
        dg = jnp.zeros((1, HEAD), F32)
        for h in range(W // HEAD):
            sl = slice(h * HEAD, (h + 1) * HEAD)
            xh = xv[:, sl]
            r = lax.rsqrt(jnp.mean(xh * xh, axis=-1, keepdims=True) + EPS)
            xh = xh * r
            dyh = dyv[:, sl]
            dg = dg + jnp.sum(dyh * xh, axis=0, keepdims=True)
            dxh = dyh * g_ref[...]
            o_ref[0, :, sl] = (r * (dxh - xh * jnp.mean(dxh * xh, axis=-1, keepdims=True))).astype(o_ref.dtype)
        dg_ref[...] += dg
        pv = p_refs[0][...].astype(F32)
        for r in p_refs[1:]:
            pv = pv + r[...].astype(F32)
        o_ref[1] = pv.astype(o_ref.dtype)

    row = pl.BlockSpec((tr, W), lambda i: (i, 0))
    vec = pl.BlockSpec((1, HEAD), lambda i: (0, 0))
    return pl.pallas_call(
        body, name=name, grid=(S // tr,),
        in_specs=[row] * nd + [pl.BlockSpec((None, tr, W), lambda i: (0, i, 0)), vec] + [row] * npass,
        out_specs=[pl.BlockSpec((2, tr, W), lambda i: (0, i, 0)), vec],
        out_shape=[_sds((2, S, W), BF16), _sds((1, HEAD), F32)], compiler_params=_cp("arbitrary"),
    )(*dys, x3, gain, *passes)


def _gate_fwd(o, qg, name):
    S, W = o.shape
    tr = _t(S, 512)

    def body(o_ref, g_ref, y_ref):
        y_ref[...] = (o_ref[...].astype(F32) * jax.nn.sigmoid(g_ref[...].astype(F32))).astype(y_ref.dtype)

    row = pl.BlockSpec((tr, W), lambda i: (i, 0))
    return pl.pallas_call(
        body, name=name, grid=(S // tr,), in_specs=[row, pl.BlockSpec((None, tr, W), lambda i: (1, i, 0))],
        out_specs=row, out_shape=_sds((S, W), BF16), compiler_params=_cp("parallel"),
    )(o, qg)


def _gate_bwd(dog, o, qg, name):
    S, W = o.shape
    H = W // HEAD
    tr = _t(S, 512)

    def body(dy_ref, o_ref, g_ref, do_ref, dg_ref, dl_ref):
        sg = jax.nn.sigmoid(g_ref[...].astype(F32))
        dy = dy_ref[...].astype(F32)
        ov = o_ref[...].astype(F32)
        dob = (dy * sg).astype(do_ref.dtype)
        do_ref[...] = dob
        dg_ref[...] = (dy * ov * (sg * (1.0 - sg))).astype(dg_ref.dtype)
        prod = dob.astype(F32) * ov
        for h in range(H):
            dl_ref[h] = jnp.sum(prod[:, h * HEAD:(h + 1) * HEAD], axis=-1, keepdims=True)

    row = pl.BlockSpec((tr, W), lambda i: (i, 0))
    return pl.pallas_call(
        body, name=name, grid=(S // tr,), in_specs=[row, row, pl.BlockSpec((None, tr, W), lambda i: (1, i, 0))],
        out_specs=[row, row, pl.BlockSpec((H, tr, 1), lambda i: (0, i, 0))],
        out_shape=[_sds((S, W), BF16), _sds((S, W), BF16), _sds((H, S, 1), F32)], compiler_params=_cp("parallel"),
    )(dog, o, qg)


def _logf_cumsum(fpre, bf, name):
    S, C = fpre.shape
    n = S // CHUNK

    def body(f_ref, b_ref, c_ref, carry):
        @pl.when(pl.program_id(0) == 0)
        def _():
            carry[...] = jnp.zeros_like(carry)

        lf = jax.nn.log_sigmoid(f_ref[...] + b_ref[...])
        tri = (lax.broadcasted_iota(jnp.int32, (CHUNK, CHUNK), 0)
               >= lax.broadcasted_iota(jnp.int32, (CHUNK, CHUNK), 1)).astype(F32)
        c_ref[...] = jnp.dot(tri, lf, preferred_element_type=F32, precision=lax.Precision.HIGHEST) + carry[...]
        carry[...] += jnp.sum(lf, axis=0, keepdims=True)

    return pl.pallas_call(
        body, name=name, grid=(n,),
        in_specs=[pl.BlockSpec((CHUNK, C), lambda i: (i, 0)), pl.BlockSpec((1, C), lambda i: (0, 0))],
        out_specs=pl.BlockSpec((CHUNK, C), lambda i: (i, 0)), out_shape=_sds((S, C), F32),
        scratch_shapes=[pltpu.VMEM((1, C), F32)], compiler_params=_cp("arbitrary"),
    )(fpre, bf)


def _logf_cumsum_bwd(dc, fpre, bf, name):
    S, C = fpre.shape
    n = S // CHUNK

    def body(dc_ref, f_ref, b_ref, df_ref, db_ref, carry):
        @pl.when(pl.program_id(0) == 0)
        def _():
            carry[...] = jnp.zeros_like(carry)
            db_ref[...] = jnp.zeros_like(db_ref)

        dcv = dc_ref[...]
        tri = (lax.broadcasted_iota(jnp.int32, (CHUNK, CHUNK), 0)
               <= lax.broadcasted_iota(jnp.int32, (CHUNK, CHUNK), 1)).astype(F32)
        dlf = jnp.dot(tri, dcv, preferred_element_type=F32, precision=lax.Precision.HIGHEST) + carry[...]
        carry[...] += jnp.sum(dcv, axis=0, keepdims=True)
        df = dlf * jax.nn.sigmoid(-(f_ref[...] + b_ref[...]))
        df_ref[...] = df.astype(df_ref.dtype)
        db_ref[...] += jnp.sum(df, axis=0, keepdims=True)

    rev = pl.BlockSpec((CHUNK, C), lambda i: (n - 1 - i, 0))
    vec = pl.BlockSpec((1, C), lambda i: (0, 0))
    return pl.pallas_call(
        body, name=name, grid=(n,), in_specs=[rev, rev, vec], out_specs=[rev, vec],
        out_shape=[_sds((S, C), BF16), _sds((1, C), F32)],
        scratch_shapes=[pltpu.VMEM((1, C), F32)], compiler_params=_cp("arbitrary"),
    )(dc, fpre, bf)


def _attn_tiles(S):
    return _t(S, 1024), _t(S, 512)


def _scores(q, k, ck, off, tq, tk, masked):
    s = lax.dot_general(q, k, (_DIMS["nt"], ((), ())), preferred_element_type=F32) - ck
    if masked:
        d = lax.broadcasted_iota(jnp.int32, (tq, tk), 1) - lax.broadcasted_iota(jnp.int32, (tq, tk), 0)
        s = jnp.where(d <= off, s, NEG)
    return s


def _attn_fwd(qn, kn, kv, crow, name):
    S, W = qn.shape
    H = W // HEAD
    tq, tk = _attn_tiles(S)
    nq, nk = S // tq, S // tk
    last = lambda i: ((i + 1) * tq - 1) // tk
    sub = _t(tq, 256)

    def body(q_ref, k_ref, v_ref, ck_ref, o_ref, lse_ref, m_sc, l_sc, acc_sc):
        qi, kj = pl.program_id(1), pl.program_id(2)

        @pl.when(kj == 0)
        def _():
            m_sc[...] = jnp.full_like(m_sc, NEG)
            l_sc[...] = jnp.zeros_like(l_sc)
            acc_sc[...] = jnp.zeros_like(acc_sc)

        def step(masked):
            k, ck, v = k_ref[...], ck_ref[...], v_ref[...]
            for r0 in range(0, tq, sub):
                rows = slice(r0, r0 + sub)
                s = _scores(q_ref[rows, :], k, ck, qi * tq + r0 - kj * tk, sub, tk, masked)
                m_old = m_sc[rows, :]
                m_new = jnp.maximum(m_old, jnp.max(s, axis=-1, keepdims=True))
                alpha = jnp.exp(m_old - m_new)
                p = jnp.exp(s - m_new)
                l_sc[rows, :] = alpha * l_sc[rows, :] + jnp.sum(p, axis=-1, keepdims=True)
                acc_sc[rows, :] = alpha * acc_sc[rows, :] + jnp.dot(p.astype(BF16), v, preferred_element_type=F32)
                m_sc[rows, :] = m_new

        @pl.when(kj <= last(qi))
        def _():
            step(True)

        @pl.when(kj == nk - 1)
        def _():
            o_ref[...] = (acc_sc[...] / l_sc[...]).astype(o_ref.dtype)
            lse_ref[...] = m_sc[...] + jnp.log(l_sc[...])

    return pl.pallas_call(
        body, name=name, grid=(H, nq, nk),
        in_specs=[pl.BlockSpec((tq, HEAD), lambda h, i, j: (i, h)),
                  pl.BlockSpec((tk, HEAD), lambda h, i, j: (jnp.minimum(j, last(i)), h)),
                  pl.BlockSpec((None, tk, HEAD), lambda h, i, j: (1, jnp.minimum(j, last(i)), h)),
                  pl.BlockSpec((None, 1, tk), lambda h, i, j: (h, 0, jnp.minimum(j, last(i))))],
        out_specs=[pl.BlockSpec((tq, HEAD), lambda h, i, j: (i, h)),
                   pl.BlockSpec((None, tq, 1), lambda h, i, j: (h, i, 0))],
        out_shape=[_sds((S, W), BF16), _sds((H, S, 1), F32)],
        scratch_shapes=[pltpu.VMEM((tq, 1), F32), pltpu.VMEM((tq, 1), F32), pltpu.VMEM((tq, HEAD), F32)],
        compiler_params=_cp("parallel", "parallel", "arbitrary"),
    )(qn, kn, kv, crow)


def _attn_bwd(qn, kn, kv, crow, do, lse, delta, name):
    S, W = qn.shape
    H = W // HEAD
    tq, tk = _attn_tiles(S)
    nq, nk = S // tq, S // tk
    first = lambda j: (j * tk) // tq

    def body(q_ref, k_ref, v_ref, ck_ref, do_ref, lse_ref, dl_ref, dq_ref, dr_ref, dk_ref, dv_ref, dc_ref, dk_sc, dv_sc, dc_sc):
        kj, qi = pl.program_id(1), pl.program_id(2)

        @pl.when(jnp.logical_and(kj == 0, qi == 0))
        def _():
            dq_ref[...] = jnp.zeros_like(dq_ref)
            dr_ref[...] = jnp.zeros_like(dr_ref)

        @pl.when(qi == 0)
        def _():
            dk_sc[...] = jnp.zeros_like(dk_sc)
            dv_sc[...] = jnp.zeros_like(dv_sc)
            dc_sc[...] = jnp.zeros_like(dc_sc)

        def step(masked):
            q, k, dov = q_ref[...], k_ref[...], do_ref[...]
            s = _scores(q, k, ck_ref[...], qi * tq - kj * tk, tq, tk, masked)
            p = jnp.exp(s - lse_ref[...])
            dp = lax.dot_general(dov, v_ref[...], (_DIMS["nt"], ((), ())), preferred_element_type=F32)
            ds = p * (dp - dl_ref[...])
            dsb = ds.astype(BF16)
            dc_sc[...] += jnp.sum(ds, axis=0, keepdims=True)
            dv_sc[...] += lax.dot_general(p.astype(BF16), dov, (_DIMS["tn"], ((), ())), preferred_element_type=F32)
            dk_sc[...] += lax.dot_general(dsb, q, (_DIMS["tn"], ((), ())), preferred_element_type=F32)
            rows = pl.ds(pl.multiple_of(qi * tq, tq), tq)
            dq_ref[rows, :] += jnp.dot(dsb, k, preferred_element_type=F32)
            dr_ref[rows, :] += jnp.sum(ds, axis=1, keepdims=True)

        below = (kj + 1) * tk - 1 <= qi * tq

        @pl.when(below)
        def _():
            step(False)

        @pl.when(jnp.logical_and(qi >= first(kj), jnp.logical_not(below)))
        def _():
            step(True)

        @pl.when(qi == nq - 1)
        def _():
            dk_ref[...] = dk_sc[...]
            dv_ref[...] = dv_sc[...]
            dc_ref[...] = dc_sc[...]

    qblk = pl.BlockSpec((tq, HEAD), lambda h, j, i: (jnp.maximum(i, first(j)), h))
    qcol = pl.BlockSpec((None, tq, 1), lambda h, j, i: (h, jnp.maximum(i, first(j)), 0))
    kblk = pl.BlockSpec((tk, HEAD), lambda h, j, i: (j, h))
    krow = pl.BlockSpec((None, 1, tk), lambda h, j, i: (h, 0, j))
    return pl.pallas_call(
        body, name=name, grid=(H, nk, nq),
        in_specs=[qblk, kblk, pl.BlockSpec((None, tk, HEAD), lambda h, j, i: (1, j, h)), krow, qblk, qcol, qcol],
        out_specs=[pl.BlockSpec((S, HEAD), lambda h, j, i: (0, h)), pl.BlockSpec((None, S, 1), lambda h, j, i: (h, 0, 0)),
                   kblk, kblk, krow],
        out_shape=[_sds((S, W), F32), _sds((H, S, 1), F32), _sds((S, W), F32), _sds((S, W), F32), _sds((H, 1, S), F32)],
        scratch_shapes=[pltpu.VMEM((tk, HEAD), F32), pltpu.VMEM((tk, HEAD), F32), pltpu.VMEM((1, tk), F32)],
        compiler_params=_cp("parallel", "arbitrary", "arbitrary"),
    )(qn, kn, kv, crow, do, lse, delta)


def _adamw_math(w, g, m, v):
    c1 = 1.0 - ADAM_B1 ** ADAM_STEP
    c2 = 1.0 - ADAM_B2 ** ADAM_STEP
    nm = ADAM_B1 * m + (1.0 - ADAM_B1) * g
    nv = ADAM_B2 * v + (1.0 - ADAM_B2) * (g * g)
    return -ADAM_LR * ((nm / c1) / (jnp.sqrt(nv / c2) + ADAM_EPS) + ADAM_WD * w), nm, nv


def _adamw_layer(w, g, m, v, layer, prev, name):
    L, R, C = w.shape
    tr = _t(R, max(8, (1 << 19) // C), 8)

    def body(w_ref, g_ref, m_ref, v_ref, *rest):
        go_ref, d_ref, nm_ref, nv_ref = rest[4:]
        gv = g_ref[...]
        go_ref[...] = gv
        d_ref[...], nm_ref[...], nv_ref[...] = _adamw_math(w_ref[...], gv, m_ref[...], v_ref[...])

    lay = pl.BlockSpec((None, tr, C), lambda i: (layer, i, 0))
    return pl.pallas_call(
        body, name=name, grid=(R // tr,),
        in_specs=[lay, pl.BlockSpec((None, tr, C), lambda i: (0, i, 0)), lay, lay] + [ANY] * 4, out_specs=[lay] * 4,
        out_shape=[_sds((L, R, C), F32)] * 4, input_output_aliases={4: 0, 5: 1, 6: 2, 7: 3},
        compiler_params=_cp("parallel"),
    )(w, g, m, v, *prev)


def _adamw(w, g, m, v, name):
    R, C = w.shape
    tr = _t(R, max(8, (1 << 19) // max(C, 1)), 8)

    def body(w_ref, g_ref, m_ref, v_ref, d_ref, nm_ref, nv_ref):
        d_ref[...], nm_ref[...], nv_ref[...] = _adamw_math(w_ref[...], g_ref[...], m_ref[...], v_ref[...])

    blk = pl.BlockSpec((tr, C), lambda i: (i, 0))
    return pl.pallas_call(
        body, name=name, grid=(R // tr,), in_specs=[blk] * 4, out_specs=[blk] * 3,
        out_shape=[_sds((R, C), F32)] * 3, compiler_params=_cp("parallel"),
    )(w, g, m, v)


def _place():
    x, y, c = lax.axis_index("x"), lax.axis_index("y"), lax.axis_index("c")
    chips = [(1 - x, y), (x, 1 - y), (1 - x, 1 - y)]
    return x, y, c, chips


def _place_part(part, me, dtype, name, layer=None, after=None):
    L, R, C = part.shape
    first, L = (0, L) if layer is None else (layer, 1)
    tr = _t(R, max(16, (1 << 20) // C), 16)

    def body(me_ref, x_ref, *rest):
        o_ref = rest[-1]
        o_ref[...] = x_ref[...].astype(o_ref.dtype)

    extra = _tokens(after)
    return pl.pallas_call(
        body, name=name,
        grid_spec=pltpu.PrefetchScalarGridSpec(
            num_scalar_prefetch=1, grid=(L, R // tr),
            in_specs=[pl.BlockSpec((None, tr, C), lambda l, r, m: (first + l, r, 0))] + [ANY] * len(extra),
            out_specs=pl.BlockSpec((None, None, tr, C), lambda l, r, m: (m[0], l, r, 0))),
        out_shape=_sds((N_CHIPS, L, R, C), dtype), compiler_params=_cp("parallel", "parallel"),
    )(me, part, *extra)


def _gather_chips_async(bufs, collective_id, name):
    n = len(bufs)
    refs = [jax.new_ref(b, memory_space=pltpu.MemorySpace.HBM) for b in bufs]

    @pl.kernel(mesh=plsc.ScalarSubcoreMesh(axis_name="seq", num_cores=1), name=name,
               scratch_types=(pltpu.SemaphoreType.DMA((n, 6)), pltpu.SemaphoreType.DMA((n, 6))),
               compiler_params=pltpu.CompilerParams(collective_id=collective_id))
    def launch(send_sems, recv_sems):
        x, y, c, chips = _place()
        me = 2 * x + y
        sibling = (x, y, 1 - c)
        barrier = pltpu.get_barrier_semaphore()
        for peer in [(*chip, c) for chip in chips] + [sibling]:
            pl.semaphore_signal(barrier, inc=1, device_id=peer, device_id_type=MESH)
        pl.semaphore_wait(barrier, len(chips) + 1)

        def copy(i, k, chip_index, core, to):
            h = refs[i].shape[2] // 2
            rows = refs[i].at[chip_index, :, pl.ds(core * h, h), :]
            return pltpu.make_async_remote_copy(
                src_ref=rows, dst_ref=rows, send_sem=send_sems.at[i, k], recv_sem=recv_sems.at[i, k],
                device_id=to, device_id_type=MESH)

        sent = []
        for i in range(n):
            for j, chip in enumerate(chips):
                cp = copy(i, j, me, c, (*chip, c))
                cp.start()
                sent.append(cp)
        for i in range(n):
            for j, chip in enumerate(chips):
                idx = 2 * chip[0] + chip[1]
                copy(i, j, idx, c, (x, y, c)).wait_recv()
                fw = copy(i, 3 + j, idx, c, sibling)
                fw.start()
                sent.append(fw)
        for i in range(n):
            for j, chip in enumerate(chips):
                copy(i, 3 + j, 2 * chip[0] + chip[1], 1 - c, (x, y, c)).wait_recv()
        for cp in sent:
            cp.wait_send()

    launch()
    return [r[...] for r in refs]


def _gather_chips(bufs, name):
    n = len(bufs)

    def body(*refs):
        outs = refs[n:2 * n]
        send_sems, recv_sems = refs[2 * n:]
        x, y, c, chips = _place()
        me = 2 * x + y
        sibling = (x, y, 1 - c)

        def copy(i, k, chip_index, core, to):
            h = outs[i].shape[2] // 2
            rows = outs[i].at[chip_index, :, pl.ds(core * h, h), :]
            return pltpu.make_async_remote_copy(
                src_ref=rows, dst_ref=rows, send_sem=send_sems.at[i, k], recv_sem=recv_sems.at[i, k],
                device_id=to, device_id_type=MESH)

        sent = []
        for i in range(n):
            for j, chip in enumerate(chips):
                cp = copy(i, j, me, c, (*chip, c))
                cp.start()
                sent.append(cp)
        for i in range(n):
            for j, chip in enumerate(chips):
                idx = 2 * chip[0] + chip[1]
                copy(i, j, idx, c, (x, y, c)).wait_recv()
                fw = copy(i, 3 + j, idx, c, sibling)
                fw.start()
                sent.append(fw)
        for i in range(n):
            for j, chip in enumerate(chips):
                copy(i, 3 + j, 2 * chip[0] + chip[1], 1 - c, (x, y, c)).wait_recv()
        for cp in sent:
            cp.wait_send()

    return pl.pallas_call(
        body, name=name, in_specs=[ANY] * n, out_specs=[ANY] * n,
        out_shape=[_sds(b.shape, b.dtype) for b in bufs],
        scratch_shapes=[pltpu.SemaphoreType.DMA((n, 6)), pltpu.SemaphoreType.DMA((n, 6))],
        input_output_aliases={i: i for i in range(n)},
        compiler_params=pltpu.CompilerParams(has_side_effects=True),
    )(*bufs)


def _swap_halves(gs, name):
    n = len(gs)

    def body(*refs):
        ins, outs = refs[:n], refs[n:2 * n]
        send_sems, recv_sems = refs[2 * n:]
        x, y, c, _ = _place()
        cps = []
        for i in range(n):
            h = ins[i].shape[2] // 2
            cp = pltpu.make_async_remote_copy(
                src_ref=ins[i].at[:, :, pl.ds((1 - c) * h, h), :], dst_ref=outs[i], send_sem=send_sems.at[i],
                recv_sem=recv_sems.at[i], device_id=(x, y, 1 - c), device_id_type=MESH)
            cp.start()
            cps.append(cp)
        for cp in cps:
            cp.wait()

    return pl.pallas_call(
        body, name=name, in_specs=[ANY] * n, out_specs=[ANY] * n,
        out_shape=[_sds(g.shape[:2] + (g.shape[2] // 2, g.shape[3]), g.dtype) for g in gs],
        scratch_shapes=[pltpu.SemaphoreType.DMA((n,)), pltpu.SemaphoreType.DMA((n,))],
        compiler_params=pltpu.CompilerParams(has_side_effects=True),
    )(*gs)


def _pair_sum(g, gs, core, out_dtype, name):
    ns, L, R, C = g.shape
    h = R // 2
    th = _t(h, max(16, (1 << 20) // C), 16)
    nb = h // th

    def body(core_ref, a_ref, b_ref, o_ref):
        o_ref[...] = (a_ref[...].astype(F32) + b_ref[...].astype(F32)).astype(o_ref.dtype)

    blk = (None, None, th, C)
    return pl.pallas_call(
        body, name=name,
        grid_spec=pltpu.PrefetchScalarGridSpec(
            num_scalar_prefetch=1, grid=(ns, L, nb),
            in_specs=[pl.BlockSpec(blk, lambda o, l, r, cr: (o, l, cr[0] * nb + r, 0)),
                      pl.BlockSpec(blk, lambda o, l, r, cr: (o, l, r, 0))],
            out_specs=pl.BlockSpec(blk, lambda o, l, r, cr: (o, l, r, 0))),
        out_shape=_sds((ns, L, h, C), out_dtype), compiler_params=_cp("parallel", "parallel", "parallel"),
    )(core, g, gs)


def _scatter_chips(ps, name):
    n = len(ps)

    def body(*refs):
        ins, outs = refs[:n], refs[n:2 * n]
        send_sems, recv_sems = refs[2 * n:]
        x, y, c, chips = _place()
        sent = []
        for i in range(n):
            for j, chip in enumerate(chips):
                cp = pltpu.make_async_remote_copy(
                    src_ref=ins[i].at[2 * chip[0] + chip[1]], dst_ref=outs[i].at[j], send_sem=send_sems.at[i, j],
                    recv_sem=recv_sems.at[i, j], device_id=(*chip, c), device_id_type=MESH)
                cp.start()
                sent.append(cp)
        for i in range(n):
            for j in range(len(chips)):
                slot = outs[i].at[j]
                pltpu.make_async_remote_copy(
                    src_ref=slot, dst_ref=slot, send_sem=send_sems.at[i, j], recv_sem=recv_sems.at[i, j],
                    device_id=(x, y, c), device_id_type=MESH).wait_recv()
        for cp in sent:
            cp.wait_send()

    return pl.pallas_call(
        body, name=name, in_specs=[ANY] * n, out_specs=[ANY] * n,
        out_shape=[_sds((N_CHIPS - 1,) + p.shape[1:], p.dtype) for p in ps],
        scratch_shapes=[pltpu.SemaphoreType.DMA((n, 3)), pltpu.SemaphoreType.DMA((n, 3))],
        compiler_params=pltpu.CompilerParams(has_side_effects=True),
    )(*ps)


def _scatter_chips_async(ps, collective_id, name):
    n = len(ps)
    srcs = [jax.new_ref(p, memory_space=pltpu.MemorySpace.HBM) for p in ps]
    dsts = [jax.empty_ref(_sds((N_CHIPS - 1,) + p.shape[1:], p.dtype), memory_space=pltpu.MemorySpace.HBM) for p in ps]

    @pl.kernel(mesh=plsc.ScalarSubcoreMesh(axis_name="seq", num_cores=1), name=name,
               scratch_types=(pltpu.SemaphoreType.DMA((n, 3)), pltpu.SemaphoreType.DMA((n, 3))),
               compiler_params=pltpu.CompilerParams(collective_id=collective_id))
    def launch(send_sems, recv_sems):
        x, y, c, chips = _place()
        barrier = pltpu.get_barrier_semaphore()
        for chip in chips:
            pl.semaphore_signal(barrier, inc=1, device_id=(*chip, c), device_id_type=MESH)
        pl.semaphore_wait(barrier, len(chips))
        sent = []
        for i in range(n):
            for j, chip in enumerate(chips):
                cp = pltpu.make_async_remote_copy(
                    src_ref=srcs[i].at[2 * chip[0] + chip[1]], dst_ref=dsts[i].at[j], send_sem=send_sems.at[i, j],
                    recv_sem=recv_sems.at[i, j], device_id=(*chip, c), device_id_type=MESH)
                cp.start()
                sent.append(cp)
        for i in range(n):
            for j in range(len(chips)):
                slot = dsts[i].at[j]
                pltpu.make_async_remote_copy(
                    src_ref=slot, dst_ref=slot, send_sem=send_sems.at[i, j], recv_sem=recv_sems.at[i, j],
                    device_id=(x, y, c), device_id_type=MESH).wait_recv()
        for cp in sent:
            cp.wait_send()

    launch()
    return [d[...] for d in dsts]


def _sum_chips(q, p, me, core, name, after=None):
    _, L, h, C = q.shape
    th = _t(h, max(16, (1 << 19) // C), 16)
    nb = h // th
    blk = (None, None, th, C)

    def body(me_ref, core_ref, p_ref, q0, q1, q2, *rest):
        o_ref = rest[-1]
        o_ref[...] = ((p_ref[...].astype(F32) + q0[...].astype(F32)) + q1[...].astype(F32)) + q2[...].astype(F32)

    extra = _tokens(after)
    return pl.pallas_call(
        body, name=name,
        grid_spec=pltpu.PrefetchScalarGridSpec(
            num_scalar_prefetch=2, grid=(L, nb),
            in_specs=[pl.BlockSpec(blk, lambda l, r, m, c: (m[0], l, r, 0))]
            + [pl.BlockSpec(blk, functools.partial(lambda k, l, r, m, c: (k, l, r, 0), k)) for k in range(N_CHIPS - 1)]
            + [ANY] * len(extra),
            out_specs=pl.BlockSpec((None, th, C), lambda l, r, m, c: (l, c[0] * nb + r, 0))),
        out_shape=_sds((L, 2 * h, C), F32), compiler_params=_cp("parallel", "parallel"),
    )(me, core, p, q, q, q, *extra)


def _sum_small(own, q, name):
    def body(p_ref, q_ref, o_ref):
        o_ref[...] = ((p_ref[...] + q_ref[0]) + q_ref[1]) + q_ref[2]

    return pl.pallas_call(body, name=name, out_shape=_sds(own.shape, F32))(own, q)


def _join_halves(bufs, name):
    n = len(bufs)

    def body(*refs):
        outs = refs[n:2 * n]
        send_sems, recv_sems = refs[2 * n:]
        x, y, c, _ = _place()
        cps = []
        for i in range(n):
            h = outs[i].shape[1] // 2
            mine = outs[i].at[:, pl.ds(c * h, h), :]
            cp = pltpu.make_async_remote_copy(
                src_ref=mine, dst_ref=mine, send_sem=send_sems.at[i], recv_sem=recv_sems.at[i],
                device_id=(x, y, 1 - c), device_id_type=MESH)
            cp.start()
            cps.append(cp)
        for cp in cps:
            cp.wait()

    return pl.pallas_call(
        body, name=name, in_specs=[ANY] * n, out_specs=[ANY] * n,
        out_shape=[_sds(b.shape, b.dtype) for b in bufs],
        scratch_shapes=[pltpu.SemaphoreType.DMA((n,)), pltpu.SemaphoreType.DMA((n,))],
        input_output_aliases={i: i for i in range(n)},
        compiler_params=pltpu.CompilerParams(has_side_effects=True),
    )(*bufs)


def _pack(arrs, rows_mult):
    flat = jnp.concatenate([a.reshape(-1).astype(F32) for a in arrs])
    rows = -(-flat.size // LANE)
    rows = -(-rows // rows_mult) * rows_mult
    return jnp.pad(flat, (0, rows * LANE - flat.size)).reshape(rows, LANE)


def _unpack(packed, like):
    flat = packed.reshape(-1)
    out, pos = [], 0
    for a in like:
        n = math.prod(a.shape)
        out.append(flat[pos:pos + n].reshape(a.shape))
        pos += n
    return out


def _adamw_nd(w, g, m, v, name):
    shape = w.shape
    C = shape[-1]
    d, nm, nv = _adamw(w.reshape(-1, C), g.reshape(-1, C), m.reshape(-1, C), v.reshape(-1, C), name)
    return d.reshape(shape), nm.reshape(shape), nv.reshape(shape)


def kernel(x, a_norm, a_w_in, a_v_norm, a_w_s, a_b_s, a_w_out, kv_norm, w_kvf, b_f, k_norm, b_norm, b_w_qg, q_norm, b_w_out, f_norm, f_w_up, f_conv_w, f_conv_b, f_w_down, final_norm, loss_target, m_a_norm, m_a_w_in, m_a_v_norm, m_a_w_s, m_a_b_s, m_a_w_out, m_kv_norm, m_w_kvf, m_b_f, m_k_norm, m_b_norm, m_b_w_qg, m_q_norm, m_b_w_out, m_f_norm, m_f_w_up, m_f_conv_w, m_f_conv_b, m_f_w_down, m_final_norm, v_a_norm, v_a_w_in, v_a_v_norm, v_a_w_s, v_a_b_s, v_a_w_out, v_kv_norm, v_w_kvf, v_b_f, v_k_norm, v_b_norm, v_b_w_qg, v_q_norm, v_b_w_out, v_f_norm, v_f_w_up, v_f_conv_w, v_f_conv_b, v_f_w_down, v_final_norm):
    weights = dict(a_norm=a_norm, a_w_in=a_w_in, a_v_norm=a_v_norm, a_w_s=a_w_s, a_b_s=a_b_s, a_w_out=a_w_out, kv_norm=kv_norm, w_kvf=w_kvf, b_f=b_f, k_norm=k_norm, b_norm=b_norm, b_w_qg=b_w_qg, q_norm=q_norm, b_w_out=b_w_out, f_norm=f_norm, f_w_up=f_w_up, f_conv_w=f_conv_w, f_conv_b=f_conv_b, f_w_down=f_w_down, final_norm=final_norm)
    mom1 = dict(a_norm=m_a_norm, a_w_in=m_a_w_in, a_v_norm=m_a_v_norm, a_w_s=m_a_w_s, a_b_s=m_a_b_s, a_w_out=m_a_w_out, kv_norm=m_kv_norm, w_kvf=m_w_kvf, b_f=m_b_f, k_norm=m_k_norm, b_norm=m_b_norm, b_w_qg=m_b_w_qg, q_norm=m_q_norm, b_w_out=m_b_w_out, f_norm=m_f_norm, f_w_up=m_f_w_up, f_conv_w=m_f_conv_w, f_conv_b=m_f_conv_b, f_w_down=m_f_w_down, final_norm=m_final_norm)
    mom2 = dict(a_norm=v_a_norm, a_w_in=v_a_w_in, a_v_norm=v_a_v_norm, a_w_s=v_a_w_s, a_b_s=v_a_b_s, a_w_out=v_a_w_out, kv_norm=v_kv_norm, w_kvf=v_w_kvf, b_f=v_b_f, k_norm=v_k_norm, b_norm=v_b_norm, b_w_qg=v_b_w_qg, q_norm=v_q_norm, b_w_out=v_b_w_out, f_norm=v_f_norm, f_w_up=v_f_w_up, f_conv_w=v_f_conv_w, f_conv_b=v_f_conv_b, f_w_down=v_f_w_down, final_norm=v_final_norm)
    names = list(weights)
    big = ["a_w_in", "a_w_out", "w_kvf", "b_w_qg", "b_w_out", "f_w_up", "f_w_down"]
    small_sharded = ["a_norm", "a_v_norm", "f_conv_w"]
    small_repl = ["a_w_s", "a_b_s", "kv_norm", "b_f", "k_norm", "b_norm", "q_norm", "f_norm", "f_conv_b", "final_norm"]

    _, S, D = x.shape
    NA, NB, DEPTH = a_norm.shape[0], b_norm.shape[0], f_norm.shape[0]
    W = a_w_out.shape[1] * N_CHIPS
    G = a_w_s.shape[1]
    H = b_f.shape[0]
    ATT = H * HEAD
    F = f_w_down.shape[1] * N_CHIPS
    Ckv = w_kvf.shape[1]
    Cp = -(-Ckv // LANE) * LANE
    assert W == G * CHUNK and Ckv * N_CHIPS == 2 * ATT + H and S % CHUNK == 0
    core = lax.axis_index("c").astype(jnp.int32).reshape(1)
    me = (2 * lax.axis_index("x") + lax.axis_index("y")).astype(jnp.int32).reshape(1)

    small_local = [weights[k] for k in small_sharded]
    place = lambda p, nm, dt=BF16, layer=None, after=None: _place_part(p, me, dt, "place_" + nm, layer, after)
    g_ain, g_aout, g_up, g_down, g_attn = [None] * NA, [None] * NA, [None] * DEPTH, [None] * DEPTH, {}
    groups = [("ffn", 0)] + [("mix", l) for l in range(1, NA)] + [("attn", None)] + [("ffn", l) for l in range(NA, DEPTH)]

    def launch_next(after=None):
        if not groups:
            return
        kind, l = groups.pop(0)
        cid = DEPTH + 1 - len(groups)
        if kind == "attn":
            bufs = [place(jnp.pad(w_kvf, ((0, 0), (0, Cp - Ckv)))[None], "w_kvf", after=after),
                    place(b_w_qg, "b_w_qg", after=after), place(b_w_out, "b_w_out", after=after)]
            g_attn["kvf"], g_attn["qg"], g_attn["out"] = _gather_chips_async(bufs, cid, f"gather_w{cid}")
            return
        bufs = [place(f_w_up, f"f_w_up{l}", layer=l, after=after), place(f_w_down, f"f_w_down{l}", layer=l, after=after)]
        if kind == "mix":
            bufs += [place(a_w_in, f"a_w_in{l}", layer=l, after=after), place(a_w_out, f"a_w_out{l}", layer=l, after=after)]
            g_up[l], g_down[l], g_ain[l], g_aout[l] = _gather_chips_async(bufs, cid, f"gather_w{cid}")
        else:
            g_up[l], g_down[l] = _gather_chips_async(bufs, cid, f"gather_w{cid}")

    g_small, g_ain[0], g_aout[0] = _gather_chips_async(
        [place(_pack(small_local, 32)[None], "small", F32), place(a_w_in, "a_w_in0", layer=0), place(a_w_out, "a_w_out0", layer=0)],
        0, "gather_w0")
    launch_next()

    per_chip = [_unpack(g_small[j, 0], small_local) for j in range(N_CHIPS)]
    a_norm_f, a_vnorm_f, conv_w_f = [jnp.concatenate([per_chip[j][k] for j in range(N_CHIPS)], axis=-1) for k in range(3)]
    cw = conv_w_f.reshape(DEPTH, 3, 2, F).transpose(0, 2, 1, 3)
    cb = f_conv_b.reshape(DEPTH, 2, 1, F)
    tril = jnp.tril(jnp.ones((CHUNK, CHUNK), dtype=bool))
    wm = jnp.where(tril, a_w_s, 0.0).astype(BF16)
    wmt = jnp.swapaxes(wm, -1, -2)
    bcol = a_b_s[..., None]
    bf_pad = jnp.pad(b_f, (0, LANE - H))[None]
    row = lambda v: v.reshape(1, -1)

    h = x[0]
    target = loss_target[0]
    saved = [dict() for _ in range(DEPTH)]
    kvs = {}
    for l in range(DEPTH):
        sv = saved[l]
        sv["h_m"] = h
        if l < NA:
            xn = _rms_fwd(h, row(a_norm_f[l]), f"a{l}_norm")
            zpre = _mm(m2(xn), mgc(g_ain[l], 0), "nn", m2(_sds((S, 2 * W), BF16)), tm=1024, tn=512, name=f"a{l}_in")
            if l + 1 < NA:
                launch_next(zpre)
            gated = _sgu_fwd(zpre, wm[l], bcol[l], row(a_vnorm_f[l]), f"a{l}_sgu")
            h = _mm(m2(gated), mgr(g_aout[l], 0), "nn", m2(_sds((S, D), F32)), res=m2(h), tm=1024, tn=1024, name=f"a{l}_out")
            sv.update(xn_m=xn, zpre=zpre, gated=gated)
        else:
            j = l - NA
            xn = _rms_fwd(h, row(b_norm[j]), f"b{j}_norm")
            qg = _mm(m2(xn), mgc(g_attn["qg"], j), "nn", mcs(_sds((2, S, ATT), BF16)), tm=1024, tn=512, name=f"b{j}_qg")
            qn = _headnorm_fwd(qg, 0, row(q_norm[j]) * QK_SCALE, f"b{j}_qnorm")
            o, lse = _attn_fwd(qn, kvs["kn"], kvs["kv"], kvs["crow"], f"b{j}_attn")
            og = _gate_fwd(o, qg, f"b{j}_gate")
            h = _mm(m2(og), mgr(g_attn["out"], j), "nn", m2(_sds((S, D), F32)), res=m2(h), tm=1024, tn=1024, name=f"b{j}_out")
            sv.update(xn_m=xn, qg=qg, qn=qn, o=o, lse=lse, og=og)
        sv["h_f"] = h
        xn = _rms_fwd(h, row(f_norm[l]), f"f{l}_norm")
        hup = _mm(m2(xn), mgc(g_up[l], 0), "nn", mcs(_sds((2, S, F), BF16)), tm=1024, tn=1408, name=f"f{l}_up")
        launch_next(hup)
        act = _convgate_fwd(hup, cw[l], cb[l], f"f{l}_conv")
        h = _mm(m2(act), mgr(g_down[l], 0), "nn", m2(_sds((S, D), F32)), res=m2(h), tm=1024, tn=1024, tk=1408, name=f"f{l}_down")
        sv.update(xn_f=xn, hup=hup, act=act)
        if l == NA - 1:
            kvf_full = g_attn["kvf"][:, 0, :, :Ckv].transpose(1, 0, 2).reshape(D, N_CHIPS * Ckv)
            w_kv = jnp.stack([kvf_full[:, :ATT], kvf_full[:, ATT:2 * ATT]])
            w_f = jnp.pad(kvf_full[:, 2 * ATT:], ((0, 0), (0, LANE - H)))
            xn_kv = _rms_fwd(h, row(kv_norm), "kv_norm")
            kv = _mm(m2(xn_kv), mcs(w_kv), "nn", mcs(_sds((2, S, ATT), BF16)), tm=1024, tn=512, name="kv_proj")
            launch_next(kv)
            fpre = _mm(m2(xn_kv), m2(w_f), "nn", m2(_sds((S, LANE), F32)), tm=1024, name="kv_fproj")
            kn = _headnorm_fwd(kv, 0, row(k_norm), "kv_knorm")
            cums = _logf_cumsum(fpre, bf_pad, "kv_cumsum")
            cT = cums[:, :H].T
            kvs.update(h=h, xn=xn_kv, kv=kv, fpre=fpre, kn=kn, crow=cT[:, None, :])

    loss11, dh, dhb, d_final = _final_loss(h, row(final_norm), target, "final_loss")
    loss = lax.psum(loss11[0, 0], ("x", "y", "c"))

    stacked = {k: tuple(lax.empty(weights[k].shape, F32) for _ in range(4))
               for k in ("a_w_in", "a_w_out", "b_w_qg", "b_w_out", "f_w_up", "f_w_down")}
    flat = {}
    pending = []
    next_id = [DEPTH + 2]

    def grad_buf(k):
        return _sds((N_CHIPS, 1) + weights[k].shape[1:], BF16)

    def start_chunk(entries, tag):
        Gs = [e[2] for e in entries]
        others = _swap_halves(Gs, f"grads_swap_{tag}")
        partial = [_pair_sum(g, o, core, g.dtype, f"grads_pair_{tag}{i}") for i, (g, o) in enumerate(zip(Gs, others))]
        by_chip = _scatter_chips_async(partial, next_id[0], f"grads_scatter_{tag}")
        next_id[0] += 1
        pending.append((entries, partial, by_chip, tag))
        return partial

    def finish_chunk(after):
        entries, partial, by_chip, tag = pending.pop(0)
        halves = []
        for i, ((k, l, _), p, q) in enumerate(zip(entries, partial, by_chip)):
            if k == "small":
                half = _sum_small(lax.dynamic_index_in_dim(p, me[0], 0, keepdims=False), q, "grads_sum_small")
                hs = half.shape[1]
                halves.append(lax.dynamic_update_slice(jnp.zeros((1, 2 * hs, LANE), F32), half, (0, core[0] * hs, 0)))
            else:
                halves.append(_sum_chips(q, p, me, core, f"grads_sum_{tag}{i}", after=after))
        for (k, l, _), f in zip(entries, _join_halves(halves, f"grads_join_{tag}")):
            if k in stacked:
                stacked[k] = tuple(_adamw_layer(weights[k], f, mom1[k], mom2[k], l, stacked[k], f"adamw_{k}{l}"))
            else:
                flat[k] = f

    d_anorm, d_avnorm, d_ws, d_bs = [None] * NA, [None] * NA, [None] * NA, [None] * NA
    d_bnorm, d_qnorm = [None] * NB, [None] * NB
    d_fnorm, d_cw, d_cb = [None] * DEPTH, [None] * DEPTH, [None] * DEPTH
    dkn, dvv, dck = [], [], []
    G_kvf = d_kvnorm = d_bf = d_knorm = None
    tok = None
    for l in reversed(range(DEPTH)):
        sv = saved[l]
        chunk = []
        if l == NA - 1:
            dkv, d_knorm = _headnorm_bwd(dkn, kvs["kv"], row(k_norm), dvv, "kv_knorm_bwd")
            dc = sum(dck)
            dc = jnp.pad(dc, ((0, 0), (0, LANE - H)))
            df, d_bf = _logf_cumsum_bwd(dc, kvs["fpre"], bf_pad, "kv_cumsum_bwd")
            dxn = _mm(mcs(dkv), mcs(w_kv), "nt", m2(_sds((S, D), F32)), after=tok, tm=1024, tn=1024, tk=2048, name="kv_proj_dx")
            dxn = _mm(m2(df), m2(w_f), "nt", m2(_sds((S, D), BF16)), res=m2(dxn), tm=1024, tn=512, name="kv_fproj_dx")
            dw_kv = _mm(m2(kvs["xn"]), mcs(dkv), "tn", mcs(_sds((2, D, ATT), BF16)), tm=1024, tn=1024, tk=2048, name="kv_proj_dw")
            dw_f = _mm(m2(kvs["xn"]), m2(df), "tn", m2(_sds((D, LANE), BF16)), tm=512, tk=2048, name="kv_fproj_dw")
            dh, dhb, d_kvnorm = _rms_bwd(dxn, kvs["h"], row(kv_norm), dh, "kv_norm_bwd")
            dfull = jnp.concatenate([dw_kv[0], dw_kv[1], dw_f[:, :H]], axis=1)
            G_kvf = jnp.pad(dfull.reshape(D, N_CHIPS, Ckv).transpose(1, 0, 2), ((0, 0), (0, 0), (0, Cp - Ckv)))[:, None]
            chunk.append(("w_kvf", 0, G_kvf))
        dact = _mm(m2(dhb), mgr(g_down[l], 0), "nt", m2(_sds((S, F), BF16)), after=tok, tm=1024, tn=1408, tk=2048, name=f"f{l}_down_dx")
        G_down = _mm(m2(sv["act"]), m2(dhb), "tn", mgr(grad_buf("f_w_down"), 0), tm=1408, tn=1024, tk=2048, name=f"f{l}_down_dw")
        dhup, d_cw[l], d_cb[l] = _convgate_bwd(sv["hup"], dact, cw[l], cb[l], f"f{l}_conv_bwd")
        dxn = _mm(mcs(dhup), mgc(g_up[l], 0), "nt", m2(_sds((S, D), BF16)), tm=1024, tn=2048, tk=1408, name=f"f{l}_up_dx")
        G_up = _mm(m2(sv["xn_f"]), mcs(dhup), "tn", mgc(grad_buf("f_w_up"), 0), tm=1024, tn=1408, tk=2048, name=f"f{l}_up_dw")
        dh, dhb, d_fnorm[l] = _rms_bwd(dxn, sv["h_f"], row(f_norm[l]), dh, f"f{l}_norm_bwd")
        chunk += [("f_w_up", l, G_up), ("f_w_down", l, G_down)]
        if l == 0:
            tok = start_chunk(chunk, "f0")
            finish_chunk(tok)
            chunk = []
        if l >= NA:
            j = l - NA
            dog = _mm(m2(dhb), mgr(g_attn["out"], j), "nt", m2(_sds((S, ATT), BF16)), tm=1024, tn=512, tk=2048, name=f"b{j}_out_dx")
            G_bout = _mm(m2(sv["og"]), m2(dhb), "tn", mgr(grad_buf("b_w_out"), 0), tm=512, tn=1024, tk=4096, name=f"b{j}_out_dw")
            do, dgate, delta = _gate_bwd(dog, sv["o"], sv["qg"], f"b{j}_gate_bwd")
            dqn, drow, dkn_j, dv_j, dcol = _attn_bwd(sv["qn"], kvs["kn"], kvs["kv"], kvs["crow"], do, sv["lse"], delta, f"b{j}_attn_bwd")
            dkn.append(dkn_j); dvv.append(dv_j); dck.append(drow[:, :, 0].T - dcol[:, 0, :].T)
            dqg, d_qs = _headnorm_bwd([dqn], sv["qg"], row(q_norm[j]) * QK_SCALE, [dgate], f"b{j}_qnorm_bwd")
            d_qnorm[j] = d_qs * QK_SCALE
            dxn = _mm(mcs(dqg), mgc(g_attn["qg"], j), "nt", m2(_sds((S, D), BF16)), tm=1024, tn=2048, tk=1024, name=f"b{j}_qg_dx")
            G_bqg = _mm(m2(sv["xn_m"]), mcs(dqg), "tn", mgc(grad_buf("b_w_qg"), 0), tm=1024, tn=1024, tk=2048, name=f"b{j}_qg_dw")
            dh, dhb, d_bnorm[j] = _rms_bwd(dxn, sv["h_m"], row(b_norm[j]), dh, f"b{j}_norm_bwd")
            chunk += [("b_w_qg", j, G_bqg), ("b_w_out", j, G_bout)]
        else:
            dgated = _mm(m2(dhb), mgr(g_aout[l], 0), "nt", m2(_sds((S, W), BF16)), after=tok, tm=1024, tn=512, tk=2048, name=f"a{l}_out_dx")
            G_aout = _mm(m2(sv["gated"]), m2(dhb), "tn", mgr(grad_buf("a_w_out"), 0), tm=512, tn=1024, tk=4096, name=f"a{l}_out_dw")
            dz, d_ws[l], d_bs[l], d_avnorm[l] = _sgu_bwd(sv["zpre"], dgated, wm[l], wmt[l], bcol[l], row(a_vnorm_f[l]), f"a{l}_sgu_bwd")
            dxn = _mm(m2(dz), mgc(g_ain[l], 0), "nt", m2(_sds((S, D), BF16)), tm=1024, tn=2048, tk=1024, name=f"a{l}_in_dx")
            G_ain = _mm(m2(sv["xn_m"]), m2(dz), "tn", mgc(grad_buf("a_w_in"), 0), tm=1024, tn=1024, tk=2048, name=f"a{l}_in_dw")
            dh, dhb, d_anorm[l] = _rms_bwd(dxn, sv["h_m"], row(a_norm_f[l]), dh, f"a{l}_norm_bwd")
            chunk += [("a_w_in", l, G_ain), ("a_w_out", l, G_aout)]
        if l > 0:
            tok = start_chunk(chunk, f"l{l}")
            if len(pending) > 1:
                finish_chunk(tok)
    grad_x = dh[None]

    full = dict(
        a_norm=jnp.concatenate(d_anorm, axis=0), a_v_norm=jnp.concatenate(d_avnorm, axis=0),
        f_conv_w=jnp.stack(d_cw).transpose(0, 2, 1, 3).reshape(DEPTH, 3, 2 * F),
        a_w_s=jnp.where(tril, jnp.stack(d_ws), 0.0), a_b_s=jnp.stack(d_bs)[..., 0],
        kv_norm=d_kvnorm[0], b_f=d_bf[0, :H], k_norm=d_knorm[0], b_norm=jnp.concatenate(d_bnorm, axis=0),
        q_norm=jnp.concatenate(d_qnorm, axis=0), f_norm=jnp.concatenate(d_fnorm, axis=0),
        f_conv_b=jnp.stack(d_cb).reshape(DEPTH, 2 * F), final_norm=d_final[0])
    shard_rows = []
    for j in range(N_CHIPS):
        pieces = []
        for k in small_sharded:
            n = weights[k].shape[-1]
            pieces.append(full[k][..., j * n:(j + 1) * n])
        shard_rows.append(_pack(pieces, 32))
    rs = shard_rows[0].shape[0]
    repl = _pack([full[k] for k in small_repl], N_CHIPS * 32)
    rr = repl.shape[0] // N_CHIPS
    G_small = jnp.concatenate([jnp.stack(shard_rows), repl.reshape(N_CHIPS, rr, LANE)], axis=1)[:, None]

    tok = start_chunk(chunk + [("small", 0, G_small)], "l0")
    finish_chunk(tok)
    finish_chunk(None)
    F_small = flat["small"]
    repl_buf = lax.dynamic_update_slice(jnp.zeros((N_CHIPS, 1, rr, LANE), F32), F_small[None, :, rs:, :], (me[0], 0, 0, 0))
    (repl_all,) = _gather_chips([repl_buf], "gather_small_grads")

    grads, delta, new_m, new_v = ({k: t[i] for k, t in stacked.items()} for i in range(4))
    grads["w_kvf"] = flat["w_kvf"][0, :, :Ckv]
    for k, gk in zip(small_sharded, _unpack(F_small[0, :rs], small_local)):
        grads[k] = gk
    for k, gk in zip(small_repl, _unpack(repl_all.reshape(N_CHIPS * rr, LANE), [weights[k] for k in small_repl])):
        grads[k] = gk

    delta["w_kvf"], new_m["w_kvf"], new_v["w_kvf"] = _adamw_nd(w_kvf, grads["w_kvf"], m_w_kvf, v_w_kvf, "adamw_w_kvf")
    small = small_sharded + small_repl
    packed = [_pack([t[k] for k in small], 8) for t in (weights, grads, mom1, mom2)]
    outs = _adamw(*packed, "adamw_small")
    like = [weights[k] for k in small]
    for t, o in zip((delta, new_m, new_v), outs):
        for k, a in zip(small, _unpack(o, like)):
            t[k] = a

    return (loss, grad_x, *[grads[k] for k in names], *[delta[k] for k in names],
            *[new_m[k] for k in names], *[new_v[k] for k in names])
```

```python
import functools
import math

import jax
import jax.numpy as jnp
from jax import lax
from jax.experimental import pallas as pl
from jax.experimental.pallas import tpu as pltpu
from jax.experimental.pallas import tpu_sc as plsc

F32, BF16 = jnp.float32, jnp.bfloat16
EPS = 1e-6
CHUNK = 128
HEAD = 128
LANE = 128
HALO = 16
N_CHIPS = 4
VMEM_LIMIT = 48 * 1024 * 1024
MESH = pl.DeviceIdType.MESH
ANY = pl.BlockSpec(memory_space=pl.ANY)

ADAM_LR, ADAM_B1, ADAM_B2, ADAM_EPS, ADAM_WD, ADAM_STEP = 0.001, 0.9, 0.999, 1e-08, 0.01, 10
NEG = -1e30
QK_SCALE = HEAD ** -0.5


def _cp(*sem):
    return pltpu.CompilerParams(dimension_semantics=sem, vmem_limit_bytes=VMEM_LIMIT)


def _t(dim, pref, mult=LANE):
    if dim <= pref:
        return dim
    t = (pref // mult) * mult
    while t >= mult:
        if dim % t == 0:
            return t
        t -= mult
    return dim


def _sds(shape, dtype):
    return jax.ShapeDtypeStruct(tuple(shape), dtype)


class Mat:
    def __init__(self, arr, shape, rlim, clim, block, index):
        self.arr, self.shape, self.rlim, self.clim = arr, shape, rlim, clim
        self._block, self._index = block, index

    def spec(self, tr, tc, gmap):
        assert self.rlim % tr == 0 and self.clim % tc == 0, (self.shape, self.rlim, self.clim, tr, tc)
        index = self._index(tr, tc)
        return pl.BlockSpec(self._block(tr, tc), lambda *g: index(*gmap(*g)))


def m2(arr):
    R, C = arr.shape
    return Mat(arr, (R, C), R, C, lambda tr, tc: (tr, tc), lambda tr, tc: (lambda i, j: (i, j)))


def mcs(arr):
    ns, R, Cs = arr.shape
    return Mat(arr, (R, ns * Cs), R, Cs, lambda tr, tc: (None, tr, tc),
               lambda tr, tc: (lambda i, j: (j // (Cs // tc), i, j % (Cs // tc))))


def mhalf(arr, p):
    ns, R, Cs = arr.shape
    return Mat(arr, (R, Cs), R, Cs, lambda tr, tc: (None, tr, tc), lambda tr, tc: (lambda i, j: (p, i, j)))


def mgc(arr, l):
    ns, L, R, Cs = arr.shape
    return Mat(arr, (R, ns * Cs), R, Cs, lambda tr, tc: (None, None, tr, tc),
               lambda tr, tc: (lambda i, j: (j // (Cs // tc), l, i, j % (Cs // tc))))


def mgr(arr, l):
    ns, L, Rs, C = arr.shape
    return Mat(arr, (ns * Rs, C), Rs, C, lambda tr, tc: (None, None, tr, tc),
               lambda tr, tc: (lambda i, j: (i // (Rs // tr), l, i % (Rs // tr), j)))


_DIMS = {"nn": ((1,), (0,)), "nt": ((1,), (1,)), "tn": ((0,), (0,))}


def _tokens(after):
    if after is None:
        return []
    return list(after) if isinstance(after, (list, tuple)) else [after]


def _mm(a, b, mode, out, *, res=None, alias=False, after=None, tm=512, tn=512, tk=4096, name):
    if mode == "tn":
        (K, M), (K2, N) = a.shape, b.shape
        alim_m, alim_k, blim_k, blim_n = a.clim, a.rlim, b.rlim, b.clim
    elif mode == "nt":
        (M, K), (N, K2) = a.shape, b.shape
        alim_m, alim_k, blim_k, blim_n = a.rlim, a.clim, b.clim, b.rlim
    else:
        (M, K), (K2, N) = a.shape, b.shape
        alim_m, alim_k, blim_k, blim_n = a.rlim, a.clim, b.rlim, b.clim
    assert K == K2 and out.shape == (M, N), (name, a.shape, b.shape, out.shape)
    tm = _t(math.gcd(alim_m, out.rlim), tm)
    tn = _t(math.gcd(blim_n, out.clim), tn)
    tk = _t(math.gcd(alim_k, blim_k), tk)
    grid = (M // tm, N // tn, K // tk)
    nk = grid[2]
    if mode == "tn":
        a_spec = a.spec(tk, tm, lambda i, j, k: (k, i))
    else:
        a_spec = a.spec(tm, tk, lambda i, j, k: (i, k))
    if mode == "nt":
        b_spec = b.spec(tn, tk, lambda i, j, k: (j, k))
    else:
        b_spec = b.spec(tk, tn, lambda i, j, k: (k, j))
    o_spec = out.spec(tm, tn, lambda i, j, k: (i, j))
    operands, in_specs = [a.arr, b.arr], [a_spec, b_spec]
    if res is not None:
        operands.append(res.arr)
        in_specs.append(res.spec(tm, tn, lambda i, j, k: (i, j)))
    aliases = {}
    if alias:
        aliases = {len(operands): 0}
        operands.append(out.arr)
        in_specs.append(ANY)
    tokens = _tokens(after)
    operands += tokens
    in_specs += [ANY] * len(tokens)
    dims = (_DIMS[mode], ((), ()))
    has_res = res is not None

    def body(*refs):
        a_ref, b_ref = refs[0], refs[1]
        res_ref = refs[2] if has_res else None
        n_in = 2 + has_res + alias + len(tokens)
        o_ref = refs[n_in]
        p = lax.dot_general(a_ref[...].astype(BF16), b_ref[...].astype(BF16), dims, preferred_element_type=F32)

        def finish(v):
            if has_res:
                v = v + res_ref[...].astype(F32)
            o_ref[...] = v.astype(o_ref.dtype)

        if nk == 1:
            finish(p)
        else:
            acc = refs[n_in + 1]
            k = pl.program_id(2)

            @pl.when(k == 0)
            def _():
                acc[...] = p

            @pl.when(k > 0)
            def _():
                acc[...] += p

            @pl.when(k == nk - 1)
            def _():
                finish(acc[...])

    return pl.pallas_call(
        body, name=name, grid=grid, in_specs=in_specs, out_specs=o_spec,
        out_shape=_sds(out.arr.shape, out.arr.dtype),
        scratch_shapes=[pltpu.VMEM((tm, tn), F32)] if nk > 1 else [],
        input_output_aliases=aliases,
        compiler_params=_cp("parallel", "parallel", "arbitrary"),
    )(*operands)


def _rms_fwd(x, g, name):
    S, D = x.shape
    tr = _t(S, 512)

    def body(x_ref, g_ref, o_ref):
        xv = x_ref[...]
        r = lax.rsqrt(jnp.mean(xv * xv, axis=-1, keepdims=True) + EPS)
        o_ref[...] = ((xv * r) * g_ref[...]).astype(o_ref.dtype)

    return pl.pallas_call(
        body, name=name, grid=(S // tr,),
        in_specs=[pl.BlockSpec((tr, D), lambda i: (i, 0)), pl.BlockSpec((1, D), lambda i: (0, 0))],
        out_specs=pl.BlockSpec((tr, D), lambda i: (i, 0)), out_shape=_sds((S, D), BF16),
        compiler_params=_cp("parallel"),
    )(x, g)


def _rms_bwd(dy, x, g, dres, name):
    S, D = x.shape
    tr = _t(S, 512)

    def body(dy_ref, x_ref, g_ref, dres_ref, dx_ref, dxb_ref, dg_ref):
        xv = x_ref[...]
        dyv = dy_ref[...].astype(F32)
        r = lax.rsqrt(jnp.mean(xv * xv, axis=-1, keepdims=True) + EPS)
        xh = xv * r
        dxh = dyv * g_ref[...]
        m = jnp.mean(dxh * xh, axis=-1, keepdims=True)
        dx = dres_ref[...] + r * (dxh - xh * m)
        dx_ref[...] = dx
        dxb_ref[...] = dx.astype(dxb_ref.dtype)

        @pl.when(pl.program_id(0) == 0)
        def _():
            dg_ref[...] = jnp.zeros_like(dg_ref)

        dg_ref[...] += jnp.sum(dyv * xh, axis=0, keepdims=True)

    row = pl.BlockSpec((tr, D), lambda i: (i, 0))
    vec = pl.BlockSpec((1, D), lambda i: (0, 0))
    return pl.pallas_call(
        body, name=name, grid=(S // tr,), in_specs=[row, row, vec, row], out_specs=[row, row, vec],
        out_shape=[_sds((S, D), F32), _sds((S, D), BF16), _sds((1, D), F32)], compiler_params=_cp("arbitrary"),
    )(dy, x, g, dres)


def _final_loss(h, g, target, name):
    S, D = h.shape
    tr = _t(S, 512)

    def body(x_ref, g_ref, t_ref, loss_ref, dx_ref, dxb_ref, dg_ref):
        xv = x_ref[...]
        r = lax.rsqrt(jnp.mean(xv * xv, axis=-1, keepdims=True) + EPS)
        xh = xv * r
        err = xh * g_ref[...] - t_ref[...]
        part = 0.5 * jnp.sum(jnp.mean(err * err, axis=-1, keepdims=True), axis=0, keepdims=True)
        dyv = err * (1.0 / D)
        dxh = dyv * g_ref[...]
        m = jnp.mean(dxh * xh, axis=-1, keepdims=True)
        dx = r * (dxh - xh * m)
        dx_ref[...] = dx
        dxb_ref[...] = dx.astype(dxb_ref.dtype)

        @pl.when(pl.program_id(0) == 0)
        def _():
            dg_ref[...] = jnp.zeros_like(dg_ref)
            loss_ref[...] = jnp.zeros_like(loss_ref)

        dg_ref[...] += jnp.sum(dyv * xh, axis=0, keepdims=True)
        loss_ref[...] += part

    row = pl.BlockSpec((tr, D), lambda i: (i, 0))
    vec = pl.BlockSpec((1, D), lambda i: (0, 0))
    one = pl.BlockSpec((1, 1), lambda i: (0, 0))
    return pl.pallas_call(
        body, name=name, grid=(S // tr,), in_specs=[row, vec, row], out_specs=[one, row, row, vec],
        out_shape=[_sds((1, 1), F32), _sds((S, D), F32), _sds((S, D), BF16), _sds((1, D), F32)], compiler_params=_cp("arbitrary"),
    )(h, g, target)


_RSQRT2 = 0.7071067811865476
_RSQRT2PI = 0.3989422804014327


def _gelu(x):
    return 0.5 * x * (1.0 + lax.erf(x * _RSQRT2))


def _gelu_grad(x):
    return 0.5 * (1.0 + lax.erf(x * _RSQRT2)) + x * (jnp.exp(-0.5 * x * x) * _RSQRT2PI)


def _sgu_fwd(zpre, wm, bcol, vnorm, name):
    S, W2 = zpre.shape
    W = W2 // 2
    G = W // CHUNK

    def body(z_ref, wm_ref, b_ref, vn_ref, o_ref):
        zp = z_ref[...].astype(F32)
        u = _gelu(zp[:, :W])
        v = _gelu(zp[:, W:])
        rv = lax.rsqrt(jnp.mean(v * v, axis=-1, keepdims=True) + EPS)
        vn = ((v * rv) * vn_ref[...]).astype(BF16)
        for g in range(G):
            sl = slice(g * CHUNK, (g + 1) * CHUNK)
            mixed = jnp.dot(wm_ref[g], vn[:, sl], preferred_element_type=F32) + b_ref[g]
            o_ref[:, sl] = (u[:, sl] * mixed).astype(o_ref.dtype)

    return pl.pallas_call(
        body, name=name, grid=(S // CHUNK,),
        in_specs=[pl.BlockSpec((CHUNK, W2), lambda i: (i, 0)),
                  pl.BlockSpec((G, CHUNK, CHUNK), lambda i: (0, 0, 0)),
                  pl.BlockSpec((G, CHUNK, 1), lambda i: (0, 0, 0)),
                  pl.BlockSpec((1, W), lambda i: (0, 0))],
        out_specs=pl.BlockSpec((CHUNK, W), lambda i: (i, 0)), out_shape=_sds((S, W), BF16),
        compiler_params=_cp("parallel"),
    )(zpre, wm, bcol, vnorm)


def _sgu_bwd(zpre, dgated, wm, wmt, bcol, vnorm, name):
    S, W2 = zpre.shape
    W = W2 // 2
    G = W // CHUNK

    def body(z_ref, dg_ref, wm_ref, wmt_ref, b_ref, vn_ref, dz_ref, dws_ref, dbs_ref, dvn_ref):
        @pl.when(pl.program_id(0) == 0)
        def _():
            dws_ref[...] = jnp.zeros_like(dws_ref)
            dbs_ref[...] = jnp.zeros_like(dbs_ref)
            dvn_ref[...] = jnp.zeros_like(dvn_ref)

        zp = z_ref[...].astype(F32)
        zu, zv = zp[:, :W], zp[:, W:]
        u = _gelu(zu)
        v = _gelu(zv)
        rv = lax.rsqrt(jnp.mean(v * v, axis=-1, keepdims=True) + EPS)
        vh = v * rv
        vn = (vh * vn_ref[...]).astype(BF16)
        dgv = dg_ref[...].astype(F32)
        du_parts, dvn_parts = [], []
        for g in range(G):
            sl = slice(g * CHUNK, (g + 1) * CHUNK)
            vg = vn[:, sl]
            mixed = jnp.dot(wm_ref[g], vg, preferred_element_type=F32) + b_ref[g]
            dgg = dgv[:, sl]
            du_parts.append(dgg * mixed)
            dmixed = dgg * u[:, sl]
            dbs_ref[g] += jnp.sum(dmixed, axis=1, keepdims=True)
            dmb = dmixed.astype(BF16)
            dws_ref[g] += lax.dot_general(dmb, vg, (_DIMS["nt"], ((), ())), preferred_element_type=F32)
            dvn_parts.append(jnp.dot(wmt_ref[g], dmb, preferred_element_type=F32))
        du = jnp.concatenate(du_parts, axis=1)
        dvn = jnp.concatenate(dvn_parts, axis=1)
        dvn_ref[...] += jnp.sum(dvn * vh, axis=0, keepdims=True)
        dvh = dvn * vn_ref[...]
        dv = rv * (dvh - vh * jnp.mean(dvh * vh, axis=-1, keepdims=True))
        dz_ref[:, :W] = (du * _gelu_grad(zu)).astype(dz_ref.dtype)
        dz_ref[:, W:] = (dv * _gelu_grad(zv)).astype(dz_ref.dtype)

    full3 = lambda shape: pl.BlockSpec(shape, lambda i: (0, 0, 0))
    return pl.pallas_call(
        body, name=name, grid=(S // CHUNK,),
        in_specs=[pl.BlockSpec((CHUNK, W2), lambda i: (i, 0)), pl.BlockSpec((CHUNK, W), lambda i: (i, 0)),
                  full3((G, CHUNK, CHUNK)), full3((G, CHUNK, CHUNK)), full3((G, CHUNK, 1)),
                  pl.BlockSpec((1, W), lambda i: (0, 0))],
        out_specs=[pl.BlockSpec((CHUNK, W2), lambda i: (i, 0)), full3((G, CHUNK, CHUNK)), full3((G, CHUNK, 1)),
                   pl.BlockSpec((1, W), lambda i: (0, 0))],
        out_shape=[_sds((S, W2), BF16), _sds((G, CHUNK, CHUNK), F32), _sds((G, CHUNK, 1), F32), _sds((1, W), F32)],
        compiler_params=_cp("arbitrary"),
    )(zpre, dgated, wm, wmt, bcol, vnorm)


def _conv_taps(h_ref, half, r0, R, tc):
    if r0 == 0:
        xe = jnp.concatenate([jnp.zeros((HALO, tc), F32), h_ref[half, 0:R, :].astype(F32)], axis=0)
    else:
        xe = h_ref[half, r0 - HALO:r0 + R, :].astype(F32)
    return xe[HALO:], pltpu.roll(xe, 1, 0)[HALO:], pltpu.roll(xe, 2, 0)[HALO:]


def _conv_apply(taps, w, b):
    x0, x1, x2 = taps
    return x2 * w[0:1] + x1 * w[1:2] + x0 * w[2:3] + b


def _convgate_fwd(hup, cw, cb, name):
    _, S, F = hup.shape
    tc = _t(F, 256)
    R = _t(S, 512)

    def body(h_ref, w_ref, b_ref, o_ref):
        for r0 in range(0, S, R):
            gate = _conv_apply(_conv_taps(h_ref, 0, r0, R, tc), w_ref[0], b_ref[0])
            val = _conv_apply(_conv_taps(h_ref, 1, r0, R, tc), w_ref[1], b_ref[1])
            o_ref[r0:r0 + R, :] = (gate * jax.nn.sigmoid(gate) * val).astype(o_ref.dtype)

    return pl.pallas_call(
        body, name=name, grid=(F // tc,),
        in_specs=[pl.BlockSpec((2, S, tc), lambda j: (0, 0, j)), pl.BlockSpec((2, 3, tc), lambda j: (0, 0, j)),
                  pl.BlockSpec((2, 1, tc), lambda j: (0, 0, j))],
        out_specs=pl.BlockSpec((S, tc), lambda j: (0, j)), out_shape=_sds((S, F), BF16),
        compiler_params=_cp("parallel"),
    )(hup, cw, cb)


def _convgate_bwd(hup, dact, cw, cb, name):
    _, S, F = hup.shape
    tc = _t(F, 256)
    R = _t(S, 512)

    def body(h_ref, da_ref, w_ref, b_ref, dh_ref, dw_ref, db_ref, dhc):
        dhc[:, S:S + HALO, :] = jnp.zeros((2, HALO, tc), F32)
        dw_acc = [[jnp.zeros((1, tc), F32) for _ in range(3)] for _ in range(2)]
        db_acc = [jnp.zeros((1, tc), F32) for _ in range(2)]
        for r0 in range(0, S, R):
            taps = [_conv_taps(h_ref, p, r0, R, tc) for p in range(2)]
            gate = _conv_apply(taps[0], w_ref[0], b_ref[0])
            val = _conv_apply(taps[1], w_ref[1], b_ref[1])
            da = da_ref[r0:r0 + R, :].astype(F32)
            sg = jax.nn.sigmoid(gate)
            d = [da * val * (sg * (1.0 + gate * (1.0 - sg))), da * (gate * sg)]
            for p in range(2):
                dhc[p, r0:r0 + R, :] = d[p]
                db_acc[p] = db_acc[p] + jnp.sum(d[p], axis=0, keepdims=True)
                for k in range(3):
                    dw_acc[p][k] = dw_acc[p][k] + jnp.sum(d[p] * taps[p][2 - k], axis=0, keepdims=True)
        for p in range(2):
            db_ref[p] = db_acc[p]
            dw_ref[p] = jnp.concatenate(dw_acc[p], axis=0)
            w = w_ref[p]
            for r0 in range(0, S, R):
                de = dhc[p, r0:r0 + R + HALO, :]
                d1 = pltpu.roll(de, R + HALO - 1, 0)[:R]
                d2 = pltpu.roll(de, R + HALO - 2, 0)[:R]
                dh_ref[p, r0:r0 + R, :] = (de[:R] * w[2:3] + d1 * w[1:2] + d2 * w[0:1]).astype(dh_ref.dtype)

    blk = lambda rows: pl.BlockSpec((2, rows, tc), lambda j: (0, 0, j))
    return pl.pallas_call(
        body, name=name, grid=(F // tc,),
        in_specs=[blk(S), pl.BlockSpec((S, tc), lambda j: (0, j)), blk(3), blk(1)],
        out_specs=[blk(S), blk(3), blk(1)],
        out_shape=[_sds((2, S, F), BF16), _sds((2, 3, F), F32), _sds((2, 1, F), F32)],
        scratch_shapes=[pltpu.VMEM((2, S + HALO, tc), F32)],
        compiler_params=_cp("parallel"),
    )(hup, dact, cw, cb)


def _headnorm_fwd(x3, part, gain, name):
    _, S, W = x3.shape
    tr = _t(S, 512)

    def body(x_ref, g_ref, o_ref):
        xv = x_ref[...].astype(F32)
        for h in range(W // HEAD):
            sl = slice(h * HEAD, (h + 1) * HEAD)
            xh = xv[:, sl]
            r = lax.rsqrt(jnp.mean(xh * xh, axis=-1, keepdims=True) + EPS)
            o_ref[:, sl] = ((xh * r) * g_ref[...]).astype(o_ref.dtype)

    return pl.pallas_call(
        body, name=name, grid=(S // tr,),
        in_specs=[pl.BlockSpec((None, tr, W), lambda i: (part, i, 0)), pl.BlockSpec((1, HEAD), lambda i: (0, 0))],
        out_specs=pl.BlockSpec((tr, W), lambda i: (i, 0)), out_shape=_sds((S, W), BF16),
        compiler_params=_cp("parallel"),
    )(x3, gain)


def _headnorm_bwd(dys, x3, gain, passes, name):
    _, S, W = x3.shape
    tr = _t(S, 256)
    nd, npass = len(dys), len(passes)

    def body(*refs):
        dy_refs = refs[:nd]
        x_ref, g_ref = refs[nd], refs[nd + 1]
        p_refs = refs[nd + 2:nd + 2 + npass]
        o_ref, dg_ref = refs[nd + 2 + npass], refs[nd + 3 + npass]

        @pl.when(pl.program_id(0) == 0)
        def _():
            dg_ref[...] = jnp.zeros_like(dg_ref)

        xv = x_ref[...].astype(F32)
        dyv = dy_refs[0][...].astype(F32)
        for r in dy_refs[1:]:
            dyv = dyv + r[...].astype(F32)
        dg = jnp.zeros((1, HEAD), F32)
        for h in range(W // HEAD):
            sl = slice(h * HEAD, (h + 1) * HEAD)
            xh = xv[:, sl]
            r = lax.rsqrt(jnp.mean(xh * xh, axis=-1, keepdims=True) + EPS)
            xh = xh * r
            dyh = dyv[:, sl]
            dg = dg + jnp.sum(dyh * xh, axis=0, keepdims=True)
            dxh = dyh * g_ref[...]
            o_ref[0, :, sl] = (r * (dxh - xh * jnp.mean(dxh * xh, axis=-1, keepdims=True))).astype(o_ref.dtype)
        dg_ref[...] += dg
        pv = p_refs[0][...].astype(F32)
        for r in p_refs[1:]:
            pv = pv + r[...].astype(F32)
        o_ref[1] = pv.astype(o_ref.dtype)

    row = pl.BlockSpec((tr, W), lambda i: (i, 0))
    vec = pl.BlockSpec((1, HEAD), lambda i: (0, 0))
    return pl.pallas_call(
        body, name=name, grid=(S // tr,),
        in_specs=[row] * nd + [pl.BlockSpec((None, tr, W), lambda i: (0, i, 0)), vec] + [row] * npass,
        out_specs=[pl.BlockSpec((2, tr, W), lambda i: (0, i, 0)), vec],
        out_shape=[_sds((2, S, W), BF16), _sds((1, HEAD), F32)], compiler_params=_cp("arbitrary"),
    )(*dys, x3, gain, *passes)


def _gate_fwd(o, qg, name):
    S, W = o.shape
    tr = _t(S, 512)

    def body(o_ref, g_ref, y_ref):
        y_ref[...] = (o_ref[...].astype(F32) * jax.nn.sigmoid(g_ref[...].astype(F32))).astype(y_ref.dtype)

    row = pl.BlockSpec((tr, W), lambda i: (i, 0))
    return pl.pallas_call(
        body, name=name, grid=(S // tr,), in_specs=[row, pl.BlockSpec((None, tr, W), lambda i: (1, i, 0))],
        out_specs=row, out_shape=_sds((S, W), BF16), compiler_params=_cp("parallel"),
    )(o, qg)


def _gate_bwd(dog, o, qg, name):
    S, W = o.shape
    H = W // HEAD
    tr = _t(S, 512)

    def body(dy_ref, o_ref, g_ref, do_ref, dg_ref, dl_ref):
        sg = jax.nn.sigmoid(g_ref[...].astype(F32))
        dy = dy_ref[...].astype(F32)
        ov = o_ref[...].astype(F32)
        dob = (dy * sg).astype(do_ref.dtype)
        do_ref[...] = dob
        dg_ref[...] = (dy * ov * (sg * (1.0 - sg))).astype(dg_ref.dtype)
        prod = dob.astype(F32) * ov
        for h in range(H):
            dl_ref[h] = jnp.sum(prod[:, h * HEAD:(h + 1) * HEAD], axis=-1, keepdims=True)

    row = pl.BlockSpec((tr, W), lambda i: (i, 0))
    return pl.pallas_call(
        body, name=name, grid=(S // tr,), in_specs=[row, row, pl.BlockSpec((None, tr, W), lambda i: (1, i, 0))],
        out_specs=[row, row, pl.BlockSpec((H, tr, 1), lambda i: (0, i, 0))],
        out_shape=[_sds((S, W), BF16), _sds((S, W), BF16), _sds((H, S, 1), F32)], compiler_params=_cp("parallel"),
    )(dog, o, qg)


def _logf_cumsum(fpre, bf, name):
    S, C = fpre.shape
    n = S // CHUNK

    def body(f_ref, b_ref, c_ref, carry):
        @pl.when(pl.program_id(0) == 0)
        def _():
            carry[...] = jnp.zeros_like(carry)

        lf = jax.nn.log_sigmoid(f_ref[...] + b_ref[...])
        tri = (lax.broadcasted_iota(jnp.int32, (CHUNK, CHUNK), 0)
               >= lax.broadcasted_iota(jnp.int32, (CHUNK, CHUNK), 1)).astype(F32)
        c_ref[...] = jnp.dot(tri, lf, preferred_element_type=F32, precision=lax.Precision.HIGHEST) + carry[...]
        carry[...] += jnp.sum(lf, axis=0, keepdims=True)

    return pl.pallas_call(
        body, name=name, grid=(n,),
        in_specs=[pl.BlockSpec((CHUNK, C), lambda i: (i, 0)), pl.BlockSpec((1, C), lambda i: (0, 0))],
        out_specs=pl.BlockSpec((CHUNK, C), lambda i: (i, 0)), out_shape=_sds((S, C), F32),
        scratch_shapes=[pltpu.VMEM((1, C), F32)], compiler_params=_cp("arbitrary"),
    )(fpre, bf)


def _logf_cumsum_bwd(dc, fpre, bf, name):
    S, C = fpre.shape
    n = S // CHUNK

    def body(dc_ref, f_ref, b_ref, df_ref, db_ref, carry):
        @pl.when(pl.program_id(0) == 0)
        def _():
            carry[...] = jnp.zeros_like(carry)
            db_ref[...] = jnp.zeros_like(db_ref)

        dcv = dc_ref[...]
        tri = (lax.broadcasted_iota(jnp.int32, (CHUNK, CHUNK), 0)
               <= lax.broadcasted_iota(jnp.int32, (CHUNK, CHUNK), 1)).astype(F32)
        dlf = jnp.dot(tri, dcv, preferred_element_type=F32, precision=lax.Precision.HIGHEST) + carry[...]
        carry[...] += jnp.sum(dcv, axis=0, keepdims=True)
        df = dlf * jax.nn.sigmoid(-(f_ref[...] + b_ref[...]))
        df_ref[...] = df.astype(df_ref.dtype)
        db_ref[...] += jnp.sum(df, axis=0, keepdims=True)

    rev = pl.BlockSpec((CHUNK, C), lambda i: (n - 1 - i, 0))
    vec = pl.BlockSpec((1, C), lambda i: (0, 0))
    return pl.pallas_call(
        body, name=name, grid=(n,), in_specs=[rev, rev, vec], out_specs=[rev, vec],
        out_shape=[_sds((S, C), BF16), _sds((1, C), F32)],
        scratch_shapes=[pltpu.VMEM((1, C), F32)], compiler_params=_cp("arbitrary"),
    )(dc, fpre, bf)


def _attn_tiles(S):
    return _t(S, 1024), _t(S, 512)


def _scores(q, k, ck, off, tq, tk, masked):
    s = lax.dot_general(q, k, (_DIMS["nt"], ((), ())), preferred_element_type=F32) - ck
    if masked:
        d = lax.broadcasted_iota(jnp.int32, (tq, tk), 1) - lax.broadcasted_iota(jnp.int32, (tq, tk), 0)
        s = jnp.where(d <= off, s, NEG)
    return s


def _attn_fwd(qn, kn, kv, crow, name):
    S, W = qn.shape
    H = W // HEAD
    tq, tk = _attn_tiles(S)
    nq, nk = S // tq, S // tk
    last = lambda i: ((i + 1) * tq - 1) // tk
    sub = _t(tq, 256)

    def body(q_ref, k_ref, v_ref, ck_ref, o_ref, lse_ref, m_sc, l_sc, acc_sc):
        qi, kj = pl.program_id(1), pl.program_id(2)

        @pl.when(kj == 0)
        def _():
            m_sc[...] = jnp.full_like(m_sc, NEG)
            l_sc[...] = jnp.zeros_like(l_sc)
            acc_sc[...] = jnp.zeros_like(acc_sc)

        def step(masked):
            k, ck, v = k_ref[...], ck_ref[...], v_ref[...]
            for r0 in range(0, tq, sub):
                rows = slice(r0, r0 + sub)
                s = _scores(q_ref[rows, :], k, ck, qi * tq + r0 - kj * tk, sub, tk, masked)
                m_old = m_sc[rows, :]
                m_new = jnp.maximum(m_old, jnp.max(s, axis=-1, keepdims=True))
                alpha = jnp.exp(m_old - m_new)
                p = jnp.exp(s - m_new)
                l_sc[rows, :] = alpha * l_sc[rows, :] + jnp.sum(p, axis=-1, keepdims=True)
                acc_sc[rows, :] = alpha * acc_sc[rows, :] + jnp.dot(p.astype(BF16), v, preferred_element_type=F32)
                m_sc[rows, :] = m_new

        @pl.when(kj <= last(qi))
        def _():
            step(True)

        @pl.when(kj == nk - 1)
        def _():
            o_ref[...] = (acc_sc[...] / l_sc[...]).astype(o_ref.dtype)
            lse_ref[...] = m_sc[...] + jnp.log(l_sc[...])

    return pl.pallas_call(
        body, name=name, grid=(H, nq, nk),
        in_specs=[pl.BlockSpec((tq, HEAD), lambda h, i, j: (i, h)),
                  pl.BlockSpec((tk, HEAD), lambda h, i, j: (jnp.minimum(j, last(i)), h)),
                  pl.BlockSpec((None, tk, HEAD), lambda h, i, j: (1, jnp.minimum(j, last(i)), h)),
                  pl.BlockSpec((None, 1, tk), lambda h, i, j: (h, 0, jnp.minimum(j, last(i))))],
        out_specs=[pl.BlockSpec((tq, HEAD), lambda h, i, j: (i, h)),
                   pl.BlockSpec((None, tq, 1), lambda h, i, j: (h, i, 0))],
        out_shape=[_sds((S, W), BF16), _sds((H, S, 1), F32)],
        scratch_shapes=[pltpu.VMEM((tq, 1), F32), pltpu.VMEM((tq, 1), F32), pltpu.VMEM((tq, HEAD), F32)],
        compiler_params=_cp("parallel", "parallel", "arbitrary"),
    )(qn, kn, kv, crow)


def _attn_bwd(qn, kn, kv, crow, do, lse, delta, name):
    S, W = qn.shape
    H = W // HEAD
    tq, tk = _attn_tiles(S)
    nq, nk = S // tq, S // tk
    first = lambda j: (j * tk) // tq

    def body(q_ref, k_ref, v_ref, ck_ref, do_ref, lse_ref, dl_ref, dq_ref, dr_ref, dk_ref, dv_ref, dc_ref, dk_sc, dv_sc, dc_sc):
        kj, qi = pl.program_id(1), pl.program_id(2)

        @pl.when(jnp.logical_and(kj == 0, qi == 0))
        def _():
            dq_ref[...] = jnp.zeros_like(dq_ref)
            dr_ref[...] = jnp.zeros_like(dr_ref)

        @pl.when(qi == 0)
        def _():
            dk_sc[...] = jnp.zeros_like(dk_sc)
            dv_sc[...] = jnp.zeros_like(dv_sc)
            dc_sc[...] = jnp.zeros_like(dc_sc)

        def step(masked):
            q, k, dov = q_ref[...], k_ref[...], do_ref[...]
            s = _scores(q, k, ck_ref[...], qi * tq - kj * tk, tq, tk, masked)
            p = jnp.exp(s - lse_ref[...])
            dp = lax.dot_general(dov, v_ref[...], (_DIMS["nt"], ((), ())), preferred_element_type=F32)
            ds = p * (dp - dl_ref[...])
            dsb = ds.astype(BF16)
            dc_sc[...] += jnp.sum(ds, axis=0, keepdims=True)
            dv_sc[...] += lax.dot_general(p.astype(BF16), dov, (_DIMS["tn"], ((), ())), preferred_element_type=F32)
            dk_sc[...] += lax.dot_general(dsb, q, (_DIMS["tn"], ((), ())), preferred_element_type=F32)
            rows = pl.ds(pl.multiple_of(qi * tq, tq), tq)
            dq_ref[rows, :] += jnp.dot(dsb, k, preferred_element_type=F32)
            dr_ref[rows, :] += jnp.sum(ds, axis=1, keepdims=True)

        below = (kj + 1) * tk - 1 <= qi * tq

        @pl.when(below)
        def _():
            step(False)

        @pl.when(jnp.logical_and(qi >= first(kj), jnp.logical_not(below)))
        def _():
            step(True)

        @pl.when(qi == nq - 1)
        def _():
            dk_ref[...] = dk_sc[...]
            dv_ref[...] = dv_sc[...]
            dc_ref[...] = dc_sc[...]

    qblk = pl.BlockSpec((tq, HEAD), lambda h, j, i: (jnp.maximum(i, first(j)), h))
    qcol = pl.BlockSpec((None, tq, 1), lambda h, j, i: (h, jnp.maximum(i, first(j)), 0))
    kblk = pl.BlockSpec((tk, HEAD), lambda h, j, i: (j, h))
    krow = pl.BlockSpec((None, 1, tk), lambda h, j, i: (h, 0, j))
    return pl.pallas_call(
        body, name=name, grid=(H, nk, nq),
        in_specs=[qblk, kblk, pl.BlockSpec((None, tk, HEAD), lambda h, j, i: (1, j, h)), krow, qblk, qcol, qcol],
        out_specs=[pl.BlockSpec((S, HEAD), lambda h, j, i: (0, h)), pl.BlockSpec((None, S, 1), lambda h, j, i: (h, 0, 0)),
                   kblk, kblk, krow],
        out_shape=[_sds((S, W), F32), _sds((H, S, 1), F32), _sds((S, W), F32), _sds((S, W), F32), _sds((H, 1, S), F32)],
        scratch_shapes=[pltpu.VMEM((tk, HEAD), F32), pltpu.VMEM((tk, HEAD), F32), pltpu.VMEM((1, tk), F32)],
        compiler_params=_cp("parallel", "arbitrary", "arbitrary"),
    )(qn, kn, kv, crow, do, lse, delta)


def _adamw_math(w, g, m, v):
    c1 = 1.0 - ADAM_B1 ** ADAM_STEP
    c2 = 1.0 - ADAM_B2 ** ADAM_STEP
    nm = ADAM_B1 * m + (1.0 - ADAM_B1) * g
    nv = ADAM_B2 * v + (1.0 - ADAM_B2) * (g * g)
    return -ADAM_LR * ((nm / c1) / (jnp.sqrt(nv / c2) + ADAM_EPS) + ADAM_WD * w), nm, nv


def _adamw_layer(w, g, m, v, layer, prev, name):
    L, R, C = w.shape
    tr = _t(R, max(8, (1 << 19) // C), 8)

    def body(w_ref, g_ref, m_ref, v_ref, *rest):
        go_ref, d_ref, nm_ref, nv_ref = rest[4:]
        gv = g_ref[...]
        go_ref[...] = gv
        d_ref[...], nm_ref[...], nv_ref[...] = _adamw_math(w_ref[...], gv, m_ref[...], v_ref[...])

    lay = pl.BlockSpec((None, tr, C), lambda i: (layer, i, 0))
    return pl.pallas_call(
        body, name=name, grid=(R // tr,),
        in_specs=[lay, pl.BlockSpec((None, tr, C), lambda i: (0, i, 0)), lay, lay] + [ANY] * 4, out_specs=[lay] * 4,
        out_shape=[_sds((L, R, C), F32)] * 4, input_output_aliases={4: 0, 5: 1, 6: 2, 7: 3},
        compiler_params=_cp("parallel"),
    )(w, g, m, v, *prev)


def _adamw(w, g, m, v, name):
    R, C = w.shape
    tr = _t(R, max(8, (1 << 19) // max(C, 1)), 8)

    def body(w_ref, g_ref, m_ref, v_ref, d_ref, nm_ref, nv_ref):
        d_ref[...], nm_ref[...], nv_ref[...] = _adamw_math(w_ref[...], g_ref[...], m_ref[...], v_ref[...])

    blk = pl.BlockSpec((tr, C), lambda i: (i, 0))
    return pl.pallas_call(
        body, name=name, grid=(R // tr,), in_specs=[blk] * 4, out_specs=[blk] * 3,
        out_shape=[_sds((R, C), F32)] * 3, compiler_params=_cp("parallel"),
    )(w, g, m, v)


def _place():
    x, y, c = lax.axis_index("x"), lax.axis_index("y"), lax.axis_index("c")
    chips = [(1 - x, y), (x, 1 - y), (1 - x, 1 - y)]
    return x, y, c, chips


def _place_part(part, me, dtype, name, layer=None, after=None):
    L, R, C = part.shape
    first, L = (0, L) if layer is None else (layer, 1)
    tr = _t(R, max(16, (1 << 20) // C), 16)

    def body(me_ref, x_ref, *rest):
        o_ref = rest[-1]
        o_ref[...] = x_ref[...].astype(o_ref.dtype)

    extra = _tokens(after)
    return pl.pallas_call(
        body, name=name,
        grid_spec=pltpu.PrefetchScalarGridSpec(
            num_scalar_prefetch=1, grid=(L, R // tr),
            in_specs=[pl.BlockSpec((None, tr, C), lambda l, r, m: (first + l, r, 0))] + [ANY] * len(extra),
            out_specs=pl.BlockSpec((None, None, tr, C), lambda l, r, m: (m[0], l, r, 0))),
        out_shape=_sds((N_CHIPS, L, R, C), dtype), compiler_params=_cp("parallel", "parallel"),
    )(me, part, *extra)


def _gather_chips_async(bufs, collective_id, name):
    n = len(bufs)
    refs = [jax.new_ref(b, memory_space=pltpu.MemorySpace.HBM) for b in bufs]

    @pl.kernel(mesh=plsc.ScalarSubcoreMesh(axis_name="seq", num_cores=1), name=name,
               scratch_types=(pltpu.SemaphoreType.DMA((n, 6)), pltpu.SemaphoreType.DMA((n, 6))),
               compiler_params=pltpu.CompilerParams(collective_id=collective_id))
    def launch(send_sems, recv_sems):
        x, y, c, chips = _place()
        me = 2 * x + y
        sibling = (x, y, 1 - c)
        barrier = pltpu.get_barrier_semaphore()
        for peer in [(*chip, c) for chip in chips] + [sibling]:
            pl.semaphore_signal(barrier, inc=1, device_id=peer, device_id_type=MESH)
        pl.semaphore_wait(barrier, len(chips) + 1)

        def copy(i, k, chip_index, core, to):
            h = refs[i].shape[2] // 2
            rows = refs[i].at[chip_index, :, pl.ds(core * h, h), :]
            return pltpu.make_async_remote_copy(
                src_ref=rows, dst_ref=rows, send_sem=send_sems.at[i, k], recv_sem=recv_sems.at[i, k],
                device_id=to, device_id_type=MESH)

        sent = []
        for i in range(n):
            for j, chip in enumerate(chips):
                cp = copy(i, j, me, c, (*chip, c))
                cp.start()
                sent.append(cp)
        for i in range(n):
            for j, chip in enumerate(chips):
                idx = 2 * chip[0] + chip[1]
                copy(i, j, idx, c, (x, y, c)).wait_recv()
                fw = copy(i, 3 + j, idx, c, sibling)
                fw.start()
                sent.append(fw)
        for i in range(n):
            for j, chip in enumerate(chips):
                copy(i, 3 + j, 2 * chip[0] + chip[1], 1 - c, (x, y, c)).wait_recv()
        for cp in sent:
            cp.wait_send()

    launch()
    return [r[...] for r in refs]


def _gather_chips(bufs, name):
    n = len(bufs)

    def body(*refs):
        outs = refs[n:2 * n]
        send_sems, recv_sems = refs[2 * n:]
        x, y, c, chips = _place()
        me = 2 * x + y
        sibling = (x, y, 1 - c)

        def copy(i, k, chip_index, core, to):
            h = outs[i].shape[2] // 2
            rows = outs[i].at[chip_index, :, pl.ds(core * h, h), :]
            return pltpu.make_async_remote_copy(
                src_ref=rows, dst_ref=rows, send_sem=send_sems.at[i, k], recv_sem=recv_sems.at[i, k],
                device_id=to, device_id_type=MESH)

        sent = []
        for i in range(n):
            for j, chip in enumerate(chips):
                cp = copy(i, j, me, c, (*chip, c))
                cp.start()
                sent.append(cp)
        for i in range(n):
            for j, chip in enumerate(chips):
                idx = 2 * chip[0] + chip[1]
                copy(i, j, idx, c, (x, y, c)).wait_recv()
                fw = copy(i, 3 + j, idx, c, sibling)
                fw.start()
                sent.append(fw)
        for i in range(n):
            for j, chip in enumerate(chips):
                copy(i, 3 + j, 2 * chip[0] + chip[1], 1 - c, (x, y, c)).wait_recv()
        for cp in sent:
            cp.wait_send()

    return pl.pallas_call(
        body, name=name, in_specs=[ANY] * n, out_specs=[ANY] * n,
        out_shape=[_sds(b.shape, b.dtype) for b in bufs],
        scratch_shapes=[pltpu.SemaphoreType.DMA((n, 6)), pltpu.SemaphoreType.DMA((n, 6))],
        input_output_aliases={i: i for i in range(n)},
        compiler_params=pltpu.CompilerParams(has_side_effects=True),
    )(*bufs)


def _swap_halves(gs, name):
    n = len(gs)

    def body(*refs):
        ins, outs = refs[:n], refs[n:2 * n]
        send_sems, recv_sems = refs[2 * n:]
        x, y, c, _ = _place()
        cps = []
        for i in range(n):
            h = ins[i].shape[2] // 2
            cp = pltpu.make_async_remote_copy(
                src_ref=ins[i].at[:, :, pl.ds((1 - c) * h, h), :], dst_ref=outs[i], send_sem=send_sems.at[i],
                recv_sem=recv_sems.at[i], device_id=(x, y, 1 - c), device_id_type=MESH)
            cp.start()
            cps.append(cp)
        for cp in cps:
            cp.wait()

    return pl.pallas_call(
        body, name=name, in_specs=[ANY] * n, out_specs=[ANY] * n,
        out_shape=[_sds(g.shape[:2] + (g.shape[2] // 2, g.shape[3]), g.dtype) for g in gs],
        scratch_shapes=[pltpu.SemaphoreType.DMA((n,)), pltpu.SemaphoreType.DMA((n,))],
        compiler_params=pltpu.CompilerParams(has_side_effects=True),
    )(*gs)


def _pair_sum(g, gs, core, out_dtype, name):
    ns, L, R, C = g.shape
    h = R // 2
    th = _t(h, max(16, (1 << 20) // C), 16)
    nb = h // th

    def body(core_ref, a_ref, b_ref, o_ref):
        o_ref[...] = (a_ref[...].astype(F32) + b_ref[...].astype(F32)).astype(o_ref.dtype)

    blk = (None, None, th, C)
    return pl.pallas_call(
        body, name=name,
        grid_spec=pltpu.PrefetchScalarGridSpec(
            num_scalar_prefetch=1, grid=(ns, L, nb),
            in_specs=[pl.BlockSpec(blk, lambda o, l, r, cr: (o, l, cr[0] * nb + r, 0)),
                      pl.BlockSpec(blk, lambda o, l, r, cr: (o, l, r, 0))],
            out_specs=pl.BlockSpec(blk, lambda o, l, r, cr: (o, l, r, 0))),
        out_shape=_sds((ns, L, h, C), out_dtype), compiler_params=_cp("parallel", "parallel", "parallel"),
    )(core, g, gs)


def _scatter_chips(ps, name):
    n = len(ps)

    def body(*refs):
        ins, outs = refs[:n], refs[n:2 * n]
        send_sems, recv_sems = refs[2 * n:]
        x, y, c, chips = _place()
        sent = []
        for i in range(n):
            for j, chip in enumerate(chips):
                cp = pltpu.make_async_remote_copy(
                    src_ref=ins[i].at[2 * chip[0] + chip[1]], dst_ref=outs[i].at[j], send_sem=send_sems.at[i, j],
                    recv_sem=recv_sems.at[i, j], device_id=(*chip, c), device_id_type=MESH)
                cp.start()
                sent.append(cp)
        for i in range(n):
            for j in range(len(chips)):
                slot = outs[i].at[j]
                pltpu.make_async_remote_copy(
                    src_ref=slot, dst_ref=slot, send_sem=send_sems.at[i, j], recv_sem=recv_sems.at[i, j],
                    device_id=(x, y, c), device_id_type=MESH).wait_recv()
        for cp in sent:
            cp.wait_send()

    return pl.pallas_call(
        body, name=name, in_specs=[ANY] * n, out_specs=[ANY] * n,
        out_shape=[_sds((N_CHIPS - 1,) + p.shape[1:], p.dtype) for p in ps],
        scratch_shapes=[pltpu.SemaphoreType.DMA((n, 3)), pltpu.SemaphoreType.DMA((n, 3))],
        compiler_params=pltpu.CompilerParams(has_side_effects=True),
    )(*ps)


def _scatter_chips_async(ps, collective_id, name):
    n = len(ps)
    srcs = [jax.new_ref(p, memory_space=pltpu.MemorySpace.HBM) for p in ps]
    dsts = [jax.empty_ref(_sds((N_CHIPS - 1,) + p.shape[1:], p.dtype), memory_space=pltpu.MemorySpace.HBM) for p in ps]

    @pl.kernel(mesh=plsc.ScalarSubcoreMesh(axis_name="seq", num_cores=1), name=name,
               scratch_types=(pltpu.SemaphoreType.DMA((n, 3)), pltpu.SemaphoreType.DMA((n, 3))),
               compiler_params=pltpu.CompilerParams(collective_id=collective_id))
    def launch(send_sems, recv_sems):
        x, y, c, chips = _place()
        barrier = pltpu.get_barrier_semaphore()
        for chip in chips:
            pl.semaphore_signal(barrier, inc=1, device_id=(*chip, c), device_id_type=MESH)
        pl.semaphore_wait(barrier, len(chips))
        sent = []
        for i in range(n):
            for j, chip in enumerate(chips):
                cp = pltpu.make_async_remote_copy(
                    src_ref=srcs[i].at[2 * chip[0] + chip[1]], dst_ref=dsts[i].at[j], send_sem=send_sems.at[i, j],
                    recv_sem=recv_sems.at[i, j], device_id=(*chip, c), device_id_type=MESH)
                cp.start()
                sent.append(cp)
        for i in range(n):
            for j in range(len(chips)):
                slot = dsts[i].at[j]
                pltpu.make_async_remote_copy(
                    src_ref=slot, dst_ref=slot, send_sem=send_sems.at[i, j], recv_sem=recv_sems.at[i, j],
                    device_id=(x, y, c), device_id_type=MESH).wait_recv()
        for cp in sent:
            cp.wait_send()

    launch()
    return [d[...] for d in dsts]


def _sum_chips(q, p, me, core, name, after=None):
    _, L, h, C = q.shape
    th = _t(h, max(16, (1 << 19) // C), 16)
    nb = h // th
    blk = (None, None, th, C)

    def body(me_ref, core_ref, p_ref, q0, q1, q2, *rest):
        o_ref = rest[-1]
        o_ref[...] = ((p_ref[...].astype(F32) + q0[...].astype(F32)) + q1[...].astype(F32)) + q2[...].astype(F32)

    extra = _tokens(after)
    return pl.pallas_call(
        body, name=name,
        grid_spec=pltpu.PrefetchScalarGridSpec(
            num_scalar_prefetch=2, grid=(L, nb),
            in_specs=[pl.BlockSpec(blk, lambda l, r, m, c: (m[0], l, r, 0))]
            + [pl.BlockSpec(blk, functools.partial(lambda k, l, r, m, c: (k, l, r, 0), k)) for k in range(N_CHIPS - 1)]
            + [ANY] * len(extra),
            out_specs=pl.BlockSpec((None, th, C), lambda l, r, m, c: (l, c[0] * nb + r, 0))),
        out_shape=_sds((L, 2 * h, C), F32), compiler_params=_cp("parallel", "parallel"),
    )(me, core, p, q, q, q, *extra)


def _sum_small(own, q, name):
    def body(p_ref, q_ref, o_ref):
        o_ref[...] = ((p_ref[...] + q_ref[0]) + q_ref[1]) + q_ref[2]

    return pl.pallas_call(body, name=name, out_shape=_sds(own.shape, F32))(own, q)


def _join_halves(bufs, name):
    n = len(bufs)

    def body(*refs):
        outs = refs[n:2 * n]
        send_sems, recv_sems = refs[2 * n:]
        x, y, c, _ = _place()
        cps = []
        for i in range(n):
            h = outs[i].shape[1] // 2
            mine = outs[i].at[:, pl.ds(c * h, h), :]
            cp = pltpu.make_async_remote_copy(
                src_ref=mine, dst_ref=mine, send_sem=send_sems.at[i], recv_sem=recv_sems.at[i],
                device_id=(x, y, 1 - c), device_id_type=MESH)
            cp.start()
            cps.append(cp)
        for cp in cps:
            cp.wait()

    return pl.pallas_call(
        body, name=name, in_specs=[ANY] * n, out_specs=[ANY] * n,
        out_shape=[_sds(b.shape, b.dtype) for b in bufs],
        scratch_shapes=[pltpu.SemaphoreType.DMA((n,)), pltpu.SemaphoreType.DMA((n,))],
        input_output_aliases={i: i for i in range(n)},
        compiler_params=pltpu.CompilerParams(has_side_effects=True),
    )(*bufs)


def _pack(arrs, rows_mult):
    flat = jnp.concatenate([a.reshape(-1).astype(F32) for a in arrs])
    rows = -(-flat.size // LANE)
    rows = -(-rows // rows_mult) * rows_mult
    return jnp.pad(flat, (0, rows * LANE - flat.size)).reshape(rows, LANE)


def _unpack(packed, like):
    flat = packed.reshape(-1)
    out, pos = [], 0
    for a in like:
        n = math.prod(a.shape)
        out.append(flat[pos:pos + n].reshape(a.shape))
        pos += n
    return out


def _adamw_nd(w, g, m, v, name):
    shape = w.shape
    C = shape[-1]
    d, nm, nv = _adamw(w.reshape(-1, C), g.reshape(-1, C), m.reshape(-1, C), v.reshape(-1, C), name)
    return d.reshape(shape), nm.reshape(shape), nv.reshape(shape)


def kernel(x, a_norm, a_w_in, a_v_norm, a_w_s, a_b_s, a_w_out, kv_norm, w_kvf, b_f, k_norm, b_norm, b_w_qg, q_norm, b_w_out, f_norm, f_w_up, f_conv_w, f_conv_b, f_w_down, final_norm, loss_target, m_a_norm, m_a_w_in, m_a_v_norm, m_a_w_s, m_a_b_s, m_a_w_out, m_kv_norm, m_w_kvf, m_b_f, m_k_norm, m_b_norm, m_b_w_qg, m_q_norm, m_b_w_out, m_f_norm, m_f_w_up, m_f_conv_w, m_f_conv_b, m_f_w_down, m_final_norm, v_a_norm, v_a_w_in, v_a_v_norm, v_a_w_s, v_a_b_s, v_a_w_out, v_kv_norm, v_w_kvf, v_b_f, v_k_norm, v_b_norm, v_b_w_qg, v_q_norm, v_b_w_out, v_f_norm, v_f_w_up, v_f_conv_w, v_f_conv_b, v_f_w_down, v_final_norm):
    weights = dict(a_norm=a_norm, a_w_in=a_w_in, a_v_norm=a_v_norm, a_w_s=a_w_s, a_b_s=a_b_s, a_w_out=a_w_out, kv_norm=kv_norm, w_kvf=w_kvf, b_f=b_f, k_norm=k_norm, b_norm=b_norm, b_w_qg=b_w_qg, q_norm=q_norm, b_w_out=b_w_out, f_norm=f_norm, f_w_up=f_w_up, f_conv_w=f_conv_w, f_conv_b=f_conv_b, f_w_down=f_w_down, final_norm=final_norm)
    mom1 = dict(a_norm=m_a_norm, a_w_in=m_a_w_in, a_v_norm=m_a_v_norm, a_w_s=m_a_w_s, a_b_s=m_a_b_s, a_w_out=m_a_w_out, kv_norm=m_kv_norm, w_kvf=m_w_kvf, b_f=m_b_f, k_norm=m_k_norm, b_norm=m_b_norm, b_w_qg=m_b_w_qg, q_norm=m_q_norm, b_w_out=m_b_w_out, f_norm=m_f_norm, f_w_up=m_f_w_up, f_conv_w=m_f_conv_w, f_conv_b=m_f_conv_b, f_w_down=m_f_w_down, final_norm=m_final_norm)
    mom2 = dict(a_norm=v_a_norm, a_w_in=v_a_w_in, a_v_norm=v_a_v_norm, a_w_s=v_a_w_s, a_b_s=v_a_b_s, a_w_out=v_a_w_out, kv_norm=v_kv_norm, w_kvf=v_w_kvf, b_f=v_b_f, k_norm=v_k_norm, b_norm=v_b_norm, b_w_qg=v_b_w_qg, q_norm=v_q_norm, b_w_out=v_b_w_out, f_norm=v_f_norm, f_w_up=v_f_w_up, f_conv_w=v_f_conv_w, f_conv_b=v_f_conv_b, f_w_down=v_f_w_down, final_norm=v_final_norm)
    names = list(weights)
    big = ["a_w_in", "a_w_out", "w_kvf", "b_w_qg", "b_w_out", "f_w_up", "f_w_down"]
    small_sharded = ["a_norm", "a_v_norm", "f_conv_w"]
    small_repl = ["a_w_s", "a_b_s", "kv_norm", "b_f", "k_norm", "b_norm", "q_norm", "f_norm", "f_conv_b", "final_norm"]

    _, S, D = x.shape
    NA, NB, DEPTH = a_norm.shape[0], b_norm.shape[0], f_norm.shape[0]
    W = a_w_out.shape[1] * N_CHIPS
    G = a_w_s.shape[1]
    H = b_f.shape[0]
    ATT = H * HEAD
    F = f_w_down.shape[1] * N_CHIPS
    Ckv = w_kvf.shape[1]
    Cp = -(-Ckv // LANE) * LANE
    assert W == G * CHUNK and Ckv * N_CHIPS == 2 * ATT + H and S % CHUNK == 0
    core = lax.axis_index("c").astype(jnp.int32).reshape(1)
    me = (2 * lax.axis_index("x") + lax.axis_index("y")).astype(jnp.int32).reshape(1)

    small_local = [weights[k] for k in small_sharded]
    place = lambda p, nm, dt=BF16, layer=None, after=None: _place_part(p, me, dt, "place_" + nm, layer, after)
    g_ain, g_aout, g_up, g_down, g_attn = [None] * NA, [None] * NA, [None] * DEPTH, [None] * DEPTH, {}
    groups = [("ffn", 0)] + [("mix", l) for l in range(1, NA)] + [("attn", None)] + [("ffn", l) for l in range(NA, DEPTH)]

    def launch_next(after=None):
        if not groups:
            return
        kind, l = groups.pop(0)
        cid = DEPTH + 1 - len(groups)
        if kind == "attn":
            bufs = [place(jnp.pad(w_kvf, ((0, 0), (0, Cp - Ckv)))[None], "w_kvf", after=after),
                    place(b_w_qg, "b_w_qg", after=after), place(b_w_out, "b_w_out", after=after)]
            g_attn["kvf"], g_attn["qg"], g_attn["out"] = _gather_chips_async(bufs, cid, f"gather_w{cid}")
            return
        bufs = [place(f_w_up, f"f_w_up{l}", layer=l, after=after), place(f_w_down, f"f_w_down{l}", layer=l, after=after)]
        if kind == "mix":
            bufs += [place(a_w_in, f"a_w_in{l}", layer=l, after=after), place(a_w_out, f"a_w_out{l}", layer=l, after=after)]
            g_up[l], g_down[l], g_ain[l], g_aout[l] = _gather_chips_async(bufs, cid, f"gather_w{cid}")
        else:
            g_up[l], g_down[l] = _gather_chips_async(bufs, cid, f"gather_w{cid}")

    g_small, g_ain[0], g_aout[0] = _gather_chips_async(
        [place(_pack(small_local, 32)[None], "small", F32), place(a_w_in, "a_w_in0", layer=0), place(a_w_out, "a_w_out0", layer=0)],
        0, "gather_w0")
    launch_next()

    per_chip = [_unpack(g_small[j, 0], small_local) for j in range(N_CHIPS)]
    a_norm_f, a_vnorm_f, conv_w_f = [jnp.concatenate([per_chip[j][k] for j in range(N_CHIPS)], axis=-1) for k in range(3)]
    cw = conv_w_f.reshape(DEPTH, 3, 2, F).transpose(0, 2, 1, 3)
    cb = f_conv_b.reshape(DEPTH, 2, 1, F)
    tril = jnp.tril(jnp.ones((CHUNK, CHUNK), dtype=bool))
    wm = jnp.where(tril, a_w_s, 0.0).astype(BF16)
    wmt = jnp.swapaxes(wm, -1, -2)
    bcol = a_b_s[..., None]
    bf_pad = jnp.pad(b_f, (0, LANE - H))[None]
    row = lambda v: v.reshape(1, -1)

    h = x[0]
    target = loss_target[0]
    saved = [dict() for _ in range(DEPTH)]
    kvs = {}
    for l in range(DEPTH):
        sv = saved[l]
        sv["h_m"] = h
        if l < NA:
            xn = _rms_fwd(h, row(a_norm_f[l]), f"a{l}_norm")
            zpre = _mm(m2(xn), mgc(g_ain[l], 0), "nn", m2(_sds((S, 2 * W), BF16)), tm=1024, tn=512, name=f"a{l}_in")
            if l + 1 < NA:
                launch_next(zpre)
            gated = _sgu_fwd(zpre, wm[l], bcol[l], row(a_vnorm_f[l]), f"a{l}_sgu")
            h = _mm(m2(gated), mgr(g_aout[l], 0), "nn", m2(_sds((S, D), F32)), res=m2(h), tm=1024, tn=1024, name=f"a{l}_out")
            sv.update(xn_m=xn, zpre=zpre, gated=gated)
        else:
            j = l - NA
            xn = _rms_fwd(h, row(b_norm[j]), f"b{j}_norm")
            qg = _mm(m2(xn), mgc(g_attn["qg"], j), "nn", mcs(_sds((2, S, ATT), BF16)), tm=1024, tn=512, name=f"b{j}_qg")
            qn = _headnorm_fwd(qg, 0, row(q_norm[j]) * QK_SCALE, f"b{j}_qnorm")
            o, lse = _attn_fwd(qn, kvs["kn"], kvs["kv"], kvs["crow"], f"b{j}_attn")
            og = _gate_fwd(o, qg, f"b{j}_gate")
            h = _mm(m2(og), mgr(g_attn["out"], j), "nn", m2(_sds((S, D), F32)), res=m2(h), tm=1024, tn=1024, name=f"b{j}_out")
            sv.update(xn_m=xn, qg=qg, qn=qn, o=o, lse=lse, og=og)
        sv["h_f"] = h
        xn = _rms_fwd(h, row(f_norm[l]), f"f{l}_norm")
        hup = _mm(m2(xn), mgc(g_up[l], 0), "nn", mcs(_sds((2, S, F), BF16)), tm=1024, tn=1408, name=f"f{l}_up")
        launch_next(hup)
        act = _convgate_fwd(hup, cw[l], cb[l], f"f{l}_conv")
        h = _mm(m2(act), mgr(g_down[l], 0), "nn", m2(_sds((S, D), F32)), res=m2(h), tm=1024, tn=1024, tk=1408, name=f"f{l}_down")
        sv.update(xn_f=xn, hup=hup, act=act)
        if l == NA - 1:
            w_kvf_g = mgc(g_attn["kvf"], 0)
            xn_kv = _rms_fwd(h, row(kv_norm), "kv_norm")
            kvp = _mm(m2(xn_kv), w_kvf_g, "nn", m2(_sds((S, N_CHIPS * Cp), F32)), tm=1024, tn=Cp, name="kv_proj")
            launch_next(kvp)
            kvflat = jnp.concatenate([kvp[:, j * Cp:j * Cp + Ckv] for j in range(N_CHIPS)], axis=1)
            kv = jnp.stack([kvflat[:, :ATT], kvflat[:, ATT:2 * ATT]]).astype(BF16)
            fpre = jnp.pad(kvflat[:, 2 * ATT:], ((0, 0), (0, LANE - H)))
            kn = _headnorm_fwd(kv, 0, row(k_norm), "kv_knorm")
            cums = _logf_cumsum(fpre, bf_pad, "kv_cumsum")
            cT = cums[:, :H].T
            kvs.update(h=h, xn=xn_kv, kv=kv, fpre=fpre, kn=kn, crow=cT[:, None, :])

    loss11, dh, dhb, d_final = _final_loss(h, row(final_norm), target, "final_loss")
    loss = lax.psum(loss11[0, 0], ("x", "y", "c"))

    stacked = {k: tuple(lax.empty(weights[k].shape, F32) for _ in range(4))
               for k in ("a_w_in", "a_w_out", "b_w_qg", "b_w_out", "f_w_up", "f_w_down")}
    flat = {}
    pending = []
    next_id = [DEPTH + 2]

    def grad_buf(k):
        return _sds((N_CHIPS, 1) + weights[k].shape[1:], BF16)

    def start_chunk(entries, tag):
        Gs = [e[2] for e in entries]
        others = _swap_halves(Gs, f"grads_swap_{tag}")
        partial = [_pair_sum(g, o, core, g.dtype, f"grads_pair_{tag}{i}") for i, (g, o) in enumerate(zip(Gs, others))]
        by_chip = _scatter_chips_async(partial, next_id[0], f"grads_scatter_{tag}")
        next_id[0] += 1
        pending.append((entries, partial, by_chip, tag))
        return partial

    def finish_chunk(after):
        entries, partial, by_chip, tag = pending.pop(0)
        halves = []
        for i, ((k, l, _), p, q) in enumerate(zip(entries, partial, by_chip)):
            if k == "small":
                half = _sum_small(lax.dynamic_index_in_dim(p, me[0], 0, keepdims=False), q, "grads_sum_small")
                hs = half.shape[1]
                halves.append(lax.dynamic_update_slice(jnp.zeros((1, 2 * hs, LANE), F32), half, (0, core[0] * hs, 0)))
            else:
                halves.append(_sum_chips(q, p, me, core, f"grads_sum_{tag}{i}", after=after))
        for (k, l, _), f in zip(entries, _join_halves(halves, f"grads_join_{tag}")):
            if k in stacked:
                stacked[k] = tuple(_adamw_layer(weights[k], f, mom1[k], mom2[k], l, stacked[k], f"adamw_{k}{l}"))
            else:
                flat[k] = f

    d_anorm, d_avnorm, d_ws, d_bs = [None] * NA, [None] * NA, [None] * NA, [None] * NA
    d_bnorm, d_qnorm = [None] * NB, [None] * NB
    d_fnorm, d_cw, d_cb = [None] * DEPTH, [None] * DEPTH, [None] * DEPTH
    dkn, dvv, dck = [], [], []
    G_kvf = d_kvnorm = d_bf = d_knorm = None
    tok = None
    for l in reversed(range(DEPTH)):
        sv = saved[l]
        chunk = []
        if l == NA - 1:
            dkv, d_knorm = _headnorm_bwd(dkn, kvs["kv"], row(k_norm), dvv, "kv_knorm_bwd")
            dc = sum(dck)
            dc = jnp.pad(dc, ((0, 0), (0, LANE - H)))
            df, d_bf = _logf_cumsum_bwd(dc, kvs["fpre"], bf_pad, "kv_cumsum_bwd")
            dflat = jnp.concatenate([dkv[0], dkv[1], df[:, :H]], axis=1)
            dpad = jnp.concatenate([jnp.pad(dflat[:, j * Ckv:(j + 1) * Ckv], ((0, 0), (0, Cp - Ckv))) for j in range(N_CHIPS)], axis=1)
            dxn = _mm(m2(dpad), w_kvf_g, "nt", m2(_sds((S, D), BF16)), after=tok, tm=1024, tn=2048, tk=Cp, name="kv_proj_dx")
            G_kvf = _mm(m2(kvs["xn"]), m2(dpad), "tn", mgc(_sds((N_CHIPS, 1, D, Cp), BF16), 0), tm=1024, tn=Cp, tk=2048, name="kv_proj_dw")
            dh, dhb, d_kvnorm = _rms_bwd(dxn, kvs["h"], row(kv_norm), dh, "kv_norm_bwd")
            chunk.append(("w_kvf", 0, G_kvf))
        dact = _mm(m2(dhb), mgr(g_down[l], 0), "nt", m2(_sds((S, F), BF16)), after=tok, tm=1024, tn=1408, tk=2048, name=f"f{l}_down_dx")
        G_down = _mm(m2(sv["act"]), m2(dhb), "tn", mgr(grad_buf("f_w_down"), 0), tm=1408, tn=1024, tk=2048, name=f"f{l}_down_dw")
        dhup, d_cw[l], d_cb[l] = _convgate_bwd(sv["hup"], dact, cw[l], cb[l], f"f{l}_conv_bwd")
        dxn = _mm(mcs(dhup), mgc(g_up[l], 0), "nt", m2(_sds((S, D), BF16)), tm=1024, tn=2048, tk=1408, name=f"f{l}_up_dx")
        G_up = _mm(m2(sv["xn_f"]), mcs(dhup), "tn", mgc(grad_buf("f_w_up"), 0), tm=1024, tn=1408, tk=2048, name=f"f{l}_up_dw")
        dh, dhb, d_fnorm[l] = _rms_bwd(dxn, sv["h_f"], row(f_norm[l]), dh, f"f{l}_norm_bwd")
        chunk += [("f_w_up", l, G_up), ("f_w_down", l, G_down)]
        if l == 0:
            tok = start_chunk(chunk, "f0")
            finish_chunk(tok)
            chunk = []
        if l >= NA:
            j = l - NA
            dog = _mm(m2(dhb), mgr(g_attn["out"], j), "nt", m2(_sds((S, ATT), BF16)), tm=1024, tn=512, tk=2048, name=f"b{j}_out_dx")
            G_bout = _mm(m2(sv["og"]), m2(dhb), "tn", mgr(grad_buf("b_w_out"), 0), tm=512, tn=1024, tk=4096, name=f"b{j}_out_dw")
            do, dgate, delta = _gate_bwd(dog, sv["o"], sv["qg"], f"b{j}_gate_bwd")
            dqn, drow, dkn_j, dv_j, dcol = _attn_bwd(sv["qn"], kvs["kn"], kvs["kv"], kvs["crow"], do, sv["lse"], delta, f"b{j}_attn_bwd")
            dkn.append(dkn_j); dvv.append(dv_j); dck.append(drow[:, :, 0].T - dcol[:, 0, :].T)
            dqg, d_qs = _headnorm_bwd([dqn], sv["qg"], row(q_norm[j]) * QK_SCALE, [dgate], f"b{j}_qnorm_bwd")
            d_qnorm[j] = d_qs * QK_SCALE
            dxn = _mm(mcs(dqg), mgc(g_attn["qg"], j), "nt", m2(_sds((S, D), BF16)), tm=1024, tn=2048, tk=1024, name=f"b{j}_qg_dx")
            G_bqg = _mm(m2(sv["xn_m"]), mcs(dqg), "tn", mgc(grad_buf("b_w_qg"), 0), tm=1024, tn=1024, tk=2048, name=f"b{j}_qg_dw")
            dh, dhb, d_bnorm[j] = _rms_bwd(dxn, sv["h_m"], row(b_norm[j]), dh, f"b{j}_norm_bwd")
            chunk += [("b_w_qg", j, G_bqg), ("b_w_out", j, G_bout)]
        else:
            dgated = _mm(m2(dhb), mgr(g_aout[l], 0), "nt", m2(_sds((S, W), BF16)), after=tok, tm=1024, tn=512, tk=2048, name=f"a{l}_out_dx")
            G_aout = _mm(m2(sv["gated"]), m2(dhb), "tn", mgr(grad_buf("a_w_out"), 0), tm=512, tn=1024, tk=4096, name=f"a{l}_out_dw")
            dz, d_ws[l], d_bs[l], d_avnorm[l] = _sgu_bwd(sv["zpre"], dgated, wm[l], wmt[l], bcol[l], row(a_vnorm_f[l]), f"a{l}_sgu_bwd")
            dxn = _mm(m2(dz), mgc(g_ain[l], 0), "nt", m2(_sds((S, D), BF16)), tm=1024, tn=2048, tk=1024, name=f"a{l}_in_dx")
            G_ain = _mm(m2(sv["xn_m"]), m2(dz), "tn", mgc(grad_buf("a_w_in"), 0), tm=1024, tn=1024, tk=2048, name=f"a{l}_in_dw")
            dh, dhb, d_anorm[l] = _rms_bwd(dxn, sv["h_m"], row(a_norm_f[l]), dh, f"a{l}_norm_bwd")
            chunk += [("a_w_in", l, G_ain), ("a_w_out", l, G_aout)]
        if l > 0:
            tok = start_chunk(chunk, f"l{l}")
            if len(pending) > 1:
                finish_chunk(tok)
    grad_x = dh[None]

    full = dict(
        a_norm=jnp.concatenate(d_anorm, axis=0), a_v_norm=jnp.concatenate(d_avnorm, axis=0),
        f_conv_w=jnp.stack(d_cw).transpose(0, 2, 1, 3).reshape(DEPTH, 3, 2 * F),
        a_w_s=jnp.where(tril, jnp.stack(d_ws), 0.0), a_b_s=jnp.stack(d_bs)[..., 0],
        kv_norm=d_kvnorm[0], b_f=d_bf[0, :H], k_norm=d_knorm[0], b_norm=jnp.concatenate(d_bnorm, axis=0),
        q_norm=jnp.concatenate(d_qnorm, axis=0), f_norm=jnp.concatenate(d_fnorm, axis=0),
        f_conv_b=jnp.stack(d_cb).reshape(DEPTH, 2 * F), final_norm=d_final[0])
    shard_rows = []
    for j in range(N_CHIPS):
        pieces = []
        for k in small_sharded:
            n = weights[k].shape[-1]
            pieces.append(full[k][..., j * n:(j + 1) * n])
        shard_rows.append(_pack(pieces, 32))
    rs = shard_rows[0].shape[0]
    repl = _pack([full[k] for k in small_repl], N_CHIPS * 32)
    rr = repl.shape[0] // N_CHIPS
    G_small = jnp.concatenate([jnp.stack(shard_rows), repl.reshape(N_CHIPS, rr, LANE)], axis=1)[:, None]

    tok = start_chunk(chunk + [("small", 0, G_small)], "l0")
    finish_chunk(tok)
    finish_chunk(None)
    F_small = flat["small"]
    repl_buf = lax.dynamic_update_slice(jnp.zeros((N_CHIPS, 1, rr, LANE), F32), F_small[None, :, rs:, :], (me[0], 0, 0, 0))
    (repl_all,) = _gather_chips([repl_buf], "gather_small_grads")

    grads, delta, new_m, new_v = ({k: t[i] for k, t in stacked.items()} for i in range(4))
    grads["w_kvf"] = flat["w_kvf"][0]
    for k, gk in zip(small_sharded, _unpack(F_small[0, :rs], small_local)):
        grads[k] = gk
    for k, gk in zip(small_repl, _unpack(repl_all.reshape(N_CHIPS * rr, LANE), [weights[k] for k in small_repl])):
        grads[k] = gk

    widen = lambda a: jnp.pad(a, ((0, 0), (0, Cp - Ckv)))
    wide = _adamw(widen(w_kvf), grads["w_kvf"], widen(m_w_kvf), widen(v_w_kvf), "adamw_w_kvf")
    delta["w_kvf"], new_m["w_kvf"], new_v["w_kvf"] = (a[:, :Ckv] for a in wide)
    grads["w_kvf"] = grads["w_kvf"][:, :Ckv]
    small = small_sharded + small_repl
    packed = [_pack([t[k] for k in small], 8) for t in (weights, grads, mom1, mom2)]
    outs = _adamw(*packed, "adamw_small")
    like = [weights[k] for k in small]
    for t, o in zip((delta, new_m, new_v), outs):
        for k, a in zip(small, _unpack(o, like)):
            t[k] = a

    return (loss, grad_x, *[grads[k] for k in names], *[delta[k] for k in names],
            *[new_m[k] for k in names], *[new_v[k] for k in names])
```

```python
import functools
import math

import jax
import jax.numpy as jnp
from jax import lax
from jax.experimental import pallas as pl
from jax.experimental.pallas import tpu as pltpu
from jax.experimental.pallas import tpu_sc as plsc

F32, BF16 = jnp.float32, jnp.bfloat16
EPS = 1e-6
CHUNK = 128
HEAD = 128
LANE = 128
HALO = 16
N_CHIPS = 4
VMEM_LIMIT = 48 * 1024 * 1024
MESH = pl.DeviceIdType.MESH
ANY = pl.BlockSpec(memory_space=pl.ANY)

ADAM_LR, ADAM_B1, ADAM_B2, ADAM_EPS, ADAM_WD, ADAM_STEP = 0.001, 0.9, 0.999, 1e-08, 0.01, 10
NEG = -1e30
QK_SCALE = HEAD ** -0.5


def _cp(*sem):
    return pltpu.CompilerParams(dimension_semantics=sem, vmem_limit_bytes=VMEM_LIMIT)


def _t(dim, pref, mult=LANE):
    if dim <= pref:
        return dim
    t = (pref // mult) * mult
    while t >= mult:
        if dim % t == 0:
            return t
        t -= mult
    return dim


def _sds(shape, dtype):
    return jax.ShapeDtypeStruct(tuple(shape), dtype)


class Mat:
    def __init__(self, arr, shape, rlim, clim, block, index):
        self.arr, self.shape, self.rlim, self.clim = arr, shape, rlim, clim
        self._block, self._index = block, index

    def spec(self, tr, tc, gmap):
        assert self.rlim % tr == 0 and self.clim % tc == 0, (self.shape, self.rlim, self.clim, tr, tc)
        index = self._index(tr, tc)
        return pl.BlockSpec(self._block(tr, tc), lambda *g: index(*gmap(*g)))


def m2(arr):
    R, C = arr.shape
    return Mat(arr, (R, C), R, C, lambda tr, tc: (tr, tc), lambda tr, tc: (lambda i, j: (i, j)))


def mcs(arr):
    ns, R, Cs = arr.shape
    return Mat(arr, (R, ns * Cs), R, Cs, lambda tr, tc: (None, tr, tc),
               lambda tr, tc: (lambda i, j: (j // (Cs // tc), i, j % (Cs // tc))))


def mhalf(arr, p):
    ns, R, Cs = arr.shape
    return Mat(arr, (R, Cs), R, Cs, lambda tr, tc: (None, tr, tc), lambda tr, tc: (lambda i, j: (p, i, j)))


def mgc(arr, l):
    ns, L, R, Cs = arr.shape
    return Mat(arr, (R, ns * Cs), R, Cs, lambda tr, tc: (None, None, tr, tc),
               lambda tr, tc: (lambda i, j: (j // (Cs // tc), l, i, j % (Cs // tc))))


def mgr(arr, l):
    ns, L, Rs, C = arr.shape
    return Mat(arr, (ns * Rs, C), Rs, C, lambda tr, tc: (None, None, tr, tc),
               lambda tr, tc: (lambda i, j: (i // (Rs // tr), l, i % (Rs // tr), j)))


_DIMS = {"nn": ((1,), (0,)), "nt": ((1,), (1,)), "tn": ((0,), (0,))}


def _tokens(after):
    if after is None:
        return []
    return list(after) if isinstance(after, (list, tuple)) else [after]


def _mm(a, b, mode, out, *, res=None, alias=False, after=None, tm=512, tn=512, tk=4096, name):
    if mode == "tn":
        (K, M), (K2, N) = a.shape, b.shape
        alim_m, alim_k, blim_k, blim_n = a.clim, a.rlim, b.rlim, b.clim
    elif mode == "nt":
        (M, K), (N, K2) = a.shape, b.shape
        alim_m, alim_k, blim_k, blim_n = a.rlim, a.clim, b.clim, b.rlim
    else:
        (M, K), (K2, N) = a.shape, b.shape
        alim_m, alim_k, blim_k, blim_n = a.rlim, a.clim, b.rlim, b.clim
    assert K == K2 and out.shape == (M, N), (name, a.shape, b.shape, out.shape)
    tm = _t(math.gcd(alim_m, out.rlim), tm)
    tn = _t(math.gcd(blim_n, out.clim), tn)
    tk = _t(math.gcd(alim_k, blim_k), tk)
    grid = (M // tm, N // tn, K // tk)
    nk = grid[2]
    if mode == "tn":
        a_spec = a.spec(tk, tm, lambda i, j, k: (k, i))
    else:
        a_spec = a.spec(tm, tk, lambda i, j, k: (i, k))
    if mode == "nt":
        b_spec = b.spec(tn, tk, lambda i, j, k: (j, k))
    else:
        b_spec = b.spec(tk, tn, lambda i, j, k: (k, j))
    o_spec = out.spec(tm, tn, lambda i, j, k: (i, j))
    operands, in_specs = [a.arr, b.arr], [a_spec, b_spec]
    if res is not None:
        operands.append(res.arr)
        in_specs.append(res.spec(tm, tn, lambda i, j, k: (i, j)))
    aliases = {}
    if alias:
        aliases = {len(operands): 0}
        operands.append(out.arr)
        in_specs.append(ANY)
    tokens = _tokens(after)
    operands += tokens
    in_specs += [ANY] * len(tokens)
    dims = (_DIMS[mode], ((), ()))
    has_res = res is not None

    def body(*refs):
        a_ref, b_ref = refs[0], refs[1]
        res_ref = refs[2] if has_res else None
        n_in = 2 + has_res + alias + len(tokens)
        o_ref = refs[n_in]
        p = lax.dot_general(a_ref[...].astype(BF16), b_ref[...].astype(BF16), dims, preferred_element_type=F32)

        def finish(v):
            if has_res:
                v = v + res_ref[...].astype(F32)
            o_ref[...] = v.astype(o_ref.dtype)

        if nk == 1:
            finish(p)
        else:
            acc = refs[n_in + 1]
            k = pl.program_id(2)

            @pl.when(k == 0)
            def _():
                acc[...] = p

            @pl.when(k > 0)
            def _():
                acc[...] += p

            @pl.when(k == nk - 1)
            def _():
                finish(acc[...])

    return pl.pallas_call(
        body, name=name, grid=grid, in_specs=in_specs, out_specs=o_spec,
        out_shape=_sds(out.arr.shape, out.arr.dtype),
        scratch_shapes=[pltpu.VMEM((tm, tn), F32)] if nk > 1 else [],
        input_output_aliases=aliases,
        compiler_params=_cp("parallel", "parallel", "arbitrary"),
    )(*operands)


def _rms_fwd(x, g, name):
    S, D = x.shape
    tr = _t(S, 512)

    def body(x_ref, g_ref, o_ref):
        xv = x_ref[...]
        r = lax.rsqrt(jnp.mean(xv * xv, axis=-1, keepdims=True) + EPS)
        o_ref[...] = ((xv * r) * g_ref[...]).astype(o_ref.dtype)

    return pl.pallas_call(
        body, name=name, grid=(S // tr,),
        in_specs=[pl.BlockSpec((tr, D), lambda i: (i, 0)), pl.BlockSpec((1, D), lambda i: (0, 0))],
        out_specs=pl.BlockSpec((tr, D), lambda i: (i, 0)), out_shape=_sds((S, D), BF16),
        compiler_params=_cp("parallel"),
    )(x, g)


def _rms_bwd(dy, x, g, dres, name):
    S, D = x.shape
    tr = _t(S, 512)

    def body(dy_ref, x_ref, g_ref, dres_ref, dx_ref, dxb_ref, dg_ref):
        xv = x_ref[...]
        dyv = dy_ref[...].astype(F32)
        r = lax.rsqrt(jnp.mean(xv * xv, axis=-1, keepdims=True) + EPS)
        xh = xv * r
        dxh = dyv * g_ref[...]
        m = jnp.mean(dxh * xh, axis=-1, keepdims=True)
        dx = dres_ref[...] + r * (dxh - xh * m)
        dx_ref[...] = dx
        dxb_ref[...] = dx.astype(dxb_ref.dtype)

        @pl.when(pl.program_id(0) == 0)
        def _():
            dg_ref[...] = jnp.zeros_like(dg_ref)

        dg_ref[...] += jnp.sum(dyv * xh, axis=0, keepdims=True)

    row = pl.BlockSpec((tr, D), lambda i: (i, 0))
    vec = pl.BlockSpec((1, D), lambda i: (0, 0))
    return pl.pallas_call(
        body, name=name, grid=(S // tr,), in_specs=[row, row, vec, row], out_specs=[row, row, vec],
        out_shape=[_sds((S, D), F32), _sds((S, D), BF16), _sds((1, D), F32)], compiler_params=_cp("arbitrary"),
    )(dy, x, g, dres)


def _final_loss(h, g, target, name):
    S, D = h.shape
    tr = _t(S, 512)

    def body(x_ref, g_ref, t_ref, loss_ref, dx_ref, dxb_ref, dg_ref):
        xv = x_ref[...]
        r = lax.rsqrt(jnp.mean(xv * xv, axis=-1, keepdims=True) + EPS)
        xh = xv * r
        err = xh * g_ref[...] - t_ref[...]
        part = 0.5 * jnp.sum(jnp.mean(err * err, axis=-1, keepdims=True), axis=0, keepdims=True)
        dyv = err * (1.0 / D)
        dxh = dyv * g_ref[...]
        m = jnp.mean(dxh * xh, axis=-1, keepdims=True)
        dx = r * (dxh - xh * m)
        dx_ref[...] = dx
        dxb_ref[...] = dx.astype(dxb_ref.dtype)

        @pl.when(pl.program_id(0) == 0)
        def _():
            dg_ref[...] = jnp.zeros_like(dg_ref)
            loss_ref[...] = jnp.zeros_like(loss_ref)

        dg_ref[...] += jnp.sum(dyv * xh, axis=0, keepdims=True)
        loss_ref[...] += part

    row = pl.BlockSpec((tr, D), lambda i: (i, 0))
    vec = pl.BlockSpec((1, D), lambda i: (0, 0))
    one = pl.BlockSpec((1, 1), lambda i: (0, 0))
    return pl.pallas_call(
        body, name=name, grid=(S // tr,), in_specs=[row, vec, row], out_specs=[one, row, row, vec],
        out_shape=[_sds((1, 1), F32), _sds((S, D), F32), _sds((S, D), BF16), _sds((1, D), F32)], compiler_params=_cp("arbitrary"),
    )(h, g, target)


_RSQRT2 = 0.7071067811865476
_RSQRT2PI = 0.3989422804014327


def _gelu(x):
    return 0.5 * x * (1.0 + lax.erf(x * _RSQRT2))


def _gelu_grad(x):
    return 0.5 * (1.0 + lax.erf(x * _RSQRT2)) + x * (jnp.exp(-0.5 * x * x) * _RSQRT2PI)


def _sgu_fwd(zpre, wm, bcol, vnorm, name):
    S, W2 = zpre.shape
    W = W2 // 2
    G = W // CHUNK

    def body(z_ref, wm_ref, b_ref, vn_ref, o_ref):
        zp = z_ref[...].astype(F32)
        u = _gelu(zp[:, :W])
        v = _gelu(zp[:, W:])
        rv = lax.rsqrt(jnp.mean(v * v, axis=-1, keepdims=True) + EPS)
        vn = ((v * rv) * vn_ref[...]).astype(BF16)
        for g in range(G):
            sl = slice(g * CHUNK, (g + 1) * CHUNK)
            mixed = jnp.dot(wm_ref[g], vn[:, sl], preferred_element_type=F32) + b_ref[g]
            o_ref[:, sl] = (u[:, sl] * mixed).astype(o_ref.dtype)

    return pl.pallas_call(
        body, name=name, grid=(S // CHUNK,),
        in_specs=[pl.BlockSpec((CHUNK, W2), lambda i: (i, 0)),
                  pl.BlockSpec((G, CHUNK, CHUNK), lambda i: (0, 0, 0)),
                  pl.BlockSpec((G, CHUNK, 1), lambda i: (0, 0, 0)),
                  pl.BlockSpec((1, W), lambda i: (0, 0))],
        out_specs=pl.BlockSpec((CHUNK, W), lambda i: (i, 0)), out_shape=_sds((S, W), BF16),
        compiler_params=_cp("parallel"),
    )(zpre, wm, bcol, vnorm)


def _sgu_bwd(zpre, dgated, wm, wmt, bcol, vnorm, name):
    S, W2 = zpre.shape
    W = W2 // 2
    G = W // CHUNK

    def body(z_ref, dg_ref, wm_ref, wmt_ref, b_ref, vn_ref, dz_ref, dws_ref, dbs_ref, dvn_ref):
        @pl.when(pl.program_id(0) == 0)
        def _():
            dws_ref[...] = jnp.zeros_like(dws_ref)
            dbs_ref[...] = jnp.zeros_like(dbs_ref)
            dvn_ref[...] = jnp.zeros_like(dvn_ref)

        zp = z_ref[...].astype(F32)
        zu, zv = zp[:, :W], zp[:, W:]
        u = _gelu(zu)
        v = _gelu(zv)
        rv = lax.rsqrt(jnp.mean(v * v, axis=-1, keepdims=True) + EPS)
        vh = v * rv
        vn = (vh * vn_ref[...]).astype(BF16)
        dgv = dg_ref[...].astype(F32)
        du_parts, dvn_parts = [], []
        for g in range(G):
            sl = slice(g * CHUNK, (g + 1) * CHUNK)
            vg = vn[:, sl]
            mixed = jnp.dot(wm_ref[g], vg, preferred_element_type=F32) + b_ref[g]
            dgg = dgv[:, sl]
            du_parts.append(dgg * mixed)
            dmixed = dgg * u[:, sl]
            dbs_ref[g] += jnp.sum(dmixed, axis=1, keepdims=True)
            dmb = dmixed.astype(BF16)
            dws_ref[g] += lax.dot_general(dmb, vg, (_DIMS["nt"], ((), ())), preferred_element_type=F32)
            dvn_parts.append(jnp.dot(wmt_ref[g], dmb, preferred_element_type=F32))
        du = jnp.concatenate(du_parts, axis=1)
        dvn = jnp.concatenate(dvn_parts, axis=1)
        dvn_ref[...] += jnp.sum(dvn * vh, axis=0, keepdims=True)
        dvh = dvn * vn_ref[...]
        dv = rv * (dvh - vh * jnp.mean(dvh * vh, axis=-1, keepdims=True))
        dz_ref[:, :W] = (du * _gelu_grad(zu)).astype(dz_ref.dtype)
        dz_ref[:, W:] = (dv * _gelu_grad(zv)).astype(dz_ref.dtype)

    full3 = lambda shape: pl.BlockSpec(shape, lambda i: (0, 0, 0))
    return pl.pallas_call(
        body, name=name, grid=(S // CHUNK,),
        in_specs=[pl.BlockSpec((CHUNK, W2), lambda i: (i, 0)), pl.BlockSpec((CHUNK, W), lambda i: (i, 0)),
                  full3((G, CHUNK, CHUNK)), full3((G, CHUNK, CHUNK)), full3((G, CHUNK, 1)),
                  pl.BlockSpec((1, W), lambda i: (0, 0))],
        out_specs=[pl.BlockSpec((CHUNK, W2), lambda i: (i, 0)), full3((G, CHUNK, CHUNK)), full3((G, CHUNK, 1)),
                   pl.BlockSpec((1, W), lambda i: (0, 0))],
        out_shape=[_sds((S, W2), BF16), _sds((G, CHUNK, CHUNK), F32), _sds((G, CHUNK, 1), F32), _sds((1, W), F32)],
        compiler_params=_cp("arbitrary"),
    )(zpre, dgated, wm, wmt, bcol, vnorm)


def _conv_taps(h_ref, half, r0, R, tc):
    if r0 == 0:
        xe = jnp.concatenate([jnp.zeros((HALO, tc), F32), h_ref[half, 0:R, :].astype(F32)], axis=0)
    else:
        xe = h_ref[half, r0 - HALO:r0 + R, :].astype(F32)
    return xe[HALO:], pltpu.roll(xe, 1, 0)[HALO:], pltpu.roll(xe, 2, 0)[HALO:]


def _conv_apply(taps, w, b):
    x0, x1, x2 = taps
    return x2 * w[0:1] + x1 * w[1:2] + x0 * w[2:3] + b


def _convgate_fwd(hup, cw, cb, name):
    _, S, F = hup.shape
    tc = _t(F, 256)
    R = _t(S, 512)

    def body(h_ref, w_ref, b_ref, o_ref):
        for r0 in range(0, S, R):
            gate = _conv_apply(_conv_taps(h_ref, 0, r0, R, tc), w_ref[0], b_ref[0])
            val = _conv_apply(_conv_taps(h_ref, 1, r0, R, tc), w_ref[1], b_ref[1])
            o_ref[r0:r0 + R, :] = (gate * jax.nn.sigmoid(gate) * val).astype(o_ref.dtype)

    return pl.pallas_call(
        body, name=name, grid=(F // tc,),
        in_specs=[pl.BlockSpec((2, S, tc), lambda j: (0, 0, j)), pl.BlockSpec((2, 3, tc), lambda j: (0, 0, j)),
                  pl.BlockSpec((2, 1, tc), lambda j: (0, 0, j))],
        out_specs=pl.BlockSpec((S, tc), lambda j: (0, j)), out_shape=_sds((S, F), BF16),
        compiler_params=_cp("parallel"),
    )(hup, cw, cb)


def _convgate_bwd(hup, dact, cw, cb, name):
    _, S, F = hup.shape
    tc = _t(F, 256)
    R = _t(S, 512)

    def body(h_ref, da_ref, w_ref, b_ref, dh_ref, dw_ref, db_ref, dhc):
        dhc[:, S:S + HALO, :] = jnp.zeros((2, HALO, tc), F32)
        dw_acc = [[jnp.zeros((1, tc), F32) for _ in range(3)] for _ in range(2)]
        db_acc = [jnp.zeros((1, tc), F32) for _ in range(2)]
        for r0 in range(0, S, R):
            taps = [_conv_taps(h_ref, p, r0, R, tc) for p in range(2)]
            gate = _conv_apply(taps[0], w_ref[0], b_ref[0])
            val = _conv_apply(taps[1], w_ref[1], b_ref[1])
            da = da_ref[r0:r0 + R, :].astype(F32)
            sg = jax.nn.sigmoid(gate)
            d = [da * val * (sg * (1.0 + gate * (1.0 - sg))), da * (gate * sg)]
            for p in range(2):
                dhc[p, r0:r0 + R, :] = d[p]
                db_acc[p] = db_acc[p] + jnp.sum(d[p], axis=0, keepdims=True)
                for k in range(3):
                    dw_acc[p][k] = dw_acc[p][k] + jnp.sum(d[p] * taps[p][2 - k], axis=0, keepdims=True)
        for p in range(2):
            db_ref[p] = db_acc[p]
            dw_ref[p] = jnp.concatenate(dw_acc[p], axis=0)
            w = w_ref[p]
            for r0 in range(0, S, R):
                de = dhc[p, r0:r0 + R + HALO, :]
                d1 = pltpu.roll(de, R + HALO - 1, 0)[:R]
                d2 = pltpu.roll(de, R + HALO - 2, 0)[:R]
                dh_ref[p, r0:r0 + R, :] = (de[:R] * w[2:3] + d1 * w[1:2] + d2 * w[0:1]).astype(dh_ref.dtype)

    blk = lambda rows: pl.BlockSpec((2, rows, tc), lambda j: (0, 0, j))
    return pl.pallas_call(
        body, name=name, grid=(F // tc,),
        in_specs=[blk(S), pl.BlockSpec((S, tc), lambda j: (0, j)), blk(3), blk(1)],
        out_specs=[blk(S), blk(3), blk(1)],
        out_shape=[_sds((2, S, F), BF16), _sds((2, 3, F), F32), _sds((2, 1, F), F32)],
        scratch_shapes=[pltpu.VMEM((2, S + HALO, tc), F32)],
        compiler_params=_cp("parallel"),
    )(hup, dact, cw, cb)


def _headnorm_fwd(x3, part, gain, name):
    _, S, W = x3.shape
    tr = _t(S, 512)

    def body(x_ref, g_ref, o_ref):
        xv = x_ref[...].astype(F32)
        for h in range(W // HEAD):
            sl = slice(h * HEAD, (h + 1) * HEAD)
            xh = xv[:, sl]
            r = lax.rsqrt(jnp.mean(xh * xh, axis=-1, keepdims=True) + EPS)
            o_ref[:, sl] = ((xh * r) * g_ref[...]).astype(o_ref.dtype)

    return pl.pallas_call(
        body, name=name, grid=(S // tr,),
        in_specs=[pl.BlockSpec((None, tr, W), lambda i: (part, i, 0)), pl.BlockSpec((1, HEAD), lambda i: (0, 0))],
        out_specs=pl.BlockSpec((tr, W), lambda i: (i, 0)), out_shape=_sds((S, W), BF16),
        compiler_params=_cp("parallel"),
    )(x3, gain)


def _headnorm_bwd(dys, x3, gain, passes, name):
    _, S, W = x3.shape
    tr = _t(S, 256)
    nd, npass = len(dys), len(passes)

    def body(*refs):
        dy_refs = refs[:nd]
        x_ref, g_ref = refs[nd], refs[nd + 1]
        p_refs = refs[nd + 2:nd + 2 + npass]
        o_ref, dg_ref = refs[nd + 2 + npass], refs[nd + 3 + npass]

        @pl.when(pl.program_id(0) == 0)
        def _():
            dg_ref[...] = jnp.zeros_like(dg_ref)

        xv = x_ref[...].astype(F32)
        dyv = dy_refs[0][...].astype(F32)
        for r in dy_refs[1:]:
            dyv = dyv + r[...].astype(F32)
        dg = jnp.zeros((1, HEAD), F32)
        for h in range(W // HEAD):
            sl = slice(h * HEAD, (h + 1) * HEAD)
            xh = xv[:, sl]
            r = lax.rsqrt(jnp.mean(xh * xh, axis=-1, keepdims=True) + EPS)
            xh = xh * r
            dyh = dyv[:, sl]
            dg = dg + jnp.sum(dyh * xh, axis=0, keepdims=True)
            dxh = dyh * g_ref[...]
            o_ref[0, :, sl] = (r * (dxh - xh * jnp.mean(dxh * xh, axis=-1, keepdims=True))).astype(o_ref.dtype)
        dg_ref[...] += dg
        pv = p_refs[0][...].astype(F32)
        for r in p_refs[1:]:
            pv = pv + r[...].astype(F32)
        o_ref[1] = pv.astype(o_ref.dtype)

    row = pl.BlockSpec((tr, W), lambda i: (i, 0))
    vec = pl.BlockSpec((1, HEAD), lambda i: (0, 0))
    return pl.pallas_call(
        body, name=name, grid=(S // tr,),
        in_specs=[row] * nd + [pl.BlockSpec((None, tr, W), lambda i: (0, i, 0)), vec] + [row] * npass,
        out_specs=[pl.BlockSpec((2, tr, W), lambda i: (0, i, 0)), vec],
        out_shape=[_sds((2, S, W), BF16), _sds((1, HEAD), F32)], compiler_params=_cp("arbitrary"),
    )(*dys, x3, gain, *passes)


def _gate_fwd(o, qg, name):
    S, W = o.shape
    tr = _t(S, 512)

    def body(o_ref, g_ref, y_ref):
        y_ref[...] = (o_ref[...].astype(F32) * jax.nn.sigmoid(g_ref[...].astype(F32))).astype(y_ref.dtype)

    row = pl.BlockSpec((tr, W), lambda i: (i, 0))
    return pl.pallas_call(
        body, name=name, grid=(S // tr,), in_specs=[row, pl.BlockSpec((None, tr, W), lambda i: (1, i, 0))],
        out_specs=row, out_shape=_sds((S, W), BF16), compiler_params=_cp("parallel"),
    )(o, qg)


def _gate_bwd(dog, o, qg, name):
    S, W = o.shape
    H = W // HEAD
    tr = _t(S, 512)

    def body(dy_ref, o_ref, g_ref, do_ref, dg_ref, dl_ref):
        sg = jax.nn.sigmoid(g_ref[...].astype(F32))
        dy = dy_ref[...].astype(F32)
        ov = o_ref[...].astype(F32)
        dob = (dy * sg).astype(do_ref.dtype)
        do_ref[...] = dob
        dg_ref[...] = (dy * ov * (sg * (1.0 - sg))).astype(dg_ref.dtype)
        prod = dob.astype(F32) * ov
        for h in range(H):
            dl_ref[h] = jnp.sum(prod[:, h * HEAD:(h + 1) * HEAD], axis=-1, keepdims=True)

    row = pl.BlockSpec((tr, W), lambda i: (i, 0))
    return pl.pallas_call(
        body, name=name, grid=(S // tr,), in_specs=[row, row, pl.BlockSpec((None, tr, W), lambda i: (1, i, 0))],
        out_specs=[row, row, pl.BlockSpec((H, tr, 1), lambda i: (0, i, 0))],
        out_shape=[_sds((S, W), BF16), _sds((S, W), BF16), _sds((H, S, 1), F32)], compiler_params=_cp("parallel"),
    )(dog, o, qg)


def _logf_cumsum(fpre, bf, name):
    S, C = fpre.shape
    n = S // CHUNK

    def body(f_ref, b_ref, c_ref, carry):
        @pl.when(pl.program_id(0) == 0)
        def _():
            carry[...] = jnp.zeros_like(carry)

        lf = jax.nn.log_sigmoid(f_ref[...] + b_ref[...])
        tri = (lax.broadcasted_iota(jnp.int32, (CHUNK, CHUNK), 0)
               >= lax.broadcasted_iota(jnp.int32, (CHUNK, CHUNK), 1)).astype(F32)
        c_ref[...] = jnp.dot(tri, lf, preferred_element_type=F32, precision=lax.Precision.HIGHEST) + carry[...]
        carry[...] += jnp.sum(lf, axis=0, keepdims=True)

    return pl.pallas_call(
        body, name=name, grid=(n,),
        in_specs=[pl.BlockSpec((CHUNK, C), lambda i: (i, 0)), pl.BlockSpec((1, C), lambda i: (0, 0))],
        out_specs=pl.BlockSpec((CHUNK, C), lambda i: (i, 0)), out_shape=_sds((S, C), F32),
        scratch_shapes=[pltpu.VMEM((1, C), F32)], compiler_params=_cp("arbitrary"),
    )(fpre, bf)


def _logf_cumsum_bwd(dc, fpre, bf, name):
    S, C = fpre.shape
    n = S // CHUNK

    def body(dc_ref, f_ref, b_ref, df_ref, db_ref, carry):
        @pl.when(pl.program_id(0) == 0)
        def _():
            carry[...] = jnp.zeros_like(carry)
            db_ref[...] = jnp.zeros_like(db_ref)

        dcv = dc_ref[...]
        tri = (lax.broadcasted_iota(jnp.int32, (CHUNK, CHUNK), 0)
               <= lax.broadcasted_iota(jnp.int32, (CHUNK, CHUNK), 1)).astype(F32)
        dlf = jnp.dot(tri, dcv, preferred_element_type=F32, precision=lax.Precision.HIGHEST) + carry[...]
        carry[...] += jnp.sum(dcv, axis=0, keepdims=True)
        df = dlf * jax.nn.sigmoid(-(f_ref[...] + b_ref[...]))
        df_ref[...] = df.astype(df_ref.dtype)
        db_ref[...] += jnp.sum(df, axis=0, keepdims=True)

    rev = pl.BlockSpec((CHUNK, C), lambda i: (n - 1 - i, 0))
    vec = pl.BlockSpec((1, C), lambda i: (0, 0))
    return pl.pallas_call(
        body, name=name, grid=(n,), in_specs=[rev, rev, vec], out_specs=[rev, vec],
        out_shape=[_sds((S, C), BF16), _sds((1, C), F32)],
        scratch_shapes=[pltpu.VMEM((1, C), F32)], compiler_params=_cp("arbitrary"),
    )(dc, fpre, bf)


def _attn_tiles(S):
    return _t(S, 1024), _t(S, 512)


def _scores(q, k, ck, off, tq, tk, masked):
    s = lax.dot_general(q, k, (_DIMS["nt"], ((), ())), preferred_element_type=F32) - ck
    if masked:
        d = lax.broadcasted_iota(jnp.int32, (tq, tk), 1) - lax.broadcasted_iota(jnp.int32, (tq, tk), 0)
        s = jnp.where(d <= off, s, NEG)
    return s


def _attn_fwd(qn, kn, kv, crow, name):
    S, W = qn.shape
    H = W // HEAD
    tq, tk = _attn_tiles(S)
    nq, nk = S // tq, S // tk
    last = lambda i: ((i + 1) * tq - 1) // tk
    sub = _t(tq, 256)

    def body(q_ref, k_ref, v_ref, ck_ref, o_ref, lse_ref, m_sc, l_sc, acc_sc):
        qi, kj = pl.program_id(1), pl.program_id(2)

        @pl.when(kj == 0)
        def _():
            m_sc[...] = jnp.full_like(m_sc, NEG)
            l_sc[...] = jnp.zeros_like(l_sc)
            acc_sc[...] = jnp.zeros_like(acc_sc)

        def step(masked):
            k, ck, v = k_ref[...], ck_ref[...], v_ref[...]
            for r0 in range(0, tq, sub):
                rows = slice(r0, r0 + sub)
                s = _scores(q_ref[rows, :], k, ck, qi * tq + r0 - kj * tk, sub, tk, masked)
                m_old = m_sc[rows, :]
                m_new = jnp.maximum(m_old, jnp.max(s, axis=-1, keepdims=True))
                alpha = jnp.exp(m_old - m_new)
                p = jnp.exp(s - m_new)
                l_sc[rows, :] = alpha * l_sc[rows, :] + jnp.sum(p, axis=-1, keepdims=True)
                acc_sc[rows, :] = alpha * acc_sc[rows, :] + jnp.dot(p.astype(BF16), v, preferred_element_type=F32)
                m_sc[rows, :] = m_new

        @pl.when(kj <= last(qi))
        def _():
            step(True)

        @pl.when(kj == nk - 1)
        def _():
            o_ref[...] = (acc_sc[...] / l_sc[...]).astype(o_ref.dtype)
            lse_ref[...] = m_sc[...] + jnp.log(l_sc[...])

    return pl.pallas_call(
        body, name=name, grid=(H, nq, nk),
        in_specs=[pl.BlockSpec((tq, HEAD), lambda h, i, j: (i, h)),
                  pl.BlockSpec((tk, HEAD), lambda h, i, j: (jnp.minimum(j, last(i)), h)),
                  pl.BlockSpec((None, tk, HEAD), lambda h, i, j: (1, jnp.minimum(j, last(i)), h)),
                  pl.BlockSpec((None, 1, tk), lambda h, i, j: (h, 0, jnp.minimum(j, last(i))))],
        out_specs=[pl.BlockSpec((tq, HEAD), lambda h, i, j: (i, h)),
                   pl.BlockSpec((None, tq, 1), lambda h, i, j: (h, i, 0))],
        out_shape=[_sds((S, W), BF16), _sds((H, S, 1), F32)],
        scratch_shapes=[pltpu.VMEM((tq, 1), F32), pltpu.VMEM((tq, 1), F32), pltpu.VMEM((tq, HEAD), F32)],
        compiler_params=_cp("parallel", "parallel", "arbitrary"),
    )(qn, kn, kv, crow)


def _attn_bwd(qn, kn, kv, crow, do, lse, delta, name):
    S, W = qn.shape
    H = W // HEAD
    tq, tk = _attn_tiles(S)
    nq, nk = S // tq, S // tk
    first = lambda j: (j * tk) // tq

    def body(q_ref, k_ref, v_ref, ck_ref, do_ref, lse_ref, dl_ref, dq_ref, dr_ref, dk_ref, dv_ref, dc_ref, dk_sc, dv_sc, dc_sc):
        kj, qi = pl.program_id(1), pl.program_id(2)

        @pl.when(jnp.logical_and(kj == 0, qi == 0))
        def _():
            dq_ref[...] = jnp.zeros_like(dq_ref)
            dr_ref[...] = jnp.zeros_like(dr_ref)

        @pl.when(qi == 0)
        def _():
            dk_sc[...] = jnp.zeros_like(dk_sc)
            dv_sc[...] = jnp.zeros_like(dv_sc)
            dc_sc[...] = jnp.zeros_like(dc_sc)

        def step(masked):
            q, k, dov = q_ref[...], k_ref[...], do_ref[...]
            s = _scores(q, k, ck_ref[...], qi * tq - kj * tk, tq, tk, masked)
            p = jnp.exp(s - lse_ref[...])
            dp = lax.dot_general(dov, v_ref[...], (_DIMS["nt"], ((), ())), preferred_element_type=F32)
            ds = p * (dp - dl_ref[...])
            dsb = ds.astype(BF16)
            dc_sc[...] += jnp.sum(ds, axis=0, keepdims=True)
            dv_sc[...] += lax.dot_general(p.astype(BF16), dov, (_DIMS["tn"], ((), ())), preferred_element_type=F32)
            dk_sc[...] += lax.dot_general(dsb, q, (_DIMS["tn"], ((), ())), preferred_element_type=F32)
            rows = pl.ds(pl.multiple_of(qi * tq, tq), tq)
            dq_ref[rows, :] += jnp.dot(dsb, k, preferred_element_type=F32)
            dr_ref[rows, :] += jnp.sum(ds, axis=1, keepdims=True)

        below = (kj + 1) * tk - 1 <= qi * tq

        @pl.when(below)
        def _():
            step(False)

        @pl.when(jnp.logical_and(qi >= first(kj), jnp.logical_not(below)))
        def _():
            step(True)

        @pl.when(qi == nq - 1)
        def _():
            dk_ref[...] = dk_sc[...]
            dv_ref[...] = dv_sc[...]
            dc_ref[...] = dc_sc[...]

    qblk = pl.BlockSpec((tq, HEAD), lambda h, j, i: (jnp.maximum(i, first(j)), h))
    qcol = pl.BlockSpec((None, tq, 1), lambda h, j, i: (h, jnp.maximum(i, first(j)), 0))
    kblk = pl.BlockSpec((tk, HEAD), lambda h, j, i: (j, h))
    krow = pl.BlockSpec((None, 1, tk), lambda h, j, i: (h, 0, j))
    return pl.pallas_call(
        body, name=name, grid=(H, nk, nq),
        in_specs=[qblk, kblk, pl.BlockSpec((None, tk, HEAD), lambda h, j, i: (1, j, h)), krow, qblk, qcol, qcol],
        out_specs=[pl.BlockSpec((S, HEAD), lambda h, j, i: (0, h)), pl.BlockSpec((None, S, 1), lambda h, j, i: (h, 0, 0)),
                   kblk, kblk, krow],
        out_shape=[_sds((S, W), F32), _sds((H, S, 1), F32), _sds((S, W), F32), _sds((S, W), F32), _sds((H, 1, S), F32)],
        scratch_shapes=[pltpu.VMEM((tk, HEAD), F32), pltpu.VMEM((tk, HEAD), F32), pltpu.VMEM((1, tk), F32)],
        compiler_params=_cp("parallel", "arbitrary", "arbitrary"),
    )(qn, kn, kv, crow, do, lse, delta)


def _adamw_math(w, g, m, v):
    c1 = 1.0 - ADAM_B1 ** ADAM_STEP
    c2 = 1.0 - ADAM_B2 ** ADAM_STEP
    nm = ADAM_B1 * m + (1.0 - ADAM_B1) * g
    nv = ADAM_B2 * v + (1.0 - ADAM_B2) * (g * g)
    return -ADAM_LR * ((nm / c1) / (jnp.sqrt(nv / c2) + ADAM_EPS) + ADAM_WD * w), nm, nv


def _adamw_layer(w, g, m, v, layer, prev, name):
    L, R, C = w.shape
    tr = _t(R, max(8, (1 << 19) // C), 8)

    def body(w_ref, g_ref, m_ref, v_ref, *rest):
        go_ref, d_ref, nm_ref, nv_ref = rest[4:]
        gv = g_ref[...]
        go_ref[...] = gv
        d_ref[...], nm_ref[...], nv_ref[...] = _adamw_math(w_ref[...], gv, m_ref[...], v_ref[...])

    lay = pl.BlockSpec((None, tr, C), lambda i: (layer, i, 0))
    return pl.pallas_call(
        body, name=name, grid=(R // tr,),
        in_specs=[lay, pl.BlockSpec((None, tr, C), lambda i: (0, i, 0)), lay, lay] + [ANY] * 4, out_specs=[lay] * 4,
        out_shape=[_sds((L, R, C), F32)] * 4, input_output_aliases={4: 0, 5: 1, 6: 2, 7: 3},
        compiler_params=_cp("parallel"),
    )(w, g, m, v, *prev)


def _adamw(w, g, m, v, name):
    R, C = w.shape
    tr = _t(R, max(8, (1 << 19) // max(C, 1)), 8)

    def body(w_ref, g_ref, m_ref, v_ref, d_ref, nm_ref, nv_ref):
        d_ref[...], nm_ref[...], nv_ref[...] = _adamw_math(w_ref[...], g_ref[...], m_ref[...], v_ref[...])

    blk = pl.BlockSpec((tr, C), lambda i: (i, 0))
    return pl.pallas_call(
        body, name=name, grid=(R // tr,), in_specs=[blk] * 4, out_specs=[blk] * 3,
        out_shape=[_sds((R, C), F32)] * 3, compiler_params=_cp("parallel"),
    )(w, g, m, v)


def _place():
    x, y, c = lax.axis_index("x"), lax.axis_index("y"), lax.axis_index("c")
    chips = [(1 - x, y), (x, 1 - y), (1 - x, 1 - y)]
    return x, y, c, chips


def _place_part(part, me, dtype, name, layer=None, after=None):
    L, R, C = part.shape
    first, L = (0, L) if layer is None else (layer, 1)
    tr = _t(R, max(16, (1 << 20) // C), 16)

    def body(me_ref, x_ref, *rest):
        o_ref = rest[-1]
        o_ref[...] = x_ref[...].astype(o_ref.dtype)

    extra = _tokens(after)
    return pl.pallas_call(
        body, name=name,
        grid_spec=pltpu.PrefetchScalarGridSpec(
            num_scalar_prefetch=1, grid=(L, R // tr),
            in_specs=[pl.BlockSpec((None, tr, C), lambda l, r, m: (first + l, r, 0))] + [ANY] * len(extra),
            out_specs=pl.BlockSpec((None, None, tr, C), lambda l, r, m: (m[0], l, r, 0))),
        out_shape=_sds((N_CHIPS, L, R, C), dtype), compiler_params=_cp("parallel", "parallel"),
    )(me, part, *extra)


def _gather_chips_async(bufs, collective_id, name):
    n = len(bufs)
    refs = [jax.new_ref(b, memory_space=pltpu.MemorySpace.HBM) for b in bufs]

    @pl.kernel(mesh=plsc.ScalarSubcoreMesh(axis_name="seq", num_cores=1), name=name,
               scratch_types=(pltpu.SemaphoreType.DMA((n, 6)), pltpu.SemaphoreType.DMA((n, 6))),
               compiler_params=pltpu.CompilerParams(collective_id=collective_id))
    def launch(send_sems, recv_sems):
        x, y, c, chips = _place()
        me = 2 * x + y
        sibling = (x, y, 1 - c)
        barrier = pltpu.get_barrier_semaphore()
        for peer in [(*chip, c) for chip in chips] + [sibling]:
            pl.semaphore_signal(barrier, inc=1, device_id=peer, device_id_type=MESH)
        pl.semaphore_wait(barrier, len(chips) + 1)

        def copy(i, k, chip_index, core, to):
            h = refs[i].shape[2] // 2
            rows = refs[i].at[chip_index, :, pl.ds(core * h, h), :]
            return pltpu.make_async_remote_copy(
                src_ref=rows, dst_ref=rows, send_sem=send_sems.at[i, k], recv_sem=recv_sems.at[i, k],
                device_id=to, device_id_type=MESH)

        sent = []
        for i in range(n):
            for j, chip in enumerate(chips):
                cp = copy(i, j, me, c, (*chip, c))
                cp.start()
                sent.append(cp)
        for i in range(n):
            for j, chip in enumerate(chips):
                idx = 2 * chip[0] + chip[1]
                copy(i, j, idx, c, (x, y, c)).wait_recv()
                fw = copy(i, 3 + j, idx, c, sibling)
                fw.start()
                sent.append(fw)
        for i in range(n):
            for j, chip in enumerate(chips):
                copy(i, 3 + j, 2 * chip[0] + chip[1], 1 - c, (x, y, c)).wait_recv()
        for cp in sent:
            cp.wait_send()

    launch()
    return [r[...] for r in refs]


def _gather_chips(bufs, name):
    n = len(bufs)

    def body(*refs):
        outs = refs[n:2 * n]
        send_sems, recv_sems = refs[2 * n:]
        x, y, c, chips = _place()
        me = 2 * x + y
        sibling = (x, y, 1 - c)

        def copy(i, k, chip_index, core, to):
            h = outs[i].shape[2] // 2
            rows = outs[i].at[chip_index, :, pl.ds(core * h, h), :]
            return pltpu.make_async_remote_copy(
                src_ref=rows, dst_ref=rows, send_sem=send_sems.at[i, k], recv_sem=recv_sems.at[i, k],
                device_id=to, device_id_type=MESH)

        sent = []
        for i in range(n):
            for j, chip in enumerate(chips):
                cp = copy(i, j, me, c, (*chip, c))
                cp.start()
                sent.append(cp)
        for i in range(n):
            for j, chip in enumerate(chips):
                idx = 2 * chip[0] + chip[1]
                copy(i, j, idx, c, (x, y, c)).wait_recv()
                fw = copy(i, 3 + j, idx, c, sibling)
                fw.start()
                sent.append(fw)
        for i in range(n):
            for j, chip in enumerate(chips):
                copy(i, 3 + j, 2 * chip[0] + chip[1], 1 - c, (x, y, c)).wait_recv()
        for cp in sent:
            cp.wait_send()

    return pl.pallas_call(
        body, name=name, in_specs=[ANY] * n, out_specs=[ANY] * n,
        out_shape=[_sds(b.shape, b.dtype) for b in bufs],
        scratch_shapes=[pltpu.SemaphoreType.DMA((n, 6)), pltpu.SemaphoreType.DMA((n, 6))],
        input_output_aliases={i: i for i in range(n)},
        compiler_params=pltpu.CompilerParams(has_side_effects=True),
    )(*bufs)


def _swap_halves(gs, name):
    n = len(gs)

    def body(*refs):
        ins, outs = refs[:n], refs[n:2 * n]
        send_sems, recv_sems = refs[2 * n:]
        x, y, c, _ = _place()
        cps = []
        for i in range(n):
            h = ins[i].shape[2] // 2
            cp = pltpu.make_async_remote_copy(
                src_ref=ins[i].at[:, :, pl.ds((1 - c) * h, h), :], dst_ref=outs[i], send_sem=send_sems.at[i],
                recv_sem=recv_sems.at[i], device_id=(x, y, 1 - c), device_id_type=MESH)
            cp.start()
            cps.append(cp)
        for cp in cps:
            cp.wait()

    return pl.pallas_call(
        body, name=name, in_specs=[ANY] * n, out_specs=[ANY] * n,
        out_shape=[_sds(g.shape[:2] + (g.shape[2] // 2, g.shape[3]), g.dtype) for g in gs],
        scratch_shapes=[pltpu.SemaphoreType.DMA((n,)), pltpu.SemaphoreType.DMA((n,))],
        compiler_params=pltpu.CompilerParams(has_side_effects=True),
    )(*gs)


def _pair_sum(g, gs, core, out_dtype, name):
    ns, L, R, C = g.shape
    h = R // 2
    th = _t(h, max(16, (1 << 20) // C), 16)
    nb = h // th

    def body(core_ref, a_ref, b_ref, o_ref):
        o_ref[...] = (a_ref[...].astype(F32) + b_ref[...].astype(F32)).astype(o_ref.dtype)

    blk = (None, None, th, C)
    return pl.pallas_call(
        body, name=name,
        grid_spec=pltpu.PrefetchScalarGridSpec(
            num_scalar_prefetch=1, grid=(ns, L, nb),
            in_specs=[pl.BlockSpec(blk, lambda o, l, r, cr: (o, l, cr[0] * nb + r, 0)),
                      pl.BlockSpec(blk, lambda o, l, r, cr: (o, l, r, 0))],
            out_specs=pl.BlockSpec(blk, lambda o, l, r, cr: (o, l, r, 0))),
        out_shape=_sds((ns, L, h, C), out_dtype), compiler_params=_cp("parallel", "parallel", "parallel"),
    )(core, g, gs)


def _scatter_chips(ps, name):
    n = len(ps)

    def body(*refs):
        ins, outs = refs[:n], refs[n:2 * n]
        send_sems, recv_sems = refs[2 * n:]
        x, y, c, chips = _place()
        sent = []
        for i in range(n):
            for j, chip in enumerate(chips):
                cp = pltpu.make_async_remote_copy(
                    src_ref=ins[i].at[2 * chip[0] + chip[1]], dst_ref=outs[i].at[j], send_sem=send_sems.at[i, j],
                    recv_sem=recv_sems.at[i, j], device_id=(*chip, c), device_id_type=MESH)
                cp.start()
                sent.append(cp)
        for i in range(n):
            for j in range(len(chips)):
                slot = outs[i].at[j]
                pltpu.make_async_remote_copy(
                    src_ref=slot, dst_ref=slot, send_sem=send_sems.at[i, j], recv_sem=recv_sems.at[i, j],
                    device_id=(x, y, c), device_id_type=MESH).wait_recv()
        for cp in sent:
            cp.wait_send()

    return pl.pallas_call(
        body, name=name, in_specs=[ANY] * n, out_specs=[ANY] * n,
        out_shape=[_sds((N_CHIPS - 1,) + p.shape[1:], p.dtype) for p in ps],
        scratch_shapes=[pltpu.SemaphoreType.DMA((n, 3)), pltpu.SemaphoreType.DMA((n, 3))],
        compiler_params=pltpu.CompilerParams(has_side_effects=True),
    )(*ps)


def _scatter_chips_async(ps, collective_id, name):
    n = len(ps)

    def launch(*refs):
        srcs, dsts = refs[:n], refs[n:2 * n]
        send_sems, recv_sems = refs[2 * n:]
        x, y, c, chips = _place()
        barrier = pltpu.get_barrier_semaphore()
        for chip in chips:
            pl.semaphore_signal(barrier, inc=1, device_id=(*chip, c), device_id_type=MESH)
        pl.semaphore_wait(barrier, len(chips))
        sent = []
        for i in range(n):
            for j, chip in enumerate(chips):
                cp = pltpu.make_async_remote_copy(
                    src_ref=srcs[i].at[2 * chip[0] + chip[1]], dst_ref=dsts[i].at[j], send_sem=send_sems.at[i, j],
                    recv_sem=recv_sems.at[i, j], device_id=(*chip, c), device_id_type=MESH)
                cp.start()
                sent.append(cp)
        for i in range(n):
            for j in range(len(chips)):
                slot = dsts[i].at[j]
                pltpu.make_async_remote_copy(
                    src_ref=slot, dst_ref=slot, send_sem=send_sems.at[i, j], recv_sem=recv_sems.at[i, j],
                    device_id=(x, y, c), device_id_type=MESH).wait_recv()
        for cp in sent:
            cp.wait_send()

    return pl.kernel(
        launch, name=name, out_type=[_sds((N_CHIPS - 1,) + p.shape[1:], p.dtype) for p in ps],
        mesh=plsc.ScalarSubcoreMesh(axis_name="seq", num_cores=1),
        scratch_types=[pltpu.SemaphoreType.DMA((n, 3)), pltpu.SemaphoreType.DMA((n, 3))],
        compiler_params=pltpu.CompilerParams(collective_id=collective_id),
    )(*ps)


def _sum_chips(q, p, me, core, name, after=None):
    _, L, h, C = q.shape
    th = _t(h, max(16, (1 << 19) // C), 16)
    nb = h // th
    blk = (None, None, th, C)

    def body(me_ref, core_ref, p_ref, q0, q1, q2, *rest):
        o_ref = rest[-1]
        o_ref[...] = ((p_ref[...].astype(F32) + q0[...].astype(F32)) + q1[...].astype(F32)) + q2[...].astype(F32)

    extra = _tokens(after)
    return pl.pallas_call(
        body, name=name,
        grid_spec=pltpu.PrefetchScalarGridSpec(
            num_scalar_prefetch=2, grid=(L, nb),
            in_specs=[pl.BlockSpec(blk, lambda l, r, m, c: (m[0], l, r, 0))]
            + [pl.BlockSpec(blk, functools.partial(lambda k, l, r, m, c: (k, l, r, 0), k)) for k in range(N_CHIPS - 1)]
            + [ANY] * len(extra),
            out_specs=pl.BlockSpec((None, th, C), lambda l, r, m, c: (l, c[0] * nb + r, 0))),
        out_shape=_sds((L, 2 * h, C), F32), compiler_params=_cp("parallel", "parallel"),
    )(me, core, p, q, q, q, *extra)


def _sum_small(own, q, name):
    def body(p_ref, q_ref, o_ref):
        o_ref[...] = ((p_ref[...] + q_ref[0]) + q_ref[1]) + q_ref[2]

    return pl.pallas_call(body, name=name, out_shape=_sds(own.shape, F32))(own, q)


def _join_halves(bufs, name):
    n = len(bufs)

    def body(*refs):
        outs = refs[n:2 * n]
        send_sems, recv_sems = refs[2 * n:]
        x, y, c, _ = _place()
        cps = []
        for i in range(n):
            h = outs[i].shape[1] // 2
            mine = outs[i].at[:, pl.ds(c * h, h), :]
            cp = pltpu.make_async_remote_copy(
                src_ref=mine, dst_ref=mine, send_sem=send_sems.at[i], recv_sem=recv_sems.at[i],
                device_id=(x, y, 1 - c), device_id_type=MESH)
            cp.start()
            cps.append(cp)
        for cp in cps:
            cp.wait()

    return pl.pallas_call(
        body, name=name, in_specs=[ANY] * n, out_specs=[ANY] * n,
        out_shape=[_sds(b.shape, b.dtype) for b in bufs],
        scratch_shapes=[pltpu.SemaphoreType.DMA((n,)), pltpu.SemaphoreType.DMA((n,))],
        input_output_aliases={i: i for i in range(n)},
        compiler_params=pltpu.CompilerParams(has_side_effects=True),
    )(*bufs)


def _pack(arrs, rows_mult):
    flat = jnp.concatenate([a.reshape(-1).astype(F32) for a in arrs])
    rows = -(-flat.size // LANE)
    rows = -(-rows // rows_mult) * rows_mult
    return jnp.pad(flat, (0, rows * LANE - flat.size)).reshape(rows, LANE)


def _unpack(packed, like):
    flat = packed.reshape(-1)
    out, pos = [], 0
    for a in like:
        n = math.prod(a.shape)
        out.append(flat[pos:pos + n].reshape(a.shape))
        pos += n
    return out


def _adamw_nd(w, g, m, v, name):
    shape = w.shape
    C = shape[-1]
    d, nm, nv = _adamw(w.reshape(-1, C), g.reshape(-1, C), m.reshape(-1, C), v.reshape(-1, C), name)
    return d.reshape(shape), nm.reshape(shape), nv.reshape(shape)


def kernel(x, a_norm, a_w_in, a_v_norm, a_w_s, a_b_s, a_w_out, kv_norm, w_kvf, b_f, k_norm, b_norm, b_w_qg, q_norm, b_w_out, f_norm, f_w_up, f_conv_w, f_conv_b, f_w_down, final_norm, loss_target, m_a_norm, m_a_w_in, m_a_v_norm, m_a_w_s, m_a_b_s, m_a_w_out, m_kv_norm, m_w_kvf, m_b_f, m_k_norm, m_b_norm, m_b_w_qg, m_q_norm, m_b_w_out, m_f_norm, m_f_w_up, m_f_conv_w, m_f_conv_b, m_f_w_down, m_final_norm, v_a_norm, v_a_w_in, v_a_v_norm, v_a_w_s, v_a_b_s, v_a_w_out, v_kv_norm, v_w_kvf, v_b_f, v_k_norm, v_b_norm, v_b_w_qg, v_q_norm, v_b_w_out, v_f_norm, v_f_w_up, v_f_conv_w, v_f_conv_b, v_f_w_down, v_final_norm):
    weights = dict(a_norm=a_norm, a_w_in=a_w_in, a_v_norm=a_v_norm, a_w_s=a_w_s, a_b_s=a_b_s, a_w_out=a_w_out, kv_norm=kv_norm, w_kvf=w_kvf, b_f=b_f, k_norm=k_norm, b_norm=b_norm, b_w_qg=b_w_qg, q_norm=q_norm, b_w_out=b_w_out, f_norm=f_norm, f_w_up=f_w_up, f_conv_w=f_conv_w, f_conv_b=f_conv_b, f_w_down=f_w_down, final_norm=final_norm)
    mom1 = dict(a_norm=m_a_norm, a_w_in=m_a_w_in, a_v_norm=m_a_v_norm, a_w_s=m_a_w_s, a_b_s=m_a_b_s, a_w_out=m_a_w_out, kv_norm=m_kv_norm, w_kvf=m_w_kvf, b_f=m_b_f, k_norm=m_k_norm, b_norm=m_b_norm, b_w_qg=m_b_w_qg, q_norm=m_q_norm, b_w_out=m_b_w_out, f_norm=m_f_norm, f_w_up=m_f_w_up, f_conv_w=m_f_conv_w, f_conv_b=m_f_conv_b, f_w_down=m_f_w_down, final_norm=m_final_norm)
    mom2 = dict(a_norm=v_a_norm, a_w_in=v_a_w_in, a_v_norm=v_a_v_norm, a_w_s=v_a_w_s, a_b_s=v_a_b_s, a_w_out=v_a_w_out, kv_norm=v_kv_norm, w_kvf=v_w_kvf, b_f=v_b_f, k_norm=v_k_norm, b_norm=v_b_norm, b_w_qg=v_b_w_qg, q_norm=v_q_norm, b_w_out=v_b_w_out, f_norm=v_f_norm, f_w_up=v_f_w_up, f_conv_w=v_f_conv_w, f_conv_b=v_f_conv_b, f_w_down=v_f_w_down, final_norm=v_final_norm)
    names = list(weights)
    big = ["a_w_in", "a_w_out", "w_kvf", "b_w_qg", "b_w_out", "f_w_up", "f_w_down"]
    small_sharded = ["a_norm", "a_v_norm", "f_conv_w"]
    small_repl = ["a_w_s", "a_b_s", "kv_norm", "b_f", "k_norm", "b_norm", "q_norm", "f_norm", "f_conv_b", "final_norm"]

    _, S, D = x.shape
    NA, NB, DEPTH = a_norm.shape[0], b_norm.shape[0], f_norm.shape[0]
    W = a_w_out.shape[1] * N_CHIPS
    G = a_w_s.shape[1]
    H = b_f.shape[0]
    ATT = H * HEAD
    F = f_w_down.shape[1] * N_CHIPS
    Ckv = w_kvf.shape[1]
    Cp = -(-Ckv // LANE) * LANE
    assert W == G * CHUNK and Ckv * N_CHIPS == 2 * ATT + H and S % CHUNK == 0
    core = lax.axis_index("c").astype(jnp.int32).reshape(1)
    me = (2 * lax.axis_index("x") + lax.axis_index("y")).astype(jnp.int32).reshape(1)

    small_local = [weights[k] for k in small_sharded]
    place = lambda p, nm, dt=BF16, layer=None, after=None: _place_part(p, me, dt, "place_" + nm, layer, after)
    g_ain, g_aout, g_up, g_down, g_attn = [None] * NA, [None] * NA, [None] * DEPTH, [None] * DEPTH, {}
    groups = [("ffn", 0)] + [("mix", l) for l in range(1, NA)] + [("attn", None)] + [("ffn", l) for l in range(NA, DEPTH)]

    def launch_next(after=None):
        if not groups:
            return
        kind, l = groups.pop(0)
        cid = DEPTH + 1 - len(groups)
        if kind == "attn":
            bufs = [place(jnp.pad(w_kvf, ((0, 0), (0, Cp - Ckv)))[None], "w_kvf", after=after),
                    place(b_w_qg, "b_w_qg", after=after), place(b_w_out, "b_w_out", after=after)]
            g_attn["kvf"], g_attn["qg"], g_attn["out"] = _gather_chips_async(bufs, cid, f"gather_w{cid}")
            return
        bufs = [place(f_w_up, f"f_w_up{l}", layer=l, after=after), place(f_w_down, f"f_w_down{l}", layer=l, after=after)]
        if kind == "mix":
            bufs += [place(a_w_in, f"a_w_in{l}", layer=l, after=after), place(a_w_out, f"a_w_out{l}", layer=l, after=after)]
            g_up[l], g_down[l], g_ain[l], g_aout[l] = _gather_chips_async(bufs, cid, f"gather_w{cid}")
        else:
            g_up[l], g_down[l] = _gather_chips_async(bufs, cid, f"gather_w{cid}")

    g_small, g_ain[0], g_aout[0] = _gather_chips_async(
        [place(_pack(small_local, 32)[None], "small", F32), place(a_w_in, "a_w_in0", layer=0), place(a_w_out, "a_w_out0", layer=0)],
        0, "gather_w0")
    launch_next()

    per_chip = [_unpack(g_small[j, 0], small_local) for j in range(N_CHIPS)]
    a_norm_f, a_vnorm_f, conv_w_f = [jnp.concatenate([per_chip[j][k] for j in range(N_CHIPS)], axis=-1) for k in range(3)]
    cw = conv_w_f.reshape(DEPTH, 3, 2, F).transpose(0, 2, 1, 3)
    cb = f_conv_b.reshape(DEPTH, 2, 1, F)
    tril = jnp.tril(jnp.ones((CHUNK, CHUNK), dtype=bool))
    wm = jnp.where(tril, a_w_s, 0.0).astype(BF16)
    wmt = jnp.swapaxes(wm, -1, -2)
    bcol = a_b_s[..., None]
    bf_pad = jnp.pad(b_f, (0, LANE - H))[None]
    row = lambda v: v.reshape(1, -1)

    h = x[0]
    target = loss_target[0]
    saved = [dict() for _ in range(DEPTH)]
    kvs = {}
    for l in range(DEPTH):
        sv = saved[l]
        sv["h_m"] = h
        if l < NA:
            xn = _rms_fwd(h, row(a_norm_f[l]), f"a{l}_norm")
            zpre = _mm(m2(xn), mgc(g_ain[l], 0), "nn", m2(_sds((S, 2 * W), BF16)), tm=1024, tn=512, name=f"a{l}_in")
            if l + 1 < NA:
                launch_next(zpre)
            gated = _sgu_fwd(zpre, wm[l], bcol[l], row(a_vnorm_f[l]), f"a{l}_sgu")
            h = _mm(m2(gated), mgr(g_aout[l], 0), "nn", m2(_sds((S, D), F32)), res=m2(h), tm=1024, tn=1024, name=f"a{l}_out")
            sv.update(xn_m=xn, zpre=zpre, gated=gated)
        else:
            j = l - NA
            xn = _rms_fwd(h, row(b_norm[j]), f"b{j}_norm")
            qg = _mm(m2(xn), mgc(g_attn["qg"], j), "nn", mcs(_sds((2, S, ATT), BF16)), tm=1024, tn=512, name=f"b{j}_qg")
            qn = _headnorm_fwd(qg, 0, row(q_norm[j]) * QK_SCALE, f"b{j}_qnorm")
            o, lse = _attn_fwd(qn, kvs["kn"], kvs["kv"], kvs["crow"], f"b{j}_attn")
            og = _gate_fwd(o, qg, f"b{j}_gate")
            h = _mm(m2(og), mgr(g_attn["out"], j), "nn", m2(_sds((S, D), F32)), res=m2(h), tm=1024, tn=1024, name=f"b{j}_out")
            sv.update(xn_m=xn, qg=qg, qn=qn, o=o, lse=lse, og=og)
        sv["h_f"] = h
        xn = _rms_fwd(h, row(f_norm[l]), f"f{l}_norm")
        hup = _mm(m2(xn), mgc(g_up[l], 0), "nn", mcs(_sds((2, S, F), BF16)), tm=1024, tn=1408, name=f"f{l}_up")
        launch_next(hup)
        act = _convgate_fwd(hup, cw[l], cb[l], f"f{l}_conv")
        h = _mm(m2(act), mgr(g_down[l], 0), "nn", m2(_sds((S, D), F32)), res=m2(h), tm=1024, tn=1024, tk=1408, name=f"f{l}_down")
        sv.update(xn_f=xn, hup=hup, act=act)
        if l == NA - 1:
            w_kvf_g = mgc(g_attn["kvf"], 0)
            xn_kv = _rms_fwd(h, row(kv_norm), "kv_norm")
            kvp = _mm(m2(xn_kv), w_kvf_g, "nn", m2(_sds((S, N_CHIPS * Cp), F32)), tm=1024, tn=Cp, name="kv_proj")
            launch_next(kvp)
            kvflat = jnp.concatenate([kvp[:, j * Cp:j * Cp + Ckv] for j in range(N_CHIPS)], axis=1)
            kv = jnp.stack([kvflat[:, :ATT], kvflat[:, ATT:2 * ATT]]).astype(BF16)
            fpre = jnp.pad(kvflat[:, 2 * ATT:], ((0, 0), (0, LANE - H)))
            kn = _headnorm_fwd(kv, 0, row(k_norm), "kv_knorm")
            cums = _logf_cumsum(fpre, bf_pad, "kv_cumsum")
            cT = cums[:, :H].T
            kvs.update(h=h, xn=xn_kv, kv=kv, fpre=fpre, kn=kn, crow=cT[:, None, :])

    loss11, dh, dhb, d_final = _final_loss(h, row(final_norm), target, "final_loss")
    loss = lax.psum(loss11[0, 0], ("x", "y", "c"))

    stacked = {k: tuple(lax.empty(weights[k].shape, F32) for _ in range(4))
               for k in ("a_w_in", "a_w_out", "b_w_qg", "b_w_out", "f_w_up", "f_w_down")}
    flat = {}
    pending = []
    next_id = [DEPTH + 2]

    def grad_buf(k):
        return _sds((N_CHIPS, 1) + weights[k].shape[1:], BF16)

    def start_chunk(entries, tag):
        Gs = [e[2] for e in entries]
        others = _swap_halves(Gs, f"grads_swap_{tag}")
        partial = [_pair_sum(g, o, core, g.dtype, f"grads_pair_{tag}{i}") for i, (g, o) in enumerate(zip(Gs, others))]
        by_chip = _scatter_chips_async(partial, next_id[0], f"grads_scatter_{tag}")
        next_id[0] += 1
        pending.append((entries, partial, by_chip, tag))
        return partial

    def finish_chunk(after):
        entries, partial, by_chip, tag = pending.pop(0)
        halves = []
        for i, ((k, l, _), p, q) in enumerate(zip(entries, partial, by_chip)):
            if k == "small":
                half = _sum_small(lax.dynamic_index_in_dim(p, me[0], 0, keepdims=False), q, "grads_sum_small")
                hs = half.shape[1]
                halves.append(lax.dynamic_update_slice(jnp.zeros((1, 2 * hs, LANE), F32), half, (0, core[0] * hs, 0)))
            else:
                halves.append(_sum_chips(q, p, me, core, f"grads_sum_{tag}{i}", after=after))
        for (k, l, _), f in zip(entries, _join_halves(halves, f"grads_join_{tag}")):
            if k in stacked:
                stacked[k] = tuple(_adamw_layer(weights[k], f, mom1[k], mom2[k], l, stacked[k], f"adamw_{k}{l}"))
            else:
                flat[k] = f

    d_anorm, d_avnorm, d_ws, d_bs = [None] * NA, [None] * NA, [None] * NA, [None] * NA
    d_bnorm, d_qnorm = [None] * NB, [None] * NB
    d_fnorm, d_cw, d_cb = [None] * DEPTH, [None] * DEPTH, [None] * DEPTH
    dkn, dvv, dck = [], [], []
    G_kvf = d_kvnorm = d_bf = d_knorm = None
    tok = None
    for l in reversed(range(DEPTH)):
        sv = saved[l]
        chunk = []
        if l == NA - 1:
            dkv, d_knorm = _headnorm_bwd(dkn, kvs["kv"], row(k_norm), dvv, "kv_knorm_bwd")
            dc = sum(dck)
            dc = jnp.pad(dc, ((0, 0), (0, LANE - H)))
            df, d_bf = _logf_cumsum_bwd(dc, kvs["fpre"], bf_pad, "kv_cumsum_bwd")
            dflat = jnp.concatenate([dkv[0], dkv[1], df[:, :H]], axis=1)
            dpad = jnp.concatenate([jnp.pad(dflat[:, j * Ckv:(j + 1) * Ckv], ((0, 0), (0, Cp - Ckv))) for j in range(N_CHIPS)], axis=1)
            dxn = _mm(m2(dpad), w_kvf_g, "nt", m2(_sds((S, D), BF16)), after=tok, tm=1024, tn=2048, tk=Cp, name="kv_proj_dx")
            dw_kv = _mm(m2(kvs["xn"]), mcs(dkv), "tn", mcs(_sds((2, D, ATT), BF16)), tm=1024, tn=1024, tk=2048, name="kv_proj_dw")
            dw_f = _mm(m2(kvs["xn"]), m2(df), "tn", m2(_sds((D, LANE), BF16)), tm=512, tk=2048, name="kv_fproj_dw")
            dfull = jnp.concatenate([dw_kv[0], dw_kv[1], dw_f[:, :H]], axis=1)
            G_kvf = jnp.pad(dfull.reshape(D, N_CHIPS, Ckv).transpose(1, 0, 2), ((0, 0), (0, 0), (0, Cp - Ckv)))[:, None]
            dh, dhb, d_kvnorm = _rms_bwd(dxn, kvs["h"], row(kv_norm), dh, "kv_norm_bwd")
            chunk.append(("w_kvf", 0, G_kvf))
        dact = _mm(m2(dhb), mgr(g_down[l], 0), "nt", m2(_sds((S, F), BF16)), after=tok, tm=1024, tn=1408, tk=2048, name=f"f{l}_down_dx")
        G_down = _mm(m2(sv["act"]), m2(dhb), "tn", mgr(grad_buf("f_w_down"), 0), tm=1408, tn=1024, tk=2048, name=f"f{l}_down_dw")
        dhup, d_cw[l], d_cb[l] = _convgate_bwd(sv["hup"], dact, cw[l], cb[l], f"f{l}_conv_bwd")
        dxn = _mm(mcs(dhup), mgc(g_up[l], 0), "nt", m2(_sds((S, D), BF16)), tm=1024, tn=2048, tk=1408, name=f"f{l}_up_dx")
        G_up = _mm(m2(sv["xn_f"]), mcs(dhup), "tn", mgc(grad_buf("f_w_up"), 0), tm=1024, tn=1408, tk=2048, name=f"f{l}_up_dw")
        dh, dhb, d_fnorm[l] = _rms_bwd(dxn, sv["h_f"], row(f_norm[l]), dh, f"f{l}_norm_bwd")
        chunk += [("f_w_up", l, G_up), ("f_w_down", l, G_down)]
        if l == 0:
            tok = start_chunk(chunk, "f0")
            finish_chunk(tok)
            chunk = []
        if l >= NA:
            j = l - NA
            dog = _mm(m2(dhb), mgr(g_attn["out"], j), "nt", m2(_sds((S, ATT), BF16)), tm=1024, tn=512, tk=2048, name=f"b{j}_out_dx")
            G_bout = _mm(m2(sv["og"]), m2(dhb), "tn", mgr(grad_buf("b_w_out"), 0), tm=512, tn=1024, tk=4096, name=f"b{j}_out_dw")
            do, dgate, delta = _gate_bwd(dog, sv["o"], sv["qg"], f"b{j}_gate_bwd")
            dqn, drow, dkn_j, dv_j, dcol = _attn_bwd(sv["qn"], kvs["kn"], kvs["kv"], kvs["crow"], do, sv["lse"], delta, f"b{j}_attn_bwd")
            dkn.append(dkn_j); dvv.append(dv_j); dck.append(drow[:, :, 0].T - dcol[:, 0, :].T)
            dqg, d_qs = _headnorm_bwd([dqn], sv["qg"], row(q_norm[j]) * QK_SCALE, [dgate], f"b{j}_qnorm_bwd")
            d_qnorm[j] = d_qs * QK_SCALE
            dxn = _mm(mcs(dqg), mgc(g_attn["qg"], j), "nt", m2(_sds((S, D), BF16)), tm=1024, tn=2048, tk=1024, name=f"b{j}_qg_dx")
            G_bqg = _mm(m2(sv["xn_m"]), mcs(dqg), "tn", mgc(grad_buf("b_w_qg"), 0), tm=1024, tn=1024, tk=2048, name=f"b{j}_qg_dw")
            dh, dhb, d_bnorm[j] = _rms_bwd(dxn, sv["h_m"], row(b_norm[j]), dh, f"b{j}_norm_bwd")
            chunk += [("b_w_qg", j, G_bqg), ("b_w_out", j, G_bout)]
        else:
            dgated = _mm(m2(dhb), mgr(g_aout[l], 0), "nt", m2(_sds((S, W), BF16)), after=tok, tm=1024, tn=512, tk=2048, name=f"a{l}_out_dx")
            G_aout = _mm(m2(sv["gated"]), m2(dhb), "tn", mgr(grad_buf("a_w_out"), 0), tm=512, tn=1024, tk=4096, name=f"a{l}_out_dw")
            dz, d_ws[l], d_bs[l], d_avnorm[l] = _sgu_bwd(sv["zpre"], dgated, wm[l], wmt[l], bcol[l], row(a_vnorm_f[l]), f"a{l}_sgu_bwd")
            dxn = _mm(m2(dz), mgc(g_ain[l], 0), "nt", m2(_sds((S, D), BF16)), tm=1024, tn=2048, tk=1024, name=f"a{l}_in_dx")
            G_ain = _mm(m2(sv["xn_m"]), m2(dz), "tn", mgc(grad_buf("a_w_in"), 0), tm=1024, tn=1024, tk=2048, name=f"a{l}_in_dw")
            dh, dhb, d_anorm[l] = _rms_bwd(dxn, sv["h_m"], row(a_norm_f[l]), dh, f"a{l}_norm_bwd")
            chunk += [("a_w_in", l, G_ain), ("a_w_out", l, G_aout)]
        if l > 0:
            tok = start_chunk(chunk, f"l{l}")
            if len(pending) > 1:
                finish_chunk(tok)
    grad_x = dh[None]

    full = dict(
        a_norm=jnp.concatenate(d_anorm, axis=0), a_v_norm=jnp.concatenate(d_avnorm, axis=0),
        f_conv_w=jnp.stack(d_cw).transpose(0, 2, 1, 3).reshape(DEPTH, 3, 2 * F),
        a_w_s=jnp.where(tril, jnp.stack(d_ws), 0.0), a_b_s=jnp.stack(d_bs)[..., 0],
        kv_norm=d_kvnorm[0], b_f=d_bf[0, :H], k_norm=d_knorm[0], b_norm=jnp.concatenate(d_bnorm, axis=0),
        q_norm=jnp.concatenate(d_qnorm, axis=0), f_norm=jnp.concatenate(d_fnorm, axis=0),
        f_conv_b=jnp.stack(d_cb).reshape(DEPTH, 2 * F), final_norm=d_final[0])
    shard_rows = []
    for j in range(N_CHIPS):
        pieces = []
        for k in small_sharded:
            n = weights[k].shape[-1]
            pieces.append(full[k][..., j * n:(j + 1) * n])
        shard_rows.append(_pack(pieces, 32))
    rs = shard_rows[0].shape[0]
    repl = _pack([full[k] for k in small_repl], N_CHIPS * 32)
    rr = repl.shape[0] // N_CHIPS
    G_small = jnp.concatenate([jnp.stack(shard_rows), repl.reshape(N_CHIPS, rr, LANE)], axis=1)[:, None]

    tok = start_chunk(chunk + [("small", 0, G_small)], "l0")
    finish_chunk(tok)
    finish_chunk(None)
    F_small = flat["small"]
    repl_buf = lax.dynamic_update_slice(jnp.zeros((N_CHIPS, 1, rr, LANE), F32), F_small[None, :, rs:, :], (me[0], 0, 0, 0))
    (repl_all,) = _gather_chips([repl_buf], "gather_small_grads")

    grads, delta, new_m, new_v = ({k: t[i] for k, t in stacked.items()} for i in range(4))
    grads["w_kvf"] = flat["w_kvf"][0, :, :Ckv]
    for k, gk in zip(small_sharded, _unpack(F_small[0, :rs], small_local)):
        grads[k] = gk
    for k, gk in zip(small_repl, _unpack(repl_all.reshape(N_CHIPS * rr, LANE), [weights[k] for k in small_repl])):
        grads[k] = gk

    delta["w_kvf"], new_m["w_kvf"], new_v["w_kvf"] = _adamw_nd(w_kvf, grads["w_kvf"], m_w_kvf, v_w_kvf, "adamw_w_kvf")
    small = small_sharded + small_repl
    packed = [_pack([t[k] for k in small], 8) for t in (weights, grads, mom1, mom2)]
    outs = _adamw(*packed, "adamw_small")
    like = [weights[k] for k in small]
    for t, o in zip((delta, new_m, new_v), outs):
        for k, a in zip(small, _unpack(o, like)):
            t[k] = a

    return (loss, grad_x, *[grads[k] for k in names], *[delta[k] for k in names],
            *[new_m[k] for k in names], *[new_v[k] for k in names])
```

```python
import functools
import math

import jax
import jax.numpy as jnp
from jax import lax
from jax.experimental import pallas as pl
from jax.experimental.pallas import tpu as pltpu
from jax.experimental.pallas import tpu_sc as plsc

F32, BF16 = jnp.float32, jnp.bfloat16
EPS = 1e-6
CHUNK = 128
HEAD = 128
LANE = 128
HALO = 16
N_CHIPS = 4
VMEM_LIMIT = 48 * 1024 * 1024
MESH = pl.DeviceIdType.MESH
ANY = pl.BlockSpec(memory_space=pl.ANY)

ADAM_LR, ADAM_B1, ADAM_B2, ADAM_EPS, ADAM_WD, ADAM_STEP = 0.001, 0.9, 0.999, 1e-08, 0.01, 10
NEG = -1e30
QK_SCALE = HEAD ** -0.5


def _cp(*sem):
    return pltpu.CompilerParams(dimension_semantics=sem, vmem_limit_bytes=VMEM_LIMIT)


def _t(dim, pref, mult=LANE):
    if dim <= pref:
        return dim
    t = (pref // mult) * mult
    while t >= mult:
        if dim % t == 0:
            return t
        t -= mult
    return dim


def _sds(shape, dtype):
    return jax.ShapeDtypeStruct(tuple(shape), dtype)


class Mat:
    def __init__(self, arr, shape, rlim, clim, block, index):
        self.arr, self.shape, self.rlim, self.clim = arr, shape, rlim, clim
        self._block, self._index = block, index

    def spec(self, tr, tc, gmap):
        assert self.rlim % tr == 0 and self.clim % tc == 0, (self.shape, self.rlim, self.clim, tr, tc)
        index = self._index(tr, tc)
        return pl.BlockSpec(self._block(tr, tc), lambda *g: index(*gmap(*g)))


def m2(arr):
    R, C = arr.shape
    return Mat(arr, (R, C), R, C, lambda tr, tc: (tr, tc), lambda tr, tc: (lambda i, j: (i, j)))


def mcs(arr):
    ns, R, Cs = arr.shape
    return Mat(arr, (R, ns * Cs), R, Cs, lambda tr, tc: (None, tr, tc),
               lambda tr, tc: (lambda i, j: (j // (Cs // tc), i, j % (Cs // tc))))


def mhalf(arr, p):
    ns, R, Cs = arr.shape
    return Mat(arr, (R, Cs), R, Cs, lambda tr, tc: (None, tr, tc), lambda tr, tc: (lambda i, j: (p, i, j)))


def mgc(arr, l):
    ns, L, R, Cs = arr.shape
    return Mat(arr, (R, ns * Cs), R, Cs, lambda tr, tc: (None, None, tr, tc),
               lambda tr, tc: (lambda i, j: (j // (Cs // tc), l, i, j % (Cs // tc))))


def mgr(arr, l):
    ns, L, Rs, C = arr.shape
    return Mat(arr, (ns * Rs, C), Rs, C, lambda tr, tc: (None, None, tr, tc),
               lambda tr, tc: (lambda i, j: (i // (Rs // tr), l, i % (Rs // tr), j)))


_DIMS = {"nn": ((1,), (0,)), "nt": ((1,), (1,)), "tn": ((0,), (0,))}


def _tokens(after):
    if after is None:
        return []
    return list(after) if isinstance(after, (list, tuple)) else [after]


def _mm(a, b, mode, out, *, res=None, alias=False, after=None, tm=512, tn=512, tk=4096, name):
    if mode == "tn":
        (K, M), (K2, N) = a.shape, b.shape
        alim_m, alim_k, blim_k, blim_n = a.clim, a.rlim, b.rlim, b.clim
    elif mode == "nt":
        (M, K), (N, K2) = a.shape, b.shape
        alim_m, alim_k, blim_k, blim_n = a.rlim, a.clim, b.clim, b.rlim
    else:
        (M, K), (K2, N) = a.shape, b.shape
        alim_m, alim_k, blim_k, blim_n = a.rlim, a.clim, b.rlim, b.clim
    assert K == K2 and out.shape == (M, N), (name, a.shape, b.shape, out.shape)
    tm = _t(math.gcd(alim_m, out.rlim), tm)
    tn = _t(math.gcd(blim_n, out.clim), tn)
    tk = _t(math.gcd(alim_k, blim_k), tk)
    grid = (M // tm, N // tn, K // tk)
    nk = grid[2]
    if mode == "tn":
        a_spec = a.spec(tk, tm, lambda i, j, k: (k, i))
    else:
        a_spec = a.spec(tm, tk, lambda i, j, k: (i, k))
    if mode == "nt":
        b_spec = b.spec(tn, tk, lambda i, j, k: (j, k))
    else:
        b_spec = b.spec(tk, tn, lambda i, j, k: (k, j))
    o_spec = out.spec(tm, tn, lambda i, j, k: (i, j))
    operands, in_specs = [a.arr, b.arr], [a_spec, b_spec]
    if res is not None:
        operands.append(res.arr)
        in_specs.append(res.spec(tm, tn, lambda i, j, k: (i, j)))
    aliases = {}
    if alias:
        aliases = {len(operands): 0}
        operands.append(out.arr)
        in_specs.append(ANY)
    tokens = _tokens(after)
    operands += tokens
    in_specs += [ANY] * len(tokens)
    dims = (_DIMS[mode], ((), ()))
    has_res = res is not None

    def body(*refs):
        a_ref, b_ref = refs[0], refs[1]
        res_ref = refs[2] if has_res else None
        n_in = 2 + has_res + alias + len(tokens)
        o_ref = refs[n_in]
        p = lax.dot_general(a_ref[...].astype(BF16), b_ref[...].astype(BF16), dims, preferred_element_type=F32)

        def finish(v):
            if has_res:
                v = v + res_ref[...].astype(F32)
            o_ref[...] = v.astype(o_ref.dtype)

        if nk == 1:
            finish(p)
        else:
            acc = refs[n_in + 1]
            k = pl.program_id(2)

            @pl.when(k == 0)
            def _():
                acc[...] = p

            @pl.when(k > 0)
            def _():
                acc[...] += p

            @pl.when(k == nk - 1)
            def _():
                finish(acc[...])

    return pl.pallas_call(
        body, name=name, grid=grid, in_specs=in_specs, out_specs=o_spec,
        out_shape=_sds(out.arr.shape, out.arr.dtype),
        scratch_shapes=[pltpu.VMEM((tm, tn), F32)] if nk > 1 else [],
        input_output_aliases=aliases,
        compiler_params=_cp("parallel", "parallel", "arbitrary"),
    )(*operands)


def _rms_fwd(x, g, name):
    S, D = x.shape
    tr = _t(S, 512)

    def body(x_ref, g_ref, o_ref):
        xv = x_ref[...]
        r = lax.rsqrt(jnp.mean(xv * xv, axis=-1, keepdims=True) + EPS)
        o_ref[...] = ((xv * r) * g_ref[...]).astype(o_ref.dtype)

    return pl.pallas_call(
        body, name=name, grid=(S // tr,),
        in_specs=[pl.BlockSpec((tr, D), lambda i: (i, 0)), pl.BlockSpec((1, D), lambda i: (0, 0))],
        out_specs=pl.BlockSpec((tr, D), lambda i: (i, 0)), out_shape=_sds((S, D), BF16),
        compiler_params=_cp("parallel"),
    )(x, g)


def _rms_bwd(dy, x, g, dres, name):
    S, D = x.shape
    tr = _t(S, 512)

    def body(dy_ref, x_ref, g_ref, dres_ref, dx_ref, dxb_ref, dg_ref):
        xv = x_ref[...]
        dyv = dy_ref[...].astype(F32)
        r = lax.rsqrt(jnp.mean(xv * xv, axis=-1, keepdims=True) + EPS)
        xh = xv * r
        dxh = dyv * g_ref[...]
        m = jnp.mean(dxh * xh, axis=-1, keepdims=True)
        dx = dres_ref[...] + r * (dxh - xh * m)
        dx_ref[...] = dx
        dxb_ref[...] = dx.astype(dxb_ref.dtype)

        @pl.when(pl.program_id(0) == 0)
        def _():
            dg_ref[...] = jnp.zeros_like(dg_ref)

        dg_ref[...] += jnp.sum(dyv * xh, axis=0, keepdims=True)

    row = pl.BlockSpec((tr, D), lambda i: (i, 0))
    vec = pl.BlockSpec((1, D), lambda i: (0, 0))
    return pl.pallas_call(
        body, name=name, grid=(S // tr,), in_specs=[row, row, vec, row], out_specs=[row, row, vec],
        out_shape=[_sds((S, D), F32), _sds((S, D), BF16), _sds((1, D), F32)], compiler_params=_cp("arbitrary"),
    )(dy, x, g, dres)


def _final_loss(h, g, target, name):
    S, D = h.shape
    tr = _t(S, 512)

    def body(x_ref, g_ref, t_ref, loss_ref, dx_ref, dxb_ref, dg_ref):
        xv = x_ref[...]
        r = lax.rsqrt(jnp.mean(xv * xv, axis=-1, keepdims=True) + EPS)
        xh = xv * r
        err = xh * g_ref[...] - t_ref[...]
        part = 0.5 * jnp.sum(jnp.mean(err * err, axis=-1, keepdims=True), axis=0, keepdims=True)
        dyv = err * (1.0 / D)
        dxh = dyv * g_ref[...]
        m = jnp.mean(dxh * xh, axis=-1, keepdims=True)
        dx = r * (dxh - xh * m)
        dx_ref[...] = dx
        dxb_ref[...] = dx.astype(dxb_ref.dtype)

        @pl.when(pl.program_id(0) == 0)
        def _():
            dg_ref[...] = jnp.zeros_like(dg_ref)
            loss_ref[...] = jnp.zeros_like(loss_ref)

        dg_ref[...] += jnp.sum(dyv * xh, axis=0, keepdims=True)
        loss_ref[...] += part

    row = pl.BlockSpec((tr, D), lambda i: (i, 0))
    vec = pl.BlockSpec((1, D), lambda i: (0, 0))
    one = pl.BlockSpec((1, 1), lambda i: (0, 0))
    return pl.pallas_call(
        body, name=name, grid=(S // tr,), in_specs=[row, vec, row], out_specs=[one, row, row, vec],
        out_shape=[_sds((1, 1), F32), _sds((S, D), F32), _sds((S, D), BF16), _sds((1, D), F32)], compiler_params=_cp("arbitrary"),
    )(h, g, target)


_RSQRT2 = 0.7071067811865476
_RSQRT2PI = 0.3989422804014327


def _gelu(x):
    return 0.5 * x * (1.0 + lax.erf(x * _RSQRT2))


def _gelu_grad(x):
    return 0.5 * (1.0 + lax.erf(x * _RSQRT2)) + x * (jnp.exp(-0.5 * x * x) * _RSQRT2PI)


def _sgu_fwd(zpre, wm, bcol, vnorm, name):
    S, W2 = zpre.shape
    W = W2 // 2
    G = W // CHUNK

    def body(z_ref, wm_ref, b_ref, vn_ref, o_ref):
        zp = z_ref[...].astype(F32)
        u = _gelu(zp[:, :W])
        v = _gelu(zp[:, W:])
        rv = lax.rsqrt(jnp.mean(v * v, axis=-1, keepdims=True) + EPS)
        vn = ((v * rv) * vn_ref[...]).astype(BF16)
        for g in range(G):
            sl = slice(g * CHUNK, (g + 1) * CHUNK)
            mixed = jnp.dot(wm_ref[g], vn[:, sl], preferred_element_type=F32) + b_ref[g]
            o_ref[:, sl] = (u[:, sl] * mixed).astype(o_ref.dtype)

    return pl.pallas_call(
        body, name=name, grid=(S // CHUNK,),
        in_specs=[pl.BlockSpec((CHUNK, W2), lambda i: (i, 0)),
                  pl.BlockSpec((G, CHUNK, CHUNK), lambda i: (0, 0, 0)),
                  pl.BlockSpec((G, CHUNK, 1), lambda i: (0, 0, 0)),
                  pl.BlockSpec((1, W), lambda i: (0, 0))],
        out_specs=pl.BlockSpec((CHUNK, W), lambda i: (i, 0)), out_shape=_sds((S, W), BF16),
        compiler_params=_cp("parallel"),
    )(zpre, wm, bcol, vnorm)


def _sgu_bwd(zpre, dgated, wm, wmt, bcol, vnorm, name):
    S, W2 = zpre.shape
    W = W2 // 2
    G = W // CHUNK

    def body(z_ref, dg_ref, wm_ref, wmt_ref, b_ref, vn_ref, dz_ref, dws_ref, dbs_ref, dvn_ref):
        @pl.when(pl.program_id(0) == 0)
        def _():
            dws_ref[...] = jnp.zeros_like(dws_ref)
            dbs_ref[...] = jnp.zeros_like(dbs_ref)
            dvn_ref[...] = jnp.zeros_like(dvn_ref)

        zp = z_ref[...].astype(F32)
        zu, zv = zp[:, :W], zp[:, W:]
        u = _gelu(zu)
        v = _gelu(zv)
        rv = lax.rsqrt(jnp.mean(v * v, axis=-1, keepdims=True) + EPS)
        vh = v * rv
        vn = (vh * vn_ref[...]).astype(BF16)
        dgv = dg_ref[...].astype(F32)
        du_parts, dvn_parts = [], []
        for g in range(G):
            sl = slice(g * CHUNK, (g + 1) * CHUNK)
            vg = vn[:, sl]
            mixed = jnp.dot(wm_ref[g], vg, preferred_element_type=F32) + b_ref[g]
            dgg = dgv[:, sl]
            du_parts.append(dgg * mixed)
            dmixed = dgg * u[:, sl]
            dbs_ref[g] += jnp.sum(dmixed, axis=1, keepdims=True)
            dmb = dmixed.astype(BF16)
            dws_ref[g] += lax.dot_general(dmb, vg, (_DIMS["nt"], ((), ())), preferred_element_type=F32)
            dvn_parts.append(jnp.dot(wmt_ref[g], dmb, preferred_element_type=F32))
        du = jnp.concatenate(du_parts, axis=1)
        dvn = jnp.concatenate(dvn_parts, axis=1)
        dvn_ref[...] += jnp.sum(dvn * vh, axis=0, keepdims=True)
        dvh = dvn * vn_ref[...]
        dv = rv * (dvh - vh * jnp.mean(dvh * vh, axis=-1, keepdims=True))
        dz_ref[:, :W] = (du * _gelu_grad(zu)).astype(dz_ref.dtype)
        dz_ref[:, W:] = (dv * _gelu_grad(zv)).astype(dz_ref.dtype)

    full3 = lambda shape: pl.BlockSpec(shape, lambda i: (0, 0, 0))
    return pl.pallas_call(
        body, name=name, grid=(S // CHUNK,),
        in_specs=[pl.BlockSpec((CHUNK, W2), lambda i: (i, 0)), pl.BlockSpec((CHUNK, W), lambda i: (i, 0)),
                  full3((G, CHUNK, CHUNK)), full3((G, CHUNK, CHUNK)), full3((G, CHUNK, 1)),
                  pl.BlockSpec((1, W), lambda i: (0, 0))],
        out_specs=[pl.BlockSpec((CHUNK, W2), lambda i: (i, 0)), full3((G, CHUNK, CHUNK)), full3((G, CHUNK, 1)),
                   pl.BlockSpec((1, W), lambda i: (0, 0))],
        out_shape=[_sds((S, W2), BF16), _sds((G, CHUNK, CHUNK), F32), _sds((G, CHUNK, 1), F32), _sds((1, W), F32)],
        compiler_params=_cp("arbitrary"),
    )(zpre, dgated, wm, wmt, bcol, vnorm)


def _conv_taps(h_ref, half, r0, R, tc):
    if r0 == 0:
        xe = jnp.concatenate([jnp.zeros((HALO, tc), F32), h_ref[half, 0:R, :].astype(F32)], axis=0)
    else:
        xe = h_ref[half, r0 - HALO:r0 + R, :].astype(F32)
    return xe[HALO:], pltpu.roll(xe, 1, 0)[HALO:], pltpu.roll(xe, 2, 0)[HALO:]


def _conv_apply(taps, w, b):
    x0, x1, x2 = taps
    return x2 * w[0:1] + x1 * w[1:2] + x0 * w[2:3] + b


def _convgate_fwd(hup, cw, cb, name):
    _, S, F = hup.shape
    tc = _t(F, 256)
    R = _t(S, 512)

    def body(h_ref, w_ref, b_ref, o_ref):
        for r0 in range(0, S, R):
            gate = _conv_apply(_conv_taps(h_ref, 0, r0, R, tc), w_ref[0], b_ref[0])
            val = _conv_apply(_conv_taps(h_ref, 1, r0, R, tc), w_ref[1], b_ref[1])
            o_ref[r0:r0 + R, :] = (gate * jax.nn.sigmoid(gate) * val).astype(o_ref.dtype)

    return pl.pallas_call(
        body, name=name, grid=(F // tc,),
        in_specs=[pl.BlockSpec((2, S, tc), lambda j: (0, 0, j)), pl.BlockSpec((2, 3, tc), lambda j: (0, 0, j)),
                  pl.BlockSpec((2, 1, tc), lambda j: (0, 0, j))],
        out_specs=pl.BlockSpec((S, tc), lambda j: (0, j)), out_shape=_sds((S, F), BF16),
        compiler_params=_cp("parallel"),
    )(hup, cw, cb)


def _convgate_bwd(hup, dact, cw, cb, name):
    _, S, F = hup.shape
    tc = _t(F, 256)
    R = _t(S, 512)

    def body(h_ref, da_ref, w_ref, b_ref, dh_ref, dw_ref, db_ref, dhc):
        dhc[:, S:S + HALO, :] = jnp.zeros((2, HALO, tc), F32)
        dw_acc = [[jnp.zeros((1, tc), F32) for _ in range(3)] for _ in range(2)]
        db_acc = [jnp.zeros((1, tc), F32) for _ in range(2)]
        for r0 in range(0, S, R):
            taps = [_conv_taps(h_ref, p, r0, R, tc) for p in range(2)]
            gate = _conv_apply(taps[0], w_ref[0], b_ref[0])
            val = _conv_apply(taps[1], w_ref[1], b_ref[1])
            da = da_ref[r0:r0 + R, :].astype(F32)
            sg = jax.nn.sigmoid(gate)
            d = [da * val * (sg * (1.0 + gate * (1.0 - sg))), da * (gate * sg)]
            for p in range(2):
                dhc[p, r0:r0 + R, :] = d[p]
                db_acc[p] = db_acc[p] + jnp.sum(d[p], axis=0, keepdims=True)
                for k in range(3):
                    dw_acc[p][k] = dw_acc[p][k] + jnp.sum(d[p] * taps[p][2 - k], axis=0, keepdims=True)
        for p in range(2):
            db_ref[p] = db_acc[p]
            dw_ref[p] = jnp.concatenate(dw_acc[p], axis=0)
            w = w_ref[p]
            for r0 in range(0, S, R):
                de = dhc[p, r0:r0 + R + HALO, :]
                d1 = pltpu.roll(de, R + HALO - 1, 0)[:R]
                d2 = pltpu.roll(de, R + HALO - 2, 0)[:R]
                dh_ref[p, r0:r0 + R, :] = (de[:R] * w[2:3] + d1 * w[1:2] + d2 * w[0:1]).astype(dh_ref.dtype)

    blk = lambda rows: pl.BlockSpec((2, rows, tc), lambda j: (0, 0, j))
    return pl.pallas_call(
        body, name=name, grid=(F // tc,),
        in_specs=[blk(S), pl.BlockSpec((S, tc), lambda j: (0, j)), blk(3), blk(1)],
        out_specs=[blk(S), blk(3), blk(1)],
        out_shape=[_sds((2, S, F), BF16), _sds((2, 3, F), F32), _sds((2, 1, F), F32)],
        scratch_shapes=[pltpu.VMEM((2, S + HALO, tc), F32)],
        compiler_params=_cp("parallel"),
    )(hup, dact, cw, cb)


def _headnorm_fwd(x3, part, gain, name):
    _, S, W = x3.shape
    tr = _t(S, 512)

    def body(x_ref, g_ref, o_ref):
        xv = x_ref[...].astype(F32)
        for h in range(W // HEAD):
            sl = slice(h * HEAD, (h + 1) * HEAD)
            xh = xv[:, sl]
            r = lax.rsqrt(jnp.mean(xh * xh, axis=-1, keepdims=True) + EPS)
            o_ref[:, sl] = ((xh * r) * g_ref[...]).astype(o_ref.dtype)

    return pl.pallas_call(
        body, name=name, grid=(S // tr,),
        in_specs=[pl.BlockSpec((None, tr, W), lambda i: (part, i, 0)), pl.BlockSpec((1, HEAD), lambda i: (0, 0))],
        out_specs=pl.BlockSpec((tr, W), lambda i: (i, 0)), out_shape=_sds((S, W), BF16),
        compiler_params=_cp("parallel"),
    )(x3, gain)


def _headnorm_bwd(dys, x3, gain, passes, name):
    _, S, W = x3.shape
    tr = _t(S, 256)
    nd, npass = len(dys), len(passes)

    def body(*refs):
        dy_refs = refs[:nd]
        x_ref, g_ref = refs[nd], refs[nd + 1]
        p_refs = refs[nd + 2:nd + 2 + npass]
        o_ref, dg_ref = refs[nd + 2 + npass], refs[nd + 3 + npass]

        @pl.when(pl.program_id(0) == 0)
        def _():
            dg_ref[...] = jnp.zeros_like(dg_ref)

        xv = x_ref[...].astype(F32)
        dyv = dy_refs[0][...].astype(F32)
        for r in dy_refs[1:]:
            dyv = dyv + r[...].astype(F32)
        dg = jnp.zeros((1, HEAD), F32)
        for h in range(W // HEAD):
            sl = slice(h * HEAD, (h + 1) * HEAD)
            xh = xv[:, sl]
            r = lax.rsqrt(jnp.mean(xh * xh, axis=-1, keepdims=True) + EPS)
            xh = xh * r
            dyh = dyv[:, sl]
            dg = dg + jnp.sum(dyh * xh, axis=0, keepdims=True)
            dxh = dyh * g_ref[...]
            o_ref[0, :, sl] = (r * (dxh - xh * jnp.mean(dxh * xh, axis=-1, keepdims=True))).astype(o_ref.dtype)
        dg_ref[...] += dg
        pv = p_refs[0][...].astype(F32)
        for r in p_refs[1:]:
            pv = pv + r[...].astype(F32)
        o_ref[1] = pv.astype(o_ref.dtype)

    row = pl.BlockSpec((tr, W), lambda i: (i, 0))
    vec = pl.BlockSpec((1, HEAD), lambda i: (0, 0))
    return pl.pallas_call(
        body, name=name, grid=(S // tr,),
        in_specs=[row] * nd + [pl.BlockSpec((None, tr, W), lambda i: (0, i, 0)), vec] + [row] * npass,
        out_specs=[pl.BlockSpec((2, tr, W), lambda i: (0, i, 0)), vec],
        out_shape=[_sds((2, S, W), BF16), _sds((1, HEAD), F32)], compiler_params=_cp("arbitrary"),
    )(*dys, x3, gain, *passes)


def _gate_fwd(o, qg, name):
    S, W = o.shape
    tr = _t(S, 512)

    def body(o_ref, g_ref, y_ref):
        y_ref[...] = (o_ref[...].astype(F32) * jax.nn.sigmoid(g_ref[...].astype(F32))).astype(y_ref.dtype)

    row = pl.BlockSpec((tr, W), lambda i: (i, 0))
    return pl.pallas_call(
        body, name=name, grid=(S // tr,), in_specs=[row, pl.BlockSpec((None, tr, W), lambda i: (1, i, 0))],
        out_specs=row, out_shape=_sds((S, W), BF16), compiler_params=_cp("parallel"),
    )(o, qg)


def _gate_bwd(dog, o, qg, name):
    S, W = o.shape
    H = W // HEAD
    tr = _t(S, 512)

    def body(dy_ref, o_ref, g_ref, do_ref, dg_ref, dl_ref):
        sg = jax.nn.sigmoid(g_ref[...].astype(F32))
        dy = dy_ref[...].astype(F32)
        ov = o_ref[...].astype(F32)
        dob = (dy * sg).astype(do_ref.dtype)
        do_ref[...] = dob
        dg_ref[...] = (dy * ov * (sg * (1.0 - sg))).astype(dg_ref.dtype)
        prod = dob.astype(F32) * ov
        for h in range(H):
            dl_ref[h] = jnp.sum(prod[:, h * HEAD:(h + 1) * HEAD], axis=-1, keepdims=True)

    row = pl.BlockSpec((tr, W), lambda i: (i, 0))
    return pl.pallas_call(
        body, name=name, grid=(S // tr,), in_specs=[row, row, pl.BlockSpec((None, tr, W), lambda i: (1, i, 0))],
        out_specs=[row, row, pl.BlockSpec((H, tr, 1), lambda i: (0, i, 0))],
        out_shape=[_sds((S, W), BF16), _sds((S, W), BF16), _sds((H, S, 1), F32)], compiler_params=_cp("parallel"),
    )(dog, o, qg)


def _logf_cumsum(fpre, bf, name):
    S, C = fpre.shape
    n = S // CHUNK

    def body(f_ref, b_ref, c_ref, carry):
        @pl.when(pl.program_id(0) == 0)
        def _():
            carry[...] = jnp.zeros_like(carry)

        lf = jax.nn.log_sigmoid(f_ref[...] + b_ref[...])
        tri = (lax.broadcasted_iota(jnp.int32, (CHUNK, CHUNK), 0)
               >= lax.broadcasted_iota(jnp.int32, (CHUNK, CHUNK), 1)).astype(F32)
        c_ref[...] = jnp.dot(tri, lf, preferred_element_type=F32, precision=lax.Precision.HIGHEST) + carry[...]
        carry[...] += jnp.sum(lf, axis=0, keepdims=True)

    return pl.pallas_call(
        body, name=name, grid=(n,),
        in_specs=[pl.BlockSpec((CHUNK, C), lambda i: (i, 0)), pl.BlockSpec((1, C), lambda i: (0, 0))],
        out_specs=pl.BlockSpec((CHUNK, C), lambda i: (i, 0)), out_shape=_sds((S, C), F32),
        scratch_shapes=[pltpu.VMEM((1, C), F32)], compiler_params=_cp("arbitrary"),
    )(fpre, bf)


def _logf_cumsum_bwd(dc, fpre, bf, name):
    S, C = fpre.shape
    n = S // CHUNK

    def body(dc_ref, f_ref, b_ref, df_ref, db_ref, carry):
        @pl.when(pl.program_id(0) == 0)
        def _():
            carry[...] = jnp.zeros_like(carry)
            db_ref[...] = jnp.zeros_like(db_ref)

        dcv = dc_ref[...]
        tri = (lax.broadcasted_iota(jnp.int32, (CHUNK, CHUNK), 0)
               <= lax.broadcasted_iota(jnp.int32, (CHUNK, CHUNK), 1)).astype(F32)
        dlf = jnp.dot(tri, dcv, preferred_element_type=F32, precision=lax.Precision.HIGHEST) + carry[...]
        carry[...] += jnp.sum(dcv, axis=0, keepdims=True)
        df = dlf * jax.nn.sigmoid(-(f_ref[...] + b_ref[...]))
        df_ref[...] = df.astype(df_ref.dtype)
        db_ref[...] += jnp.sum(df, axis=0, keepdims=True)

    rev = pl.BlockSpec((CHUNK, C), lambda i: (n - 1 - i, 0))
    vec = pl.BlockSpec((1, C), lambda i: (0, 0))
    return pl.pallas_call(
        body, name=name, grid=(n,), in_specs=[rev, rev, vec], out_specs=[rev, vec],
        out_shape=[_sds((S, C), BF16), _sds((1, C), F32)],
        scratch_shapes=[pltpu.VMEM((1, C), F32)], compiler_params=_cp("arbitrary"),
    )(dc, fpre, bf)


def _attn_tiles(S):
    return _t(S, 1024), _t(S, 512)


def _scores(q, k, ck, off, tq, tk, masked):
    s = lax.dot_general(q, k, (_DIMS["nt"], ((), ())), preferred_element_type=F32) - ck
    if masked:
        d = lax.broadcasted_iota(jnp.int32, (tq, tk), 1) - lax.broadcasted_iota(jnp.int32, (tq, tk), 0)
        s = jnp.where(d <= off, s, NEG)
    return s


def _attn_fwd(qn, kn, kv, crow, name):
    S, W = qn.shape
    H = W // HEAD
    tq, tk = _attn_tiles(S)
    nq, nk = S // tq, S // tk
    last = lambda i: ((i + 1) * tq - 1) // tk
    sub = _t(tq, 256)

    def body(q_ref, k_ref, v_ref, ck_ref, o_ref, lse_ref, m_sc, l_sc, acc_sc):
        qi, kj = pl.program_id(1), pl.program_id(2)

        @pl.when(kj == 0)
        def _():
            m_sc[...] = jnp.full_like(m_sc, NEG)
            l_sc[...] = jnp.zeros_like(l_sc)
            acc_sc[...] = jnp.zeros_like(acc_sc)

        def step(masked):
            k, ck, v = k_ref[...], ck_ref[...], v_ref[...]
            for r0 in range(0, tq, sub):
                rows = slice(r0, r0 + sub)
                s = _scores(q_ref[rows, :], k, ck, qi * tq + r0 - kj * tk, sub, tk, masked)
                m_old = m_sc[rows, :]
                m_new = jnp.maximum(m_old, jnp.max(s, axis=-1, keepdims=True))
                alpha = jnp.exp(m_old - m_new)
                p = jnp.exp(s - m_new)
                l_sc[rows, :] = alpha * l_sc[rows, :] + jnp.sum(p, axis=-1, keepdims=True)
                acc_sc[rows, :] = alpha * acc_sc[rows, :] + jnp.dot(p.astype(BF16), v, preferred_element_type=F32)
                m_sc[rows, :] = m_new

        @pl.when(kj <= last(qi))
        def _():
            step(True)

        @pl.when(kj == nk - 1)
        def _():
            o_ref[...] = (acc_sc[...] / l_sc[...]).astype(o_ref.dtype)
            lse_ref[...] = m_sc[...] + jnp.log(l_sc[...])

    return pl.pallas_call(
        body, name=name, grid=(H, nq, nk),
        in_specs=[pl.BlockSpec((tq, HEAD), lambda h, i, j: (i, h)),
                  pl.BlockSpec((tk, HEAD), lambda h, i, j: (jnp.minimum(j, last(i)), h)),
                  pl.BlockSpec((None, tk, HEAD), lambda h, i, j: (1, jnp.minimum(j, last(i)), h)),
                  pl.BlockSpec((None, 1, tk), lambda h, i, j: (h, 0, jnp.minimum(j, last(i))))],
        out_specs=[pl.BlockSpec((tq, HEAD), lambda h, i, j: (i, h)),
                   pl.BlockSpec((None, tq, 1), lambda h, i, j: (h, i, 0))],
        out_shape=[_sds((S, W), BF16), _sds((H, S, 1), F32)],
        scratch_shapes=[pltpu.VMEM((tq, 1), F32), pltpu.VMEM((tq, 1), F32), pltpu.VMEM((tq, HEAD), F32)],
        compiler_params=_cp("parallel", "parallel", "arbitrary"),
    )(qn, kn, kv, crow)


def _attn_bwd(qn, kn, kv, crow, do, lse, delta, name):
    S, W = qn.shape
    H = W // HEAD
    tq, tk = _attn_tiles(S)
    nq, nk = S // tq, S // tk
    first = lambda j: (j * tk) // tq

    def body(q_ref, k_ref, v_ref, ck_ref, do_ref, lse_ref, dl_ref, dq_ref, dr_ref, dk_ref, dv_ref, dc_ref, dk_sc, dv_sc, dc_sc):
        kj, qi = pl.program_id(1), pl.program_id(2)

        @pl.when(jnp.logical_and(kj == 0, qi == 0))
        def _():
            dq_ref[...] = jnp.zeros_like(dq_ref)
            dr_ref[...] = jnp.zeros_like(dr_ref)

        @pl.when(qi == 0)
        def _():
            dk_sc[...] = jnp.zeros_like(dk_sc)
            dv_sc[...] = jnp.zeros_like(dv_sc)
            dc_sc[...] = jnp.zeros_like(dc_sc)

        def step(masked):
            q, k, dov = q_ref[...], k_ref[...], do_ref[...]
            s = _scores(q, k, ck_ref[...], qi * tq - kj * tk, tq, tk, masked)
            p = jnp.exp(s - lse_ref[...])
            dp = lax.dot_general(dov, v_ref[...], (_DIMS["nt"], ((), ())), preferred_element_type=F32)
            ds = p * (dp - dl_ref[...])
            dsb = ds.astype(BF16)
            dc_sc[...] += jnp.sum(ds, axis=0, keepdims=True)
            dv_sc[...] += lax.dot_general(p.astype(BF16), dov, (_DIMS["tn"], ((), ())), preferred_element_type=F32)
            dk_sc[...] += lax.dot_general(dsb, q, (_DIMS["tn"], ((), ())), preferred_element_type=F32)
            rows = pl.ds(pl.multiple_of(qi * tq, tq), tq)
            dq_ref[rows, :] += jnp.dot(dsb, k, preferred_element_type=F32)
            dr_ref[rows, :] += jnp.sum(ds, axis=1, keepdims=True)

        below = (kj + 1) * tk - 1 <= qi * tq

        @pl.when(below)
        def _():
            step(False)

        @pl.when(jnp.logical_and(qi >= first(kj), jnp.logical_not(below)))
        def _():
            step(True)

        @pl.when(qi == nq - 1)
        def _():
            dk_ref[...] = dk_sc[...]
            dv_ref[...] = dv_sc[...]
            dc_ref[...] = dc_sc[...]

    qblk = pl.BlockSpec((tq, HEAD), lambda h, j, i: (jnp.maximum(i, first(j)), h))
    qcol = pl.BlockSpec((None, tq, 1), lambda h, j, i: (h, jnp.maximum(i, first(j)), 0))
    kblk = pl.BlockSpec((tk, HEAD), lambda h, j, i: (j, h))
    krow = pl.BlockSpec((None, 1, tk), lambda h, j, i: (h, 0, j))
    return pl.pallas_call(
        body, name=name, grid=(H, nk, nq),
        in_specs=[qblk, kblk, pl.BlockSpec((None, tk, HEAD), lambda h, j, i: (1, j, h)), krow, qblk, qcol, qcol],
        out_specs=[pl.BlockSpec((S, HEAD), lambda h, j, i: (0, h)), pl.BlockSpec((None, S, 1), lambda h, j, i: (h, 0, 0)),
                   kblk, kblk, krow],
        out_shape=[_sds((S, W), F32), _sds((H, S, 1), F32), _sds((S, W), F32), _sds((S, W), F32), _sds((H, 1, S), F32)],
        scratch_shapes=[pltpu.VMEM((tk, HEAD), F32), pltpu.VMEM((tk, HEAD), F32), pltpu.VMEM((1, tk), F32)],
        compiler_params=_cp("parallel", "arbitrary", "arbitrary"),
    )(qn, kn, kv, crow, do, lse, delta)


def _adamw_math(w, g, m, v):
    c1 = 1.0 - ADAM_B1 ** ADAM_STEP
    c2 = 1.0 - ADAM_B2 ** ADAM_STEP
    nm = ADAM_B1 * m + (1.0 - ADAM_B1) * g
    nv = ADAM_B2 * v + (1.0 - ADAM_B2) * (g * g)
    return -ADAM_LR * ((nm / c1) / (jnp.sqrt(nv / c2) + ADAM_EPS) + ADAM_WD * w), nm, nv


def _adamw_layer(w, g, m, v, layer, prev, name):
    L, R, C = w.shape
    tr = _t(R, max(8, (1 << 19) // C), 8)

    def body(w_ref, g_ref, m_ref, v_ref, *rest):
        go_ref, d_ref, nm_ref, nv_ref = rest[4:]
        gv = g_ref[...]
        go_ref[...] = gv
        d_ref[...], nm_ref[...], nv_ref[...] = _adamw_math(w_ref[...], gv, m_ref[...], v_ref[...])

    lay = pl.BlockSpec((None, tr, C), lambda i: (layer, i, 0))
    return pl.pallas_call(
        body, name=name, grid=(R // tr,),
        in_specs=[lay, pl.BlockSpec((None, tr, C), lambda i: (0, i, 0)), lay, lay] + [ANY] * 4, out_specs=[lay] * 4,
        out_shape=[_sds((L, R, C), F32)] * 4, input_output_aliases={4: 0, 5: 1, 6: 2, 7: 3},
        compiler_params=_cp("parallel"),
    )(w, g, m, v, *prev)


def _adamw(w, g, m, v, name):
    R, C = w.shape
    tr = _t(R, max(8, (1 << 19) // max(C, 1)), 8)

    def body(w_ref, g_ref, m_ref, v_ref, d_ref, nm_ref, nv_ref):
        d_ref[...], nm_ref[...], nv_ref[...] = _adamw_math(w_ref[...], g_ref[...], m_ref[...], v_ref[...])

    blk = pl.BlockSpec((tr, C), lambda i: (i, 0))
    return pl.pallas_call(
        body, name=name, grid=(R // tr,), in_specs=[blk] * 4, out_specs=[blk] * 3,
        out_shape=[_sds((R, C), F32)] * 3, compiler_params=_cp("parallel"),
    )(w, g, m, v)


def _place():
    x, y, c = lax.axis_index("x"), lax.axis_index("y"), lax.axis_index("c")
    chips = [(1 - x, y), (x, 1 - y), (1 - x, 1 - y)]
    return x, y, c, chips


def _place_part(part, me, dtype, name, layer=None, after=None):
    L, R, C = part.shape
    first, L = (0, L) if layer is None else (layer, 1)
    tr = _t(R, max(16, (1 << 20) // C), 16)

    def body(me_ref, x_ref, *rest):
        o_ref = rest[-1]
        o_ref[...] = x_ref[...].astype(o_ref.dtype)

    extra = _tokens(after)
    return pl.pallas_call(
        body, name=name,
        grid_spec=pltpu.PrefetchScalarGridSpec(
            num_scalar_prefetch=1, grid=(L, R // tr),
            in_specs=[pl.BlockSpec((None, tr, C), lambda l, r, m: (first + l, r, 0))] + [ANY] * len(extra),
            out_specs=pl.BlockSpec((None, None, tr, C), lambda l, r, m: (m[0], l, r, 0))),
        out_shape=_sds((N_CHIPS, L, R, C), dtype), compiler_params=_cp("parallel", "parallel"),
    )(me, part, *extra)


def _gather_chips_async(bufs, collective_id, name):
    n = len(bufs)
    refs = [jax.new_ref(b, memory_space=pltpu.MemorySpace.HBM) for b in bufs]

    @pl.kernel(mesh=plsc.ScalarSubcoreMesh(axis_name="seq", num_cores=1), name=name,
               scratch_types=(pltpu.SemaphoreType.DMA((n, 6)), pltpu.SemaphoreType.DMA((n, 6))),
               compiler_params=pltpu.CompilerParams(collective_id=collective_id))
    def launch(send_sems, recv_sems):
        x, y, c, chips = _place()
        me = 2 * x + y
        sibling = (x, y, 1 - c)
        barrier = pltpu.get_barrier_semaphore()
        for peer in [(*chip, c) for chip in chips] + [sibling]:
            pl.semaphore_signal(barrier, inc=1, device_id=peer, device_id_type=MESH)
        pl.semaphore_wait(barrier, len(chips) + 1)

        def copy(i, k, chip_index, core, to):
            h = refs[i].shape[2] // 2
            rows = refs[i].at[chip_index, :, pl.ds(core * h, h), :]
            return pltpu.make_async_remote_copy(
                src_ref=rows, dst_ref=rows, send_sem=send_sems.at[i, k], recv_sem=recv_sems.at[i, k],
                device_id=to, device_id_type=MESH)

        sent = []
        for i in range(n):
            for j, chip in enumerate(chips):
                cp = copy(i, j, me, c, (*chip, c))
                cp.start()
                sent.append(cp)
        for i in range(n):
            for j, chip in enumerate(chips):
                idx = 2 * chip[0] + chip[1]
                copy(i, j, idx, c, (x, y, c)).wait_recv()
                fw = copy(i, 3 + j, idx, c, sibling)
                fw.start()
                sent.append(fw)
        for i in range(n):
            for j, chip in enumerate(chips):
                copy(i, 3 + j, 2 * chip[0] + chip[1], 1 - c, (x, y, c)).wait_recv()
        for cp in sent:
            cp.wait_send()

    launch()
    return [r[...] for r in refs]


def _gather_chips(bufs, name):
    n = len(bufs)

    def body(*refs):
        outs = refs[n:2 * n]
        send_sems, recv_sems = refs[2 * n:]
        x, y, c, chips = _place()
        me = 2 * x + y
        sibling = (x, y, 1 - c)

        def copy(i, k, chip_index, core, to):
            h = outs[i].shape[2] // 2
            rows = outs[i].at[chip_index, :, pl.ds(core * h, h), :]
            return pltpu.make_async_remote_copy(
                src_ref=rows, dst_ref=rows, send_sem=send_sems.at[i, k], recv_sem=recv_sems.at[i, k],
                device_id=to, device_id_type=MESH)

        sent = []
        for i in range(n):
            for j, chip in enumerate(chips):
                cp = copy(i, j, me, c, (*chip, c))
                cp.start()
                sent.append(cp)
        for i in range(n):
            for j, chip in enumerate(chips):
                idx = 2 * chip[0] + chip[1]
                copy(i, j, idx, c, (x, y, c)).wait_recv()
                fw = copy(i, 3 + j, idx, c, sibling)
                fw.start()
                sent.append(fw)
        for i in range(n):
            for j, chip in enumerate(chips):
                copy(i, 3 + j, 2 * chip[0] + chip[1], 1 - c, (x, y, c)).wait_recv()
        for cp in sent:
            cp.wait_send()

    return pl.pallas_call(
        body, name=name, in_specs=[ANY] * n, out_specs=[ANY] * n,
        out_shape=[_sds(b.shape, b.dtype) for b in bufs],
        scratch_shapes=[pltpu.SemaphoreType.DMA((n, 6)), pltpu.SemaphoreType.DMA((n, 6))],
        input_output_aliases={i: i for i in range(n)},
        compiler_params=pltpu.CompilerParams(has_side_effects=True),
    )(*bufs)


def _swap_halves(gs, name):
    n = len(gs)

    def body(*refs):
        ins, outs = refs[:n], refs[n:2 * n]
        send_sems, recv_sems = refs[2 * n:]
        x, y, c, _ = _place()
        cps = []
        for i in range(n):
            h = ins[i].shape[2] // 2
            cp = pltpu.make_async_remote_copy(
                src_ref=ins[i].at[:, :, pl.ds((1 - c) * h, h), :], dst_ref=outs[i], send_sem=send_sems.at[i],
                recv_sem=recv_sems.at[i], device_id=(x, y, 1 - c), device_id_type=MESH)
            cp.start()
            cps.append(cp)
        for cp in cps:
            cp.wait()

    return pl.pallas_call(
        body, name=name, in_specs=[ANY] * n, out_specs=[ANY] * n,
        out_shape=[_sds(g.shape[:2] + (g.shape[2] // 2, g.shape[3]), g.dtype) for g in gs],
        scratch_shapes=[pltpu.SemaphoreType.DMA((n,)), pltpu.SemaphoreType.DMA((n,))],
        compiler_params=pltpu.CompilerParams(has_side_effects=True),
    )(*gs)


def _pair_sum(g, gs, core, out_dtype, name):
    ns, L, R, C = g.shape
    h = R // 2
    th = _t(h, max(16, (1 << 20) // C), 16)
    nb = h // th

    def body(core_ref, a_ref, b_ref, o_ref):
        o_ref[...] = (a_ref[...].astype(F32) + b_ref[...].astype(F32)).astype(o_ref.dtype)

    blk = (None, None, th, C)
    return pl.pallas_call(
        body, name=name,
        grid_spec=pltpu.PrefetchScalarGridSpec(
            num_scalar_prefetch=1, grid=(ns, L, nb),
            in_specs=[pl.BlockSpec(blk, lambda o, l, r, cr: (o, l, cr[0] * nb + r, 0)),
                      pl.BlockSpec(blk, lambda o, l, r, cr: (o, l, r, 0))],
            out_specs=pl.BlockSpec(blk, lambda o, l, r, cr: (o, l, r, 0))),
        out_shape=_sds((ns, L, h, C), out_dtype), compiler_params=_cp("parallel", "parallel", "parallel"),
    )(core, g, gs)


def _scatter_chips(ps, name):
    n = len(ps)

    def body(*refs):
        ins, outs = refs[:n], refs[n:2 * n]
        send_sems, recv_sems = refs[2 * n:]
        x, y, c, chips = _place()
        sent = []
        for i in range(n):
            for j, chip in enumerate(chips):
                cp = pltpu.make_async_remote_copy(
                    src_ref=ins[i].at[2 * chip[0] + chip[1]], dst_ref=outs[i].at[j], send_sem=send_sems.at[i, j],
                    recv_sem=recv_sems.at[i, j], device_id=(*chip, c), device_id_type=MESH)
                cp.start()
                sent.append(cp)
        for i in range(n):
            for j in range(len(chips)):
                slot = outs[i].at[j]
                pltpu.make_async_remote_copy(
                    src_ref=slot, dst_ref=slot, send_sem=send_sems.at[i, j], recv_sem=recv_sems.at[i, j],
                    device_id=(x, y, c), device_id_type=MESH).wait_recv()
        for cp in sent:
            cp.wait_send()

    return pl.pallas_call(
        body, name=name, in_specs=[ANY] * n, out_specs=[ANY] * n,
        out_shape=[_sds((N_CHIPS - 1,) + p.shape[1:], p.dtype) for p in ps],
        scratch_shapes=[pltpu.SemaphoreType.DMA((n, 3)), pltpu.SemaphoreType.DMA((n, 3))],
        compiler_params=pltpu.CompilerParams(has_side_effects=True),
    )(*ps)


def _scatter_chips_async(ps, collective_id, name):
    n = len(ps)

    def launch(*refs):
        srcs, dsts = refs[:n], refs[n:2 * n]
        send_sems, recv_sems = refs[2 * n:]
        x, y, c, chips = _place()
        barrier = pltpu.get_barrier_semaphore()
        for chip in chips:
            pl.semaphore_signal(barrier, inc=1, device_id=(*chip, c), device_id_type=MESH)
        pl.semaphore_wait(barrier, len(chips))
        sent = []
        for i in range(n):
            for j, chip in enumerate(chips):
                cp = pltpu.make_async_remote_copy(
                    src_ref=srcs[i].at[2 * chip[0] + chip[1]], dst_ref=dsts[i].at[j], send_sem=send_sems.at[i, j],
                    recv_sem=recv_sems.at[i, j], device_id=(*chip, c), device_id_type=MESH)
                cp.start()
                sent.append(cp)
        for i in range(n):
            for j in range(len(chips)):
                slot = dsts[i].at[j]
                pltpu.make_async_remote_copy(
                    src_ref=slot, dst_ref=slot, send_sem=send_sems.at[i, j], recv_sem=recv_sems.at[i, j],
                    device_id=(x, y, c), device_id_type=MESH).wait_recv()
        for cp in sent:
            cp.wait_send()

    return pl.kernel(
        launch, name=name, out_type=[_sds((N_CHIPS - 1,) + p.shape[1:], p.dtype) for p in ps],
        mesh=plsc.ScalarSubcoreMesh(axis_name="seq", num_cores=1),
        scratch_types=[pltpu.SemaphoreType.DMA((n, 3)), pltpu.SemaphoreType.DMA((n, 3))],
        compiler_params=pltpu.CompilerParams(collective_id=collective_id),
    )(*ps)


def _sum_chips(q, p, me, core, name, after=None):
    _, L, h, C = q.shape
    th = _t(h, max(16, (1 << 19) // C), 16)
    nb = h // th
    blk = (None, None, th, C)

    def body(me_ref, core_ref, p_ref, q0, q1, q2, *rest):
        o_ref = rest[-1]
        o_ref[...] = ((p_ref[...].astype(F32) + q0[...].astype(F32)) + q1[...].astype(F32)) + q2[...].astype(F32)

    extra = _tokens(after)
    return pl.pallas_call(
        body, name=name,
        grid_spec=pltpu.PrefetchScalarGridSpec(
            num_scalar_prefetch=2, grid=(L, nb),
            in_specs=[pl.BlockSpec(blk, lambda l, r, m, c: (m[0], l, r, 0))]
            + [pl.BlockSpec(blk, functools.partial(lambda k, l, r, m, c: (k, l, r, 0), k)) for k in range(N_CHIPS - 1)]
            + [ANY] * len(extra),
            out_specs=pl.BlockSpec((None, th, C), lambda l, r, m, c: (l, c[0] * nb + r, 0))),
        out_shape=_sds((L, 2 * h, C), F32), compiler_params=_cp("parallel", "parallel"),
    )(me, core, p, q, q, q, *extra)


def _sum_small(own, q, name):
    def body(p_ref, q_ref, o_ref):
        o_ref[...] = ((p_ref[...] + q_ref[0]) + q_ref[1]) + q_ref[2]

    return pl.pallas_call(body, name=name, out_shape=_sds(own.shape, F32))(own, q)


def _join_halves(bufs, name):
    n = len(bufs)

    def body(*refs):
        outs = refs[n:2 * n]
        send_sems, recv_sems = refs[2 * n:]
        x, y, c, _ = _place()
        cps = []
        for i in range(n):
            h = outs[i].shape[1] // 2
            mine = outs[i].at[:, pl.ds(c * h, h), :]
            cp = pltpu.make_async_remote_copy(
                src_ref=mine, dst_ref=mine, send_sem=send_sems.at[i], recv_sem=recv_sems.at[i],
                device_id=(x, y, 1 - c), device_id_type=MESH)
            cp.start()
            cps.append(cp)
        for cp in cps:
            cp.wait()

    return pl.pallas_call(
        body, name=name, in_specs=[ANY] * n, out_specs=[ANY] * n,
        out_shape=[_sds(b.shape, b.dtype) for b in bufs],
        scratch_shapes=[pltpu.SemaphoreType.DMA((n,)), pltpu.SemaphoreType.DMA((n,))],
        input_output_aliases={i: i for i in range(n)},
        compiler_params=pltpu.CompilerParams(has_side_effects=True),
    )(*bufs)


def _pack(arrs, rows_mult):
    flat = jnp.concatenate([a.reshape(-1).astype(F32) for a in arrs])
    rows = -(-flat.size // LANE)
    rows = -(-rows // rows_mult) * rows_mult
    return jnp.pad(flat, (0, rows * LANE - flat.size)).reshape(rows, LANE)


def _unpack(packed, like):
    flat = packed.reshape(-1)
    out, pos = [], 0
    for a in like:
        n = math.prod(a.shape)
        out.append(flat[pos:pos + n].reshape(a.shape))
        pos += n
    return out


def _adamw_nd(w, g, m, v, name):
    shape = w.shape
    C = shape[-1]
    d, nm, nv = _adamw(w.reshape(-1, C), g.reshape(-1, C), m.reshape(-1, C), v.reshape(-1, C), name)
    return d.reshape(shape), nm.reshape(shape), nv.reshape(shape)


def kernel(x, a_norm, a_w_in, a_v_norm, a_w_s, a_b_s, a_w_out, kv_norm, w_kvf, b_f, k_norm, b_norm, b_w_qg, q_norm, b_w_out, f_norm, f_w_up, f_conv_w, f_conv_b, f_w_down, final_norm, loss_target, m_a_norm, m_a_w_in, m_a_v_norm, m_a_w_s, m_a_b_s, m_a_w_out, m_kv_norm, m_w_kvf, m_b_f, m_k_norm, m_b_norm, m_b_w_qg, m_q_norm, m_b_w_out, m_f_norm, m_f_w_up, m_f_conv_w, m_f_conv_b, m_f_w_down, m_final_norm, v_a_norm, v_a_w_in, v_a_v_norm, v_a_w_s, v_a_b_s, v_a_w_out, v_kv_norm, v_w_kvf, v_b_f, v_k_norm, v_b_norm, v_b_w_qg, v_q_norm, v_b_w_out, v_f_norm, v_f_w_up, v_f_conv_w, v_f_conv_b, v_f_w_down, v_final_norm):
    weights = dict(a_norm=a_norm, a_w_in=a_w_in, a_v_norm=a_v_norm, a_w_s=a_w_s, a_b_s=a_b_s, a_w_out=a_w_out, kv_norm=kv_norm, w_kvf=w_kvf, b_f=b_f, k_norm=k_norm, b_norm=b_norm, b_w_qg=b_w_qg, q_norm=q_norm, b_w_out=b_w_out, f_norm=f_norm, f_w_up=f_w_up, f_conv_w=f_conv_w, f_conv_b=f_conv_b, f_w_down=f_w_down, final_norm=final_norm)
    mom1 = dict(a_norm=m_a_norm, a_w_in=m_a_w_in, a_v_norm=m_a_v_norm, a_w_s=m_a_w_s, a_b_s=m_a_b_s, a_w_out=m_a_w_out, kv_norm=m_kv_norm, w_kvf=m_w_kvf, b_f=m_b_f, k_norm=m_k_norm, b_norm=m_b_norm, b_w_qg=m_b_w_qg, q_norm=m_q_norm, b_w_out=m_b_w_out, f_norm=m_f_norm, f_w_up=m_f_w_up, f_conv_w=m_f_conv_w, f_conv_b=m_f_conv_b, f_w_down=m_f_w_down, final_norm=m_final_norm)
    mom2 = dict(a_norm=v_a_norm, a_w_in=v_a_w_in, a_v_norm=v_a_v_norm, a_w_s=v_a_w_s, a_b_s=v_a_b_s, a_w_out=v_a_w_out, kv_norm=v_kv_norm, w_kvf=v_w_kvf, b_f=v_b_f, k_norm=v_k_norm, b_norm=v_b_norm, b_w_qg=v_b_w_qg, q_norm=v_q_norm, b_w_out=v_b_w_out, f_norm=v_f_norm, f_w_up=v_f_w_up, f_conv_w=v_f_conv_w, f_conv_b=v_f_conv_b, f_w_down=v_f_w_down, final_norm=v_final_norm)
    names = list(weights)
    big = ["a_w_in", "a_w_out", "w_kvf", "b_w_qg", "b_w_out", "f_w_up", "f_w_down"]
    small_sharded = ["a_norm", "a_v_norm", "f_conv_w"]
    small_repl = ["a_w_s", "a_b_s", "kv_norm", "b_f", "k_norm", "b_norm", "q_norm", "f_norm", "f_conv_b", "final_norm"]

    _, S, D = x.shape
    NA, NB, DEPTH = a_norm.shape[0], b_norm.shape[0], f_norm.shape[0]
    W = a_w_out.shape[1] * N_CHIPS
    G = a_w_s.shape[1]
    H = b_f.shape[0]
    ATT = H * HEAD
    F = f_w_down.shape[1] * N_CHIPS
    Ckv = w_kvf.shape[1]
    Cp = -(-Ckv // LANE) * LANE
    assert W == G * CHUNK and Ckv * N_CHIPS == 2 * ATT + H and S % CHUNK == 0
    core = lax.axis_index("c").astype(jnp.int32).reshape(1)
    me = (2 * lax.axis_index("x") + lax.axis_index("y")).astype(jnp.int32).reshape(1)

    small_local = [weights[k] for k in small_sharded]
    place = lambda p, nm, dt=BF16, layer=None, after=None: _place_part(p, me, dt, "place_" + nm, layer, after)
    g_ain, g_aout, g_up, g_down, g_attn = [None] * NA, [None] * NA, [None] * DEPTH, [None] * DEPTH, {}
    groups = [("ffn", 0)] + [("mix", l) for l in range(1, NA)] + [("attn", None)] + [("ffn", l) for l in range(NA, DEPTH)]

    def launch_next(after=None):
        if not groups:
            return
        kind, l = groups.pop(0)
        cid = DEPTH + 1 - len(groups)
        if kind == "attn":
            bufs = [place(jnp.pad(w_kvf, ((0, 0), (0, Cp - Ckv)))[None], "w_kvf", after=after),
                    place(b_w_qg, "b_w_qg", after=after), place(b_w_out, "b_w_out", after=after)]
            g_attn["kvf"], g_attn["qg"], g_attn["out"] = _gather_chips_async(bufs, cid, f"gather_w{cid}")
            return
        bufs = [place(f_w_up, f"f_w_up{l}", layer=l, after=after), place(f_w_down, f"f_w_down{l}", layer=l, after=after)]
        if kind == "mix":
            bufs += [place(a_w_in, f"a_w_in{l}", layer=l, after=after), place(a_w_out, f"a_w_out{l}", layer=l, after=after)]
            g_up[l], g_down[l], g_ain[l], g_aout[l] = _gather_chips_async(bufs, cid, f"gather_w{cid}")
        else:
            g_up[l], g_down[l] = _gather_chips_async(bufs, cid, f"gather_w{cid}")

    g_small, g_ain[0], g_aout[0] = _gather_chips_async(
        [place(_pack(small_local, 32)[None], "small", F32), place(a_w_in, "a_w_in0", layer=0), place(a_w_out, "a_w_out0", layer=0)],
        0, "gather_w0")
    launch_next()

    per_chip = [_unpack(g_small[j, 0], small_local) for j in range(N_CHIPS)]
    a_norm_f, a_vnorm_f, conv_w_f = [jnp.concatenate([per_chip[j][k] for j in range(N_CHIPS)], axis=-1) for k in range(3)]
    cw = conv_w_f.reshape(DEPTH, 3, 2, F).transpose(0, 2, 1, 3)
    cb = f_conv_b.reshape(DEPTH, 2, 1, F)
    tril = jnp.tril(jnp.ones((CHUNK, CHUNK), dtype=bool))
    wm = jnp.where(tril, a_w_s, 0.0).astype(BF16)
    wmt = jnp.swapaxes(wm, -1, -2)
    bcol = a_b_s[..., None]
    bf_pad = jnp.pad(b_f, (0, LANE - H))[None]
    row = lambda v: v.reshape(1, -1)

    h = x[0]
    target = loss_target[0]
    saved = [dict() for _ in range(DEPTH)]
    kvs = {}
    for l in range(DEPTH):
        sv = saved[l]
        sv["h_m"] = h
        if l < NA:
            xn = _rms_fwd(h, row(a_norm_f[l]), f"a{l}_norm")
            zpre = _mm(m2(xn), mgc(g_ain[l], 0), "nn", m2(_sds((S, 2 * W), BF16)), tm=1024, tn=512, name=f"a{l}_in")
            if l + 1 < NA:
                launch_next(zpre)
            gated = _sgu_fwd(zpre, wm[l], bcol[l], row(a_vnorm_f[l]), f"a{l}_sgu")
            h = _mm(m2(gated), mgr(g_aout[l], 0), "nn", m2(_sds((S, D), F32)), res=m2(h), tm=1024, tn=1024, name=f"a{l}_out")
            sv.update(xn_m=xn, zpre=zpre, gated=gated)
        else:
            j = l - NA
            xn = _rms_fwd(h, row(b_norm[j]), f"b{j}_norm")
            qg = _mm(m2(xn), mgc(g_attn["qg"], j), "nn", mcs(_sds((2, S, ATT), BF16)), tm=1024, tn=512, name=f"b{j}_qg")
            qn = _headnorm_fwd(qg, 0, row(q_norm[j]) * QK_SCALE, f"b{j}_qnorm")
            o, lse = _attn_fwd(qn, kvs["kn"], kvs["kv"], kvs["crow"], f"b{j}_attn")
            og = _gate_fwd(o, qg, f"b{j}_gate")
            h = _mm(m2(og), mgr(g_attn["out"], j), "nn", m2(_sds((S, D), F32)), res=m2(h), tm=1024, tn=1024, name=f"b{j}_out")
            sv.update(xn_m=xn, qg=qg, qn=qn, o=o, lse=lse, og=og)
        sv["h_f"] = h
        xn = _rms_fwd(h, row(f_norm[l]), f"f{l}_norm")
        hup = _mm(m2(xn), mgc(g_up[l], 0), "nn", mcs(_sds((2, S, F), BF16)), tm=1024, tn=1408, name=f"f{l}_up")
        launch_next(hup)
        act = _convgate_fwd(hup, cw[l], cb[l], f"f{l}_conv")
        h = _mm(m2(act), mgr(g_down[l], 0), "nn", m2(_sds((S, D), F32)), res=m2(h), tm=1024, tn=1024, tk=1408, name=f"f{l}_down")
        sv.update(xn_f=xn, hup=hup, act=act)
        if l == NA - 1:
            w_kvf_g = mgc(g_attn["kvf"], 0)
            xn_kv = _rms_fwd(h, row(kv_norm), "kv_norm")
            kvp = _mm(m2(xn_kv), w_kvf_g, "nn", m2(_sds((S, N_CHIPS * Cp), F32)), tm=1024, tn=Cp, name="kv_proj")
            launch_next(kvp)
            kvflat = jnp.concatenate([kvp[:, j * Cp:j * Cp + Ckv] for j in range(N_CHIPS)], axis=1)
            kv = jnp.stack([kvflat[:, :ATT], kvflat[:, ATT:2 * ATT]]).astype(BF16)
            fpre = jnp.pad(kvflat[:, 2 * ATT:], ((0, 0), (0, LANE - H)))
            kn = _headnorm_fwd(kv, 0, row(k_norm), "kv_knorm")
            cums = _logf_cumsum(fpre, bf_pad, "kv_cumsum")
            cT = cums[:, :H].T
            kvs.update(h=h, xn=xn_kv, kv=kv, fpre=fpre, kn=kn, crow=cT[:, None, :])

    loss11, dh, dhb, d_final = _final_loss(h, row(final_norm), target, "final_loss")
    loss = lax.psum(loss11[0, 0], ("x", "y", "c"))

    stacked = {k: tuple(lax.empty(weights[k].shape, F32) for _ in range(4))
               for k in ("a_w_in", "a_w_out", "b_w_qg", "b_w_out", "f_w_up", "f_w_down")}
    flat = {}
    pending = []
    next_id = [DEPTH + 2]

    def grad_buf(k):
        return _sds((N_CHIPS, 1) + weights[k].shape[1:], BF16)

    def start_chunk(entries, tag, last=False):
        Gs = [e[2] for e in entries]
        others = _swap_halves(Gs, f"grads_swap_{tag}")
        partial = [_pair_sum(g, o, core, g.dtype, f"grads_pair_{tag}{i}") for i, (g, o) in enumerate(zip(Gs, others))]
        if last:
            by_chip = _scatter_chips(partial, f"grads_scatter_{tag}")
        else:
            by_chip = _scatter_chips_async(partial, next_id[0], f"grads_scatter_{tag}")
        next_id[0] += 1
        pending.append((entries, partial, by_chip, tag))
        return partial

    def finish_chunk(after):
        entries, partial, by_chip, tag = pending.pop(0)
        halves = []
        for i, ((k, l, _), p, q) in enumerate(zip(entries, partial, by_chip)):
            if k == "small":
                half = _sum_small(lax.dynamic_index_in_dim(p, me[0], 0, keepdims=False), q, "grads_sum_small")
                hs = half.shape[1]
                halves.append(lax.dynamic_update_slice(jnp.zeros((1, 2 * hs, LANE), F32), half, (0, core[0] * hs, 0)))
            else:
                halves.append(_sum_chips(q, p, me, core, f"grads_sum_{tag}{i}", after=after))
        for (k, l, _), f in zip(entries, _join_halves(halves, f"grads_join_{tag}")):
            if k in stacked:
                stacked[k] = tuple(_adamw_layer(weights[k], f, mom1[k], mom2[k], l, stacked[k], f"adamw_{k}{l}"))
            else:
                flat[k] = f

    d_anorm, d_avnorm, d_ws, d_bs = [None] * NA, [None] * NA, [None] * NA, [None] * NA
    d_bnorm, d_qnorm = [None] * NB, [None] * NB
    d_fnorm, d_cw, d_cb = [None] * DEPTH, [None] * DEPTH, [None] * DEPTH
    dkn, dvv, dck = [], [], []
    G_kvf = d_kvnorm = d_bf = d_knorm = None
    tok = None
    for l in reversed(range(DEPTH)):
        sv = saved[l]
        chunk = []
        if l == NA - 1:
            dkv, d_knorm = _headnorm_bwd(dkn, kvs["kv"], row(k_norm), dvv, "kv_knorm_bwd")
            dc = sum(dck)
            dc = jnp.pad(dc, ((0, 0), (0, LANE - H)))
            df, d_bf = _logf_cumsum_bwd(dc, kvs["fpre"], bf_pad, "kv_cumsum_bwd")
            dflat = jnp.concatenate([dkv[0], dkv[1], df[:, :H]], axis=1)
            dpad = jnp.concatenate([jnp.pad(dflat[:, j * Ckv:(j + 1) * Ckv], ((0, 0), (0, Cp - Ckv))) for j in range(N_CHIPS)], axis=1)
            dxn = _mm(m2(dpad), w_kvf_g, "nt", m2(_sds((S, D), BF16)), after=tok, tm=1024, tn=2048, tk=Cp, name="kv_proj_dx")
            dw_kv = _mm(m2(kvs["xn"]), mcs(dkv), "tn", mcs(_sds((2, D, ATT), BF16)), tm=1024, tn=1024, tk=2048, name="kv_proj_dw")
            dw_f = _mm(m2(kvs["xn"]), m2(df), "tn", m2(_sds((D, LANE), BF16)), tm=512, tk=2048, name="kv_fproj_dw")
            dfull = jnp.concatenate([dw_kv[0], dw_kv[1], dw_f[:, :H]], axis=1)
            G_kvf = jnp.pad(dfull.reshape(D, N_CHIPS, Ckv).transpose(1, 0, 2), ((0, 0), (0, 0), (0, Cp - Ckv)))[:, None]
            dh, dhb, d_kvnorm = _rms_bwd(dxn, kvs["h"], row(kv_norm), dh, "kv_norm_bwd")
            chunk.append(("w_kvf", 0, G_kvf))
        dact = _mm(m2(dhb), mgr(g_down[l], 0), "nt", m2(_sds((S, F), BF16)), after=tok, tm=1024, tn=1408, tk=2048, name=f"f{l}_down_dx")
        G_down = _mm(m2(sv["act"]), m2(dhb), "tn", mgr(grad_buf("f_w_down"), 0), tm=1408, tn=1024, tk=2048, name=f"f{l}_down_dw")
        dhup, d_cw[l], d_cb[l] = _convgate_bwd(sv["hup"], dact, cw[l], cb[l], f"f{l}_conv_bwd")
        dxn = _mm(mcs(dhup), mgc(g_up[l], 0), "nt", m2(_sds((S, D), BF16)), tm=1024, tn=2048, tk=1408, name=f"f{l}_up_dx")
        G_up = _mm(m2(sv["xn_f"]), mcs(dhup), "tn", mgc(grad_buf("f_w_up"), 0), tm=1024, tn=1408, tk=2048, name=f"f{l}_up_dw")
        dh, dhb, d_fnorm[l] = _rms_bwd(dxn, sv["h_f"], row(f_norm[l]), dh, f"f{l}_norm_bwd")
        chunk += [("f_w_up", l, G_up), ("f_w_down", l, G_down)]
        if l == 0:
            tok = start_chunk(chunk, "f0")
            finish_chunk(tok)
            chunk = []
        if l >= NA:
            j = l - NA
            dog = _mm(m2(dhb), mgr(g_attn["out"], j), "nt", m2(_sds((S, ATT), BF16)), tm=1024, tn=512, tk=2048, name=f"b{j}_out_dx")
            G_bout = _mm(m2(sv["og"]), m2(dhb), "tn", mgr(grad_buf("b_w_out"), 0), tm=512, tn=1024, tk=4096, name=f"b{j}_out_dw")
            do, dgate, delta = _gate_bwd(dog, sv["o"], sv["qg"], f"b{j}_gate_bwd")
            dqn, drow, dkn_j, dv_j, dcol = _attn_bwd(sv["qn"], kvs["kn"], kvs["kv"], kvs["crow"], do, sv["lse"], delta, f"b{j}_attn_bwd")
            dkn.append(dkn_j); dvv.append(dv_j); dck.append(drow[:, :, 0].T - dcol[:, 0, :].T)
            dqg, d_qs = _headnorm_bwd([dqn], sv["qg"], row(q_norm[j]) * QK_SCALE, [dgate], f"b{j}_qnorm_bwd")
            d_qnorm[j] = d_qs * QK_SCALE
            dxn = _mm(mcs(dqg), mgc(g_attn["qg"], j), "nt", m2(_sds((S, D), BF16)), tm=1024, tn=2048, tk=1024, name=f"b{j}_qg_dx")
            G_bqg = _mm(m2(sv["xn_m"]), mcs(dqg), "tn", mgc(grad_buf("b_w_qg"), 0), tm=1024, tn=1024, tk=2048, name=f"b{j}_qg_dw")
            dh, dhb, d_bnorm[j] = _rms_bwd(dxn, sv["h_m"], row(b_norm[j]), dh, f"b{j}_norm_bwd")
            chunk += [("b_w_qg", j, G_bqg), ("b_w_out", j, G_bout)]
        else:
            dgated = _mm(m2(dhb), mgr(g_aout[l], 0), "nt", m2(_sds((S, W), BF16)), after=tok, tm=1024, tn=512, tk=2048, name=f"a{l}_out_dx")
            G_aout = _mm(m2(sv["gated"]), m2(dhb), "tn", mgr(grad_buf("a_w_out"), 0), tm=512, tn=1024, tk=4096, name=f"a{l}_out_dw")
            dz, d_ws[l], d_bs[l], d_avnorm[l] = _sgu_bwd(sv["zpre"], dgated, wm[l], wmt[l], bcol[l], row(a_vnorm_f[l]), f"a{l}_sgu_bwd")
            dxn = _mm(m2(dz), mgc(g_ain[l], 0), "nt", m2(_sds((S, D), BF16)), tm=1024, tn=2048, tk=1024, name=f"a{l}_in_dx")
            G_ain = _mm(m2(sv["xn_m"]), m2(dz), "tn", mgc(grad_buf("a_w_in"), 0), tm=1024, tn=1024, tk=2048, name=f"a{l}_in_dw")
            dh, dhb, d_anorm[l] = _rms_bwd(dxn, sv["h_m"], row(a_norm_f[l]), dh, f"a{l}_norm_bwd")
            chunk += [("a_w_in", l, G_ain), ("a_w_out", l, G_aout)]
        if l > 0:
            tok = start_chunk(chunk, f"l{l}")
            if len(pending) > 1:
                finish_chunk(tok)
    grad_x = dh[None]

    full = dict(
        a_norm=jnp.concatenate(d_anorm, axis=0), a_v_norm=jnp.concatenate(d_avnorm, axis=0),
        f_conv_w=jnp.stack(d_cw).transpose(0, 2, 1, 3).reshape(DEPTH, 3, 2 * F),
        a_w_s=jnp.where(tril, jnp.stack(d_ws), 0.0), a_b_s=jnp.stack(d_bs)[..., 0],
        kv_norm=d_kvnorm[0], b_f=d_bf[0, :H], k_norm=d_knorm[0], b_norm=jnp.concatenate(d_bnorm, axis=0),
        q_norm=jnp.concatenate(d_qnorm, axis=0), f_norm=jnp.concatenate(d_fnorm, axis=0),
        f_conv_b=jnp.stack(d_cb).reshape(DEPTH, 2 * F), final_norm=d_final[0])
    shard_rows = []
    for j in range(N_CHIPS):
        pieces = []
        for k in small_sharded:
            n = weights[k].shape[-1]
            pieces.append(full[k][..., j * n:(j + 1) * n])
        shard_rows.append(_pack(pieces, 32))
    rs = shard_rows[0].shape[0]
    repl = _pack([full[k] for k in small_repl], N_CHIPS * 32)
    rr = repl.shape[0] // N_CHIPS
    G_small = jnp.concatenate([jnp.stack(shard_rows), repl.reshape(N_CHIPS, rr, LANE)], axis=1)[:, None]

    tok = start_chunk(chunk + [("small", 0, G_small)], "l0", last=True)
    finish_chunk(tok)
    finish_chunk(None)
    F_small = flat["small"]
    repl_buf = lax.dynamic_update_slice(jnp.zeros((N_CHIPS, 1, rr, LANE), F32), F_small[None, :, rs:, :], (me[0], 0, 0, 0))
    (repl_all,) = _gather_chips([repl_buf], "gather_small_grads")

    grads, delta, new_m, new_v = ({k: t[i] for k, t in stacked.items()} for i in range(4))
    grads["w_kvf"] = flat["w_kvf"][0, :, :Ckv]
    for k, gk in zip(small_sharded, _unpack(F_small[0, :rs], small_local)):
        grads[k] = gk
    for k, gk in zip(small_repl, _unpack(repl_all.reshape(N_CHIPS * rr, LANE), [weights[k] for k in small_repl])):
        grads[k] = gk

    delta["w_kvf"], new_m["w_kvf"], new_v["w_kvf"] = _adamw_nd(w_kvf, grads["w_kvf"], m_w_kvf, v_w_kvf, "adamw_w_kvf")
    small = small_sharded + small_repl
    packed = [_pack([t[k] for k in small], 8) for t in (weights, grads, mom1, mom2)]
    outs = _adamw(*packed, "adamw_small")
    like = [weights[k] for k in small]
    for t, o in zip((delta, new_m, new_v), outs):
        for k, a in zip(small, _unpack(o, like)):
            t[k] = a

    return (loss, grad_x, *[grads[k] for k in names], *[delta[k] for k in names],
            *[new_m[k] for k in names], *[new_v[k] for k in names])
```
